```python
import math
import jax, jax.numpy as jnp
from jax import lax
import numpy as np

D_MODEL = 2048
BATCH = 8
SEQ = 8192
DEPTH = 4

CHUNK = 64
Q_BLOCK = 128
N_MIXERS = 2
MLA_HEADS = 16
Q_LORA = 512
KV_LORA = 512
NOPE_DIM = 128
ROPE_DIM = 64
V_DIM = 128
ROPE_THETA = 10000.0
SB_HEADS = 16
SB_HEAD_DIM = D_MODEL // SB_HEADS
N_MEM = 256
MEM_HEADS = 4
MEM_HEAD_DIM = 128
D_FF = 5632
CONV_W = 3
EPS = 1e-6

N_A = (DEPTH + 1) // 2
N_B = DEPTH // 2

kernel_name = "hybrid_mla_stickbreaking_convffn_trunk"


def _rmsnorm(x, g):
    xf = x.astype(jnp.float32)
    y = xf * lax.rsqrt(jnp.mean(xf * xf, axis=-1, keepdims=True) + EPS)
    return (y * g.astype(jnp.float32)).astype(x.dtype)


def _rope_tables(positions, dtype):
    inv_freq = ROPE_THETA ** (-jnp.arange(0, ROPE_DIM, 2, dtype=jnp.float32) / ROPE_DIM)
    ang = positions.astype(jnp.float32)[..., None] * inv_freq
    ang = jnp.concatenate([ang, ang], axis=-1)
    return jnp.cos(ang).astype(dtype), jnp.sin(ang).astype(dtype)


def _apply_rope(x, cos, sin):
    half = ROPE_DIM // 2
    rot = jnp.concatenate([-x[..., half:], x[..., :half]], axis=-1)
    return x * cos + rot * sin


def _sweep_query_blocks(block_fn, qs):
    b, h, s, _ = qs[0].shape
    nb = s // Q_BLOCK
    qb = tuple(q.reshape(b, h, nb, Q_BLOCK, q.shape[-1]).transpose(2, 0, 1, 3, 4) for q in qs)
    starts = jnp.arange(nb, dtype=jnp.int32) * Q_BLOCK
    out = lax.map(lambda a: block_fn(*a), (*qb, starts))
    return out.transpose(1, 2, 0, 3, 4).reshape(b, h, s, out.shape[-1])


def _mla(h, cos, sin, w_down, q_norm, w_uq, kv_norm, w_ukv, w_o):
    b, s, _ = h.shape
    down = h @ w_down
    c_q = _rmsnorm(down[..., :Q_LORA], q_norm)
    c_kv = _rmsnorm(down[..., Q_LORA:Q_LORA + KV_LORA], kv_norm)
    k_rope = _apply_rope(down[..., Q_LORA + KV_LORA:], cos, sin)
    q = (c_q @ w_uq).reshape(b, s, MLA_HEADS, NOPE_DIM + ROPE_DIM)
    q_nope = q[..., :NOPE_DIM].transpose(0, 2, 1, 3)
    q_rope = _apply_rope(q[..., NOPE_DIM:], cos[:, :, None], sin[:, :, None]).transpose(0, 2, 1, 3)
    kv = (c_kv @ w_ukv).reshape(b, s, MLA_HEADS, NOPE_DIM + V_DIM)
    k_nope = kv[..., :NOPE_DIM].transpose(0, 2, 1, 3)
    v = kv[..., NOPE_DIM:].transpose(0, 2, 1, 3)
    scale = 1.0 / math.sqrt(NOPE_DIM + ROPE_DIM)
    k_chunk = jnp.arange(s, dtype=jnp.int32) // CHUNK

    def block(qn, qr, start):
        sc = (jnp.einsum('bhqd,bhkd->bhqk', qn, k_nope)
              + jnp.einsum('bhqd,bkd->bhqk', qr, k_rope)).astype(jnp.float32) * scale
        q_chunk = (start + jnp.arange(Q_BLOCK, dtype=jnp.int32)) // CHUNK
        allowed = k_chunk[None, :] <= q_chunk[:, None]
        p = jax.nn.softmax(jnp.where(allowed, sc, -jnp.inf), axis=-1).astype(v.dtype)
        return jnp.einsum('bhqk,bhkd->bhqd', p, v)

    o = _sweep_query_blocks(block, (q_nope, q_rope))
    return o.transpose(0, 2, 1, 3).reshape(b, s, MLA_HEADS * V_DIM) @ w_o


def _stick_breaking(h, w_qkv, w_o):
    b, s, _ = h.shape
    qkv = (h @ w_qkv).reshape(b, s, 3, SB_HEADS, SB_HEAD_DIM).transpose(2, 0, 3, 1, 4)
    q, k, v = qkv[0], qkv[1], qkv[2]
    scale = 1.0 / math.sqrt(SB_HEAD_DIM)
    k_idx = jnp.arange(s, dtype=jnp.int32)

    def block(qb, start):
        z = jnp.einsum('bhqd,bhkd->bhqk', qb, k).astype(jnp.float32) * scale
        t_idx = start + jnp.arange(Q_BLOCK, dtype=jnp.int32)
        strict = k_idx[None, :] < t_idx[:, None]
        log_keep = jnp.where(strict, jax.nn.log_sigmoid(-z), 0.0)
        later = lax.cumsum(log_keep, axis=3, reverse=True) - log_keep
        a = jnp.where(strict, jnp.exp(jax.nn.log_sigmoid(z) + later), 0.0).astype(v.dtype)
        return jnp.einsum('bhqk,bhkd->bhqd', a, v)

    o = _sweep_query_blocks(block, (q,))
    return o.transpose(0, 2, 1, 3).reshape(b, s, SB_HEADS * SB_HEAD_DIM) @ w_o


def _mem_cross_attn(h, hm, w_q, w_kv, w_o):
    b, s, _ = h.shape
    q = (h @ w_q).reshape(b, s, MEM_HEADS, MEM_HEAD_DIM)
    kv = (hm @ w_kv).reshape(b, hm.shape[1], 2, MEM_HEADS, MEM_HEAD_DIM)
    sc = jnp.einsum('bqhd,bkhd->bhqk', q, kv[:, :, 0]).astype(jnp.float32) / math.sqrt(MEM_HEAD_DIM)
    p = jax.nn.softmax(sc, axis=-1).astype(h.dtype)
    o = jnp.einsum('bhqk,bkhd->bqhd', p, kv[:, :, 1])
    return o.reshape(b, s, MEM_HEADS * MEM_HEAD_DIM) @ w_o


def _conv_ffn(h, w_in, conv_w, conv_b, w_out):
    s = h.shape[1]
    u = h @ w_in
    pad = jnp.pad(u, ((0, 0), (CONV_W - 1, 0), (0, 0)))
    u = conv_b + sum(conv_w[j] * pad[:, j:j + s] for j in range(CONV_W))
    gate, up = u[..., :D_FF], u[..., D_FF:]
    return (jax.nn.silu(gate) * up) @ w_out


def _fwd_setup_inputs(seed: int = 0) -> dict:
    key = jax.random.key(seed)
    ks = jax.random.split(key, 24)
    f32 = jnp.float32

    def w(k, shape, fan_in):
        return jax.random.normal(k, shape, f32) * (fan_in ** -0.5)

    def gain(k, shape):
        return 1.0 + 0.01 * jax.random.normal(k, shape, f32)

    D = D_MODEL
    return {
        "x": jax.random.normal(ks[0], (BATCH, SEQ, D), f32),
        "mem": jax.random.normal(ks[1], (BATCH, N_MEM, D), f32),
        "positions": jnp.broadcast_to(jnp.arange(SEQ, dtype=jnp.int32), (BATCH, SEQ)),
        "norm_mix": gain(ks[2], (DEPTH, D)),
        "norm_mem_q": gain(ks[3], (DEPTH, D)),
        "norm_mem_kv": gain(ks[4], (DEPTH, D)),
        "norm_ffn": gain(ks[5], (DEPTH, D)),
        "norm_final": gain(ks[6], (D,)),
        "mla_w_down": w(ks[7], (N_A, D, Q_LORA + KV_LORA + ROPE_DIM), D),
        "mla_q_norm": gain(ks[8], (N_A, Q_LORA)),
        "mla_w_uq": w(ks[9], (N_A, Q_LORA, MLA_HEADS * (NOPE_DIM + ROPE_DIM)), Q_LORA),
        "mla_kv_norm": gain(ks[10], (N_A, KV_LORA)),
        "mla_w_ukv": w(ks[11], (N_A, KV_LORA, MLA_HEADS * (NOPE_DIM + V_DIM)), KV_LORA),
        "mla_w_o": w(ks[12], (N_A, MLA_HEADS * V_DIM, D), MLA_HEADS * V_DIM),
        "sb_w_qkv": w(ks[13], (N_B, D, 3 * SB_HEADS * SB_HEAD_DIM), D),
        "sb_w_o": w(ks[14], (N_B, SB_HEADS * SB_HEAD_DIM, D), SB_HEADS * SB_HEAD_DIM),
        "mem_w_q": w(ks[15], (DEPTH, D, MEM_HEADS * MEM_HEAD_DIM), D),
        "mem_w_kv": w(ks[16], (DEPTH, D, 2 * MEM_HEADS * MEM_HEAD_DIM), D),
        "mem_w_o": w(ks[17], (DEPTH, MEM_HEADS * MEM_HEAD_DIM, D), MEM_HEADS * MEM_HEAD_DIM),
        "ffn_w_in": w(ks[18], (DEPTH, D, 2 * D_FF), D),
        "ffn_conv_w": w(ks[19], (DEPTH, CONV_W, 2 * D_FF), CONV_W),
        "ffn_conv_b": 0.02 * jax.random.normal(ks[20], (DEPTH, 2 * D_FF), f32),
        "ffn_w_out": w(ks[21], (DEPTH, D_FF, D), D_FF),
    }


def _fwd_reference(x, mem, positions, norm_mix, norm_mem_q, norm_mem_kv, norm_ffn, norm_final,
              mla_w_down, mla_q_norm, mla_w_uq, mla_kv_norm, mla_w_ukv, mla_w_o,
              sb_w_qkv, sb_w_o, mem_w_q, mem_w_kv, mem_w_o,
              ffn_w_in, ffn_conv_w, ffn_conv_b, ffn_w_out):
    cos, sin = _rope_tables(positions, x.dtype)
    for i in range(DEPTH):
        h = _rmsnorm(x, norm_mix[i])
        j = i // N_MIXERS
        if i % N_MIXERS == 0:
            x = x + _mla(h, cos, sin, mla_w_down[j], mla_q_norm[j], mla_w_uq[j],
                         mla_kv_norm[j], mla_w_ukv[j], mla_w_o[j])
        else:
            x = x + _stick_breaking(h, sb_w_qkv[j], sb_w_o[j])
        x = x + _mem_cross_attn(_rmsnorm(x, norm_mem_q[i]), _rmsnorm(mem, norm_mem_kv[i]),
                                mem_w_q[i], mem_w_kv[i], mem_w_o[i])
        x = x + _conv_ffn(_rmsnorm(x, norm_ffn[i]), ffn_w_in[i], ffn_conv_w[i],
                          ffn_conv_b[i], ffn_w_out[i])
    return _rmsnorm(x, norm_final)


import jax as _jax
import jax.numpy as _jnp

TWIN_FORMAT = 'train_step'
FWD_PARAMS = ['x', 'mem', 'positions', 'norm_mix', 'norm_mem_q', 'norm_mem_kv', 'norm_ffn', 'norm_final', 'mla_w_down', 'mla_q_norm', 'mla_w_uq', 'mla_kv_norm', 'mla_w_ukv', 'mla_w_o', 'sb_w_qkv', 'sb_w_o', 'mem_w_q', 'mem_w_kv', 'mem_w_o', 'ffn_w_in', 'ffn_conv_w', 'ffn_conv_b', 'ffn_w_out']
TWIN_WEIGHTS = ['norm_mix', 'norm_mem_q', 'norm_mem_kv', 'norm_ffn', 'norm_final', 'mla_w_down', 'mla_q_norm', 'mla_w_uq', 'mla_kv_norm', 'mla_w_ukv', 'mla_w_o', 'sb_w_qkv', 'sb_w_o', 'mem_w_q', 'mem_w_kv', 'mem_w_o', 'ffn_w_in', 'ffn_conv_w', 'ffn_conv_b', 'ffn_w_out']
TWIN_DIFF_INPUT = 'x'
TWIN_INPUTS = ['x', 'mem', 'positions', 'norm_mix', 'norm_mem_q', 'norm_mem_kv', 'norm_ffn', 'norm_final', 'mla_w_down', 'mla_q_norm', 'mla_w_uq', 'mla_kv_norm', 'mla_w_ukv', 'mla_w_o', 'sb_w_qkv', 'sb_w_o', 'mem_w_q', 'mem_w_kv', 'mem_w_o', 'ffn_w_in', 'ffn_conv_w', 'ffn_conv_b', 'ffn_w_out', 'loss_target', 'm_norm_mix', 'm_norm_mem_q', 'm_norm_mem_kv', 'm_norm_ffn', 'm_norm_final', 'm_mla_w_down', 'm_mla_q_norm', 'm_mla_w_uq', 'm_mla_kv_norm', 'm_mla_w_ukv', 'm_mla_w_o', 'm_sb_w_qkv', 'm_sb_w_o', 'm_mem_w_q', 'm_mem_w_kv', 'm_mem_w_o', 'm_ffn_w_in', 'm_ffn_conv_w', 'm_ffn_conv_b', 'm_ffn_w_out', 'v_norm_mix', 'v_norm_mem_q', 'v_norm_mem_kv', 'v_norm_ffn', 'v_norm_final', 'v_mla_w_down', 'v_mla_q_norm', 'v_mla_w_uq', 'v_mla_kv_norm', 'v_mla_w_ukv', 'v_mla_w_o', 'v_sb_w_qkv', 'v_sb_w_o', 'v_mem_w_q', 'v_mem_w_kv', 'v_mem_w_o', 'v_ffn_w_in', 'v_ffn_conv_w', 'v_ffn_conv_b', 'v_ffn_w_out']
TWIN_OUTPUTS = ['loss', 'grad_x', 'grad_norm_mix', 'grad_norm_mem_q', 'grad_norm_mem_kv', 'grad_norm_ffn', 'grad_norm_final', 'grad_mla_w_down', 'grad_mla_q_norm', 'grad_mla_w_uq', 'grad_mla_kv_norm', 'grad_mla_w_ukv', 'grad_mla_w_o', 'grad_sb_w_qkv', 'grad_sb_w_o', 'grad_mem_w_q', 'grad_mem_w_kv', 'grad_mem_w_o', 'grad_ffn_w_in', 'grad_ffn_conv_w', 'grad_ffn_conv_b', 'grad_ffn_w_out', 'delta_norm_mix', 'delta_norm_mem_q', 'delta_norm_mem_kv', 'delta_norm_ffn', 'delta_norm_final', 'delta_mla_w_down', 'delta_mla_q_norm', 'delta_mla_w_uq', 'delta_mla_kv_norm', 'delta_mla_w_ukv', 'delta_mla_w_o', 'delta_sb_w_qkv', 'delta_sb_w_o', 'delta_mem_w_q', 'delta_mem_w_kv', 'delta_mem_w_o', 'delta_ffn_w_in', 'delta_ffn_conv_w', 'delta_ffn_conv_b', 'delta_ffn_w_out', 'new_m_norm_mix', 'new_m_norm_mem_q', 'new_m_norm_mem_kv', 'new_m_norm_ffn', 'new_m_norm_final', 'new_m_mla_w_down', 'new_m_mla_q_norm', 'new_m_mla_w_uq', 'new_m_mla_kv_norm', 'new_m_mla_w_ukv', 'new_m_mla_w_o', 'new_m_sb_w_qkv', 'new_m_sb_w_o', 'new_m_mem_w_q', 'new_m_mem_w_kv', 'new_m_mem_w_o', 'new_m_ffn_w_in', 'new_m_ffn_conv_w', 'new_m_ffn_conv_b', 'new_m_ffn_w_out', 'new_v_norm_mix', 'new_v_norm_mem_q', 'new_v_norm_mem_kv', 'new_v_norm_ffn', 'new_v_norm_final', 'new_v_mla_w_down', 'new_v_mla_q_norm', 'new_v_mla_w_uq', 'new_v_mla_kv_norm', 'new_v_mla_w_ukv', 'new_v_mla_w_o', 'new_v_sb_w_qkv', 'new_v_sb_w_o', 'new_v_mem_w_q', 'new_v_mem_w_kv', 'new_v_mem_w_o', 'new_v_ffn_w_in', 'new_v_ffn_conv_w', 'new_v_ffn_conv_b', 'new_v_ffn_w_out']
TWIN_LEAF_KINDS = {'loss': 'loss', 'grad_x': 'grad_x', 'grad_norm_mix': 'grad_w', 'grad_norm_mem_q': 'grad_w', 'grad_norm_mem_kv': 'grad_w', 'grad_norm_ffn': 'grad_w', 'grad_norm_final': 'grad_w', 'grad_mla_w_down': 'grad_w', 'grad_mla_q_norm': 'grad_w', 'grad_mla_w_uq': 'grad_w', 'grad_mla_kv_norm': 'grad_w', 'grad_mla_w_ukv': 'grad_w', 'grad_mla_w_o': 'grad_w', 'grad_sb_w_qkv': 'grad_w', 'grad_sb_w_o': 'grad_w', 'grad_mem_w_q': 'grad_w', 'grad_mem_w_kv': 'grad_w', 'grad_mem_w_o': 'grad_w', 'grad_ffn_w_in': 'grad_w', 'grad_ffn_conv_w': 'grad_w', 'grad_ffn_conv_b': 'grad_w', 'grad_ffn_w_out': 'grad_w', 'delta_norm_mix': 'delta_w', 'delta_norm_mem_q': 'delta_w', 'delta_norm_mem_kv': 'delta_w', 'delta_norm_ffn': 'delta_w', 'delta_norm_final': 'delta_w', 'delta_mla_w_down': 'delta_w', 'delta_mla_q_norm': 'delta_w', 'delta_mla_w_uq': 'delta_w', 'delta_mla_kv_norm': 'delta_w', 'delta_mla_w_ukv': 'delta_w', 'delta_mla_w_o': 'delta_w', 'delta_sb_w_qkv': 'delta_w', 'delta_sb_w_o': 'delta_w', 'delta_mem_w_q': 'delta_w', 'delta_mem_w_kv': 'delta_w', 'delta_mem_w_o': 'delta_w', 'delta_ffn_w_in': 'delta_w', 'delta_ffn_conv_w': 'delta_w', 'delta_ffn_conv_b': 'delta_w', 'delta_ffn_w_out': 'delta_w', 'new_m_norm_mix': 'new_m', 'new_m_norm_mem_q': 'new_m', 'new_m_norm_mem_kv': 'new_m', 'new_m_norm_ffn': 'new_m', 'new_m_norm_final': 'new_m', 'new_m_mla_w_down': 'new_m', 'new_m_mla_q_norm': 'new_m', 'new_m_mla_w_uq': 'new_m', 'new_m_mla_kv_norm': 'new_m', 'new_m_mla_w_ukv': 'new_m', 'new_m_mla_w_o': 'new_m', 'new_m_sb_w_qkv': 'new_m', 'new_m_sb_w_o': 'new_m', 'new_m_mem_w_q': 'new_m', 'new_m_mem_w_kv': 'new_m', 'new_m_mem_w_o': 'new_m', 'new_m_ffn_w_in': 'new_m', 'new_m_ffn_conv_w': 'new_m', 'new_m_ffn_conv_b': 'new_m', 'new_m_ffn_w_out': 'new_m', 'new_v_norm_mix': 'new_v', 'new_v_norm_mem_q': 'new_v', 'new_v_norm_mem_kv': 'new_v', 'new_v_norm_ffn': 'new_v', 'new_v_norm_final': 'new_v', 'new_v_mla_w_down': 'new_v', 'new_v_mla_q_norm': 'new_v', 'new_v_mla_w_uq': 'new_v', 'new_v_mla_kv_norm': 'new_v', 'new_v_mla_w_ukv': 'new_v', 'new_v_mla_w_o': 'new_v', 'new_v_sb_w_qkv': 'new_v', 'new_v_sb_w_o': 'new_v', 'new_v_mem_w_q': 'new_v', 'new_v_mem_w_kv': 'new_v', 'new_v_mem_w_o': 'new_v', 'new_v_ffn_w_in': 'new_v', 'new_v_ffn_conv_w': 'new_v', 'new_v_ffn_conv_b': 'new_v', 'new_v_ffn_w_out': 'new_v'}


def _forward(args):
    return _fwd_reference(*[args[k] for k in FWD_PARAMS])


def _output_shape():
    def fwd():
        inp = _fwd_setup_inputs(0)
        return _fwd_reference(*[inp[k] for k in FWD_PARAMS])
    out = _jax.eval_shape(fwd)
    return out.shape, out.dtype

N_MICROBATCH = 1
ADAM_LR = 0.001
ADAM_B1 = 0.9
ADAM_B2 = 0.999
ADAM_EPS = 1e-08
ADAM_WD = 0.01
ADAM_STEP = 10
PER_EXAMPLE_BATCH_AXIS = {'x': 0, 'mem': 0, 'positions': 0, 'loss_target': 0}
SHARED_INPUTS = []
_WEIGHT_DTYPES = {'norm_mix': _jnp.float32, 'norm_mem_q': _jnp.float32, 'norm_mem_kv': _jnp.float32, 'norm_ffn': _jnp.float32, 'norm_final': _jnp.float32, 'mla_w_down': _jnp.float32, 'mla_q_norm': _jnp.float32, 'mla_w_uq': _jnp.float32, 'mla_kv_norm': _jnp.float32, 'mla_w_ukv': _jnp.float32, 'mla_w_o': _jnp.float32, 'sb_w_qkv': _jnp.float32, 'sb_w_o': _jnp.float32, 'mem_w_q': _jnp.float32, 'mem_w_kv': _jnp.float32, 'mem_w_o': _jnp.float32, 'ffn_w_in': _jnp.float32, 'ffn_conv_w': _jnp.float32, 'ffn_conv_b': _jnp.float32, 'ffn_w_out': _jnp.float32}
MOMENT_SCALE = {'norm_mix': 6.090082e-02, 'norm_mem_q': 1.317917e-02, 'norm_mem_kv': 1.995266e-02, 'norm_ffn': 8.950472e-02, 'norm_final': 3.196070e+01, 'mla_w_down': 5.325858e-02, 'mla_q_norm': 4.426045e-02, 'mla_w_uq': 1.730634e-02, 'mla_kv_norm': 6.624745e-02, 'mla_w_ukv': 2.198625e-02, 'mla_w_o': 2.564811e-02, 'sb_w_qkv': 4.495968e-02, 'sb_w_o': 6.553327e-02, 'mem_w_q': 2.594480e-02, 'mem_w_kv': 2.737331e-02, 'mem_w_o': 1.424089e-02, 'ffn_w_in': 3.840770e-02, 'ffn_conv_w': 3.849223e-02, 'ffn_conv_b': 3.906953e-02, 'ffn_w_out': 6.284462e-02}


def _to_microbatches(a, axis):
    t = _jnp.moveaxis(a, axis, 0)
    t = t.reshape((N_MICROBATCH, t.shape[0] // N_MICROBATCH) + t.shape[1:])
    return _jnp.moveaxis(t, 1, axis + 1)


def setup_inputs(seed: int = 0) -> dict:
    inp = _fwd_setup_inputs(seed)
    key = _jax.random.fold_in(_jax.random.key(seed), 7919)
    shape, _ = _output_shape()
    out = dict(inp)
    out["loss_target"] = _jax.random.normal(_jax.random.fold_in(key, 0), shape, _jnp.float32)
    for i, name in enumerate(TWIN_WEIGHTS):
        w = inp[name].astype(_jnp.float32)
        if MOMENT_SCALE is None:
            s = _jnp.sqrt(_jnp.mean(_jnp.square(w)) + 1e-30)
        else:
            s = MOMENT_SCALE[name]
        km, kv = _jax.random.split(_jax.random.fold_in(key, i + 1))
        out[name] = w
        out["m_" + name] = s * _jax.random.normal(km, w.shape, _jnp.float32)
        out["v_" + name] = (s * s) * _jax.random.uniform(kv, w.shape, _jnp.float32, 0.5, 1.5)
    if N_MICROBATCH > 1:
        for name, axis in PER_EXAMPLE_BATCH_AXIS.items():
            out[name] = _to_microbatches(out[name], axis)
    return {'x': out['x'], 'mem': out['mem'], 'positions': out['positions'], 'norm_mix': out['norm_mix'], 'norm_mem_q': out['norm_mem_q'], 'norm_mem_kv': out['norm_mem_kv'], 'norm_ffn': out['norm_ffn'], 'norm_final': out['norm_final'], 'mla_w_down': out['mla_w_down'], 'mla_q_norm': out['mla_q_norm'], 'mla_w_uq': out['mla_w_uq'], 'mla_kv_norm': out['mla_kv_norm'], 'mla_w_ukv': out['mla_w_ukv'], 'mla_w_o': out['mla_w_o'], 'sb_w_qkv': out['sb_w_qkv'], 'sb_w_o': out['sb_w_o'], 'mem_w_q': out['mem_w_q'], 'mem_w_kv': out['mem_w_kv'], 'mem_w_o': out['mem_w_o'], 'ffn_w_in': out['ffn_w_in'], 'ffn_conv_w': out['ffn_conv_w'], 'ffn_conv_b': out['ffn_conv_b'], 'ffn_w_out': out['ffn_w_out'], 'loss_target': out['loss_target'], 'm_norm_mix': out['m_norm_mix'], 'm_norm_mem_q': out['m_norm_mem_q'], 'm_norm_mem_kv': out['m_norm_mem_kv'], 'm_norm_ffn': out['m_norm_ffn'], 'm_norm_final': out['m_norm_final'], 'm_mla_w_down': out['m_mla_w_down'], 'm_mla_q_norm': out['m_mla_q_norm'], 'm_mla_w_uq': out['m_mla_w_uq'], 'm_mla_kv_norm': out['m_mla_kv_norm'], 'm_mla_w_ukv': out['m_mla_w_ukv'], 'm_mla_w_o': out['m_mla_w_o'], 'm_sb_w_qkv': out['m_sb_w_qkv'], 'm_sb_w_o': out['m_sb_w_o'], 'm_mem_w_q': out['m_mem_w_q'], 'm_mem_w_kv': out['m_mem_w_kv'], 'm_mem_w_o': out['m_mem_w_o'], 'm_ffn_w_in': out['m_ffn_w_in'], 'm_ffn_conv_w': out['m_ffn_conv_w'], 'm_ffn_conv_b': out['m_ffn_conv_b'], 'm_ffn_w_out': out['m_ffn_w_out'], 'v_norm_mix': out['v_norm_mix'], 'v_norm_mem_q': out['v_norm_mem_q'], 'v_norm_mem_kv': out['v_norm_mem_kv'], 'v_norm_ffn': out['v_norm_ffn'], 'v_norm_final': out['v_norm_final'], 'v_mla_w_down': out['v_mla_w_down'], 'v_mla_q_norm': out['v_mla_q_norm'], 'v_mla_w_uq': out['v_mla_w_uq'], 'v_mla_kv_norm': out['v_mla_kv_norm'], 'v_mla_w_ukv': out['v_mla_w_ukv'], 'v_mla_w_o': out['v_mla_w_o'], 'v_sb_w_qkv': out['v_sb_w_qkv'], 'v_sb_w_o': out['v_sb_w_o'], 'v_mem_w_q': out['v_mem_w_q'], 'v_mem_w_kv': out['v_mem_w_kv'], 'v_mem_w_o': out['v_mem_w_o'], 'v_ffn_w_in': out['v_ffn_w_in'], 'v_ffn_conv_w': out['v_ffn_conv_w'], 'v_ffn_conv_b': out['v_ffn_conv_b'], 'v_ffn_w_out': out['v_ffn_w_out']}


def _loss(weights, diff, rest, loss_target):
    with _jax.named_scope("forward"):
        args = {**rest, TWIN_DIFF_INPUT: diff, **{k: w.astype(_WEIGHT_DTYPES[k]) for k, w in weights.items()}}
        y = _forward(args)
    with _jax.named_scope("loss_head"):
        err = _jnp.square(y.astype(_jnp.float32) - loss_target)
        return 0.5 * _jnp.sum(_jnp.mean(err, axis=-1)) if err.ndim else 0.5 * err


def _adamw(w, g, m, v):
    m = ADAM_B1 * m + (1.0 - ADAM_B1) * g
    v = ADAM_B2 * v + (1.0 - ADAM_B2) * _jnp.square(g)
    m_hat = m / (1.0 - ADAM_B1 ** ADAM_STEP)
    v_hat = v / (1.0 - ADAM_B2 ** ADAM_STEP)
    delta = -ADAM_LR * (m_hat / (_jnp.sqrt(v_hat) + ADAM_EPS) + ADAM_WD * w)
    return delta, m, v


def reference(x, mem, positions, norm_mix, norm_mem_q, norm_mem_kv, norm_ffn, norm_final, mla_w_down, mla_q_norm, mla_w_uq, mla_kv_norm, mla_w_ukv, mla_w_o, sb_w_qkv, sb_w_o, mem_w_q, mem_w_kv, mem_w_o, ffn_w_in, ffn_conv_w, ffn_conv_b, ffn_w_out, loss_target, m_norm_mix, m_norm_mem_q, m_norm_mem_kv, m_norm_ffn, m_norm_final, m_mla_w_down, m_mla_q_norm, m_mla_w_uq, m_mla_kv_norm, m_mla_w_ukv, m_mla_w_o, m_sb_w_qkv, m_sb_w_o, m_mem_w_q, m_mem_w_kv, m_mem_w_o, m_ffn_w_in, m_ffn_conv_w, m_ffn_conv_b, m_ffn_w_out, v_norm_mix, v_norm_mem_q, v_norm_mem_kv, v_norm_ffn, v_norm_final, v_mla_w_down, v_mla_q_norm, v_mla_w_uq, v_mla_kv_norm, v_mla_w_ukv, v_mla_w_o, v_sb_w_qkv, v_sb_w_o, v_mem_w_q, v_mem_w_kv, v_mem_w_o, v_ffn_w_in, v_ffn_conv_w, v_ffn_conv_b, v_ffn_w_out):
    given = dict(x=x, mem=mem, positions=positions, norm_mix=norm_mix, norm_mem_q=norm_mem_q, norm_mem_kv=norm_mem_kv, norm_ffn=norm_ffn, norm_final=norm_final, mla_w_down=mla_w_down, mla_q_norm=mla_q_norm, mla_w_uq=mla_w_uq, mla_kv_norm=mla_kv_norm, mla_w_ukv=mla_w_ukv, mla_w_o=mla_w_o, sb_w_qkv=sb_w_qkv, sb_w_o=sb_w_o, mem_w_q=mem_w_q, mem_w_kv=mem_w_kv, mem_w_o=mem_w_o, ffn_w_in=ffn_w_in, ffn_conv_w=ffn_conv_w, ffn_conv_b=ffn_conv_b, ffn_w_out=ffn_w_out, loss_target=loss_target, m_norm_mix=m_norm_mix, m_norm_mem_q=m_norm_mem_q, m_norm_mem_kv=m_norm_mem_kv, m_norm_ffn=m_norm_ffn, m_norm_final=m_norm_final, m_mla_w_down=m_mla_w_down, m_mla_q_norm=m_mla_q_norm, m_mla_w_uq=m_mla_w_uq, m_mla_kv_norm=m_mla_kv_norm, m_mla_w_ukv=m_mla_w_ukv, m_mla_w_o=m_mla_w_o, m_sb_w_qkv=m_sb_w_qkv, m_sb_w_o=m_sb_w_o, m_mem_w_q=m_mem_w_q, m_mem_w_kv=m_mem_w_kv, m_mem_w_o=m_mem_w_o, m_ffn_w_in=m_ffn_w_in, m_ffn_conv_w=m_ffn_conv_w, m_ffn_conv_b=m_ffn_conv_b, m_ffn_w_out=m_ffn_w_out, v_norm_mix=v_norm_mix, v_norm_mem_q=v_norm_mem_q, v_norm_mem_kv=v_norm_mem_kv, v_norm_ffn=v_norm_ffn, v_norm_final=v_norm_final, v_mla_w_down=v_mla_w_down, v_mla_q_norm=v_mla_q_norm, v_mla_w_uq=v_mla_w_uq, v_mla_kv_norm=v_mla_kv_norm, v_mla_w_ukv=v_mla_w_ukv, v_mla_w_o=v_mla_w_o, v_sb_w_qkv=v_sb_w_qkv, v_sb_w_o=v_sb_w_o, v_mem_w_q=v_mem_w_q, v_mem_w_kv=v_mem_w_kv, v_mem_w_o=v_mem_w_o, v_ffn_w_in=v_ffn_w_in, v_ffn_conv_w=v_ffn_conv_w, v_ffn_conv_b=v_ffn_conv_b, v_ffn_w_out=v_ffn_w_out)
    weights = {n: given[n] for n in TWIN_WEIGHTS}
    shared = {n: given[n] for n in SHARED_INPUTS}
    per_example = {n: given[n] for n in ['x', 'mem', 'positions']}
    grad_fn = _jax.value_and_grad(_loss, argnums=(0, 1))

    def one_microbatch(ex, loss_target):
        ex = dict(ex)
        diff = ex.pop(TWIN_DIFF_INPUT)
        return grad_fn(weights, diff, {**shared, **ex}, loss_target)

    if N_MICROBATCH == 1:
        loss, (grad_w, grad_x) = one_microbatch(per_example, given["loss_target"])
    else:
        def body(carry, xs):
            loss_sum, grad_sum = carry
            l_k, (gw_k, gx_k) = one_microbatch(xs[0], xs[1])
            with _jax.named_scope("update"):
                return (loss_sum + l_k, _jax.tree.map(_jnp.add, grad_sum, gw_k)), gx_k

        init = (_jnp.zeros((), _jnp.float32), _jax.tree.map(_jnp.zeros_like, weights))
        (loss, grad_w), grad_x = _jax.lax.scan(body, init, (per_example, given["loss_target"]))
    with _jax.named_scope("update"):
        delta_w, new_m, new_v = {}, {}, {}
        for n in TWIN_WEIGHTS:
            delta_w[n], new_m[n], new_v[n] = _adamw(weights[n], grad_w[n], given["m_" + n], given["v_" + n])
    return (loss, grad_x, *[grad_w[n] for n in TWIN_WEIGHTS], *[delta_w[n] for n in TWIN_WEIGHTS],
            *[new_m[n] for n in TWIN_WEIGHTS], *[new_v[n] for n in TWIN_WEIGHTS])
```

```python
import functools
import math

import jax
import jax.numpy as jnp
from jax import lax
from jax.experimental import pallas as pl
from jax.experimental.pallas import tpu as pltpu

F32 = jnp.float32
BF16 = jnp.bfloat16
MESH = pl.DeviceIdType.MESH

EPS = 1e-6
CHUNK_SHIFT = 6
HEAD = 128
ROPE_DIM = 64
ROPE_THETA = 10000.0
N_CHIPS = 4
ADAM_LR, ADAM_B1, ADAM_B2, ADAM_EPS, ADAM_WD, ADAM_STEP = 0.001, 0.9, 0.999, 1e-08, 0.01, 10

VMEM_LIMIT_BYTES = 48 * 1024 * 1024
MM_TM, MM_TN, MM_TK = (512, 256, 128), (1024, 512, 256, 128), (1024, 512, 256, 128)
ROW_TILE = (256, 128, 64, 32, 16, 8)
ATT_T = 512
SB_TQ, SB_TK = 512, 256
MEM_TQ = 512
CONV_TM, CONV_TN = 512, (512, 256, 128)
HALO = 16


def _pick(dim, prefs):
    for p in prefs:
        if dim % p == 0:
            return p
    return dim


def _dot(a, b, ca, cb):
    return lax.dot_general(a, b, (((ca,), (cb,)), ((), ())), preferred_element_type=F32)


def _params(sem):
    return pltpu.CompilerParams(dimension_semantics=sem, vmem_limit_bytes=VMEM_LIMIT_BYTES)


def mm(a, b, *, mode="nn", add=None, out_dtype=BF16, name):
    if mode == "nn":
        (m, k), (k2, n) = a.shape, b.shape
    elif mode == "nt":
        (m, k), (n, k2) = a.shape, b.shape
    else:
        (k, m), (k2, n) = a.shape, b.shape
    assert k == k2, (a.shape, b.shape, mode)
    tm, tn, tk = _pick(m, MM_TM), _pick(n, MM_TN), _pick(k, MM_TK)
    nk = k // tk
    ca, cb = {"nn": (1, 0), "nt": (1, 1), "tn": (0, 0)}[mode]

    def body(a_ref, b_ref, *rest):
        if add is None:
            o_ref, acc_ref = rest
        else:
            add_ref, o_ref, acc_ref = rest
        kk = pl.program_id(2)

        @pl.when(kk == 0)
        def _():
            acc_ref[...] = jnp.zeros_like(acc_ref)

        acc_ref[...] += _dot(a_ref[...].astype(BF16), b_ref[...].astype(BF16), ca, cb)

        @pl.when(kk == nk - 1)
        def _():
            r = acc_ref[...]
            if add is not None:
                r = r + add_ref[...]
            o_ref[...] = r.astype(o_ref.dtype)

    if mode == "tn":
        a_spec = pl.BlockSpec((tk, tm), lambda i, j, kk: (kk, i))
    else:
        a_spec = pl.BlockSpec((tm, tk), lambda i, j, kk: (i, kk))
    if mode == "nt":
        b_spec = pl.BlockSpec((tn, tk), lambda i, j, kk: (j, kk))
    else:
        b_spec = pl.BlockSpec((tk, tn), lambda i, j, kk: (kk, j))
    o_spec = pl.BlockSpec((tm, tn), lambda i, j, kk: (i, j))
    in_specs, args = [a_spec, b_spec], [a, b]
    if add is not None:
        in_specs.append(o_spec)
        args.append(add)
    return pl.pallas_call(
        body, name=name, grid=(m // tm, n // tn, nk), in_specs=in_specs, out_specs=o_spec,
        out_shape=jax.ShapeDtypeStruct((m, n), out_dtype), scratch_shapes=[pltpu.VMEM((tm, tn), F32)],
        compiler_params=_params(("parallel", "parallel", "arbitrary")),
    )(*args)


def rmsnorm_fwd(x, g, *, name):
    m, d = x.shape
    tm = _pick(m, ROW_TILE)

    def body(x_ref, g_ref, o_ref):
        xv = x_ref[...]
        r = lax.rsqrt(jnp.mean(xv * xv, axis=-1, keepdims=True) + EPS)
        o_ref[...] = (xv * r * g_ref[...]).astype(o_ref.dtype)

    return pl.pallas_call(
        body, name=name, grid=(m // tm,),
        in_specs=[pl.BlockSpec((tm, d), lambda i: (i, 0)), pl.BlockSpec((1, d), lambda i: (0, 0))],
        out_specs=pl.BlockSpec((tm, d), lambda i: (i, 0)), out_shape=jax.ShapeDtypeStruct((m, d), BF16),
        compiler_params=_params(("parallel",)),
    )(x, g.reshape(1, d))


def rmsnorm_bwd(dh, x, g, res=None, *, name):
    m, d = x.shape
    tm = _pick(m, ROW_TILE)

    def body(dh_ref, x_ref, g_ref, *rest):
        if res is None:
            dx_ref, dg_ref = rest
        else:
            res_ref, dx_ref, dg_ref = rest

        @pl.when(pl.program_id(0) == 0)
        def _():
            dg_ref[...] = jnp.zeros_like(dg_ref)

        xv = x_ref[...]
        dhv = dh_ref[...].astype(F32)
        r = lax.rsqrt(jnp.mean(xv * xv, axis=-1, keepdims=True) + EPS)
        y = xv * r
        dhg = dhv * g_ref[...]
        dx = r * (dhg - y * jnp.mean(dhg * y, axis=-1, keepdims=True))
        if res is not None:
            dx = dx + res_ref[...]
        dx_ref[...] = dx
        dg_ref[...] += jnp.sum(dhv * y, axis=0, keepdims=True)

    row = pl.BlockSpec((tm, d), lambda i: (i, 0))
    vec = pl.BlockSpec((1, d), lambda i: (0, 0))
    in_specs, args = [row, row, vec], [dh, x, g.reshape(1, d)]
    if res is not None:
        in_specs.append(row)
        args.append(res)
    return pl.pallas_call(
        body, name=name, grid=(m // tm,), in_specs=in_specs, out_specs=[row, vec],
        out_shape=[jax.ShapeDtypeStruct((m, d), F32), jax.ShapeDtypeStruct((1, d), F32)],
        compiler_params=_params(("arbitrary",)),
    )(*args)


def final_loss(x, g, target, *, name):
    m, d = x.shape
    tm = _pick(m, ROW_TILE)

    def body(x_ref, g_ref, t_ref, dx_ref, dg_ref, loss_ref):
        @pl.when(pl.program_id(0) == 0)
        def _():
            dg_ref[...] = jnp.zeros_like(dg_ref)
            loss_ref[...] = jnp.zeros_like(loss_ref)

        xv = x_ref[...]
        gv = g_ref[...]
        r = lax.rsqrt(jnp.mean(xv * xv, axis=-1, keepdims=True) + EPS)
        y = xv * r
        err = y * gv - t_ref[...]
        loss_ref[...] += 0.5 * jnp.sum(jnp.mean(err * err, axis=-1, keepdims=True))
        dy = err * (1.0 / d)
        dyg = dy * gv
        dx_ref[...] = r * (dyg - y * jnp.mean(dyg * y, axis=-1, keepdims=True))
        dg_ref[...] += jnp.sum(dy * y, axis=0, keepdims=True)

    row = pl.BlockSpec((tm, d), lambda i: (i, 0))
    vec = pl.BlockSpec((1, d), lambda i: (0, 0))
    return pl.pallas_call(
        body, name=name, grid=(m // tm,), in_specs=[row, vec, row],
        out_specs=[row, vec, pl.BlockSpec((8, 128), lambda i: (0, 0))],
        out_shape=[jax.ShapeDtypeStruct((m, d), F32), jax.ShapeDtypeStruct((1, d), F32),
                   jax.ShapeDtypeStruct((8, 128), F32)],
        compiler_params=_params(("arbitrary",)),
    )(x, g.reshape(1, d), target)


def rope(xin, cos, sin, *, col0, n, inverse, name):
    s = xin.shape[0]
    tm = _pick(s, ROW_TILE)
    half = ROPE_DIM // 2

    def body(x_ref, c_ref, s_ref, o_ref):
        xv = x_ref[...].astype(F32)
        lane = lax.broadcasted_iota(jnp.int32, xv.shape, 1)
        rot = jnp.where(lane < half, -pltpu.roll(xv, HEAD - half, 1), pltpu.roll(xv, half, 1))
        sv = s_ref[...]
        if inverse:
            sv = -sv
        o_ref[...] = (xv * c_ref[...] + rot * sv).astype(o_ref.dtype)

    tab = pl.BlockSpec((tm, HEAD), lambda i, j: (i, 0))
    return pl.pallas_call(
        body, name=name, grid=(s // tm, n),
        in_specs=[pl.BlockSpec((tm, HEAD), lambda i, j: (i, j + col0)), tab, tab],
        out_specs=pl.BlockSpec((tm, HEAD), lambda i, j: (i, j)),
        out_shape=jax.ShapeDtypeStruct((s, n * HEAD), BF16),
        compiler_params=_params(("parallel", "parallel")),
    )(xin, cos, sin)


def _mla_scores(qn, qr, kn, kr, qi, kj, t, scale):
    sc = (_dot(qn, kn, 1, 1) + _dot(qr, kr, 1, 1)) * scale
    qpos = qi * t + lax.broadcasted_iota(jnp.int32, sc.shape, 0)
    kpos = kj * t + lax.broadcasted_iota(jnp.int32, sc.shape, 1)
    allowed = jnp.right_shift(kpos, CHUNK_SHIFT) <= jnp.right_shift(qpos, CHUNK_SHIFT)
    return jnp.where(allowed, sc, -jnp.inf)


def mla_fwd(qall, qr, kv, kr, heads, *, name):
    s = qr.shape[0]
    t = min(ATT_T, s)
    nq = s // t
    scale = 1.0 / math.sqrt(HEAD + ROPE_DIM)

    def body(qn_ref, qr_ref, kv_ref, kr_ref, o_ref, lse_ref, m_ref, l_ref, acc_ref):
        qi, kj = pl.program_id(1), pl.program_id(2)

        @pl.when(kj == 0)
        def _():
            m_ref[...] = jnp.full_like(m_ref, -jnp.inf)
            l_ref[...] = jnp.zeros_like(l_ref)
            acc_ref[...] = jnp.zeros_like(acc_ref)

        @pl.when(kj <= qi)
        def _():
            sc = _mla_scores(qn_ref[...], qr_ref[...], kv_ref[:, :HEAD], kr_ref[...], qi, kj, t, scale)
            m_prev = m_ref[...]
            m_new = jnp.maximum(m_prev, jnp.max(sc, axis=-1, keepdims=True))
            p = jnp.exp(sc - m_new)
            alpha = jnp.exp(m_prev - m_new)
            l_ref[...] = alpha * l_ref[...] + jnp.sum(p, axis=-1, keepdims=True)
            acc_ref[...] = alpha * acc_ref[...] + _dot(p.astype(BF16), kv_ref[:, HEAD:], 1, 0)
            m_ref[...] = m_new

        @pl.when(kj == nq - 1)
        def _():
            o_ref[...] = (acc_ref[...] / l_ref[...]).astype(o_ref.dtype)
            lse_ref[...] = jnp.broadcast_to(m_ref[...] + jnp.log(l_ref[...]), lse_ref.shape)

    qspec = pl.BlockSpec((t, HEAD), lambda h, qi, kj: (qi, h))
    return pl.pallas_call(
        body, name=name, grid=(heads, nq, nq),
        in_specs=[qspec, qspec,
                  pl.BlockSpec((t, 2 * HEAD), lambda h, qi, kj: (jnp.minimum(kj, qi), h)),
                  pl.BlockSpec((t, HEAD), lambda h, qi, kj: (jnp.minimum(kj, qi), 0))],
        out_specs=[qspec, qspec],
        out_shape=[jax.ShapeDtypeStruct((s, heads * HEAD), BF16), jax.ShapeDtypeStruct((s, heads * HEAD), F32)],
        scratch_shapes=[pltpu.VMEM((t, 1), F32), pltpu.VMEM((t, 1), F32), pltpu.VMEM((t, HEAD), F32)],
        compiler_params=_params(("parallel", "parallel", "arbitrary")),
    )(qall, qr, kv, kr)


def mla_dq(qall, qr, kv, kr, o, do, lse, heads, *, name):
    s = qr.shape[0]
    t = min(ATT_T, s)
    nq = s // t
    scale = 1.0 / math.sqrt(HEAD + ROPE_DIM)

    def body(qn_ref, qr_ref, kv_ref, kr_ref, o_ref, do_ref, lse_ref, dqn_ref, dqr_ref, dl_ref, an_ref, ar_ref):
        qi, kj = pl.program_id(1), pl.program_id(2)

        @pl.when(kj == 0)
        def _():
            dl_ref[...] = jnp.sum(do_ref[...].astype(F32) * o_ref[...].astype(F32), axis=-1, keepdims=True)
            an_ref[...] = jnp.zeros_like(an_ref)
            ar_ref[...] = jnp.zeros_like(ar_ref)

        @pl.when(kj <= qi)
        def _():
            kn = kv_ref[:, :HEAD]
            sc = _mla_scores(qn_ref[...], qr_ref[...], kn, kr_ref[...], qi, kj, t, scale)
            p = jnp.exp(sc - lse_ref[:, :1])
            dp = _dot(do_ref[...], kv_ref[:, HEAD:], 1, 1)
            ds = (p * (dp - dl_ref[...]) * scale).astype(BF16)
            an_ref[...] += _dot(ds, kn, 1, 0)
            ar_ref[...] += _dot(ds, kr_ref[...], 1, 0)

        @pl.when(kj == nq - 1)
        def _():
            dqn_ref[...] = an_ref[...].astype(dqn_ref.dtype)
            dqr_ref[...] = ar_ref[...].astype(dqr_ref.dtype)

    qspec = pl.BlockSpec((t, HEAD), lambda h, qi, kj: (qi, h))
    return pl.pallas_call(
        body, name=name, grid=(heads, nq, nq),
        in_specs=[qspec, qspec,
                  pl.BlockSpec((t, 2 * HEAD), lambda h, qi, kj: (jnp.minimum(kj, qi), h)),
                  pl.BlockSpec((t, HEAD), lambda h, qi, kj: (jnp.minimum(kj, qi), 0)),
                  qspec, qspec, qspec],
        out_specs=[qspec, qspec],
        out_shape=[jax.ShapeDtypeStruct((s, heads * HEAD), BF16), jax.ShapeDtypeStruct((s, heads * HEAD), BF16)],
        scratch_shapes=[pltpu.VMEM((t, 1), F32), pltpu.VMEM((t, HEAD), F32), pltpu.VMEM((t, HEAD), F32)],
        compiler_params=_params(("parallel", "parallel", "arbitrary")),
    )(qall, qr, kv, kr, o, do, lse)


def mla_dkv(qall, qr, kv, kr, o, do, lse, heads, *, name):
    s = qr.shape[0]
    t = min(ATT_T, s)
    nq = s // t
    scale = 1.0 / math.sqrt(HEAD + ROPE_DIM)

    def body(qn_ref, qr_ref, kv_ref, kr_ref, o_ref, do_ref, lse_ref, dkv_ref, dkr_ref, akn_ref, av_ref):
        kj, h, qi = pl.program_id(0), pl.program_id(1), pl.program_id(2)

        @pl.when(qi == 0)
        def _():
            akn_ref[...] = jnp.zeros_like(akn_ref)
            av_ref[...] = jnp.zeros_like(av_ref)

        @pl.when((qi == 0) & (h == 0))
        def _():
            dkr_ref[...] = jnp.zeros_like(dkr_ref)

        @pl.when(qi >= kj)
        def _():
            qn, qrv, dov = qn_ref[...], qr_ref[...], do_ref[...]
            sc = _mla_scores(qn, qrv, kv_ref[:, :HEAD], kr_ref[...], qi, kj, t, scale)
            p = jnp.exp(sc - lse_ref[:, :1])
            delta = jnp.sum(dov.astype(F32) * o_ref[...].astype(F32), axis=-1, keepdims=True)
            dp = _dot(dov, kv_ref[:, HEAD:], 1, 1)
            ds = (p * (dp - delta) * scale).astype(BF16)
            av_ref[...] += _dot(p.astype(BF16), dov, 0, 0)
            akn_ref[...] += _dot(ds, qn, 0, 0)
            dkr_ref[...] += _dot(ds, qrv, 0, 0)

        @pl.when(qi == nq - 1)
        def _():
            dkv_ref[:, :HEAD] = akn_ref[...].astype(dkv_ref.dtype)
            dkv_ref[:, HEAD:] = av_ref[...].astype(dkv_ref.dtype)

    qspec = pl.BlockSpec((t, HEAD), lambda kj, h, qi: (jnp.maximum(qi, kj), h))
    kvspec = pl.BlockSpec((t, 2 * HEAD), lambda kj, h, qi: (kj, h))
    krspec = pl.BlockSpec((t, HEAD), lambda kj, h, qi: (kj, 0))
    return pl.pallas_call(
        body, name=name, grid=(nq, heads, nq),
        in_specs=[qspec, qspec, kvspec, krspec, qspec, qspec, qspec],
        out_specs=[kvspec, krspec],
        out_shape=[jax.ShapeDtypeStruct((s, heads * 2 * HEAD), BF16), jax.ShapeDtypeStruct((s, HEAD), F32)],
        scratch_shapes=[pltpu.VMEM((t, HEAD), F32), pltpu.VMEM((t, HEAD), F32)],
        compiler_params=_params(("parallel", "arbitrary", "arbitrary")),
    )(qall, qr, kv, kr, o, do, lse)


def _split_dot(val, tri, cb):
    hi = val.astype(BF16)
    lo = (val - hi.astype(F32)).astype(BF16)
    return _dot(hi, tri, 1, cb) + _dot(lo, tri, 1, cb)


def _sb_logs(q, k, qi, kj, tq, tk, scale):
    z = _dot(q, k, 1, 1) * scale
    qpos = qi * tq + lax.broadcasted_iota(jnp.int32, z.shape, 0)
    kpos = kj * tk + lax.broadcasted_iota(jnp.int32, z.shape, 1)
    strict = kpos < qpos
    sp = jnp.log(1.0 + jnp.exp(-jnp.abs(z)))
    ls = jnp.minimum(z, 0.0) - sp
    lk = jnp.where(strict, jnp.minimum(-z, 0.0) - sp, 0.0)
    return strict, ls, lk


def _lane_pick(blk, idx):
    lane = lax.broadcasted_iota(jnp.int32, blk.shape, 1)
    return jnp.sum(jnp.where(lane == idx, blk, 0.0), axis=-1, keepdims=True)


def _lane_put(blk, idx, col):
    lane = lax.broadcasted_iota(jnp.int32, blk.shape, 1)
    return jnp.where(lane == idx, col, blk)


def _sb_tiles(s):
    tq = min(SB_TQ, s)
    tk = min(SB_TK, tq)
    assert s // tk <= HEAD
    return tq, tk


def sb_fwd(qkv, tri, heads, *, name):
    s = qkv.shape[0]
    tq, tk = _sb_tiles(s)
    nq, nkb, ratio = s // tq, s // tk, tq // tk
    scale = 1.0 / math.sqrt(HEAD)

    def body(q_ref, k_ref, v_ref, tri_ref, o_ref, c_ref, carry_ref, acc_ref):
        qi, st = pl.program_id(1), pl.program_id(2)
        kj = (qi + 1) * ratio - 1 - st

        @pl.when(st == 0)
        def _():
            carry_ref[...] = jnp.zeros_like(carry_ref)
            acc_ref[...] = jnp.zeros_like(acc_ref)
            c_ref[...] = jnp.zeros_like(c_ref)

        @pl.when(kj >= 0)
        def _():
            strict, ls, lk = _sb_logs(q_ref[...], k_ref[...], qi, kj, tq, tk, scale)
            carry = carry_ref[...]
            later = _split_dot(lk, tri_ref[...], 0) + carry
            a = jnp.where(strict, jnp.exp(ls + later), 0.0)
            acc_ref[...] += _dot(a.astype(BF16), v_ref[...], 1, 0)
            c_ref[...] = _lane_put(c_ref[...], kj, carry)
            carry_ref[...] = carry + jnp.sum(lk, axis=-1, keepdims=True)

        @pl.when(st == nkb - 1)
        def _():
            o_ref[...] = acc_ref[...].astype(o_ref.dtype)

    def kidx(qi, st):
        return jnp.maximum((qi + 1) * ratio - 1 - st, 0)

    qspec = pl.BlockSpec((tq, HEAD), lambda h, qi, st: (qi, h))
    return pl.pallas_call(
        body, name=name, grid=(heads, nq, nkb),
        in_specs=[qspec,
                  pl.BlockSpec((tk, HEAD), lambda h, qi, st: (kidx(qi, st), heads + h)),
                  pl.BlockSpec((tk, HEAD), lambda h, qi, st: (kidx(qi, st), 2 * heads + h)),
                  pl.BlockSpec((tk, tk), lambda h, qi, st: (0, 0))],
        out_specs=[qspec, qspec],
        out_shape=[jax.ShapeDtypeStruct((s, heads * HEAD), BF16), jax.ShapeDtypeStruct((s, heads * HEAD), F32)],
        scratch_shapes=[pltpu.VMEM((tq, 1), F32), pltpu.VMEM((tq, HEAD), F32)],
        compiler_params=_params(("parallel", "parallel", "arbitrary")),
    )(qkv, qkv, qkv, tri)


def _sb_grads(q, k, v, do, tri, cblk, dprev, qi, kj, tq, tk, scale):
    strict, ls, lk = _sb_logs(q, k, qi, kj, tq, tk, scale)
    later = _split_dot(lk, tri, 0) + _lane_pick(cblk, kj)
    a = jnp.where(strict, jnp.exp(ls + later), 0.0)
    g = _dot(do, v, 1, 1) * a
    before = _split_dot(g, tri, 1) + dprev
    beta = jnp.exp(ls)
    dz = jnp.where(strict, g * (1.0 - beta) - before * beta, 0.0)
    return a, dz, jnp.sum(g, axis=-1, keepdims=True)


def sb_dq(qkv, do, cmat, tri, heads, *, name):
    s = qkv.shape[0]
    tq, tk = _sb_tiles(s)
    nq, nkb, ratio = s // tq, s // tk, tq // tk
    scale = 1.0 / math.sqrt(HEAD)

    def body(q_ref, k_ref, v_ref, do_ref, c_ref, tri_ref, dq_ref, d_ref, carry_ref, acc_ref):
        qi, kj = pl.program_id(1), pl.program_id(2)

        @pl.when(kj == 0)
        def _():
            carry_ref[...] = jnp.zeros_like(carry_ref)
            acc_ref[...] = jnp.zeros_like(acc_ref)
            d_ref[...] = jnp.zeros_like(d_ref)

        @pl.when(kj < (qi + 1) * ratio)
        def _():
            carry = carry_ref[...]
            kblk = k_ref[...]
            _, dz, gsum = _sb_grads(q_ref[...], kblk, v_ref[...], do_ref[...], tri_ref[...], c_ref[...], carry,
                                    qi, kj, tq, tk, scale)
            acc_ref[...] += _dot(dz.astype(BF16), kblk, 1, 0) * scale
            d_ref[...] = _lane_put(d_ref[...], kj, carry)
            carry_ref[...] = carry + gsum

        @pl.when(kj == nkb - 1)
        def _():
            dq_ref[...] = acc_ref[...].astype(dq_ref.dtype)

    def kidx(qi, kj):
        return jnp.minimum(kj, (qi + 1) * ratio - 1)

    qspec = pl.BlockSpec((tq, HEAD), lambda h, qi, kj: (qi, h))
    return pl.pallas_call(
        body, name=name, grid=(heads, nq, nkb),
        in_specs=[qspec,
                  pl.BlockSpec((tk, HEAD), lambda h, qi, kj: (kidx(qi, kj), heads + h)),
                  pl.BlockSpec((tk, HEAD), lambda h, qi, kj: (kidx(qi, kj), 2 * heads + h)),
                  qspec, qspec, pl.BlockSpec((tk, tk), lambda h, qi, kj: (0, 0))],
        out_specs=[qspec, qspec],
        out_shape=[jax.ShapeDtypeStruct((s, heads * HEAD), BF16), jax.ShapeDtypeStruct((s, heads * HEAD), F32)],
        scratch_shapes=[pltpu.VMEM((tq, 1), F32), pltpu.VMEM((tq, HEAD), F32)],
        compiler_params=_params(("parallel", "parallel", "arbitrary")),
    )(qkv, qkv, qkv, do, cmat, tri)


def sb_dkv(qkv, do, cmat, dmat, tri, heads, *, name):
    s = qkv.shape[0]
    tq, tk = _sb_tiles(s)
    nq, nkb, ratio = s // tq, s // tk, tq // tk
    scale = 1.0 / math.sqrt(HEAD)

    def body(q_ref, k_ref, v_ref, do_ref, c_ref, d_ref, tri_ref, dk_ref, dv_ref, ak_ref, av_ref):
        kj, qi = pl.program_id(0), pl.program_id(2)

        @pl.when(qi == 0)
        def _():
            ak_ref[...] = jnp.zeros_like(ak_ref)
            av_ref[...] = jnp.zeros_like(av_ref)

        @pl.when(qi >= kj // ratio)
        def _():
            qv, dov = q_ref[...], do_ref[...]
            a, dz, _ = _sb_grads(qv, k_ref[...], v_ref[...], dov, tri_ref[...], c_ref[...],
                                 _lane_pick(d_ref[...], kj), qi, kj, tq, tk, scale)
            av_ref[...] += _dot(a.astype(BF16), dov, 0, 0)
            ak_ref[...] += _dot(dz.astype(BF16), qv, 0, 0) * scale

        @pl.when(qi == nq - 1)
        def _():
            dk_ref[...] = ak_ref[...].astype(dk_ref.dtype)
            dv_ref[...] = av_ref[...].astype(dv_ref.dtype)

    qspec = pl.BlockSpec((tq, HEAD), lambda kj, h, qi: (jnp.maximum(qi, kj // ratio), h))
    ospec = pl.BlockSpec((tk, HEAD), lambda kj, h, qi: (kj, h))
    return pl.pallas_call(
        body, name=name, grid=(nkb, heads, nq),
        in_specs=[qspec,
                  pl.BlockSpec((tk, HEAD), lambda kj, h, qi: (kj, heads + h)),
                  pl.BlockSpec((tk, HEAD), lambda kj, h, qi: (kj, 2 * heads + h)),
                  qspec, qspec, qspec, pl.BlockSpec((tk, tk), lambda kj, h, qi: (0, 0))],
        out_specs=[ospec, ospec],
        out_shape=[jax.ShapeDtypeStruct((s, heads * HEAD), BF16), jax.ShapeDtypeStruct((s, heads * HEAD), BF16)],
        scratch_shapes=[pltpu.VMEM((tk, HEAD), F32), pltpu.VMEM((tk, HEAD), F32)],
        compiler_params=_params(("parallel", "parallel", "arbitrary")),
    )(qkv, qkv, qkv, do, cmat, dmat, tri)


def _mem_probs(q, k, scale):
    sc = _dot(q, k, 1, 1) * scale
    e = jnp.exp(sc - jnp.max(sc, axis=-1, keepdims=True))
    return e / jnp.sum(e, axis=-1, keepdims=True)


def mem_fwd(q, kvm, heads, *, name):
    s, nm = q.shape[0], kvm.shape[0]
    tq = min(MEM_TQ, s)
    scale = 1.0 / math.sqrt(HEAD)

    def body(q_ref, k_ref, v_ref, o_ref):
        p = _mem_probs(q_ref[...], k_ref[...], scale)
        o_ref[...] = _dot(p.astype(BF16), v_ref[...], 1, 0).astype(o_ref.dtype)

    qspec = pl.BlockSpec((tq, HEAD), lambda h, qi: (qi, h))
    return pl.pallas_call(
        body, name=name, grid=(heads, s // tq),
        in_specs=[qspec, pl.BlockSpec((nm, HEAD), lambda h, qi: (0, h)),
                  pl.BlockSpec((nm, HEAD), lambda h, qi: (0, heads + h))],
        out_specs=qspec, out_shape=jax.ShapeDtypeStruct((s, heads * HEAD), BF16),
        compiler_params=_params(("parallel", "parallel")),
    )(q, kvm, kvm)


def mem_bwd(q, kvm, do, heads, *, name):
    s, nm = q.shape[0], kvm.shape[0]
    tq = min(MEM_TQ, s)
    scale = 1.0 / math.sqrt(HEAD)

    def body(q_ref, k_ref, v_ref, do_ref, dq_ref, dk_ref, dv_ref):
        @pl.when(pl.program_id(1) == 0)
        def _():
            dk_ref[...] = jnp.zeros_like(dk_ref)
            dv_ref[...] = jnp.zeros_like(dv_ref)

        qv, kvv, dov = q_ref[...], k_ref[...], do_ref[...]
        p = _mem_probs(qv, kvv, scale)
        dp = _dot(dov, v_ref[...], 1, 1)
        ds = (p * (dp - jnp.sum(dp * p, axis=-1, keepdims=True)) * scale).astype(BF16)
        dq_ref[...] = _dot(ds, kvv, 1, 0).astype(dq_ref.dtype)
        dk_ref[...] += _dot(ds, qv, 0, 0)
        dv_ref[...] += _dot(p.astype(BF16), dov, 0, 0)

    qspec = pl.BlockSpec((tq, HEAD), lambda h, qi: (qi, h))
    kspec = pl.BlockSpec((nm, HEAD), lambda h, qi: (0, h))
    return pl.pallas_call(
        body, name=name, grid=(heads, s // tq),
        in_specs=[qspec, kspec, pl.BlockSpec((nm, HEAD), lambda h, qi: (0, heads + h)), qspec],
        out_specs=[qspec, kspec, kspec],
        out_shape=[jax.ShapeDtypeStruct((s, heads * HEAD), BF16), jax.ShapeDtypeStruct((nm, heads * HEAD), F32),
                   jax.ShapeDtypeStruct((nm, heads * HEAD), F32)],
        compiler_params=_params(("parallel", "arbitrary")),
    )(q, kvm, kvm, do)


def _conv3(u, halo, w, b):
    tm = u.shape[0]
    row = lax.broadcasted_iota(jnp.int32, u.shape, 0)
    h1, h2 = halo[HALO - 1:HALO, :], halo[HALO - 2:HALO - 1, :]
    u1 = jnp.where(row == 0, h1, pltpu.roll(u, 1, 0))
    u2 = jnp.where(row == 0, h2, jnp.where(row == 1, h1, pltpu.roll(u, 2 % tm, 0)))
    return b + w[0:1, :] * u2 + w[1:2, :] * u1 + w[2:3, :] * u, u1, u2


def _conv_specs(s, f):
    tm, tn = min(CONV_TM, s), _pick(f, CONV_TN)
    return tm, tn, s // tm, f // tn


def _silu_parts(g):
    sg = 1.0 / (1.0 + jnp.exp(-g))
    return g * sg, sg


def conv_gate_fwd(u, cw, cb, *, name):
    s, f = u.shape[0], u.shape[1] // 2
    tm, tn, ni, nj = _conv_specs(s, f)
    hb = tm // HALO

    def body(ug_ref, uu_ref, hg_ref, hu_ref, wg_ref, wu_ref, bg_ref, bu_ref, a_ref):
        keep = (pl.program_id(1) > 0).astype(F32)
        gate, _, _ = _conv3(ug_ref[...].astype(F32), hg_ref[...].astype(F32) * keep, wg_ref[...], bg_ref[...])
        up, _, _ = _conv3(uu_ref[...].astype(F32), hu_ref[...].astype(F32) * keep, wu_ref[...], bu_ref[...])
        a_ref[...] = (_silu_parts(gate)[0] * up).astype(a_ref.dtype)

    def main(off):
        return pl.BlockSpec((tm, tn), lambda j, i: (i, j + off))

    def halo(off):
        return pl.BlockSpec((HALO, tn), lambda j, i: (jnp.maximum(i * hb - 1, 0), j + off))

    def par(rows, off):
        return pl.BlockSpec((rows, tn), lambda j, i: (0, j + off))

    return pl.pallas_call(
        body, name=name, grid=(nj, ni),
        in_specs=[main(0), main(nj), halo(0), halo(nj), par(3, 0), par(3, nj), par(1, 0), par(1, nj)],
        out_specs=main(0), out_shape=jax.ShapeDtypeStruct((s, f), BF16),
        compiler_params=_params(("parallel", "parallel")),
    )(u, u, u, u, cw, cw, cb, cb)


def conv_gate_bwd(u, da, cw, cb, *, name):
    s, f = u.shape[0], u.shape[1] // 2
    tm, tn, ni, nj = _conv_specs(s, f)
    hb = tm // HALO

    def body(ug_ref, uu_ref, hg_ref, hu_ref, da_ref, wg_ref, wu_ref, bg_ref, bu_ref, dg_ref, du_ref, pg_ref, pu_ref):
        @pl.when(pl.program_id(1) == 0)
        def _():
            pg_ref[...] = jnp.zeros_like(pg_ref)
            pu_ref[...] = jnp.zeros_like(pu_ref)

        keep = (pl.program_id(1) > 0).astype(F32)
        ug, uu = ug_ref[...].astype(F32), uu_ref[...].astype(F32)
        gate, ug1, ug2 = _conv3(ug, hg_ref[...].astype(F32) * keep, wg_ref[...], bg_ref[...])
        up, uu1, uu2 = _conv3(uu, hu_ref[...].astype(F32) * keep, wu_ref[...], bu_ref[...])
        act, sg = _silu_parts(gate)
        dav = da_ref[...].astype(F32)
        d_gate = dav * up * (sg * (1.0 + gate * (1.0 - sg)))
        d_up = dav * act
        dg_ref[...] = d_gate.astype(dg_ref.dtype)
        du_ref[...] = d_up.astype(du_ref.dtype)
        for p_ref, dc, taps in ((pg_ref, d_gate, (ug2, ug1, ug)), (pu_ref, d_up, (uu2, uu1, uu))):
            for r, tap in enumerate(taps):
                p_ref[r:r + 1, :] += jnp.sum(dc * tap, axis=0, keepdims=True)
            p_ref[3:4, :] += jnp.sum(dc, axis=0, keepdims=True)

    def main(off):
        return pl.BlockSpec((tm, tn), lambda j, i: (i, j + off))

    def halo(off):
        return pl.BlockSpec((HALO, tn), lambda j, i: (jnp.maximum(i * hb - 1, 0), j + off))

    def par(rows, off):
        return pl.BlockSpec((rows, tn), lambda j, i: (0, j + off))

    dg, du, pg, pu = pl.pallas_call(
        body, name=name, grid=(nj, ni),
        in_specs=[main(0), main(nj), halo(0), halo(nj), main(0), par(3, 0), par(3, nj), par(1, 0), par(1, nj)],
        out_specs=[main(0), main(0), par(8, 0), par(8, 0)],
        out_shape=[jax.ShapeDtypeStruct((s, f), BF16), jax.ShapeDtypeStruct((s, f), BF16),
                   jax.ShapeDtypeStruct((8, f), F32), jax.ShapeDtypeStruct((8, f), F32)],
        compiler_params=_params(("parallel", "arbitrary")),
    )(u, u, u, u, da, cw, cw, cb, cb)
    return dg, du, jnp.concatenate([pg, pu], axis=1)


def conv_transpose(dc, w, *, name):
    s, f = dc.shape
    tm, tn, ni, nj = _conv_specs(s, f)
    hb = tm // HALO

    def body(d_ref, h_ref, w_ref, o_ref):
        d = d_ref[...].astype(F32)
        halo = h_ref[...].astype(F32) * (pl.program_id(1) < ni - 1).astype(F32)
        row = lax.broadcasted_iota(jnp.int32, d.shape, 0)
        n0, n1 = halo[0:1, :], halo[1:2, :]
        d1 = jnp.where(row == tm - 1, n0, pltpu.roll(d, tm - 1, 0))
        d2 = jnp.where(row == tm - 2, n0, jnp.where(row == tm - 1, n1, pltpu.roll(d, tm - 2, 0)))
        wv = w_ref[...]
        o_ref[...] = (wv[2:3, :] * d + wv[1:2, :] * d1 + wv[0:1, :] * d2).astype(o_ref.dtype)

    main = pl.BlockSpec((tm, tn), lambda j, i: (i, j))
    return pl.pallas_call(
        body, name=name, grid=(nj, ni),
        in_specs=[main, pl.BlockSpec((HALO, tn), lambda j, i: (jnp.minimum((i + 1) * hb, s // HALO - 1), j)),
                  pl.BlockSpec((3, tn), lambda j, i: (0, j))],
        out_specs=main, out_shape=jax.ShapeDtypeStruct((s, f), BF16),
        compiler_params=_params(("parallel", "parallel")),
    )(dc, dc, w)


def _tile2d(r, c):
    return _pick(r, ROW_TILE), _pick(c, (2048, 1024, 512, 256, 128))


def sum_slots(buf, *, name):
    n, shape = buf.shape[0], buf.shape[1:]
    r, c = math.prod(shape[:-1]), shape[-1]
    tm, tn = _tile2d(r, c)

    def body(b_ref, o_ref):
        acc = b_ref[0]
        for k in range(1, n):
            acc = acc + b_ref[k]
        o_ref[...] = acc

    out = pl.pallas_call(
        body, name=name, grid=(r // tm, c // tn),
        in_specs=[pl.BlockSpec((n, tm, tn), lambda i, j: (0, i, j))],
        out_specs=pl.BlockSpec((tm, tn), lambda i, j: (i, j)), out_shape=jax.ShapeDtypeStruct((r, c), F32),
        compiler_params=_params(("parallel", "parallel")),
    )(buf.reshape(n, r, c))
    return out.reshape(shape)


def adamw(w, g, m, v, *, name):
    shape = w.shape
    c = shape[-1]
    r = math.prod(shape[:-1]) if len(shape) > 1 else 1
    tm, tn = _tile2d(r, c)
    bc1, bc2 = 1.0 - ADAM_B1 ** ADAM_STEP, 1.0 - ADAM_B2 ** ADAM_STEP

    def body(w_ref, g_ref, m_ref, v_ref, d_ref, mo_ref, vo_ref):
        gv = g_ref[...]
        mn = ADAM_B1 * m_ref[...] + (1.0 - ADAM_B1) * gv
        vn = ADAM_B2 * v_ref[...] + (1.0 - ADAM_B2) * (gv * gv)
        d_ref[...] = -ADAM_LR * ((mn / bc1) / (jnp.sqrt(vn / bc2) + ADAM_EPS) + ADAM_WD * w_ref[...])
        mo_ref[...] = mn
        vo_ref[...] = vn

    spec = pl.BlockSpec((tm, tn), lambda i, j: (i, j))
    outs = pl.pallas_call(
        body, name=name, grid=(r // tm, c // tn), in_specs=[spec] * 4, out_specs=[spec] * 3,
        out_shape=[jax.ShapeDtypeStruct((r, c), F32)] * 3, compiler_params=_params(("parallel", "parallel")),
    )(*(t.reshape(r, c) for t in (w, g, m, v)))
    return tuple(o.reshape(shape) for o in outs)


def _comm(name, inputs, out_shapes, aliases, plan):
    n_in, n_out = len(inputs), len(out_shapes)
    probe = plan([None] * n_in, [None] * n_out, 0, 0, 0, count_only=True)

    def body(*refs):
        in_refs, out_refs = refs[:n_in], refs[n_in:n_in + n_out]
        send_sems, recv_sems = refs[n_in + n_out:]
        x, y, c = lax.axis_index("x"), lax.axis_index("y"), lax.axis_index("c")
        xfers = plan(in_refs, out_refs, x, y, c, count_only=False)
        started = []
        for n, (src, dst, peer, _) in enumerate(xfers):
            if peer is None:
                cp = pltpu.make_async_copy(src, dst, send_sems.at[n])
            else:
                cp = pltpu.make_async_remote_copy(src_ref=src, dst_ref=dst, send_sem=send_sems.at[n],
                                                  recv_sem=recv_sems.at[n], device_id=peer, device_id_type=MESH)
            cp.start()
            started.append(cp)
        for n, (src, dst, peer, got) in enumerate(xfers):
            if peer is None:
                started[n].wait()
            else:
                pltpu.make_async_remote_copy(src_ref=got, dst_ref=got, send_sem=send_sems.at[n],
                                             recv_sem=recv_sems.at[n], device_id=peer,
                                             device_id_type=MESH).wait_recv()
                started[n].wait_send()

    hbm = pl.BlockSpec(memory_space=pl.ANY)
    return pl.pallas_call(
        body, name=name, in_specs=[hbm] * n_in, out_specs=[hbm] * n_out, out_shape=out_shapes,
        input_output_aliases=aliases,
        scratch_shapes=[pltpu.SemaphoreType.DMA((probe,)), pltpu.SemaphoreType.DMA((probe,))],
    )(*inputs)


def _other_chips(x, y):
    return [(1 - x, y), (x, 1 - y), (1 - x, 1 - y)]


def gather_weights(shards, axes):
    def out_shape(sh, ax):
        l, r, c = sh.shape
        return jax.ShapeDtypeStruct((l, N_CHIPS, 2, r // 2, c) if ax == 0 else (l, 2, r // 2, N_CHIPS * c), BF16)

    def piece(ref, ax, chip, half, width):
        if ax == 0:
            return ref.at[:, chip, half]
        return ref.at[:, half, :, pl.ds(chip * width, width)]

    def plan1(in_refs, out_refs, x, y, c, count_only):
        if count_only:
            return 4 * len(shards)
        me, xfers = 2 * x + y, []
        for sh, ax, src, out in zip(shards, axes, in_refs, out_refs):
            rows, width = sh.shape[1] // 2, sh.shape[2]
            mine = src.at[:, pl.ds(c * rows, rows), :]
            xfers.append((mine, piece(out, ax, me, c, width), None, None))
            for cx, cy in _other_chips(x, y):
                xfers.append((mine, piece(out, ax, me, c, width), (cx, cy, c), piece(out, ax, 2 * cx + cy, c, width)))
        return xfers

    def half_of(ref, ax, half):
        return ref.at[:, :, half] if ax == 0 else ref.at[:, half]

    def plan2(in_refs, out_refs, x, y, c, count_only):
        if count_only:
            return len(shards)
        return [(half_of(out, ax, c), half_of(out, ax, c), (x, y, 1 - c), half_of(out, ax, 1 - c))
                for ax, out in zip(axes, out_refs)]

    shapes = [out_shape(sh, ax) for sh, ax in zip(shards, axes)]
    part = _comm("gather_chips", list(shards), shapes, {}, plan1)
    full = _comm("gather_cores", list(part), shapes, {n: n for n in range(len(shards))}, plan2)
    return [f.reshape(sh.shape[0], N_CHIPS * sh.shape[1], sh.shape[2]) if ax == 0
            else f.reshape(sh.shape[0], sh.shape[1], N_CHIPS * sh.shape[2])
            for f, sh, ax in zip(full, shards, axes)]


def reduce_scatter_grads(grads, axes):
    n_w = len(grads)
    dims = []
    for layers, ax in zip(grads, axes):
        r, c = layers[0].shape
        dims.append((len(layers), r // N_CHIPS // 2, c) if ax == 0 else (len(layers), r // 2, c // N_CHIPS))
    flat = [g.reshape(N_CHIPS, 2, d[1], d[2]) if ax == 0 else g.reshape(2, d[1], g.shape[1])
            for layers, ax, d in zip(grads, axes, dims) for g in layers]

    def half_shape(ax, d):
        return (d[0], N_CHIPS, d[1], d[2]) if ax == 0 else (d[0], d[1], N_CHIPS * d[2])

    def plan_a(in_refs, out_refs, x, y, c, count_only):
        if count_only:
            return 2 * len(flat)
        xfers, n = [], 0
        for w, (ax, d) in enumerate(zip(axes, dims)):
            for l in range(d[0]):
                g = in_refs[n]
                n += 1
                take = (lambda h, g=g: g.at[:, h]) if ax == 0 else (lambda h, g=g: g.at[h])
                xfers.append((take(c), out_refs[w].at[c, l], None, None))
                xfers.append((take(1 - c), out_refs[w].at[c, l], (x, y, 1 - c), out_refs[w].at[1 - c, l]))
        return xfers

    pair = _comm("reduce_cores", flat, [jax.ShapeDtypeStruct((2,) + half_shape(ax, d), F32) for ax, d in zip(axes, dims)],
                 {}, plan_a)
    chip_sum = [sum_slots(p, name="sum_cores") for p in pair]

    def block(ref, ax, chip, width):
        return ref.at[:, chip] if ax == 0 else ref.at[:, :, pl.ds(chip * width, width)]

    def plan_b(in_refs, out_refs, x, y, c, count_only):
        if count_only:
            return 4 * n_w
        me, xfers = 2 * x + y, []
        for ax, d, src, out in zip(axes, dims, in_refs, out_refs):
            xfers.append((block(src, ax, me, d[2]), out.at[me], None, None))
            for cx, cy in _other_chips(x, y):
                xfers.append((block(src, ax, 2 * cx + cy, d[2]), out.at[me], (cx, cy, c), out.at[2 * cx + cy]))
        return xfers

    quad = _comm("reduce_chips", chip_sum, [jax.ShapeDtypeStruct((N_CHIPS,) + d, F32) for d in dims], {}, plan_b)
    mine = [sum_slots(q, name="sum_chips") for q in quad]

    def plan_c(in_refs, out_refs, x, y, c, count_only):
        if count_only:
            return 2 * n_w
        xfers = []
        for src, out in zip(in_refs, out_refs):
            xfers.append((src, out.at[:, c], None, None))
            xfers.append((src, out.at[:, c], (x, y, 1 - c), out.at[:, 1 - c]))
        return xfers

    both = _comm("share_cores", mine, [jax.ShapeDtypeStruct((d[0], 2, d[1], d[2]), F32) for d in dims], {}, plan_c)
    return [b.reshape(d[0], 2 * d[1], d[2]) for b, d in zip(both, dims)]


def gather_slabs(vec, *, name):
    def plan(in_refs, out_refs, x, y, c, count_only):
        if count_only:
            return 8
        me = 4 * x + 2 * y + c
        xfers = [(in_refs[0], out_refs[0].at[me], None, None)]
        for k in range(1, 8):
            px, py, pc = x ^ (k >> 2), y ^ ((k >> 1) & 1), c ^ (k & 1)
            xfers.append((in_refs[0], out_refs[0].at[me], (px, py, pc), out_refs[0].at[4 * px + 2 * py + pc]))
        return xfers

    return _comm(name, [vec], [jax.ShapeDtypeStruct((8,) + vec.shape, F32)], {}, plan)[0]


def allreduce_small(vec):
    return sum_slots(gather_slabs(vec, name="gather_small"), name="sum_small")


def gather_conv_w(block):
    l, taps, c = block.shape
    flat = block.reshape(-1)
    slabs = gather_slabs(jnp.pad(flat, (0, -flat.size % 1024)).reshape(-1, 128), name="gather_conv_w")
    per_chip = slabs[0::2].reshape(N_CHIPS, -1)[:, :flat.size].reshape(N_CHIPS, l, taps, c)
    return per_chip.transpose(1, 2, 0, 3).reshape(l, taps, N_CHIPS * c)


def _rope_tables(positions):
    inv_freq = ROPE_THETA ** (-jnp.arange(0, ROPE_DIM, 2, dtype=F32) / ROPE_DIM)
    ang = positions.astype(F32)[:, None] * inv_freq
    ang = jnp.concatenate([ang, ang], axis=-1)
    pad = ((0, 0), (0, HEAD - ROPE_DIM))
    return jnp.pad(jnp.cos(ang), pad), jnp.pad(jnp.sin(ang), pad)


def _uq_layout(w, heads):
    ql = w.shape[0]
    w = w.reshape(ql, heads, HEAD + ROPE_DIM)
    rot = jnp.pad(w[:, :, HEAD:], ((0, 0), (0, 0), (0, HEAD - ROPE_DIM)))
    return jnp.concatenate([w[:, :, :HEAD].reshape(ql, heads * HEAD), rot.reshape(ql, heads * HEAD)], axis=1)


def _uq_layout_inv(dw, heads):
    ql = dw.shape[0]
    nope = dw[:, :heads * HEAD].reshape(ql, heads, HEAD)
    rot = dw[:, heads * HEAD:].reshape(ql, heads, HEAD)[:, :, :ROPE_DIM]
    return jnp.concatenate([nope, rot], axis=-1).reshape(ql, heads * (HEAD + ROPE_DIM))


def kernel(x, mem, positions, norm_mix, norm_mem_q, norm_mem_kv, norm_ffn, norm_final, mla_w_down, mla_q_norm, mla_w_uq, mla_kv_norm, mla_w_ukv, mla_w_o, sb_w_qkv, sb_w_o, mem_w_q, mem_w_kv, mem_w_o, ffn_w_in, ffn_conv_w, ffn_conv_b, ffn_w_out, loss_target, m_norm_mix, m_norm_mem_q, m_norm_mem_kv, m_norm_ffn, m_norm_final, m_mla_w_down, m_mla_q_norm, m_mla_w_uq, m_mla_kv_norm, m_mla_w_ukv, m_mla_w_o, m_sb_w_qkv, m_sb_w_o, m_mem_w_q, m_mem_w_kv, m_mem_w_o, m_ffn_w_in, m_ffn_conv_w, m_ffn_conv_b, m_ffn_w_out, v_norm_mix, v_norm_mem_q, v_norm_mem_kv, v_norm_ffn, v_norm_final, v_mla_w_down, v_mla_q_norm, v_mla_w_uq, v_mla_kv_norm, v_mla_w_ukv, v_mla_w_o, v_sb_w_qkv, v_sb_w_o, v_mem_w_q, v_mem_w_kv, v_mem_w_o, v_ffn_w_in, v_ffn_conv_w, v_ffn_conv_b, v_ffn_w_out):
    args = dict(locals())
    big = ["mla_w_down", "mla_w_uq", "mla_w_ukv", "mla_w_o", "sb_w_qkv", "sb_w_o", "mem_w_q", "mem_w_kv",
           "mem_w_o", "ffn_w_in", "ffn_w_out"]
    col_cut = {"mla_w_uq", "mla_w_ukv", "sb_w_qkv", "mem_w_o", "ffn_w_in"}
    axes = [1 if n in col_cut else 0 for n in big]
    small = ["norm_mix", "norm_mem_q", "norm_mem_kv", "norm_ffn", "norm_final", "mla_q_norm", "mla_kv_norm",
             "ffn_conv_b", "ffn_conv_w"]
    order = ["norm_mix", "norm_mem_q", "norm_mem_kv", "norm_ffn", "norm_final", "mla_w_down", "mla_q_norm",
             "mla_w_uq", "mla_kv_norm", "mla_w_ukv", "mla_w_o", "sb_w_qkv", "sb_w_o", "mem_w_q", "mem_w_kv",
             "mem_w_o", "ffn_w_in", "ffn_conv_w", "ffn_conv_b", "ffn_w_out"]

    xs, mems, target = x[0], mem[0], loss_target[0]
    s, d = xs.shape
    depth = norm_mix.shape[0]
    ql, kvl = mla_q_norm.shape[1], mla_kv_norm.shape[1]
    mla_heads = N_CHIPS * mla_w_uq.shape[2] // (HEAD + ROPE_DIM)
    sb_heads = N_CHIPS * sb_w_qkv.shape[2] // (3 * HEAD)
    mem_heads = mem_w_q.shape[2] // HEAD
    ff = N_CHIPS * ffn_w_out.shape[1]
    chip = 2 * lax.axis_index("x") + lax.axis_index("y")

    full = dict(zip(big, gather_weights([args[n].astype(BF16) for n in big], axes)))
    w_uq = [_uq_layout(full["mla_w_uq"][j], mla_heads) for j in range(full["mla_w_uq"].shape[0])]
    cos, sin = _rope_tables(positions[0])
    _, sb_tk = _sb_tiles(s)
    tri = (jnp.arange(sb_tk)[:, None] > jnp.arange(sb_tk)[None, :]).astype(BF16)

    conv_w = gather_conv_w(ffn_conv_w)

    saved = []
    xa = xs
    for i in range(depth):
        j = i // 2
        lay = {}
        lay["xa"] = xa
        h1 = rmsnorm_fwd(xa, norm_mix[i], name="norm_fwd")
        lay["h1"] = h1
        if i % 2 == 0:
            down = mm(h1, full["mla_w_down"][j], out_dtype=F32, name="mm_down")
            cq = rmsnorm_fwd(down[:, :ql], mla_q_norm[j], name="norm_lora_fwd")
            ckv = rmsnorm_fwd(down[:, ql:ql + kvl], mla_kv_norm[j], name="norm_lora_fwd")
            kr_raw = jnp.pad(down[:, ql + kvl:], ((0, 0), (0, HEAD - ROPE_DIM)))
            kr = rope(kr_raw, cos, sin, col0=0, n=1, inverse=False, name="rope_k")
            qall = mm(cq, w_uq[j], name="mm_uq")
            qr = rope(qall, cos, sin, col0=mla_heads, n=mla_heads, inverse=False, name="rope_q")
            kv = mm(ckv, full["mla_w_ukv"][j], name="mm_ukv")
            o, lse = mla_fwd(qall, qr, kv, kr, mla_heads, name="mla_fwd")
            xb = mm(o, full["mla_w_o"][j], add=xa, out_dtype=F32, name="mm_out_res")
            lay.update(down=down, cq=cq, ckv=ckv, qall=qall, qr=qr, kv=kv, kr=kr, o=o, lse=lse)
        else:
            qkv = mm(h1, full["sb_w_qkv"][j], name="mm_qkv")
            o, cmat = sb_fwd(qkv, tri, sb_heads, name="sb_fwd")
            xb = mm(o, full["sb_w_o"][j], add=xa, out_dtype=F32, name="mm_out_res")
            lay.update(qkv=qkv, o=o, cmat=cmat)
        h2 = rmsnorm_fwd(xb, norm_mem_q[i], name="norm_fwd")
        hm = rmsnorm_fwd(mems, norm_mem_kv[i], name="norm_mem_fwd")
        qm = mm(h2, full["mem_w_q"][i], name="mm_mem_q")
        kvm = mm(hm, full["mem_w_kv"][i], name="mm_mem_kv")
        om = mem_fwd(qm, kvm, mem_heads, name="mem_fwd")
        xc = mm(om, full["mem_w_o"][i], add=xb, out_dtype=F32, name="mm_mem_out_res")
        h3 = rmsnorm_fwd(xc, norm_ffn[i], name="norm_fwd")
        u = mm(h3, full["ffn_w_in"][i], name="mm_ffn_in")
        act = conv_gate_fwd(u, conv_w[i], ffn_conv_b[i][None, :], name="conv_gate_fwd")
        xd = mm(act, full["ffn_w_out"][i], add=xc, out_dtype=F32, name="mm_ffn_out_res")
        lay.update(xb=xb, h2=h2, hm=hm, qm=qm, kvm=kvm, om=om, xc=xc, h3=h3, u=u, act=act)
        saved.append(lay)
        xa = xd

    dx, g_final, loss_part = final_loss(xa, norm_final, target, name="final_loss")

    gw = {n: [None] * full[n].shape[0] for n in big}
    g_small = {"norm_mix": [None] * depth, "norm_mem_q": [None] * depth, "norm_mem_kv": [None] * depth,
               "norm_ffn": [None] * depth, "mla_q_norm": [None] * (depth - depth // 2),
               "mla_kv_norm": [None] * (depth - depth // 2), "conv": [None] * depth}
    for i in reversed(range(depth)):
        j = i // 2
        lay = saved[i]
        gw["ffn_w_out"][i] = mm(lay["act"], dx, mode="tn", out_dtype=F32, name="mm_ffn_out_wgrad")
        da = mm(dx, full["ffn_w_out"][i], mode="nt", name="mm_ffn_out_dgrad")
        dcg, dcu, g_small["conv"][i] = conv_gate_bwd(lay["u"], da, conv_w[i], ffn_conv_b[i][None, :], name="conv_gate_bwd")
        du = jnp.concatenate([conv_transpose(dcg, conv_w[i][:, :ff], name="conv_transpose"),
                              conv_transpose(dcu, conv_w[i][:, ff:], name="conv_transpose")], axis=1)
        gw["ffn_w_in"][i] = mm(lay["h3"], du, mode="tn", out_dtype=F32, name="mm_ffn_in_wgrad")
        dh3 = mm(du, full["ffn_w_in"][i], mode="nt", name="mm_ffn_in_dgrad")
        dx, g_small["norm_ffn"][i] = rmsnorm_bwd(dh3, lay["xc"], norm_ffn[i], dx, name="norm_bwd")
        gw["mem_w_o"][i] = mm(lay["om"], dx, mode="tn", out_dtype=F32, name="mm_mem_out_wgrad")
        dom = mm(dx, full["mem_w_o"][i], mode="nt", name="mm_mem_out_dgrad")
        dqm, dkm, dvm = mem_bwd(lay["qm"], lay["kvm"], dom, mem_heads, name="mem_bwd")
        dkvm = jnp.concatenate([dkm, dvm], axis=1)
        gw["mem_w_q"][i] = mm(lay["h2"], dqm, mode="tn", out_dtype=F32, name="mm_mem_q_wgrad")
        gw["mem_w_kv"][i] = mm(lay["hm"], dkvm, mode="tn", out_dtype=F32, name="mm_mem_kv_wgrad")
        dh2 = mm(dqm, full["mem_w_q"][i], mode="nt", name="mm_mem_q_dgrad")
        dhm = mm(dkvm, full["mem_w_kv"][i], mode="nt", name="mm_mem_kv_dgrad")
        _, g_small["norm_mem_kv"][i] = rmsnorm_bwd(dhm, mems, norm_mem_kv[i], name="norm_mem_bwd")
        dx, g_small["norm_mem_q"][i] = rmsnorm_bwd(dh2, lay["xb"], norm_mem_q[i], dx, name="norm_bwd")
        if i % 2 == 0:
            gw["mla_w_o"][j] = mm(lay["o"], dx, mode="tn", out_dtype=F32, name="mm_out_wgrad")
            do = mm(dx, full["mla_w_o"][j], mode="nt", name="mm_out_dgrad")
            att = (lay["qall"], lay["qr"], lay["kv"], lay["kr"], lay["o"], do, lay["lse"], mla_heads)
            dqn, dqr = mla_dq(*att, name="mla_dq")
            dkv, dkr = mla_dkv(*att, name="mla_dkv")
            dqall = jnp.concatenate([dqn, rope(dqr, cos, sin, col0=0, n=mla_heads, inverse=True, name="rope_q_bwd")], axis=1)
            gw["mla_w_ukv"][j] = mm(lay["ckv"], dkv, mode="tn", out_dtype=F32, name="mm_ukv_wgrad")
            dckv = mm(dkv, full["mla_w_ukv"][j], mode="nt", name="mm_ukv_dgrad")
            gw["mla_w_uq"][j] = _uq_layout_inv(mm(lay["cq"], dqall, mode="tn", out_dtype=F32, name="mm_uq_wgrad"), mla_heads)
            dcq = mm(dqall, w_uq[j], mode="nt", name="mm_uq_dgrad")
            down = lay["down"]
            d_q, g_small["mla_q_norm"][j] = rmsnorm_bwd(dcq, down[:, :ql], mla_q_norm[j], name="norm_lora_bwd")
            d_kv, g_small["mla_kv_norm"][j] = rmsnorm_bwd(dckv, down[:, ql:ql + kvl], mla_kv_norm[j], name="norm_lora_bwd")
            d_kr = rope(dkr, cos, sin, col0=0, n=1, inverse=True, name="rope_k_bwd")[:, :ROPE_DIM]
            ddown = jnp.concatenate([d_q.astype(BF16), d_kv.astype(BF16), d_kr], axis=1)
            gw["mla_w_down"][j] = mm(lay["h1"], ddown, mode="tn", out_dtype=F32, name="mm_down_wgrad")
            dh1 = mm(ddown, full["mla_w_down"][j], mode="nt", name="mm_down_dgrad")
        else:
            gw["sb_w_o"][j] = mm(lay["o"], dx, mode="tn", out_dtype=F32, name="mm_out_wgrad")
            do = mm(dx, full["sb_w_o"][j], mode="nt", name="mm_out_dgrad")
            dq, dmat = sb_dq(lay["qkv"], do, lay["cmat"], tri, sb_heads, name="sb_dq")
            dk, dv = sb_dkv(lay["qkv"], do, lay["cmat"], dmat, tri, sb_heads, name="sb_dkv")
            dqkv = jnp.concatenate([dq, dk, dv], axis=1)
            gw["sb_w_qkv"][j] = mm(lay["h1"], dqkv, mode="tn", out_dtype=F32, name="mm_qkv_wgrad")
            dh1 = mm(dqkv, full["sb_w_qkv"][j], mode="nt", name="mm_qkv_dgrad")
        dx, g_small["norm_mix"][i] = rmsnorm_bwd(dh1, lay["xa"], norm_mix[i], dx, name="norm_bwd")

    g_big = dict(zip(big, reduce_scatter_grads([gw[n] for n in big], axes)))
    conv = jnp.stack(g_small["conv"])
    parts = [jnp.concatenate(g_small[n], axis=0) for n in ("norm_mix", "norm_mem_q", "norm_mem_kv", "norm_ffn")]
    parts += [g_final, jnp.concatenate(g_small["mla_q_norm"], axis=0), jnp.concatenate(g_small["mla_kv_norm"], axis=0),
              conv[:, 3, :], conv[:, :3, :], loss_part[:1, :1]]
    sizes = [p.size for p in parts]
    packed = jnp.concatenate([p.reshape(-1) for p in parts])
    packed = jnp.pad(packed, (0, -packed.size % 1024)).reshape(-1, 128)
    total = allreduce_small(packed).reshape(-1)
    g_rep, at = {}, 0
    for n, p, size in zip(small + ["loss"], parts, sizes):
        g_rep[n] = total[at:at + size].reshape(p.shape)
        at += size
    loss = g_rep.pop("loss").reshape(())
    g_rep["norm_final"] = g_rep["norm_final"].reshape(norm_final.shape)
    width = ffn_conv_w.shape[2]
    g_rep["ffn_conv_w"] = lax.dynamic_slice_in_dim(g_rep["ffn_conv_w"], chip * width, width, axis=2)

    grads = {**g_big, **g_rep}
    delta, new_m, new_v = {}, {}, {}
    for n in order:
        delta[n], new_m[n], new_v[n] = adamw(args[n], grads[n], args["m_" + n], args["v_" + n], name="adamw")
    return (loss, dx[None], *[grads[n] for n in order], *[delta[n] for n in order],
            *[new_m[n] for n in order], *[new_v[n] for n in order])
```

```python
import math

import jax
import jax.numpy as jnp
import numpy as np
from jax import lax
from jax.experimental import pallas as pl
from jax.experimental.pallas import tpu as pltpu

F32 = jnp.float32
BF16 = jnp.bfloat16
MESH = pl.DeviceIdType.MESH

EPS = 1e-6
CHUNK_SHIFT = 6
HEAD = 128
ROPE_DIM = 64
ROPE_THETA = 10000.0
N_CHIPS = 4
ADAM_LR, ADAM_B1, ADAM_B2, ADAM_EPS, ADAM_WD, ADAM_STEP = 0.001, 0.9, 0.999, 1e-08, 0.01, 10

VMEM_LIMIT_BYTES = 48 * 1024 * 1024
MM_TM, MM_TN, MM_TK = (512, 256, 128), (1024, 512, 256, 128), (1024, 512, 256, 128)
ROW_TILE = (256, 128, 64, 32, 16, 8)
ATT_T = 512
SB_TQ, SB_TK = 512, 256
MEM_TQ = 512
CONV_TM, CONV_TN = 512, (512, 256, 128)
HALO = 16


def _pick(dim, prefs):
    for p in prefs:
        if dim % p == 0:
            return p
    return dim


def _dot(a, b, ca, cb):
    return lax.dot_general(a, b, (((ca,), (cb,)), ((), ())), preferred_element_type=F32)


def _params(sem):
    return pltpu.CompilerParams(dimension_semantics=sem, vmem_limit_bytes=VMEM_LIMIT_BYTES)


def _tables(pairs):
    arr = np.asarray(pairs, dtype=np.int32)
    return jnp.asarray(arr[:, 0]), jnp.asarray(arr[:, 1])


def mm(a, b, *, mode="nn", add=None, out_dtype=BF16, name):
    if mode == "nn":
        (m, k), (k2, n) = a.shape, b.shape
    elif mode == "nt":
        (m, k), (n, k2) = a.shape, b.shape
    else:
        (k, m), (k2, n) = a.shape, b.shape
    assert k == k2, (a.shape, b.shape, mode)
    tm, tn, tk = _pick(m, MM_TM), _pick(n, MM_TN), _pick(k, MM_TK)
    nk = k // tk
    ca, cb = {"nn": (1, 0), "nt": (1, 1), "tn": (0, 0)}[mode]

    def body(a_ref, b_ref, *rest):
        if add is None:
            o_ref, acc_ref = rest
        else:
            add_ref, o_ref, acc_ref = rest
        kk = pl.program_id(2)

        @pl.when(kk == 0)
        def _():
            acc_ref[...] = jnp.zeros_like(acc_ref)

        acc_ref[...] += _dot(a_ref[...].astype(BF16), b_ref[...].astype(BF16), ca, cb)

        @pl.when(kk == nk - 1)
        def _():
            r = acc_ref[...]
            if add is not None:
                r = r + add_ref[...]
            o_ref[...] = r.astype(o_ref.dtype)

    if mode == "tn":
        a_spec = pl.BlockSpec((tk, tm), lambda i, j, kk: (kk, i))
    else:
        a_spec = pl.BlockSpec((tm, tk), lambda i, j, kk: (i, kk))
    if mode == "nt":
        b_spec = pl.BlockSpec((tn, tk), lambda i, j, kk: (j, kk))
    else:
        b_spec = pl.BlockSpec((tk, tn), lambda i, j, kk: (kk, j))
    o_spec = pl.BlockSpec((tm, tn), lambda i, j, kk: (i, j))
    in_specs, args = [a_spec, b_spec], [a, b]
    if add is not None:
        in_specs.append(o_spec)
        args.append(add)
    return pl.pallas_call(
        body, name=name, grid=(m // tm, n // tn, nk), in_specs=in_specs, out_specs=o_spec,
        out_shape=jax.ShapeDtypeStruct((m, n), out_dtype), scratch_shapes=[pltpu.VMEM((tm, tn), F32)],
        compiler_params=_params(("parallel", "parallel", "arbitrary")),
    )(*args)


def rmsnorm_fwd(x, g, *, name):
    m, d = x.shape
    tm = _pick(m, ROW_TILE)

    def body(x_ref, g_ref, o_ref):
        xv = x_ref[...]
        r = lax.rsqrt(jnp.mean(xv * xv, axis=-1, keepdims=True) + EPS)
        o_ref[...] = (xv * r * g_ref[...]).astype(o_ref.dtype)

    return pl.pallas_call(
        body, name=name, grid=(m // tm,),
        in_specs=[pl.BlockSpec((tm, d), lambda i: (i, 0)), pl.BlockSpec((1, d), lambda i: (0, 0))],
        out_specs=pl.BlockSpec((tm, d), lambda i: (i, 0)), out_shape=jax.ShapeDtypeStruct((m, d), BF16),
        compiler_params=_params(("parallel",)),
    )(x, g.reshape(1, d))


def rmsnorm_bwd(dh, x, g, res=None, *, name):
    m, d = x.shape
    tm = _pick(m, ROW_TILE)

    def body(dh_ref, x_ref, g_ref, *rest):
        if res is None:
            dx_ref, dg_ref = rest
        else:
            res_ref, dx_ref, dg_ref = rest

        @pl.when(pl.program_id(0) == 0)
        def _():
            dg_ref[...] = jnp.zeros_like(dg_ref)

        xv = x_ref[...]
        dhv = dh_ref[...].astype(F32)
        r = lax.rsqrt(jnp.mean(xv * xv, axis=-1, keepdims=True) + EPS)
        y = xv * r
        dhg = dhv * g_ref[...]
        dx = r * (dhg - y * jnp.mean(dhg * y, axis=-1, keepdims=True))
        if res is not None:
            dx = dx + res_ref[...]
        dx_ref[...] = dx
        dg_ref[...] += jnp.sum(dhv * y, axis=0, keepdims=True)

    row = pl.BlockSpec((tm, d), lambda i: (i, 0))
    vec = pl.BlockSpec((1, d), lambda i: (0, 0))
    in_specs, args = [row, row, vec], [dh, x, g.reshape(1, d)]
    if res is not None:
        in_specs.append(row)
        args.append(res)
    return pl.pallas_call(
        body, name=name, grid=(m // tm,), in_specs=in_specs, out_specs=[row, vec],
        out_shape=[jax.ShapeDtypeStruct((m, d), F32), jax.ShapeDtypeStruct((1, d), F32)],
        compiler_params=_params(("arbitrary",)),
    )(*args)


def final_loss(x, g, target, *, name):
    m, d = x.shape
    tm = _pick(m, ROW_TILE)

    def body(x_ref, g_ref, t_ref, dx_ref, dg_ref, loss_ref):
        @pl.when(pl.program_id(0) == 0)
        def _():
            dg_ref[...] = jnp.zeros_like(dg_ref)
            loss_ref[...] = jnp.zeros_like(loss_ref)

        xv = x_ref[...]
        gv = g_ref[...]
        r = lax.rsqrt(jnp.mean(xv * xv, axis=-1, keepdims=True) + EPS)
        y = xv * r
        err = y * gv - t_ref[...]
        loss_ref[...] += 0.5 * jnp.sum(jnp.mean(err * err, axis=-1, keepdims=True))
        dy = err * (1.0 / d)
        dyg = dy * gv
        dx_ref[...] = r * (dyg - y * jnp.mean(dyg * y, axis=-1, keepdims=True))
        dg_ref[...] += jnp.sum(dy * y, axis=0, keepdims=True)

    row = pl.BlockSpec((tm, d), lambda i: (i, 0))
    vec = pl.BlockSpec((1, d), lambda i: (0, 0))
    return pl.pallas_call(
        body, name=name, grid=(m // tm,), in_specs=[row, vec, row],
        out_specs=[row, vec, pl.BlockSpec((8, 128), lambda i: (0, 0))],
        out_shape=[jax.ShapeDtypeStruct((m, d), F32), jax.ShapeDtypeStruct((1, d), F32),
                   jax.ShapeDtypeStruct((8, 128), F32)],
        compiler_params=_params(("arbitrary",)),
    )(x, g.reshape(1, d), target)


def rope(xin, cos, sin, *, col0, n, inverse, name):
    s = xin.shape[0]
    tm = _pick(s, ROW_TILE)
    half = ROPE_DIM // 2

    def body(x_ref, c_ref, s_ref, o_ref):
        xv = x_ref[...].astype(F32)
        lane = lax.broadcasted_iota(jnp.int32, xv.shape, 1)
        rot = jnp.where(lane < half, -pltpu.roll(xv, HEAD - half, 1), pltpu.roll(xv, half, 1))
        sv = s_ref[...]
        if inverse:
            sv = -sv
        o_ref[...] = (xv * c_ref[...] + rot * sv).astype(o_ref.dtype)

    tab = pl.BlockSpec((tm, HEAD), lambda i, j: (i, 0))
    return pl.pallas_call(
        body, name=name, grid=(s // tm, n),
        in_specs=[pl.BlockSpec((tm, HEAD), lambda i, j: (i, j + col0)), tab, tab],
        out_specs=pl.BlockSpec((tm, HEAD), lambda i, j: (i, j)),
        out_shape=jax.ShapeDtypeStruct((s, n * HEAD), BF16),
        compiler_params=_params(("parallel", "parallel")),
    )(xin, cos, sin)


def sum_lane_tiles(xin, n, *, name):
    s = xin.shape[0]
    tm = _pick(s, ROW_TILE)

    def body(x_ref, o_ref):
        acc = x_ref[:, :HEAD]
        for k in range(1, n):
            acc = acc + x_ref[:, k * HEAD:(k + 1) * HEAD]
        o_ref[...] = acc

    return pl.pallas_call(
        body, name=name, grid=(s // tm,), in_specs=[pl.BlockSpec((tm, n * HEAD), lambda i: (i, 0))],
        out_specs=pl.BlockSpec((tm, HEAD), lambda i: (i, 0)), out_shape=jax.ShapeDtypeStruct((s, HEAD), F32),
        compiler_params=_params(("parallel",)),
    )(xin)


def _mla_scores(qn, qr, kn, kr, scale, diagonal):
    sc = (_dot(qn, kn, 1, 1) + _dot(qr, kr, 1, 1)) * scale
    if not diagonal:
        return sc
    qchunk = jnp.right_shift(lax.broadcasted_iota(jnp.int32, sc.shape, 0), CHUNK_SHIFT)
    kchunk = jnp.right_shift(lax.broadcasted_iota(jnp.int32, sc.shape, 1), CHUNK_SHIFT)
    return jnp.where(kchunk <= qchunk, sc, -jnp.inf)


def mla_fwd(qall, qr, kv, kr, heads, *, name):
    s = qr.shape[0]
    t = min(ATT_T, s)
    nq = s // t
    scale = 1.0 / math.sqrt(HEAD + ROPE_DIM)
    pairs = [(qi, kj) for qi in range(nq) for kj in range(qi + 1)]
    qtab, ktab = _tables(pairs)

    def body(qt_ref, kt_ref, qn_ref, qr_ref, kv_ref, kr_ref, o_ref, lse_ref, m_ref, l_ref, acc_ref):
        st = pl.program_id(1)
        qi, kj = qt_ref[st], kt_ref[st]

        @pl.when(kj == 0)
        def _():
            m_ref[...] = jnp.full_like(m_ref, -jnp.inf)
            l_ref[...] = jnp.zeros_like(l_ref)
            acc_ref[...] = jnp.zeros_like(acc_ref)

        def step(diagonal):
            sc = _mla_scores(qn_ref[...], qr_ref[...], kv_ref[:, :HEAD], kr_ref[...], scale, diagonal)
            m_prev = m_ref[...]
            m_new = jnp.maximum(m_prev, jnp.max(sc, axis=-1, keepdims=True))
            p = jnp.exp(sc - m_new)
            alpha = jnp.exp(m_prev - m_new)
            l_ref[...] = alpha * l_ref[...] + jnp.sum(p, axis=-1, keepdims=True)
            acc_ref[...] = alpha * acc_ref[...] + _dot(p.astype(BF16), kv_ref[:, HEAD:], 1, 0)
            m_ref[...] = m_new

        @pl.when(kj < qi)
        def _():
            step(False)

        @pl.when(kj == qi)
        def _():
            step(True)
            o_ref[...] = (acc_ref[...] / l_ref[...]).astype(o_ref.dtype)
            lse_ref[...] = jnp.broadcast_to(m_ref[...] + jnp.log(l_ref[...]), lse_ref.shape)

    qspec = pl.BlockSpec((t, HEAD), lambda h, st, qt, kt: (qt[st], h))
    grid_spec = pltpu.PrefetchScalarGridSpec(
        num_scalar_prefetch=2, grid=(heads, len(pairs)),
        in_specs=[qspec, qspec,
                  pl.BlockSpec((t, 2 * HEAD), lambda h, st, qt, kt: (kt[st], h)),
                  pl.BlockSpec((t, HEAD), lambda h, st, qt, kt: (kt[st], 0))],
        out_specs=[qspec, qspec],
        scratch_shapes=[pltpu.VMEM((t, 1), F32), pltpu.VMEM((t, 1), F32), pltpu.VMEM((t, HEAD), F32)])
    return pl.pallas_call(
        body, name=name, grid_spec=grid_spec,
        out_shape=[jax.ShapeDtypeStruct((s, heads * HEAD), BF16), jax.ShapeDtypeStruct((s, heads * HEAD), F32)],
        compiler_params=_params(("parallel", "arbitrary")),
    )(qtab, ktab, qall, qr, kv, kr)


def mla_bwd(qall, qr, kv, kr, o, do, lse, heads, *, name):
    s = qr.shape[0]
    t = min(ATT_T, s)
    nq = s // t
    scale = 1.0 / math.sqrt(HEAD + ROPE_DIM)
    pairs = [(kj, qi) for kj in range(nq) for qi in range(kj, nq)]
    ktab, qtab = _tables(pairs)
    last = len(pairs) - 1

    def body(kt_ref, qt_ref, qn_ref, qr_ref, kv_ref, kr_ref, o_ref, do_ref, lse_ref,
             dqn_ref, dqr_ref, dkv_ref, dkr_ref, fn_ref, fr_ref, akn_ref, av_ref, akr_ref):
        st = pl.program_id(1)
        kj, qi = kt_ref[st], qt_ref[st]

        @pl.when(st == 0)
        def _():
            fn_ref[...] = jnp.zeros_like(fn_ref)
            fr_ref[...] = jnp.zeros_like(fr_ref)

        @pl.when(qi == kj)
        def _():
            akn_ref[...] = jnp.zeros_like(akn_ref)
            av_ref[...] = jnp.zeros_like(av_ref)
            akr_ref[...] = jnp.zeros_like(akr_ref)

        def step(diagonal):
            rows = pl.ds(pl.multiple_of(qi * t, t), t)
            qn, qrv, dov = qn_ref[...], qr_ref[...], do_ref[...]
            kn, krv = kv_ref[:, :HEAD], kr_ref[...]
            sc = _mla_scores(qn, qrv, kn, krv, scale, diagonal)
            p = jnp.exp(sc - lse_ref[:, :1])
            delta = jnp.sum(dov.astype(F32) * o_ref[...].astype(F32), axis=-1, keepdims=True)
            dp = _dot(dov, kv_ref[:, HEAD:], 1, 1)
            ds = (p * (dp - delta) * scale).astype(BF16)
            av_ref[...] += _dot(p.astype(BF16), dov, 0, 0)
            akn_ref[...] += _dot(ds, qn, 0, 0)
            akr_ref[...] += _dot(ds, qrv, 0, 0)
            fn_ref[rows, :] += _dot(ds, kn, 1, 0)
            fr_ref[rows, :] += _dot(ds, krv, 1, 0)

        @pl.when(qi == kj)
        def _():
            step(True)

        @pl.when(qi > kj)
        def _():
            step(False)

        @pl.when(qi == nq - 1)
        def _():
            dkv_ref[:, :HEAD] = akn_ref[...].astype(dkv_ref.dtype)
            dkv_ref[:, HEAD:] = av_ref[...].astype(dkv_ref.dtype)
            dkr_ref[...] = akr_ref[...]

        @pl.when(st == last)
        def _():
            dqn_ref[...] = fn_ref[...].astype(dqn_ref.dtype)
            dqr_ref[...] = fr_ref[...].astype(dqr_ref.dtype)

    qspec = pl.BlockSpec((t, HEAD), lambda h, st, kt, qt: (qt[st], h))
    kvspec = pl.BlockSpec((t, 2 * HEAD), lambda h, st, kt, qt: (kt[st], h))
    krspec = pl.BlockSpec((t, HEAD), lambda h, st, kt, qt: (kt[st], 0))
    headspec = pl.BlockSpec((s, HEAD), lambda h, st, kt, qt: (0, h))
    grid_spec = pltpu.PrefetchScalarGridSpec(
        num_scalar_prefetch=2, grid=(heads, len(pairs)),
        in_specs=[qspec, qspec, kvspec, krspec, qspec, qspec, qspec],
        out_specs=[headspec, headspec, kvspec, pl.BlockSpec((t, HEAD), lambda h, st, kt, qt: (kt[st], h))],
        scratch_shapes=[pltpu.VMEM((s, HEAD), F32), pltpu.VMEM((s, HEAD), F32), pltpu.VMEM((t, HEAD), F32),
                        pltpu.VMEM((t, HEAD), F32), pltpu.VMEM((t, HEAD), F32)])
    return pl.pallas_call(
        body, name=name, grid_spec=grid_spec,
        out_shape=[jax.ShapeDtypeStruct((s, heads * HEAD), BF16), jax.ShapeDtypeStruct((s, heads * HEAD), BF16),
                   jax.ShapeDtypeStruct((s, heads * 2 * HEAD), BF16), jax.ShapeDtypeStruct((s, heads * HEAD), F32)],
        compiler_params=_params(("parallel", "arbitrary")),
    )(ktab, qtab, qall, qr, kv, kr, o, do, lse)


def _split_dot(val, tri, cb):
    hi = val.astype(BF16)
    lo = (val - hi.astype(F32)).astype(BF16)
    return _dot(hi, tri, 1, cb) + _dot(lo, tri, 1, cb)


def _sb_logs(q, k, offset, scale, masked):
    z = _dot(q, k, 1, 1) * scale
    sp = jnp.log(1.0 + jnp.exp(-jnp.abs(z)))
    ls = jnp.minimum(z, 0.0) - sp
    lk = ls - z
    if not masked:
        return None, ls, lk
    strict = (lax.broadcasted_iota(jnp.int32, z.shape, 1) + offset) < lax.broadcasted_iota(jnp.int32, z.shape, 0)
    return strict, ls, jnp.where(strict, lk, 0.0)


def _lane_pick(blk, idx):
    lane = lax.broadcasted_iota(jnp.int32, blk.shape, 1)
    return jnp.sum(jnp.where(lane == idx, blk, 0.0), axis=-1, keepdims=True)


def _lane_put(blk, idx, col):
    lane = lax.broadcasted_iota(jnp.int32, blk.shape, 1)
    return jnp.where(lane == idx, col, blk)


def _sb_tiles(s):
    tq = min(SB_TQ, s)
    tk = min(SB_TK, tq)
    assert s // tk <= HEAD
    return tq, tk


def sb_fwd(qkv, tri, heads, *, name):
    s = qkv.shape[0]
    tq, tk = _sb_tiles(s)
    nq, ratio = s // tq, tq // tk
    scale = 1.0 / math.sqrt(HEAD)
    pairs = [(qi, kj) for qi in range(nq) for kj in range((qi + 1) * ratio - 1, -1, -1)]
    qtab, ktab = _tables(pairs)

    def body(qt_ref, kt_ref, q_ref, k_ref, v_ref, tri_ref, o_ref, c_ref, carry_ref, acc_ref):
        st = pl.program_id(1)
        qi, kj = qt_ref[st], kt_ref[st]

        @pl.when(kj == (qi + 1) * ratio - 1)
        def _():
            carry_ref[...] = jnp.zeros_like(carry_ref)
            acc_ref[...] = jnp.zeros_like(acc_ref)
            c_ref[...] = jnp.zeros_like(c_ref)

        def step(masked):
            strict, ls, lk = _sb_logs(q_ref[...], k_ref[...], kj * tk - qi * tq, scale, masked)
            carry = carry_ref[...]
            a = jnp.exp(ls + _split_dot(lk, tri_ref[...], 0) + carry)
            if masked:
                a = jnp.where(strict, a, 0.0)
            acc_ref[...] += _dot(a.astype(BF16), v_ref[...], 1, 0)
            c_ref[...] = _lane_put(c_ref[...], kj, carry)
            carry_ref[...] = carry + jnp.sum(lk, axis=-1, keepdims=True)

        @pl.when((kj + 1) * tk > qi * tq)
        def _():
            step(True)

        @pl.when((kj + 1) * tk <= qi * tq)
        def _():
            step(False)

        @pl.when(kj == 0)
        def _():
            o_ref[...] = acc_ref[...].astype(o_ref.dtype)

    qspec = pl.BlockSpec((tq, HEAD), lambda h, st, qt, kt: (qt[st], h))
    grid_spec = pltpu.PrefetchScalarGridSpec(
        num_scalar_prefetch=2, grid=(heads, len(pairs)),
        in_specs=[qspec,
                  pl.BlockSpec((tk, HEAD), lambda h, st, qt, kt: (kt[st], heads + h)),
                  pl.BlockSpec((tk, HEAD), lambda h, st, qt, kt: (kt[st], 2 * heads + h)),
                  pl.BlockSpec((tk, tk), lambda h, st, qt, kt: (0, 0))],
        out_specs=[qspec, qspec],
        scratch_shapes=[pltpu.VMEM((tq, 1), F32), pltpu.VMEM((tq, HEAD), F32)])
    return pl.pallas_call(
        body, name=name, grid_spec=grid_spec,
        out_shape=[jax.ShapeDtypeStruct((s, heads * HEAD), BF16), jax.ShapeDtypeStruct((s, heads * HEAD), F32)],
        compiler_params=_params(("parallel", "arbitrary")),
    )(qtab, ktab, qkv, qkv, qkv, tri)


def sb_bwd(qkv, do, cmat, tri, heads, *, name):
    s = qkv.shape[0]
    tq, tk = _sb_tiles(s)
    nq, nkb, ratio = s // tq, s // tk, tq // tk
    scale = 1.0 / math.sqrt(HEAD)
    pairs = [(kj, qi) for kj in range(nkb) for qi in range(kj // ratio, nq)]
    ktab, qtab = _tables(pairs)
    last = len(pairs) - 1

    def body(kt_ref, qt_ref, q_ref, k_ref, v_ref, do_ref, c_ref, tri_ref, dq_ref, dk_ref, dv_ref,
             dqf_ref, gsum_ref, ak_ref, av_ref):
        st = pl.program_id(1)
        kj, qi = kt_ref[st], qt_ref[st]

        @pl.when(st == 0)
        def _():
            dqf_ref[...] = jnp.zeros_like(dqf_ref)
            gsum_ref[...] = jnp.zeros_like(gsum_ref)

        @pl.when(qi == kj // ratio)
        def _():
            ak_ref[...] = jnp.zeros_like(ak_ref)
            av_ref[...] = jnp.zeros_like(av_ref)

        def step(masked):
            rows = pl.ds(pl.multiple_of(qi * tq, tq), tq)
            qv, kblk, dov, tri_v = q_ref[...], k_ref[...], do_ref[...], tri_ref[...]
            strict, ls, lk = _sb_logs(qv, kblk, kj * tk - qi * tq, scale, masked)
            a = jnp.exp(ls + _split_dot(lk, tri_v, 0) + _lane_pick(c_ref[...], kj))
            if masked:
                a = jnp.where(strict, a, 0.0)
            g = _dot(dov, v_ref[...], 1, 1) * a
            before_all = gsum_ref[rows, :]
            before = _split_dot(g, tri_v, 1) + before_all[:, :1]
            beta = jnp.exp(ls)
            dz = g * (1.0 - beta) - before * beta
            if masked:
                dz = jnp.where(strict, dz, 0.0)
            dzb = dz.astype(BF16)
            av_ref[...] += _dot(a.astype(BF16), dov, 0, 0)
            ak_ref[...] += _dot(dzb, qv, 0, 0)
            dqf_ref[rows, :] += _dot(dzb, kblk, 1, 0)
            gsum_ref[rows, :] = before_all + jnp.sum(g, axis=-1, keepdims=True)

        @pl.when((kj + 1) * tk > qi * tq)
        def _():
            step(True)

        @pl.when((kj + 1) * tk <= qi * tq)
        def _():
            step(False)

        @pl.when(qi == nq - 1)
        def _():
            dk_ref[...] = (ak_ref[...] * scale).astype(dk_ref.dtype)
            dv_ref[...] = av_ref[...].astype(dv_ref.dtype)

        @pl.when(st == last)
        def _():
            dq_ref[...] = (dqf_ref[...] * scale).astype(dq_ref.dtype)

    qspec = pl.BlockSpec((tq, HEAD), lambda h, st, kt, qt: (qt[st], h))
    ospec = pl.BlockSpec((tk, HEAD), lambda h, st, kt, qt: (kt[st], h))
    grid_spec = pltpu.PrefetchScalarGridSpec(
        num_scalar_prefetch=2, grid=(heads, len(pairs)),
        in_specs=[qspec,
                  pl.BlockSpec((tk, HEAD), lambda h, st, kt, qt: (kt[st], heads + h)),
                  pl.BlockSpec((tk, HEAD), lambda h, st, kt, qt: (kt[st], 2 * heads + h)),
                  qspec, qspec, pl.BlockSpec((tk, tk), lambda h, st, kt, qt: (0, 0))],
        out_specs=[pl.BlockSpec((s, HEAD), lambda h, st, kt, qt: (0, h)), ospec, ospec],
        scratch_shapes=[pltpu.VMEM((s, HEAD), F32), pltpu.VMEM((s, HEAD), F32), pltpu.VMEM((tk, HEAD), F32),
                        pltpu.VMEM((tk, HEAD), F32)])
    return pl.pallas_call(
        body, name=name, grid_spec=grid_spec,
        out_shape=[jax.ShapeDtypeStruct((s, heads * HEAD), BF16)] * 3,
        compiler_params=_params(("parallel", "arbitrary")),
    )(ktab, qtab, qkv, qkv, qkv, do, cmat, tri)


def _mem_probs(q, k, scale):
    sc = _dot(q, k, 1, 1) * scale
    e = jnp.exp(sc - jnp.max(sc, axis=-1, keepdims=True))
    return e / jnp.sum(e, axis=-1, keepdims=True)


def mem_fwd(q, kvm, heads, *, name):
    s, nm = q.shape[0], kvm.shape[0]
    tq = min(MEM_TQ, s)
    scale = 1.0 / math.sqrt(HEAD)

    def body(q_ref, k_ref, v_ref, o_ref):
        p = _mem_probs(q_ref[...], k_ref[...], scale)
        o_ref[...] = _dot(p.astype(BF16), v_ref[...], 1, 0).astype(o_ref.dtype)

    qspec = pl.BlockSpec((tq, HEAD), lambda h, qi: (qi, h))
    return pl.pallas_call(
        body, name=name, grid=(heads, s // tq),
        in_specs=[qspec, pl.BlockSpec((nm, HEAD), lambda h, qi: (0, h)),
                  pl.BlockSpec((nm, HEAD), lambda h, qi: (0, heads + h))],
        out_specs=qspec, out_shape=jax.ShapeDtypeStruct((s, heads * HEAD), BF16),
        compiler_params=_params(("parallel", "parallel")),
    )(q, kvm, kvm)


def mem_bwd(q, kvm, do, heads, *, name):
    s, nm = q.shape[0], kvm.shape[0]
    tq = min(MEM_TQ, s)
    scale = 1.0 / math.sqrt(HEAD)

    def body(q_ref, k_ref, v_ref, do_ref, dq_ref, dk_ref, dv_ref):
        @pl.when(pl.program_id(1) == 0)
        def _():
            dk_ref[...] = jnp.zeros_like(dk_ref)
            dv_ref[...] = jnp.zeros_like(dv_ref)

        qv, kvv, dov = q_ref[...], k_ref[...], do_ref[...]
        p = _mem_probs(qv, kvv, scale)
        dp = _dot(dov, v_ref[...], 1, 1)
        ds = (p * (dp - jnp.sum(dp * p, axis=-1, keepdims=True)) * scale).astype(BF16)
        dq_ref[...] = _dot(ds, kvv, 1, 0).astype(dq_ref.dtype)
        dk_ref[...] += _dot(ds, qv, 0, 0)
        dv_ref[...] += _dot(p.astype(BF16), dov, 0, 0)

    qspec = pl.BlockSpec((tq, HEAD), lambda h, qi: (qi, h))
    kspec = pl.BlockSpec((nm, HEAD), lambda h, qi: (0, h))
    return pl.pallas_call(
        body, name=name, grid=(heads, s // tq),
        in_specs=[qspec, kspec, pl.BlockSpec((nm, HEAD), lambda h, qi: (0, heads + h)), qspec],
        out_specs=[qspec, kspec, kspec],
        out_shape=[jax.ShapeDtypeStruct((s, heads * HEAD), BF16), jax.ShapeDtypeStruct((nm, heads * HEAD), F32),
                   jax.ShapeDtypeStruct((nm, heads * HEAD), F32)],
        compiler_params=_params(("parallel", "arbitrary")),
    )(q, kvm, kvm, do)


def _conv3(u, halo, w, b):
    tm = u.shape[0]
    row = lax.broadcasted_iota(jnp.int32, u.shape, 0)
    h1, h2 = halo[HALO - 1:HALO, :], halo[HALO - 2:HALO - 1, :]
    u1 = jnp.where(row == 0, h1, pltpu.roll(u, 1, 0))
    u2 = jnp.where(row == 0, h2, jnp.where(row == 1, h1, pltpu.roll(u, 2 % tm, 0)))
    return b + w[0:1, :] * u2 + w[1:2, :] * u1 + w[2:3, :] * u, u1, u2


def _conv_specs(s, f):
    tm, tn = min(CONV_TM, s), _pick(f, CONV_TN)
    return tm, tn, s // tm, f // tn


def _silu_parts(g):
    sg = 1.0 / (1.0 + jnp.exp(-g))
    return g * sg, sg


def conv_gate_fwd(u, cw, cb, *, name):
    s, f = u.shape[0], u.shape[1] // 2
    tm, tn, ni, nj = _conv_specs(s, f)
    hb = tm // HALO

    def body(ug_ref, uu_ref, hg_ref, hu_ref, wg_ref, wu_ref, bg_ref, bu_ref, a_ref):
        keep = (pl.program_id(1) > 0).astype(F32)
        gate, _, _ = _conv3(ug_ref[...].astype(F32), hg_ref[...].astype(F32) * keep, wg_ref[...], bg_ref[...])
        up, _, _ = _conv3(uu_ref[...].astype(F32), hu_ref[...].astype(F32) * keep, wu_ref[...], bu_ref[...])
        a_ref[...] = (_silu_parts(gate)[0] * up).astype(a_ref.dtype)

    def main(off):
        return pl.BlockSpec((tm, tn), lambda j, i: (i, j + off))

    def halo(off):
        return pl.BlockSpec((HALO, tn), lambda j, i: (jnp.maximum(i * hb - 1, 0), j + off))

    def par(rows, off):
        return pl.BlockSpec((rows, tn), lambda j, i: (0, j + off))

    return pl.pallas_call(
        body, name=name, grid=(nj, ni),
        in_specs=[main(0), main(nj), halo(0), halo(nj), par(3, 0), par(3, nj), par(1, 0), par(1, nj)],
        out_specs=main(0), out_shape=jax.ShapeDtypeStruct((s, f), BF16),
        compiler_params=_params(("parallel", "parallel")),
    )(u, u, u, u, cw, cw, cb, cb)


def conv_gate_bwd(u, da, cw, cb, *, name):
    s, f = u.shape[0], u.shape[1] // 2
    tm, tn, ni, nj = _conv_specs(s, f)
    hb = tm // HALO

    def body(ug_ref, uu_ref, hg_ref, hu_ref, da_ref, wg_ref, wu_ref, bg_ref, bu_ref, dg_ref, du_ref, pg_ref, pu_ref):
        @pl.when(pl.program_id(1) == 0)
        def _():
            pg_ref[...] = jnp.zeros_like(pg_ref)
            pu_ref[...] = jnp.zeros_like(pu_ref)

        keep = (pl.program_id(1) > 0).astype(F32)
        ug, uu = ug_ref[...].astype(F32), uu_ref[...].astype(F32)
        gate, ug1, ug2 = _conv3(ug, hg_ref[...].astype(F32) * keep, wg_ref[...], bg_ref[...])
        up, uu1, uu2 = _conv3(uu, hu_ref[...].astype(F32) * keep, wu_ref[...], bu_ref[...])
        act, sg = _silu_parts(gate)
        dav = da_ref[...].astype(F32)
        d_gate = dav * up * (sg * (1.0 + gate * (1.0 - sg)))
        d_up = dav * act
        dg_ref[...] = d_gate.astype(dg_ref.dtype)
        du_ref[...] = d_up.astype(du_ref.dtype)
        for p_ref, dc, taps in ((pg_ref, d_gate, (ug2, ug1, ug)), (pu_ref, d_up, (uu2, uu1, uu))):
            for r, tap in enumerate(taps):
                p_ref[r:r + 1, :] += jnp.sum(dc * tap, axis=0, keepdims=True)
            p_ref[3:4, :] += jnp.sum(dc, axis=0, keepdims=True)

    def main(off):
        return pl.BlockSpec((tm, tn), lambda j, i: (i, j + off))

    def halo(off):
        return pl.BlockSpec((HALO, tn), lambda j, i: (jnp.maximum(i * hb - 1, 0), j + off))

    def par(rows, off):
        return pl.BlockSpec((rows, tn), lambda j, i: (0, j + off))

    dg, du, pg, pu = pl.pallas_call(
        body, name=name, grid=(nj, ni),
        in_specs=[main(0), main(nj), halo(0), halo(nj), main(0), par(3, 0), par(3, nj), par(1, 0), par(1, nj)],
        out_specs=[main(0), main(0), par(8, 0), par(8, 0)],
        out_shape=[jax.ShapeDtypeStruct((s, f), BF16), jax.ShapeDtypeStruct((s, f), BF16),
                   jax.ShapeDtypeStruct((8, f), F32), jax.ShapeDtypeStruct((8, f), F32)],
        compiler_params=_params(("parallel", "arbitrary")),
    )(u, u, u, u, da, cw, cw, cb, cb)
    return dg, du, jnp.concatenate([pg, pu], axis=1)


def conv_transpose(dc, w, *, name):
    s, f = dc.shape
    tm, tn, ni, nj = _conv_specs(s, f)
    hb = tm // HALO

    def body(d_ref, h_ref, w_ref, o_ref):
        d = d_ref[...].astype(F32)
        halo = h_ref[...].astype(F32) * (pl.program_id(1) < ni - 1).astype(F32)
        row = lax.broadcasted_iota(jnp.int32, d.shape, 0)
        n0, n1 = halo[0:1, :], halo[1:2, :]
        d1 = jnp.where(row == tm - 1, n0, pltpu.roll(d, tm - 1, 0))
        d2 = jnp.where(row == tm - 2, n0, jnp.where(row == tm - 1, n1, pltpu.roll(d, tm - 2, 0)))
        wv = w_ref[...]
        o_ref[...] = (wv[2:3, :] * d + wv[1:2, :] * d1 + wv[0:1, :] * d2).astype(o_ref.dtype)

    main = pl.BlockSpec((tm, tn), lambda j, i: (i, j))
    return pl.pallas_call(
        body, name=name, grid=(nj, ni),
        in_specs=[main, pl.BlockSpec((HALO, tn), lambda j, i: (jnp.minimum((i + 1) * hb, s // HALO - 1), j)),
                  pl.BlockSpec((3, tn), lambda j, i: (0, j))],
        out_specs=main, out_shape=jax.ShapeDtypeStruct((s, f), BF16),
        compiler_params=_params(("parallel", "parallel")),
    )(dc, dc, w)


def _tile2d(r, c):
    return _pick(r, ROW_TILE), _pick(c, (2048, 1024, 512, 256, 128))


def sum_slots(buf, *, name):
    n, shape = buf.shape[0], buf.shape[1:]
    r, c = math.prod(shape[:-1]), shape[-1]
    tm, tn = _tile2d(r, c)

    def body(b_ref, o_ref):
        acc = b_ref[0]
        for k in range(1, n):
            acc = acc + b_ref[k]
        o_ref[...] = acc

    out = pl.pallas_call(
        body, name=name, grid=(r // tm, c // tn),
        in_specs=[pl.BlockSpec((n, tm, tn), lambda i, j: (0, i, j))],
        out_specs=pl.BlockSpec((tm, tn), lambda i, j: (i, j)), out_shape=jax.ShapeDtypeStruct((r, c), F32),
        compiler_params=_params(("parallel", "parallel")),
    )(buf.reshape(n, r, c))
    return out.reshape(shape)


def add_half(g4, got, core, *, name):
    a, _, r, c = g4.shape
    tm, tn = _tile2d(r, c)

    def body(core_ref, g_ref, got_ref, o_ref):
        o_ref[...] = g_ref[...] + got_ref[...]

    part = pl.BlockSpec((None, tm, tn), lambda b, i, j, core_ref: (b, i, j))
    grid_spec = pltpu.PrefetchScalarGridSpec(
        num_scalar_prefetch=1, grid=(a, r // tm, c // tn),
        in_specs=[pl.BlockSpec((None, None, tm, tn), lambda b, i, j, core_ref: (b, core_ref[0], i, j)), part],
        out_specs=part)
    return pl.pallas_call(
        body, name=name, grid_spec=grid_spec, out_shape=jax.ShapeDtypeStruct((a, r, c), F32),
        compiler_params=_params(("parallel", "parallel", "parallel")),
    )(core, g4, got)


def add_own_block(h, got, chip, axis, *, name):
    _, l, r, c = got.shape
    tm, tn = _tile2d(r, c)
    ncb = c // tn

    def body(chip_ref, h_ref, got_ref, o_ref):
        o_ref[...] = ((h_ref[...] + got_ref[0]) + got_ref[1]) + got_ref[2]

    if axis == 0:
        h_spec = pl.BlockSpec((None, None, tm, tn), lambda b, i, j, chip_ref: (b, chip_ref[0], i, j))
    else:
        h_spec = pl.BlockSpec((None, tm, tn), lambda b, i, j, chip_ref: (b, i, chip_ref[0] * ncb + j))
    grid_spec = pltpu.PrefetchScalarGridSpec(
        num_scalar_prefetch=1, grid=(l, r // tm, ncb),
        in_specs=[h_spec, pl.BlockSpec((3, None, tm, tn), lambda b, i, j, chip_ref: (0, b, i, j))],
        out_specs=pl.BlockSpec((None, tm, tn), lambda b, i, j, chip_ref: (b, i, j)))
    return pl.pallas_call(
        body, name=name, grid_spec=grid_spec, out_shape=jax.ShapeDtypeStruct((l, r, c), F32),
        compiler_params=_params(("parallel", "parallel", "parallel")),
    )(chip, h, got)


def _adam_update(w, g, m, v):
    bc1, bc2 = 1.0 - ADAM_B1 ** ADAM_STEP, 1.0 - ADAM_B2 ** ADAM_STEP
    mn = ADAM_B1 * m + (1.0 - ADAM_B1) * g
    vn = ADAM_B2 * v + (1.0 - ADAM_B2) * (g * g)
    return -ADAM_LR * ((mn / bc1) / (jnp.sqrt(vn / bc2) + ADAM_EPS) + ADAM_WD * w), mn, vn


def adamw(w, g, m, v, *, name):
    shape = w.shape
    c = shape[-1]
    r = math.prod(shape[:-1]) if len(shape) > 1 else 1
    tm, tn = _tile2d(r, c)

    def body(w_ref, g_ref, m_ref, v_ref, d_ref, mo_ref, vo_ref):
        d_ref[...], mo_ref[...], vo_ref[...] = _adam_update(w_ref[...], g_ref[...], m_ref[...], v_ref[...])

    spec = pl.BlockSpec((tm, tn), lambda i, j: (i, j))
    outs = pl.pallas_call(
        body, name=name, grid=(r // tm, c // tn), in_specs=[spec] * 4, out_specs=[spec] * 3,
        out_shape=[jax.ShapeDtypeStruct((r, c), F32)] * 3, compiler_params=_params(("parallel", "parallel")),
    )(*(t.reshape(r, c) for t in (w, g, m, v)))
    return tuple(o.reshape(shape) for o in outs)


def adamw_halves(w, mine, got, m, v, core, *, name):
    l, r, c = w.shape
    rh = r // 2
    tm, tn = _tile2d(rh, c)

    def body(core_ref, w_ref, mine_ref, got_ref, m_ref, v_ref, g_ref, d_ref, mo_ref, vo_ref):
        g = jnp.where(pl.program_id(1) == core_ref[0], mine_ref[...], got_ref[...])
        g_ref[...] = g
        d_ref[...], mo_ref[...], vo_ref[...] = _adam_update(w_ref[...], g, m_ref[...], v_ref[...])

    full = pl.BlockSpec((None, None, tm, tn), lambda b, h, i, j, core_ref: (b, h, i, j))
    half = pl.BlockSpec((None, tm, tn), lambda b, h, i, j, core_ref: (b, i, j))
    grid_spec = pltpu.PrefetchScalarGridSpec(
        num_scalar_prefetch=1, grid=(l, 2, rh // tm, c // tn),
        in_specs=[full, half, half, full, full], out_specs=[full] * 4)
    outs = pl.pallas_call(
        body, name=name, grid_spec=grid_spec, out_shape=[jax.ShapeDtypeStruct((l, 2, rh, c), F32)] * 4,
        compiler_params=_params(("parallel", "parallel", "parallel", "parallel")),
    )(core, w.reshape(l, 2, rh, c), mine, got, m.reshape(l, 2, rh, c), v.reshape(l, 2, rh, c))
    return tuple(o.reshape(l, r, c) for o in outs)


def _comm(name, inputs, out_shapes, aliases, plan):
    n_in, n_out = len(inputs), len(out_shapes)
    probe = plan([None] * n_in, [None] * n_out, 0, 0, 0, count_only=True)

    def body(*refs):
        in_refs, out_refs = refs[:n_in], refs[n_in:n_in + n_out]
        send_sems, recv_sems = refs[n_in + n_out:]
        x, y, c = lax.axis_index("x"), lax.axis_index("y"), lax.axis_index("c")
        xfers = plan(in_refs, out_refs, x, y, c, count_only=False)
        started = []
        for n, (src, dst, peer, _) in enumerate(xfers):
            if peer is None:
                cp = pltpu.make_async_copy(src, dst, send_sems.at[n])
            else:
                cp = pltpu.make_async_remote_copy(src_ref=src, dst_ref=dst, send_sem=send_sems.at[n],
                                                  recv_sem=recv_sems.at[n], device_id=peer, device_id_type=MESH)
            cp.start()
            started.append(cp)
        for n, (src, dst, peer, got) in enumerate(xfers):
            if peer is None:
                started[n].wait()
            else:
                pltpu.make_async_remote_copy(src_ref=got, dst_ref=got, send_sem=send_sems.at[n],
                                             recv_sem=recv_sems.at[n], device_id=peer,
                                             device_id_type=MESH).wait_recv()
                started[n].wait_send()

    hbm = pl.BlockSpec(memory_space=pl.ANY)
    return pl.pallas_call(
        body, name=name, in_specs=[hbm] * n_in, out_specs=[hbm] * n_out, out_shape=out_shapes,
        input_output_aliases=aliases,
        scratch_shapes=[pltpu.SemaphoreType.DMA((probe,)), pltpu.SemaphoreType.DMA((probe,))],
    )(*inputs)


def _other_chips(x, y):
    return [(1 - x, y), (x, 1 - y), (1 - x, 1 - y)]


def gather_weights(shards, axes):
    def out_shape(sh, ax):
        l, r, c = sh.shape
        return jax.ShapeDtypeStruct((l, N_CHIPS, 2, r // 2, c) if ax == 0 else (l, 2, r // 2, N_CHIPS * c), BF16)

    def piece(ref, ax, chip, half, width):
        if ax == 0:
            return ref.at[:, chip, half]
        return ref.at[:, half, :, pl.ds(chip * width, width)]

    def plan1(in_refs, out_refs, x, y, c, count_only):
        if count_only:
            return 4 * len(shards)
        me, xfers = 2 * x + y, []
        for sh, ax, src, out in zip(shards, axes, in_refs, out_refs):
            rows, width = sh.shape[1] // 2, sh.shape[2]
            mine = src.at[:, pl.ds(c * rows, rows), :]
            xfers.append((mine, piece(out, ax, me, c, width), None, None))
            for cx, cy in _other_chips(x, y):
                xfers.append((mine, piece(out, ax, me, c, width), (cx, cy, c), piece(out, ax, 2 * cx + cy, c, width)))
        return xfers

    def half_of(ref, ax, half):
        return ref.at[:, :, half] if ax == 0 else ref.at[:, half]

    def plan2(in_refs, out_refs, x, y, c, count_only):
        if count_only:
            return len(shards)
        return [(half_of(out, ax, c), half_of(out, ax, c), (x, y, 1 - c), half_of(out, ax, 1 - c))
                for ax, out in zip(axes, out_refs)]

    shapes = [out_shape(sh, ax) for sh, ax in zip(shards, axes)]
    part = _comm("gather_chips", list(shards), shapes, {}, plan1)
    full = _comm("gather_cores", list(part), shapes, {n: n for n in range(len(shards))}, plan2)
    return [f.reshape(sh.shape[0], N_CHIPS * sh.shape[1], sh.shape[2]) if ax == 0
            else f.reshape(sh.shape[0], sh.shape[1], N_CHIPS * sh.shape[2])
            for f, sh, ax in zip(full, shards, axes)]


def reduce_scatter_grads(grads, axes, core, chip):
    dims, views = [], []
    for g, ax in zip(grads, axes):
        l, r, c = g.shape
        if ax == 0:
            dims.append((l, r // N_CHIPS // 2, c))
            views.append(g.reshape(l, N_CHIPS, 2, r // N_CHIPS // 2, c))
        else:
            dims.append((l, r // 2, c // N_CHIPS))
            views.append(g.reshape(l, 2, r // 2, c))

    def half_shape(ax, d):
        return (d[0], N_CHIPS, d[1], d[2]) if ax == 0 else (d[0], d[1], N_CHIPS * d[2])

    def take(ref, ax, half):
        return ref.at[:, :, half] if ax == 0 else ref.at[:, half]

    def plan_a(in_refs, out_refs, x, y, c, count_only):
        if count_only:
            return len(grads)
        return [(take(src, ax, 1 - c), out, (x, y, 1 - c), out) for ax, src, out in zip(axes, in_refs, out_refs)]

    got = _comm("reduce_cores", views, [jax.ShapeDtypeStruct(half_shape(ax, d), F32) for ax, d in zip(axes, dims)],
                {}, plan_a)
    chip_sum = []
    for ax, d, v, g in zip(axes, dims, views, got):
        a = d[0] * N_CHIPS if ax == 0 else d[0]
        summed = add_half(v.reshape(a, 2, d[1], -1), g.reshape(a, d[1], -1), core, name="sum_cores")
        chip_sum.append(summed.reshape(half_shape(ax, d)))

    def block(ref, ax, which, width):
        return ref.at[:, which] if ax == 0 else ref.at[:, :, pl.ds(which * width, width)]

    def plan_b(in_refs, out_refs, x, y, c, count_only):
        if count_only:
            return 3 * len(grads)
        xfers = []
        for ax, d, src, out in zip(axes, dims, in_refs, out_refs):
            for k, (cx, cy) in enumerate(_other_chips(x, y)):
                xfers.append((block(src, ax, 2 * cx + cy, d[2]), out.at[k], (cx, cy, c), out.at[k]))
        return xfers

    parts = _comm("reduce_chips", chip_sum, [jax.ShapeDtypeStruct((3,) + d, F32) for d in dims], {}, plan_b)
    mine = [add_own_block(h, p, chip, ax, name="sum_chips") for h, p, ax in zip(chip_sum, parts, axes)]

    def plan_c(in_refs, out_refs, x, y, c, count_only):
        if count_only:
            return len(grads)
        return [(src, out, (x, y, 1 - c), out) for src, out in zip(in_refs, out_refs)]

    theirs = _comm("share_cores", mine, [jax.ShapeDtypeStruct(d, F32) for d in dims], {}, plan_c)
    return list(zip(mine, theirs))


def gather_slabs(vec, *, name):
    def plan(in_refs, out_refs, x, y, c, count_only):
        if count_only:
            return 8
        me = 4 * x + 2 * y + c
        xfers = [(in_refs[0], out_refs[0].at[me], None, None)]
        for k in range(1, 8):
            px, py, pc = x ^ (k >> 2), y ^ ((k >> 1) & 1), c ^ (k & 1)
            xfers.append((in_refs[0], out_refs[0].at[me], (px, py, pc), out_refs[0].at[4 * px + 2 * py + pc]))
        return xfers

    return _comm(name, [vec], [jax.ShapeDtypeStruct((8,) + vec.shape, F32)], {}, plan)[0]


def allreduce_small(vec):
    return sum_slots(gather_slabs(vec, name="gather_small"), name="sum_small")


def gather_conv_w(block):
    l, taps, c = block.shape
    flat = block.reshape(-1)
    slabs = gather_slabs(jnp.pad(flat, (0, -flat.size % 1024)).reshape(-1, 128), name="gather_conv_w")
    per_chip = slabs[0::2].reshape(N_CHIPS, -1)[:, :flat.size].reshape(N_CHIPS, l, taps, c)
    return per_chip.transpose(1, 2, 0, 3).reshape(l, taps, N_CHIPS * c)


def _rope_tables(positions):
    inv_freq = ROPE_THETA ** (-jnp.arange(0, ROPE_DIM, 2, dtype=F32) / ROPE_DIM)
    ang = positions.astype(F32)[:, None] * inv_freq
    ang = jnp.concatenate([ang, ang], axis=-1)
    pad = ((0, 0), (0, HEAD - ROPE_DIM))
    return jnp.pad(jnp.cos(ang), pad), jnp.pad(jnp.sin(ang), pad)


def _uq_layout(w, heads):
    ql = w.shape[0]
    w = w.reshape(ql, heads, HEAD + ROPE_DIM)
    rot = jnp.pad(w[:, :, HEAD:], ((0, 0), (0, 0), (0, HEAD - ROPE_DIM)))
    return jnp.concatenate([w[:, :, :HEAD].reshape(ql, heads * HEAD), rot.reshape(ql, heads * HEAD)], axis=1)


def _uq_layout_inv(dw, heads):
    ql = dw.shape[0]
    nope = dw[:, :heads * HEAD].reshape(ql, heads, HEAD)
    rot = dw[:, heads * HEAD:].reshape(ql, heads, HEAD)[:, :, :ROPE_DIM]
    return jnp.concatenate([nope, rot], axis=-1).reshape(ql, heads * (HEAD + ROPE_DIM))


def kernel(x, mem, positions, norm_mix, norm_mem_q, norm_mem_kv, norm_ffn, norm_final, mla_w_down, mla_q_norm, mla_w_uq, mla_kv_norm, mla_w_ukv, mla_w_o, sb_w_qkv, sb_w_o, mem_w_q, mem_w_kv, mem_w_o, ffn_w_in, ffn_conv_w, ffn_conv_b, ffn_w_out, loss_target, m_norm_mix, m_norm_mem_q, m_norm_mem_kv, m_norm_ffn, m_norm_final, m_mla_w_down, m_mla_q_norm, m_mla_w_uq, m_mla_kv_norm, m_mla_w_ukv, m_mla_w_o, m_sb_w_qkv, m_sb_w_o, m_mem_w_q, m_mem_w_kv, m_mem_w_o, m_ffn_w_in, m_ffn_conv_w, m_ffn_conv_b, m_ffn_w_out, v_norm_mix, v_norm_mem_q, v_norm_mem_kv, v_norm_ffn, v_norm_final, v_mla_w_down, v_mla_q_norm, v_mla_w_uq, v_mla_kv_norm, v_mla_w_ukv, v_mla_w_o, v_sb_w_qkv, v_sb_w_o, v_mem_w_q, v_mem_w_kv, v_mem_w_o, v_ffn_w_in, v_ffn_conv_w, v_ffn_conv_b, v_ffn_w_out):
    args = dict(locals())
    big = ["mla_w_down", "mla_w_uq", "mla_w_ukv", "mla_w_o", "sb_w_qkv", "sb_w_o", "mem_w_q", "mem_w_kv",
           "mem_w_o", "ffn_w_in", "ffn_w_out"]
    col_cut = {"mla_w_uq", "mla_w_ukv", "sb_w_qkv", "mem_w_o", "ffn_w_in"}
    axes = [1 if n in col_cut else 0 for n in big]
    small = ["norm_mix", "norm_mem_q", "norm_mem_kv", "norm_ffn", "norm_final", "mla_q_norm", "mla_kv_norm",
             "ffn_conv_b", "ffn_conv_w"]
    order = ["norm_mix", "norm_mem_q", "norm_mem_kv", "norm_ffn", "norm_final", "mla_w_down", "mla_q_norm",
             "mla_w_uq", "mla_kv_norm", "mla_w_ukv", "mla_w_o", "sb_w_qkv", "sb_w_o", "mem_w_q", "mem_w_kv",
             "mem_w_o", "ffn_w_in", "ffn_conv_w", "ffn_conv_b", "ffn_w_out"]

    xs, mems, target = x[0], mem[0], loss_target[0]
    s, d = xs.shape
    depth = norm_mix.shape[0]
    ql, kvl = mla_q_norm.shape[1], mla_kv_norm.shape[1]
    mla_heads = N_CHIPS * mla_w_uq.shape[2] // (HEAD + ROPE_DIM)
    sb_heads = N_CHIPS * sb_w_qkv.shape[2] // (3 * HEAD)
    mem_heads = mem_w_q.shape[2] // HEAD
    ff = N_CHIPS * ffn_w_out.shape[1]
    chip = 2 * lax.axis_index("x") + lax.axis_index("y")
    chip_op = jnp.reshape(chip, (1,)).astype(jnp.int32)
    core_op = jnp.reshape(lax.axis_index("c"), (1,)).astype(jnp.int32)

    full = dict(zip(big, gather_weights([args[n].astype(BF16) for n in big], axes)))
    w_uq = [_uq_layout(full["mla_w_uq"][j], mla_heads) for j in range(full["mla_w_uq"].shape[0])]
    cos, sin = _rope_tables(positions[0])
    _, sb_tk = _sb_tiles(s)
    tri = (jnp.arange(sb_tk)[:, None] > jnp.arange(sb_tk)[None, :]).astype(BF16)
    conv_w = gather_conv_w(ffn_conv_w)

    saved = []
    xa = xs
    for i in range(depth):
        j = i // 2
        lay = {}
        lay["xa"] = xa
        h1 = rmsnorm_fwd(xa, norm_mix[i], name="norm_fwd")
        lay["h1"] = h1
        if i % 2 == 0:
            down = mm(h1, full["mla_w_down"][j], out_dtype=F32, name="mm_down")
            cq = rmsnorm_fwd(down[:, :ql], mla_q_norm[j], name="norm_lora_fwd")
            ckv = rmsnorm_fwd(down[:, ql:ql + kvl], mla_kv_norm[j], name="norm_lora_fwd")
            kr_raw = jnp.pad(down[:, ql + kvl:], ((0, 0), (0, HEAD - ROPE_DIM)))
            kr = rope(kr_raw, cos, sin, col0=0, n=1, inverse=False, name="rope_k")
            qall = mm(cq, w_uq[j], name="mm_uq")
            qr = rope(qall, cos, sin, col0=mla_heads, n=mla_heads, inverse=False, name="rope_q")
            kv = mm(ckv, full["mla_w_ukv"][j], name="mm_ukv")
            o, lse = mla_fwd(qall, qr, kv, kr, mla_heads, name="mla_fwd")
            xb = mm(o, full["mla_w_o"][j], add=xa, out_dtype=F32, name="mm_out_res")
            lay.update(down=down, cq=cq, ckv=ckv, qall=qall, qr=qr, kv=kv, kr=kr, o=o, lse=lse)
        else:
            qkv = mm(h1, full["sb_w_qkv"][j], name="mm_qkv")
            o, cmat = sb_fwd(qkv, tri, sb_heads, name="sb_fwd")
            xb = mm(o, full["sb_w_o"][j], add=xa, out_dtype=F32, name="mm_out_res")
            lay.update(qkv=qkv, o=o, cmat=cmat)
        h2 = rmsnorm_fwd(xb, norm_mem_q[i], name="norm_fwd")
        hm = rmsnorm_fwd(mems, norm_mem_kv[i], name="norm_mem_fwd")
        qm = mm(h2, full["mem_w_q"][i], name="mm_mem_q")
        kvm = mm(hm, full["mem_w_kv"][i], name="mm_mem_kv")
        om = mem_fwd(qm, kvm, mem_heads, name="mem_fwd")
        xc = mm(om, full["mem_w_o"][i], add=xb, out_dtype=F32, name="mm_mem_out_res")
        h3 = rmsnorm_fwd(xc, norm_ffn[i], name="norm_fwd")
        u = mm(h3, full["ffn_w_in"][i], name="mm_ffn_in")
        act = conv_gate_fwd(u, conv_w[i], ffn_conv_b[i][None, :], name="conv_gate_fwd")
        xd = mm(act, full["ffn_w_out"][i], add=xc, out_dtype=F32, name="mm_ffn_out_res")
        lay.update(xb=xb, h2=h2, hm=hm, qm=qm, kvm=kvm, om=om, xc=xc, h3=h3, u=u, act=act)
        saved.append(lay)
        xa = xd

    dx, g_final, loss_part = final_loss(xa, norm_final, target, name="final_loss")

    gw = {n: [None] * full[n].shape[0] for n in big}
    g_small = {"norm_mix": [None] * depth, "norm_mem_q": [None] * depth, "norm_mem_kv": [None] * depth,
               "norm_ffn": [None] * depth, "mla_q_norm": [None] * (depth - depth // 2),
               "mla_kv_norm": [None] * (depth - depth // 2), "conv": [None] * depth}
    for i in reversed(range(depth)):
        j = i // 2
        lay = saved[i]
        gw["ffn_w_out"][i] = mm(lay["act"], dx, mode="tn", out_dtype=F32, name="mm_ffn_out_wgrad")
        da = mm(dx, full["ffn_w_out"][i], mode="nt", name="mm_ffn_out_dgrad")
        dcg, dcu, g_small["conv"][i] = conv_gate_bwd(lay["u"], da, conv_w[i], ffn_conv_b[i][None, :], name="conv_gate_bwd")
        du = jnp.concatenate([conv_transpose(dcg, conv_w[i][:, :ff], name="conv_transpose"),
                              conv_transpose(dcu, conv_w[i][:, ff:], name="conv_transpose")], axis=1)
        gw["ffn_w_in"][i] = mm(lay["h3"], du, mode="tn", out_dtype=F32, name="mm_ffn_in_wgrad")
        dh3 = mm(du, full["ffn_w_in"][i], mode="nt", name="mm_ffn_in_dgrad")
        dx, g_small["norm_ffn"][i] = rmsnorm_bwd(dh3, lay["xc"], norm_ffn[i], dx, name="norm_bwd")
        gw["mem_w_o"][i] = mm(lay["om"], dx, mode="tn", out_dtype=F32, name="mm_mem_out_wgrad")
        dom = mm(dx, full["mem_w_o"][i], mode="nt", name="mm_mem_out_dgrad")
        dqm, dkm, dvm = mem_bwd(lay["qm"], lay["kvm"], dom, mem_heads, name="mem_bwd")
        dkvm = jnp.concatenate([dkm, dvm], axis=1)
        gw["mem_w_q"][i] = mm(lay["h2"], dqm, mode="tn", out_dtype=F32, name="mm_mem_q_wgrad")
        gw["mem_w_kv"][i] = mm(lay["hm"], dkvm, mode="tn", out_dtype=F32, name="mm_mem_kv_wgrad")
        dh2 = mm(dqm, full["mem_w_q"][i], mode="nt", name="mm_mem_q_dgrad")
        dhm = mm(dkvm, full["mem_w_kv"][i], mode="nt", name="mm_mem_kv_dgrad")
        _, g_small["norm_mem_kv"][i] = rmsnorm_bwd(dhm, mems, norm_mem_kv[i], name="norm_mem_bwd")
        dx, g_small["norm_mem_q"][i] = rmsnorm_bwd(dh2, lay["xb"], norm_mem_q[i], dx, name="norm_bwd")
        if i % 2 == 0:
            gw["mla_w_o"][j] = mm(lay["o"], dx, mode="tn", out_dtype=F32, name="mm_out_wgrad")
            do = mm(dx, full["mla_w_o"][j], mode="nt", name="mm_out_dgrad")
            dqn, dqr, dkv, dkr_heads = mla_bwd(lay["qall"], lay["qr"], lay["kv"], lay["kr"], lay["o"], do, lay["lse"],
                                               mla_heads, name="mla_bwd")
            dkr = sum_lane_tiles(dkr_heads, mla_heads, name="sum_heads")
            dqall = jnp.concatenate([dqn, rope(dqr, cos, sin, col0=0, n=mla_heads, inverse=True, name="rope_q_bwd")], axis=1)
            gw["mla_w_ukv"][j] = mm(lay["ckv"], dkv, mode="tn", out_dtype=F32, name="mm_ukv_wgrad")
            dckv = mm(dkv, full["mla_w_ukv"][j], mode="nt", name="mm_ukv_dgrad")
            gw["mla_w_uq"][j] = _uq_layout_inv(mm(lay["cq"], dqall, mode="tn", out_dtype=F32, name="mm_uq_wgrad"), mla_heads)
            dcq = mm(dqall, w_uq[j], mode="nt", name="mm_uq_dgrad")
            down = lay["down"]
            d_q, g_small["mla_q_norm"][j] = rmsnorm_bwd(dcq, down[:, :ql], mla_q_norm[j], name="norm_lora_bwd")
            d_kv, g_small["mla_kv_norm"][j] = rmsnorm_bwd(dckv, down[:, ql:ql + kvl], mla_kv_norm[j], name="norm_lora_bwd")
            d_kr = rope(dkr, cos, sin, col0=0, n=1, inverse=True, name="rope_k_bwd")[:, :ROPE_DIM]
            ddown = jnp.concatenate([d_q.astype(BF16), d_kv.astype(BF16), d_kr], axis=1)
            gw["mla_w_down"][j] = mm(lay["h1"], ddown, mode="tn", out_dtype=F32, name="mm_down_wgrad")
            dh1 = mm(ddown, full["mla_w_down"][j], mode="nt", name="mm_down_dgrad")
        else:
            gw["sb_w_o"][j] = mm(lay["o"], dx, mode="tn", out_dtype=F32, name="mm_out_wgrad")
            do = mm(dx, full["sb_w_o"][j], mode="nt", name="mm_out_dgrad")
            dq, dk, dv = sb_bwd(lay["qkv"], do, lay["cmat"], tri, sb_heads, name="sb_bwd")
            dqkv = jnp.concatenate([dq, dk, dv], axis=1)
            gw["sb_w_qkv"][j] = mm(lay["h1"], dqkv, mode="tn", out_dtype=F32, name="mm_qkv_wgrad")
            dh1 = mm(dqkv, full["sb_w_qkv"][j], mode="nt", name="mm_qkv_dgrad")
        dx, g_small["norm_mix"][i] = rmsnorm_bwd(dh1, lay["xa"], norm_mix[i], dx, name="norm_bwd")

    halves = dict(zip(big, reduce_scatter_grads([jnp.stack(gw[n]) for n in big], axes, core_op, chip_op)))
    conv = jnp.stack(g_small["conv"])
    parts = [jnp.concatenate(g_small[n], axis=0) for n in ("norm_mix", "norm_mem_q", "norm_mem_kv", "norm_ffn")]
    parts += [g_final, jnp.concatenate(g_small["mla_q_norm"], axis=0), jnp.concatenate(g_small["mla_kv_norm"], axis=0),
              conv[:, 3, :], conv[:, :3, :], loss_part[:1, :1]]
    sizes = [p.size for p in parts]
    packed = jnp.concatenate([p.reshape(-1) for p in parts])
    packed = jnp.pad(packed, (0, -packed.size % 1024)).reshape(-1, 128)
    total = allreduce_small(packed).reshape(-1)
    g_rep, at = {}, 0
    for n, p, size in zip(small + ["loss"], parts, sizes):
        g_rep[n] = total[at:at + size].reshape(p.shape)
        at += size
    loss = g_rep.pop("loss").reshape(())
    g_rep["norm_final"] = g_rep["norm_final"].reshape(norm_final.shape)
    width = ffn_conv_w.shape[2]
    g_rep["ffn_conv_w"] = lax.dynamic_slice_in_dim(g_rep["ffn_conv_w"], chip * width, width, axis=2)

    grads, delta, new_m, new_v = {}, {}, {}, {}
    for n in order:
        if n in halves:
            mine, theirs = halves[n]
            grads[n], delta[n], new_m[n], new_v[n] = adamw_halves(
                args[n], mine, theirs, args["m_" + n], args["v_" + n], core_op, name="adamw_big")
        else:
            grads[n] = g_rep[n]
            delta[n], new_m[n], new_v[n] = adamw(args[n], g_rep[n], args["m_" + n], args["v_" + n], name="adamw")
    return (loss, dx[None], *[grads[n] for n in order], *[delta[n] for n in order],
            *[new_m[n] for n in order], *[new_v[n] for n in order])
```

```python
import math

import jax
import jax.numpy as jnp
import numpy as np
from jax import lax
from jax.experimental import pallas as pl
from jax.experimental.pallas import tpu as pltpu

F32 = jnp.float32
BF16 = jnp.bfloat16
MESH = pl.DeviceIdType.MESH

EPS = 1e-6
CHUNK_SHIFT = 6
HEAD = 128
ROPE_DIM = 64
ROPE_THETA = 10000.0
N_CHIPS = 4
ADAM_LR, ADAM_B1, ADAM_B2, ADAM_EPS, ADAM_WD, ADAM_STEP = 0.001, 0.9, 0.999, 1e-08, 0.01, 10

VMEM_LIMIT_BYTES = 48 * 1024 * 1024
MM_TM, MM_TN, MM_TK = (1024, 512, 256, 128), (1024, 1408, 512, 256, 128), (1024, 512, 256, 128)
MM_WHOLE_K, MM_WHOLE_TM = 2048, (512, 256, 128)
ROW_TILE = (256, 128, 64, 32, 16, 8)
ATT_T = 512
SB_TQ, SB_TK = 512, 256
SB_HEADS_PER_STEP = 2
MEM_TQ = 512
CONV_TM, CONV_TN = 512, (512, 256, 128)
HALO = 16


def _pick(dim, prefs):
    for p in prefs:
        if dim % p == 0:
            return p
    return dim


def _dot(a, b, ca, cb):
    return lax.dot_general(a, b, (((ca,), (cb,)), ((), ())), preferred_element_type=F32)


def _params(sem):
    return pltpu.CompilerParams(dimension_semantics=sem, vmem_limit_bytes=VMEM_LIMIT_BYTES)


def _tables(pairs):
    arr = np.asarray(pairs, dtype=np.int32)
    return jnp.asarray(arr[:, 0]), jnp.asarray(arr[:, 1])


def mm(a, b, *, mode="nn", add=None, out_dtype=BF16, name):
    if mode == "nn":
        (m, k), (k2, n) = a.shape, b.shape
    elif mode == "nt":
        (m, k), (n, k2) = a.shape, b.shape
    else:
        (k, m), (k2, n) = a.shape, b.shape
    assert k == k2, (a.shape, b.shape, mode)
    ca, cb = {"nn": (1, 0), "nt": (1, 1), "tn": (0, 0)}[mode]
    if k <= MM_WHOLE_K:
        return _mm_whole_k(a, b, add, (m, n, k), (ca, cb), mode, out_dtype, name)
    tm, tn, tk = _pick(m, MM_TM), _pick(n, MM_TN), _pick(k, MM_TK)
    nk = k // tk

    def body(a_ref, b_ref, *rest):
        if add is None:
            o_ref, acc_ref = rest
        else:
            add_ref, o_ref, acc_ref = rest
        kk = pl.program_id(2)

        @pl.when(kk == 0)
        def _():
            acc_ref[...] = jnp.zeros_like(acc_ref)

        acc_ref[...] += _dot(a_ref[...].astype(BF16), b_ref[...].astype(BF16), ca, cb)

        @pl.when(kk == nk - 1)
        def _():
            r = acc_ref[...]
            if add is not None:
                r = r + add_ref[...]
            o_ref[...] = r.astype(o_ref.dtype)

    if mode == "tn":
        a_spec = pl.BlockSpec((tk, tm), lambda i, j, kk: (kk, i))
    else:
        a_spec = pl.BlockSpec((tm, tk), lambda i, j, kk: (i, kk))
    if mode == "nt":
        b_spec = pl.BlockSpec((tn, tk), lambda i, j, kk: (j, kk))
    else:
        b_spec = pl.BlockSpec((tk, tn), lambda i, j, kk: (kk, j))
    o_spec = pl.BlockSpec((tm, tn), lambda i, j, kk: (i, j))
    in_specs, args = [a_spec, b_spec], [a, b]
    if add is not None:
        in_specs.append(o_spec)
        args.append(add)
    return pl.pallas_call(
        body, name=name, grid=(m // tm, n // tn, nk), in_specs=in_specs, out_specs=o_spec,
        out_shape=jax.ShapeDtypeStruct((m, n), out_dtype), scratch_shapes=[pltpu.VMEM((tm, tn), F32)],
        compiler_params=_params(("parallel", "parallel", "arbitrary")),
    )(*args)


def _mm_whole_k(a, b, add, mnk, contract, mode, out_dtype, name):
    m, n, k = mnk
    tm, tn = _pick(m, MM_WHOLE_TM), _pick(n, MM_TN)

    def body(a_ref, b_ref, *rest):
        r = _dot(a_ref[...].astype(BF16), b_ref[...].astype(BF16), *contract)
        if add is not None:
            r = r + rest[0][...]
        rest[-1][...] = r.astype(rest[-1].dtype)

    a_spec = pl.BlockSpec((k, tm), lambda j, i: (0, i)) if mode == "tn" else pl.BlockSpec((tm, k), lambda j, i: (i, 0))
    b_spec = pl.BlockSpec((tn, k), lambda j, i: (j, 0)) if mode == "nt" else pl.BlockSpec((k, tn), lambda j, i: (0, j))
    o_spec = pl.BlockSpec((tm, tn), lambda j, i: (i, j))
    in_specs, args = [a_spec, b_spec], [a, b]
    if add is not None:
        in_specs.append(o_spec)
        args.append(add)
    return pl.pallas_call(
        body, name=name, grid=(n // tn, m // tm), in_specs=in_specs, out_specs=o_spec,
        out_shape=jax.ShapeDtypeStruct((m, n), out_dtype), compiler_params=_params(("parallel", "parallel")),
    )(*args)


def rmsnorm_fwd(x, g, *, name):
    m, d = x.shape
    tm = _pick(m, ROW_TILE)

    def body(x_ref, g_ref, o_ref):
        xv = x_ref[...]
        r = lax.rsqrt(jnp.mean(xv * xv, axis=-1, keepdims=True) + EPS)
        o_ref[...] = (xv * r * g_ref[...]).astype(o_ref.dtype)

    return pl.pallas_call(
        body, name=name, grid=(m // tm,),
        in_specs=[pl.BlockSpec((tm, d), lambda i: (i, 0)), pl.BlockSpec((1, d), lambda i: (0, 0))],
        out_specs=pl.BlockSpec((tm, d), lambda i: (i, 0)), out_shape=jax.ShapeDtypeStruct((m, d), BF16),
        compiler_params=_params(("parallel",)),
    )(x, g.reshape(1, d))


def rmsnorm_bwd(dh, x, g, res=None, *, name):
    m, d = x.shape
    tm = _pick(m, ROW_TILE)

    def body(dh_ref, x_ref, g_ref, *rest):
        if res is None:
            dx_ref, dxb_ref, dg_ref = rest
        else:
            res_ref, dx_ref, dxb_ref, dg_ref = rest

        @pl.when(pl.program_id(0) == 0)
        def _():
            dg_ref[...] = jnp.zeros_like(dg_ref)

        xv = x_ref[...]
        dhv = dh_ref[...].astype(F32)
        r = lax.rsqrt(jnp.mean(xv * xv, axis=-1, keepdims=True) + EPS)
        y = xv * r
        dhg = dhv * g_ref[...]
        dx = r * (dhg - y * jnp.mean(dhg * y, axis=-1, keepdims=True))
        if res is not None:
            dx = dx + res_ref[...]
        dx_ref[...] = dx
        dxb_ref[...] = dx.astype(BF16)
        dg_ref[...] += jnp.sum(dhv * y, axis=0, keepdims=True)

    row = pl.BlockSpec((tm, d), lambda i: (i, 0))
    vec = pl.BlockSpec((1, d), lambda i: (0, 0))
    in_specs, args = [row, row, vec], [dh, x, g.reshape(1, d)]
    if res is not None:
        in_specs.append(row)
        args.append(res)
    return pl.pallas_call(
        body, name=name, grid=(m // tm,), in_specs=in_specs, out_specs=[row, row, vec],
        out_shape=[jax.ShapeDtypeStruct((m, d), F32), jax.ShapeDtypeStruct((m, d), BF16),
                   jax.ShapeDtypeStruct((1, d), F32)],
        compiler_params=_params(("arbitrary",)),
    )(*args)


def final_loss(x, g, target, *, name):
    m, d = x.shape
    tm = _pick(m, ROW_TILE)

    def body(x_ref, g_ref, t_ref, dx_ref, dxb_ref, dg_ref, loss_ref):
        @pl.when(pl.program_id(0) == 0)
        def _():
            dg_ref[...] = jnp.zeros_like(dg_ref)
            loss_ref[...] = jnp.zeros_like(loss_ref)

        xv = x_ref[...]
        gv = g_ref[...]
        r = lax.rsqrt(jnp.mean(xv * xv, axis=-1, keepdims=True) + EPS)
        y = xv * r
        err = y * gv - t_ref[...]
        loss_ref[...] += 0.5 * jnp.sum(jnp.mean(err * err, axis=-1, keepdims=True))
        dy = err * (1.0 / d)
        dyg = dy * gv
        dx = r * (dyg - y * jnp.mean(dyg * y, axis=-1, keepdims=True))
        dx_ref[...] = dx
        dxb_ref[...] = dx.astype(BF16)
        dg_ref[...] += jnp.sum(dy * y, axis=0, keepdims=True)

    row = pl.BlockSpec((tm, d), lambda i: (i, 0))
    vec = pl.BlockSpec((1, d), lambda i: (0, 0))
    return pl.pallas_call(
        body, name=name, grid=(m // tm,), in_specs=[row, vec, row],
        out_specs=[row, row, vec, pl.BlockSpec((8, 128), lambda i: (0, 0))],
        out_shape=[jax.ShapeDtypeStruct((m, d), F32), jax.ShapeDtypeStruct((m, d), BF16),
                   jax.ShapeDtypeStruct((1, d), F32), jax.ShapeDtypeStruct((8, 128), F32)],
        compiler_params=_params(("arbitrary",)),
    )(x, g.reshape(1, d), target)


def rope(xin, cos, sin, *, col0, n, inverse, name):
    s = xin.shape[0]
    tm = _pick(s, ROW_TILE)
    half = ROPE_DIM // 2

    def body(x_ref, c_ref, s_ref, o_ref):
        xv = x_ref[...].astype(F32)
        lane = lax.broadcasted_iota(jnp.int32, xv.shape, 1)
        rot = jnp.where(lane < half, -pltpu.roll(xv, HEAD - half, 1), pltpu.roll(xv, half, 1))
        sv = s_ref[...]
        if inverse:
            sv = -sv
        o_ref[...] = (xv * c_ref[...] + rot * sv).astype(o_ref.dtype)

    tab = pl.BlockSpec((tm, HEAD), lambda i, j: (i, 0))
    return pl.pallas_call(
        body, name=name, grid=(s // tm, n),
        in_specs=[pl.BlockSpec((tm, HEAD), lambda i, j: (i, j + col0)), tab, tab],
        out_specs=pl.BlockSpec((tm, HEAD), lambda i, j: (i, j)),
        out_shape=jax.ShapeDtypeStruct((s, n * HEAD), BF16),
        compiler_params=_params(("parallel", "parallel")),
    )(xin, cos, sin)


def sum_lane_tiles(xin, n, *, name):
    s = xin.shape[0]
    tm = _pick(s, ROW_TILE)

    def body(x_ref, o_ref):
        acc = x_ref[:, :HEAD]
        for k in range(1, n):
            acc = acc + x_ref[:, k * HEAD:(k + 1) * HEAD]
        o_ref[...] = acc

    return pl.pallas_call(
        body, name=name, grid=(s // tm,), in_specs=[pl.BlockSpec((tm, n * HEAD), lambda i: (i, 0))],
        out_specs=pl.BlockSpec((tm, HEAD), lambda i: (i, 0)), out_shape=jax.ShapeDtypeStruct((s, HEAD), F32),
        compiler_params=_params(("parallel",)),
    )(xin)


def _mla_scores(qn, qr, kn, kr, scale, diagonal):
    sc = (_dot(qn, kn, 1, 1) + _dot(qr, kr, 1, 1)) * scale
    if not diagonal:
        return sc
    qchunk = jnp.right_shift(lax.broadcasted_iota(jnp.int32, sc.shape, 0), CHUNK_SHIFT)
    kchunk = jnp.right_shift(lax.broadcasted_iota(jnp.int32, sc.shape, 1), CHUNK_SHIFT)
    return jnp.where(kchunk <= qchunk, sc, -jnp.inf)


def mla_fwd(qall, qr, kv, kr, heads, *, name):
    s = qr.shape[0]
    t = min(ATT_T, s)
    nq = s // t
    scale = 1.0 / math.sqrt(HEAD + ROPE_DIM)
    pairs = [(qi, kj) for qi in range(nq) for kj in range(qi + 1)]
    qtab, ktab = _tables(pairs)

    def body(qt_ref, kt_ref, qn_ref, qr_ref, kv_ref, kr_ref, o_ref, lse_ref, m_ref, l_ref, acc_ref):
        st = pl.program_id(1)
        qi, kj = qt_ref[st], kt_ref[st]

        @pl.when(kj == 0)
        def _():
            m_ref[...] = jnp.full_like(m_ref, -jnp.inf)
            l_ref[...] = jnp.zeros_like(l_ref)
            acc_ref[...] = jnp.zeros_like(acc_ref)

        def step(diagonal):
            sc = _mla_scores(qn_ref[...], qr_ref[...], kv_ref[:, :HEAD], kr_ref[...], scale, diagonal)
            m_prev = m_ref[...]
            m_new = jnp.maximum(m_prev, jnp.max(sc, axis=-1, keepdims=True))
            p = jnp.exp(sc - m_new)
            alpha = jnp.exp(m_prev - m_new)
            l_ref[...] = alpha * l_ref[...] + jnp.sum(p, axis=-1, keepdims=True)
            acc_ref[...] = alpha * acc_ref[...] + _dot(p.astype(BF16), kv_ref[:, HEAD:], 1, 0)
            m_ref[...] = m_new

        @pl.when(kj < qi)
        def _():
            step(False)

        @pl.when(kj == qi)
        def _():
            step(True)
            o_ref[...] = (acc_ref[...] / l_ref[...]).astype(o_ref.dtype)
            lse_ref[...] = jnp.broadcast_to(m_ref[...] + jnp.log(l_ref[...]), lse_ref.shape)

    qspec = pl.BlockSpec((t, HEAD), lambda h, st, qt, kt: (qt[st], h))
    grid_spec = pltpu.PrefetchScalarGridSpec(
        num_scalar_prefetch=2, grid=(heads, len(pairs)),
        in_specs=[qspec, qspec,
                  pl.BlockSpec((t, 2 * HEAD), lambda h, st, qt, kt: (kt[st], h)),
                  pl.BlockSpec((t, HEAD), lambda h, st, qt, kt: (kt[st], 0))],
        out_specs=[qspec, qspec],
        scratch_shapes=[pltpu.VMEM((t, 1), F32), pltpu.VMEM((t, 1), F32), pltpu.VMEM((t, HEAD), F32)])
    return pl.pallas_call(
        body, name=name, grid_spec=grid_spec,
        out_shape=[jax.ShapeDtypeStruct((s, heads * HEAD), BF16), jax.ShapeDtypeStruct((s, heads * HEAD), F32)],
        compiler_params=_params(("parallel", "arbitrary")),
    )(qtab, ktab, qall, qr, kv, kr)


def mla_bwd(qall, qr, kv, kr, o, do, lse, heads, *, name):
    s = qr.shape[0]
    t = min(ATT_T, s)
    nq = s // t
    scale = 1.0 / math.sqrt(HEAD + ROPE_DIM)
    pairs = [(kj, qi) for kj in range(nq) for qi in range(kj, nq)]
    ktab, qtab = _tables(pairs)
    last = len(pairs) - 1

    def body(kt_ref, qt_ref, qn_ref, qr_ref, kv_ref, kr_ref, o_ref, do_ref, lse_ref,
             dqn_ref, dqr_ref, dkv_ref, dkr_ref, fn_ref, fr_ref, akn_ref, av_ref, akr_ref):
        st = pl.program_id(1)
        kj, qi = kt_ref[st], qt_ref[st]

        @pl.when(st == 0)
        def _():
            fn_ref[...] = jnp.zeros_like(fn_ref)
            fr_ref[...] = jnp.zeros_like(fr_ref)

        @pl.when(qi == kj)
        def _():
            akn_ref[...] = jnp.zeros_like(akn_ref)
            av_ref[...] = jnp.zeros_like(av_ref)
            akr_ref[...] = jnp.zeros_like(akr_ref)

        def step(diagonal):
            rows = pl.ds(pl.multiple_of(qi * t, t), t)
            qn, qrv, dov = qn_ref[...], qr_ref[...], do_ref[...]
            kn, krv = kv_ref[:, :HEAD], kr_ref[...]
            sc = _mla_scores(qn, qrv, kn, krv, scale, diagonal)
            p = jnp.exp(sc - lse_ref[:, :1])
            delta = jnp.sum(dov.astype(F32) * o_ref[...].astype(F32), axis=-1, keepdims=True)
            dp = _dot(dov, kv_ref[:, HEAD:], 1, 1)
            ds = (p * (dp - delta) * scale).astype(BF16)
            av_ref[...] += _dot(p.astype(BF16), dov, 0, 0)
            akn_ref[...] += _dot(ds, qn, 0, 0)
            akr_ref[...] += _dot(ds, qrv, 0, 0)
            fn_ref[rows, :] += _dot(ds, kn, 1, 0)
            fr_ref[rows, :] += _dot(ds, krv, 1, 0)

        @pl.when(qi == kj)
        def _():
            step(True)

        @pl.when(qi > kj)
        def _():
            step(False)

        @pl.when(qi == nq - 1)
        def _():
            dkv_ref[:, :HEAD] = akn_ref[...].astype(dkv_ref.dtype)
            dkv_ref[:, HEAD:] = av_ref[...].astype(dkv_ref.dtype)
            dkr_ref[...] = akr_ref[...]

        @pl.when(st == last)
        def _():
            dqn_ref[...] = fn_ref[...].astype(dqn_ref.dtype)
            dqr_ref[...] = fr_ref[...].astype(dqr_ref.dtype)

    qspec = pl.BlockSpec((t, HEAD), lambda h, st, kt, qt: (qt[st], h))
    kvspec = pl.BlockSpec((t, 2 * HEAD), lambda h, st, kt, qt: (kt[st], h))
    krspec = pl.BlockSpec((t, HEAD), lambda h, st, kt, qt: (kt[st], 0))
    headspec = pl.BlockSpec((s, HEAD), lambda h, st, kt, qt: (0, h))
    grid_spec = pltpu.PrefetchScalarGridSpec(
        num_scalar_prefetch=2, grid=(heads, len(pairs)),
        in_specs=[qspec, qspec, kvspec, krspec, qspec, qspec, qspec],
        out_specs=[headspec, headspec, kvspec, pl.BlockSpec((t, HEAD), lambda h, st, kt, qt: (kt[st], h))],
        scratch_shapes=[pltpu.VMEM((s, HEAD), F32), pltpu.VMEM((s, HEAD), F32), pltpu.VMEM((t, HEAD), F32),
                        pltpu.VMEM((t, HEAD), F32), pltpu.VMEM((t, HEAD), F32)])
    return pl.pallas_call(
        body, name=name, grid_spec=grid_spec,
        out_shape=[jax.ShapeDtypeStruct((s, heads * HEAD), BF16), jax.ShapeDtypeStruct((s, heads * HEAD), BF16),
                   jax.ShapeDtypeStruct((s, heads * 2 * HEAD), BF16), jax.ShapeDtypeStruct((s, heads * HEAD), F32)],
        compiler_params=_params(("parallel", "arbitrary")),
    )(ktab, qtab, qall, qr, kv, kr, o, do, lse)


def _split_dot(val, tri, cb):
    hi = val.astype(BF16)
    lo = (val - hi.astype(F32)).astype(BF16)
    return _dot(hi, tri, 1, cb) + _dot(lo, tri, 1, cb)


def _sb_logs(q, k, offset, scale, masked):
    z = _dot(q, k, 1, 1) * scale
    sp = jnp.log(1.0 + jnp.exp(-jnp.abs(z)))
    ls = jnp.minimum(z, 0.0) - sp
    lk = ls - z
    if not masked:
        return None, ls, lk
    strict = (lax.broadcasted_iota(jnp.int32, z.shape, 1) + offset) < lax.broadcasted_iota(jnp.int32, z.shape, 0)
    return strict, ls, jnp.where(strict, lk, 0.0)


def _lane_pick(blk, idx):
    lane = lax.broadcasted_iota(jnp.int32, blk.shape, 1)
    return jnp.sum(jnp.where(lane == idx, blk, 0.0), axis=-1, keepdims=True)


def _lane_put(blk, idx, col):
    lane = lax.broadcasted_iota(jnp.int32, blk.shape, 1)
    return jnp.where(lane == idx, col, blk)


def _sb_tiles(s):
    tq = min(SB_TQ, s)
    tk = min(SB_TK, tq)
    assert s // tk <= HEAD
    return tq, tk


def _sb_groups(heads):
    hp = SB_HEADS_PER_STEP if heads % SB_HEADS_PER_STEP == 0 else 1
    return hp, heads // hp, hp * HEAD


def sb_fwd(qkv, tri, heads, *, name):
    s = qkv.shape[0]
    tq, tk = _sb_tiles(s)
    nq, ratio = s // tq, tq // tk
    scale = 1.0 / math.sqrt(HEAD)
    pairs = [(qi, kj) for qi in range(nq) for kj in range((qi + 1) * ratio - 1, -1, -1)]
    qtab, ktab = _tables(pairs)
    hp, groups, wide = _sb_groups(heads)

    def body(qt_ref, kt_ref, q_ref, k_ref, v_ref, tri_ref, o_ref, c_ref, carry_ref, acc_ref):
        st = pl.program_id(1)
        qi, kj = qt_ref[st], kt_ref[st]

        @pl.when(kj == (qi + 1) * ratio - 1)
        def _():
            carry_ref[...] = jnp.zeros_like(carry_ref)
            acc_ref[...] = jnp.zeros_like(acc_ref)
            c_ref[...] = jnp.zeros_like(c_ref)

        def step(masked):
            for hh in range(hp):
                cols = slice(hh * HEAD, (hh + 1) * HEAD)
                strict, ls, lk = _sb_logs(q_ref[:, cols], k_ref[:, cols], kj * tk - qi * tq, scale, masked)
                carry = carry_ref[hh]
                a = jnp.exp(ls + _split_dot(lk, tri_ref[...], 0) + carry)
                if masked:
                    a = jnp.where(strict, a, 0.0)
                acc_ref[:, cols] += _dot(a.astype(BF16), v_ref[:, cols], 1, 0)
                c_ref[:, cols] = _lane_put(c_ref[:, cols], kj, carry)
                carry_ref[hh] = carry + jnp.sum(lk, axis=-1, keepdims=True)

        @pl.when((kj + 1) * tk > qi * tq)
        def _():
            step(True)

        @pl.when((kj + 1) * tk <= qi * tq)
        def _():
            step(False)

        @pl.when(kj == 0)
        def _():
            o_ref[...] = acc_ref[...].astype(o_ref.dtype)

    qspec = pl.BlockSpec((tq, wide), lambda h, st, qt, kt: (qt[st], h))
    grid_spec = pltpu.PrefetchScalarGridSpec(
        num_scalar_prefetch=2, grid=(groups, len(pairs)),
        in_specs=[qspec,
                  pl.BlockSpec((tk, wide), lambda h, st, qt, kt: (kt[st], groups + h)),
                  pl.BlockSpec((tk, wide), lambda h, st, qt, kt: (kt[st], 2 * groups + h)),
                  pl.BlockSpec((tk, tk), lambda h, st, qt, kt: (0, 0))],
        out_specs=[qspec, qspec],
        scratch_shapes=[pltpu.VMEM((hp, tq, 1), F32), pltpu.VMEM((tq, wide), F32)])
    return pl.pallas_call(
        body, name=name, grid_spec=grid_spec,
        out_shape=[jax.ShapeDtypeStruct((s, heads * HEAD), BF16), jax.ShapeDtypeStruct((s, heads * HEAD), F32)],
        compiler_params=_params(("parallel", "arbitrary")),
    )(qtab, ktab, qkv, qkv, qkv, tri)


def sb_bwd(qkv, do, cmat, tri, heads, *, name):
    s = qkv.shape[0]
    tq, tk = _sb_tiles(s)
    nq, nkb, ratio = s // tq, s // tk, tq // tk
    scale = 1.0 / math.sqrt(HEAD)
    pairs = [(kj, qi) for kj in range(nkb) for qi in range(kj // ratio, nq)]
    ktab, qtab = _tables(pairs)
    last = len(pairs) - 1
    hp, groups, wide = _sb_groups(heads)

    def body(kt_ref, qt_ref, q_ref, k_ref, v_ref, do_ref, c_ref, tri_ref, dq_ref, dk_ref, dv_ref,
             dqf_ref, gsum_ref, ak_ref, av_ref):
        st = pl.program_id(1)
        kj, qi = kt_ref[st], qt_ref[st]

        @pl.when(st == 0)
        def _():
            dqf_ref[...] = jnp.zeros_like(dqf_ref)
            gsum_ref[...] = jnp.zeros_like(gsum_ref)

        @pl.when(qi == kj // ratio)
        def _():
            ak_ref[...] = jnp.zeros_like(ak_ref)
            av_ref[...] = jnp.zeros_like(av_ref)

        def step(masked):
            rows = pl.ds(pl.multiple_of(qi * tq, tq), tq)
            tri_v = tri_ref[...]
            for hh in range(hp):
                cols = slice(hh * HEAD, (hh + 1) * HEAD)
                qv, kblk, dov = q_ref[:, cols], k_ref[:, cols], do_ref[:, cols]
                strict, ls, lk = _sb_logs(qv, kblk, kj * tk - qi * tq, scale, masked)
                a = jnp.exp(ls + _split_dot(lk, tri_v, 0) + _lane_pick(c_ref[:, cols], kj))
                if masked:
                    a = jnp.where(strict, a, 0.0)
                g = _dot(dov, v_ref[:, cols], 1, 1) * a
                before_all = gsum_ref[hh, rows, :]
                before = _split_dot(g, tri_v, 1) + before_all[:, :1]
                beta = jnp.exp(ls)
                dz = g * (1.0 - beta) - before * beta
                if masked:
                    dz = jnp.where(strict, dz, 0.0)
                dzb = dz.astype(BF16)
                av_ref[:, cols] += _dot(a.astype(BF16), dov, 0, 0)
                ak_ref[:, cols] += _dot(dzb, qv, 0, 0)
                dqf_ref[rows, cols] += _dot(dzb, kblk, 1, 0)
                gsum_ref[hh, rows, :] = before_all + jnp.sum(g, axis=-1, keepdims=True)

        @pl.when((kj + 1) * tk > qi * tq)
        def _():
            step(True)

        @pl.when((kj + 1) * tk <= qi * tq)
        def _():
            step(False)

        @pl.when(qi == nq - 1)
        def _():
            dk_ref[...] = (ak_ref[...] * scale).astype(dk_ref.dtype)
            dv_ref[...] = av_ref[...].astype(dv_ref.dtype)

        @pl.when(st == last)
        def _():
            dq_ref[...] = (dqf_ref[...] * scale).astype(dq_ref.dtype)

    qspec = pl.BlockSpec((tq, wide), lambda h, st, kt, qt: (qt[st], h))
    ospec = pl.BlockSpec((tk, wide), lambda h, st, kt, qt: (kt[st], h))
    grid_spec = pltpu.PrefetchScalarGridSpec(
        num_scalar_prefetch=2, grid=(groups, len(pairs)),
        in_specs=[qspec,
                  pl.BlockSpec((tk, wide), lambda h, st, kt, qt: (kt[st], groups + h)),
                  pl.BlockSpec((tk, wide), lambda h, st, kt, qt: (kt[st], 2 * groups + h)),
                  qspec, qspec, pl.BlockSpec((tk, tk), lambda h, st, kt, qt: (0, 0))],
        out_specs=[pl.BlockSpec((s, wide), lambda h, st, kt, qt: (0, h)), ospec, ospec],
        scratch_shapes=[pltpu.VMEM((s, wide), F32), pltpu.VMEM((hp, s, HEAD), F32), pltpu.VMEM((tk, wide), F32),
                        pltpu.VMEM((tk, wide), F32)])
    return pl.pallas_call(
        body, name=name, grid_spec=grid_spec,
        out_shape=[jax.ShapeDtypeStruct((s, heads * HEAD), BF16)] * 3,
        compiler_params=_params(("parallel", "arbitrary")),
    )(ktab, qtab, qkv, qkv, qkv, do, cmat, tri)


def _mem_probs(q, k, scale):
    sc = _dot(q, k, 1, 1) * scale
    e = jnp.exp(sc - jnp.max(sc, axis=-1, keepdims=True))
    return e / jnp.sum(e, axis=-1, keepdims=True)


def mem_fwd(q, kvm, heads, *, name):
    s, nm = q.shape[0], kvm.shape[0]
    tq = min(MEM_TQ, s)
    scale = 1.0 / math.sqrt(HEAD)

    def body(q_ref, k_ref, v_ref, o_ref):
        p = _mem_probs(q_ref[...], k_ref[...], scale)
        o_ref[...] = _dot(p.astype(BF16), v_ref[...], 1, 0).astype(o_ref.dtype)

    qspec = pl.BlockSpec((tq, HEAD), lambda h, qi: (qi, h))
    return pl.pallas_call(
        body, name=name, grid=(heads, s // tq),
        in_specs=[qspec, pl.BlockSpec((nm, HEAD), lambda h, qi: (0, h)),
                  pl.BlockSpec((nm, HEAD), lambda h, qi: (0, heads + h))],
        out_specs=qspec, out_shape=jax.ShapeDtypeStruct((s, heads * HEAD), BF16),
        compiler_params=_params(("parallel", "parallel")),
    )(q, kvm, kvm)


def mem_bwd(q, kvm, do, heads, *, name):
    s, nm = q.shape[0], kvm.shape[0]
    tq = min(MEM_TQ, s)
    scale = 1.0 / math.sqrt(HEAD)

    def body(q_ref, k_ref, v_ref, do_ref, dq_ref, dk_ref, dv_ref):
        @pl.when(pl.program_id(1) == 0)
        def _():
            dk_ref[...] = jnp.zeros_like(dk_ref)
            dv_ref[...] = jnp.zeros_like(dv_ref)

        qv, kvv, dov = q_ref[...], k_ref[...], do_ref[...]
        p = _mem_probs(qv, kvv, scale)
        dp = _dot(dov, v_ref[...], 1, 1)
        ds = (p * (dp - jnp.sum(dp * p, axis=-1, keepdims=True)) * scale).astype(BF16)
        dq_ref[...] = _dot(ds, kvv, 1, 0).astype(dq_ref.dtype)
        dk_ref[...] += _dot(ds, qv, 0, 0)
        dv_ref[...] += _dot(p.astype(BF16), dov, 0, 0)

    qspec = pl.BlockSpec((tq, HEAD), lambda h, qi: (qi, h))
    kspec = pl.BlockSpec((nm, HEAD), lambda h, qi: (0, h))
    return pl.pallas_call(
        body, name=name, grid=(heads, s // tq),
        in_specs=[qspec, kspec, pl.BlockSpec((nm, HEAD), lambda h, qi: (0, heads + h)), qspec],
        out_specs=[qspec, kspec, kspec],
        out_shape=[jax.ShapeDtypeStruct((s, heads * HEAD), BF16), jax.ShapeDtypeStruct((nm, heads * HEAD), F32),
                   jax.ShapeDtypeStruct((nm, heads * HEAD), F32)],
        compiler_params=_params(("parallel", "arbitrary")),
    )(q, kvm, kvm, do)


def _conv3(u, halo, w, b):
    tm = u.shape[0]
    row = lax.broadcasted_iota(jnp.int32, u.shape, 0)
    h1, h2 = halo[HALO - 1:HALO, :], halo[HALO - 2:HALO - 1, :]
    u1 = jnp.where(row == 0, h1, pltpu.roll(u, 1, 0))
    u2 = jnp.where(row == 0, h2, jnp.where(row == 1, h1, pltpu.roll(u, 2 % tm, 0)))
    return b + w[0:1, :] * u2 + w[1:2, :] * u1 + w[2:3, :] * u, u1, u2


def _conv_specs(s, f):
    tm, tn = min(CONV_TM, s), _pick(f, CONV_TN)
    return tm, tn, s // tm, f // tn


def _silu_parts(g):
    sg = 1.0 / (1.0 + jnp.exp(-g))
    return g * sg, sg


def conv_gate_fwd(u, cw, cb, *, name):
    s, f = u.shape[0], u.shape[1] // 2
    tm, tn, ni, nj = _conv_specs(s, f)
    hb = tm // HALO

    def body(ug_ref, uu_ref, hg_ref, hu_ref, wg_ref, wu_ref, bg_ref, bu_ref, a_ref):
        keep = (pl.program_id(1) > 0).astype(F32)
        gate, _, _ = _conv3(ug_ref[...].astype(F32), hg_ref[...].astype(F32) * keep, wg_ref[...], bg_ref[...])
        up, _, _ = _conv3(uu_ref[...].astype(F32), hu_ref[...].astype(F32) * keep, wu_ref[...], bu_ref[...])
        a_ref[...] = (_silu_parts(gate)[0] * up).astype(a_ref.dtype)

    def main(off):
        return pl.BlockSpec((tm, tn), lambda j, i: (i, j + off))

    def halo(off):
        return pl.BlockSpec((HALO, tn), lambda j, i: (jnp.maximum(i * hb - 1, 0), j + off))

    def par(rows, off):
        return pl.BlockSpec((rows, tn), lambda j, i: (0, j + off))

    return pl.pallas_call(
        body, name=name, grid=(nj, ni),
        in_specs=[main(0), main(nj), halo(0), halo(nj), par(3, 0), par(3, nj), par(1, 0), par(1, nj)],
        out_specs=main(0), out_shape=jax.ShapeDtypeStruct((s, f), BF16),
        compiler_params=_params(("parallel", "parallel")),
    )(u, u, u, u, cw, cw, cb, cb)


def conv_gate_bwd(u, da, cw, cb, *, name):
    s, f = u.shape[0], u.shape[1] // 2
    tm, tn, ni, nj = _conv_specs(s, f)
    hb = tm // HALO

    def body(ug_ref, uu_ref, hg_ref, hu_ref, da_ref, wg_ref, wu_ref, bg_ref, bu_ref, dg_ref, du_ref, pg_ref, pu_ref):
        @pl.when(pl.program_id(1) == 0)
        def _():
            pg_ref[...] = jnp.zeros_like(pg_ref)
            pu_ref[...] = jnp.zeros_like(pu_ref)

        keep = (pl.program_id(1) > 0).astype(F32)
        ug, uu = ug_ref[...].astype(F32), uu_ref[...].astype(F32)
        gate, ug1, ug2 = _conv3(ug, hg_ref[...].astype(F32) * keep, wg_ref[...], bg_ref[...])
        up, uu1, uu2 = _conv3(uu, hu_ref[...].astype(F32) * keep, wu_ref[...], bu_ref[...])
        act, sg = _silu_parts(gate)
        dav = da_ref[...].astype(F32)
        d_gate = dav * up * (sg * (1.0 + gate * (1.0 - sg)))
        d_up = dav * act
        dg_ref[...] = d_gate.astype(dg_ref.dtype)
        du_ref[...] = d_up.astype(du_ref.dtype)
        for p_ref, dc, taps in ((pg_ref, d_gate, (ug2, ug1, ug)), (pu_ref, d_up, (uu2, uu1, uu))):
            for r, tap in enumerate(taps):
                p_ref[r:r + 1, :] += jnp.sum(dc * tap, axis=0, keepdims=True)
            p_ref[3:4, :] += jnp.sum(dc, axis=0, keepdims=True)

    def main(off):
        return pl.BlockSpec((tm, tn), lambda j, i: (i, j + off))

    def halo(off):
        return pl.BlockSpec((HALO, tn), lambda j, i: (jnp.maximum(i * hb - 1, 0), j + off))

    def par(rows, off):
        return pl.BlockSpec((rows, tn), lambda j, i: (0, j + off))

    dg, du, pg, pu = pl.pallas_call(
        body, name=name, grid=(nj, ni),
        in_specs=[main(0), main(nj), halo(0), halo(nj), main(0), par(3, 0), par(3, nj), par(1, 0), par(1, nj)],
        out_specs=[main(0), main(0), par(8, 0), par(8, 0)],
        out_shape=[jax.ShapeDtypeStruct((s, f), BF16), jax.ShapeDtypeStruct((s, f), BF16),
                   jax.ShapeDtypeStruct((8, f), F32), jax.ShapeDtypeStruct((8, f), F32)],
        compiler_params=_params(("parallel", "arbitrary")),
    )(u, u, u, u, da, cw, cw, cb, cb)
    return dg, du, jnp.concatenate([pg, pu], axis=1)


def conv_transpose(dc, w, *, name):
    s, f = dc.shape
    tm, tn, ni, nj = _conv_specs(s, f)
    hb = tm // HALO

    def body(d_ref, h_ref, w_ref, o_ref):
        d = d_ref[...].astype(F32)
        halo = h_ref[...].astype(F32) * (pl.program_id(1) < ni - 1).astype(F32)
        row = lax.broadcasted_iota(jnp.int32, d.shape, 0)
        n0, n1 = halo[0:1, :], halo[1:2, :]
        d1 = jnp.where(row == tm - 1, n0, pltpu.roll(d, tm - 1, 0))
        d2 = jnp.where(row == tm - 2, n0, jnp.where(row == tm - 1, n1, pltpu.roll(d, tm - 2, 0)))
        wv = w_ref[...]
        o_ref[...] = (wv[2:3, :] * d + wv[1:2, :] * d1 + wv[0:1, :] * d2).astype(o_ref.dtype)

    main = pl.BlockSpec((tm, tn), lambda j, i: (i, j))
    return pl.pallas_call(
        body, name=name, grid=(nj, ni),
        in_specs=[main, pl.BlockSpec((HALO, tn), lambda j, i: (jnp.minimum((i + 1) * hb, s // HALO - 1), j)),
                  pl.BlockSpec((3, tn), lambda j, i: (0, j))],
        out_specs=main, out_shape=jax.ShapeDtypeStruct((s, f), BF16),
        compiler_params=_params(("parallel", "parallel")),
    )(dc, dc, w)


def _tile2d(r, c):
    return _pick(r, ROW_TILE), _pick(c, (2048, 1024, 512, 256, 128))


def sum_slots(buf, *, name):
    n, shape = buf.shape[0], buf.shape[1:]
    r, c = math.prod(shape[:-1]), shape[-1]
    tm, tn = _tile2d(r, c)

    def body(b_ref, o_ref):
        acc = b_ref[0]
        for k in range(1, n):
            acc = acc + b_ref[k]
        o_ref[...] = acc

    out = pl.pallas_call(
        body, name=name, grid=(r // tm, c // tn),
        in_specs=[pl.BlockSpec((n, tm, tn), lambda i, j: (0, i, j))],
        out_specs=pl.BlockSpec((tm, tn), lambda i, j: (i, j)), out_shape=jax.ShapeDtypeStruct((r, c), F32),
        compiler_params=_params(("parallel", "parallel")),
    )(buf.reshape(n, r, c))
    return out.reshape(shape)


def add_half(g4, got, core, *, name):
    a, _, r, c = g4.shape
    tm, tn = _tile2d(r, c)

    def body(core_ref, g_ref, got_ref, o_ref):
        o_ref[...] = (g_ref[...] + got_ref[...]).astype(o_ref.dtype)

    part = pl.BlockSpec((None, tm, tn), lambda b, i, j, core_ref: (b, i, j))
    grid_spec = pltpu.PrefetchScalarGridSpec(
        num_scalar_prefetch=1, grid=(a, r // tm, c // tn),
        in_specs=[pl.BlockSpec((None, None, tm, tn), lambda b, i, j, core_ref: (b, core_ref[0], i, j)), part],
        out_specs=part)
    return pl.pallas_call(
        body, name=name, grid_spec=grid_spec, out_shape=jax.ShapeDtypeStruct((a, r, c), BF16),
        compiler_params=_params(("parallel", "parallel", "parallel")),
    )(core, g4, got)


def add_own_block(h, got, chip, axis, *, name):
    _, l, r, c = got.shape
    tm, tn = _tile2d(r, c)
    ncb = c // tn

    def body(chip_ref, h_ref, got_ref, o_ref):
        o_ref[...] = ((h_ref[...].astype(F32) + got_ref[0].astype(F32)) + got_ref[1].astype(F32)) + got_ref[2].astype(F32)

    if axis == 0:
        h_spec = pl.BlockSpec((None, None, tm, tn), lambda b, i, j, chip_ref: (b, chip_ref[0], i, j))
    else:
        h_spec = pl.BlockSpec((None, tm, tn), lambda b, i, j, chip_ref: (b, i, chip_ref[0] * ncb + j))
    grid_spec = pltpu.PrefetchScalarGridSpec(
        num_scalar_prefetch=1, grid=(l, r // tm, ncb),
        in_specs=[h_spec, pl.BlockSpec((3, None, tm, tn), lambda b, i, j, chip_ref: (0, b, i, j))],
        out_specs=pl.BlockSpec((None, tm, tn), lambda b, i, j, chip_ref: (b, i, j)))
    return pl.pallas_call(
        body, name=name, grid_spec=grid_spec, out_shape=jax.ShapeDtypeStruct((l, r, c), F32),
        compiler_params=_params(("parallel", "parallel", "parallel")),
    )(chip, h, got)


def _adam_update(w, g, m, v):
    bc1, bc2 = 1.0 - ADAM_B1 ** ADAM_STEP, 1.0 - ADAM_B2 ** ADAM_STEP
    mn = ADAM_B1 * m + (1.0 - ADAM_B1) * g
    vn = ADAM_B2 * v + (1.0 - ADAM_B2) * (g * g)
    return -ADAM_LR * ((mn / bc1) / (jnp.sqrt(vn / bc2) + ADAM_EPS) + ADAM_WD * w), mn, vn


def adamw(w, g, m, v, *, name):
    shape = w.shape
    c = shape[-1]
    r = math.prod(shape[:-1]) if len(shape) > 1 else 1
    tm, tn = _tile2d(r, c)

    def body(w_ref, g_ref, m_ref, v_ref, d_ref, mo_ref, vo_ref):
        d_ref[...], mo_ref[...], vo_ref[...] = _adam_update(w_ref[...], g_ref[...], m_ref[...], v_ref[...])

    spec = pl.BlockSpec((tm, tn), lambda i, j: (i, j))
    outs = pl.pallas_call(
        body, name=name, grid=(r // tm, c // tn), in_specs=[spec] * 4, out_specs=[spec] * 3,
        out_shape=[jax.ShapeDtypeStruct((r, c), F32)] * 3, compiler_params=_params(("parallel", "parallel")),
    )(*(t.reshape(r, c) for t in (w, g, m, v)))
    return tuple(o.reshape(shape) for o in outs)


def adamw_halves(w, mine, got, m, v, core, *, name):
    l, r, c = w.shape
    rh = r // 2
    tm, tn = _tile2d(rh, c)

    def body(core_ref, w_ref, mine_ref, got_ref, m_ref, v_ref, g_ref, d_ref, mo_ref, vo_ref):
        g = jnp.where(pl.program_id(1) == core_ref[0], mine_ref[...], got_ref[...])
        g_ref[...] = g
        d_ref[...], mo_ref[...], vo_ref[...] = _adam_update(w_ref[...], g, m_ref[...], v_ref[...])

    full = pl.BlockSpec((None, None, tm, tn), lambda b, h, i, j, core_ref: (b, h, i, j))
    half = pl.BlockSpec((None, tm, tn), lambda b, h, i, j, core_ref: (b, i, j))
    grid_spec = pltpu.PrefetchScalarGridSpec(
        num_scalar_prefetch=1, grid=(l, 2, rh // tm, c // tn),
        in_specs=[full, half, half, full, full], out_specs=[full] * 4)
    outs = pl.pallas_call(
        body, name=name, grid_spec=grid_spec, out_shape=[jax.ShapeDtypeStruct((l, 2, rh, c), F32)] * 4,
        compiler_params=_params(("parallel", "parallel", "parallel", "parallel")),
    )(core, w.reshape(l, 2, rh, c), mine, got, m.reshape(l, 2, rh, c), v.reshape(l, 2, rh, c))
    return tuple(o.reshape(l, r, c) for o in outs)


def _comm(name, inputs, out_shapes, aliases, plan):
    n_in, n_out = len(inputs), len(out_shapes)
    probe = plan([None] * n_in, [None] * n_out, 0, 0, 0, count_only=True)

    def body(*refs):
        in_refs, out_refs = refs[:n_in], refs[n_in:n_in + n_out]
        send_sems, recv_sems = refs[n_in + n_out:]
        x, y, c = lax.axis_index("x"), lax.axis_index("y"), lax.axis_index("c")
        xfers = plan(in_refs, out_refs, x, y, c, count_only=False)
        started = []
        for n, (src, dst, peer, _) in enumerate(xfers):
            if peer is None:
                cp = pltpu.make_async_copy(src, dst, send_sems.at[n])
            else:
                cp = pltpu.make_async_remote_copy(src_ref=src, dst_ref=dst, send_sem=send_sems.at[n],
                                                  recv_sem=recv_sems.at[n], device_id=peer, device_id_type=MESH)
            cp.start()
            started.append(cp)
        for n, (src, dst, peer, got) in enumerate(xfers):
            if peer is None:
                started[n].wait()
            else:
                pltpu.make_async_remote_copy(src_ref=got, dst_ref=got, send_sem=send_sems.at[n],
                                             recv_sem=recv_sems.at[n], device_id=peer,
                                             device_id_type=MESH).wait_recv()
                started[n].wait_send()

    hbm = pl.BlockSpec(memory_space=pl.ANY)
    return pl.pallas_call(
        body, name=name, in_specs=[hbm] * n_in, out_specs=[hbm] * n_out, out_shape=out_shapes,
        input_output_aliases=aliases,
        scratch_shapes=[pltpu.SemaphoreType.DMA((probe,)), pltpu.SemaphoreType.DMA((probe,))],
    )(*inputs)


def _other_chips(x, y):
    return [(1 - x, y), (x, 1 - y), (1 - x, 1 - y)]


def gather_weights(shards, axes):
    def out_shape(sh, ax):
        l, r, c = sh.shape
        return jax.ShapeDtypeStruct((l, N_CHIPS, 2, r // 2, c) if ax == 0 else (l, 2, r // 2, N_CHIPS * c), BF16)

    def piece(ref, ax, chip, half, width):
        if ax == 0:
            return ref.at[:, chip, half]
        return ref.at[:, half, :, pl.ds(chip * width, width)]

    def plan1(in_refs, out_refs, x, y, c, count_only):
        if count_only:
            return 4 * len(shards)
        me, xfers = 2 * x + y, []
        for sh, ax, src, out in zip(shards, axes, in_refs, out_refs):
            rows, width = sh.shape[1] // 2, sh.shape[2]
            mine = src.at[:, pl.ds(c * rows, rows), :]
            xfers.append((mine, piece(out, ax, me, c, width), None, None))
            for cx, cy in _other_chips(x, y):
                xfers.append((mine, piece(out, ax, me, c, width), (cx, cy, c), piece(out, ax, 2 * cx + cy, c, width)))
        return xfers

    def half_of(ref, ax, half):
        return ref.at[:, :, half] if ax == 0 else ref.at[:, half]

    def plan2(in_refs, out_refs, x, y, c, count_only):
        if count_only:
            return len(shards)
        return [(half_of(out, ax, c), half_of(out, ax, c), (x, y, 1 - c), half_of(out, ax, 1 - c))
                for ax, out in zip(axes, out_refs)]

    shapes = [out_shape(sh, ax) for sh, ax in zip(shards, axes)]
    part = _comm("gather_chips", list(shards), shapes, {}, plan1)
    full = _comm("gather_cores", list(part), shapes, {n: n for n in range(len(shards))}, plan2)
    return [f.reshape(sh.shape[0], N_CHIPS * sh.shape[1], sh.shape[2]) if ax == 0
            else f.reshape(sh.shape[0], sh.shape[1], N_CHIPS * sh.shape[2])
            for f, sh, ax in zip(full, shards, axes)]


def reduce_scatter_grads(grads, axes, core, chip):
    dims, views = [], []
    for g, ax in zip(grads, axes):
        l, r, c = g.shape
        if ax == 0:
            dims.append((l, r // N_CHIPS // 2, c))
            views.append(g.reshape(l, N_CHIPS, 2, r // N_CHIPS // 2, c))
        else:
            dims.append((l, r // 2, c // N_CHIPS))
            views.append(g.reshape(l, 2, r // 2, c))

    def half_shape(ax, d):
        return (d[0], N_CHIPS, d[1], d[2]) if ax == 0 else (d[0], d[1], N_CHIPS * d[2])

    def take(ref, ax, half):
        return ref.at[:, :, half] if ax == 0 else ref.at[:, half]

    def plan_a(in_refs, out_refs, x, y, c, count_only):
        if count_only:
            return len(grads)
        return [(take(src, ax, 1 - c), out, (x, y, 1 - c), out) for ax, src, out in zip(axes, in_refs, out_refs)]

    got = _comm("reduce_cores", views, [jax.ShapeDtypeStruct(half_shape(ax, d), F32) for ax, d in zip(axes, dims)],
                {}, plan_a)
    chip_sum = []
    for ax, d, v, g in zip(axes, dims, views, got):
        a = d[0] * N_CHIPS if ax == 0 else d[0]
        summed = add_half(v.reshape(a, 2, d[1], -1), g.reshape(a, d[1], -1), core, name="sum_cores")
        chip_sum.append(summed.reshape(half_shape(ax, d)))

    def block(ref, ax, which, width):
        return ref.at[:, which] if ax == 0 else ref.at[:, :, pl.ds(which * width, width)]

    def plan_b(in_refs, out_refs, x, y, c, count_only):
        if count_only:
            return 3 * len(grads)
        xfers = []
        for ax, d, src, out in zip(axes, dims, in_refs, out_refs):
            for k, (cx, cy) in enumerate(_other_chips(x, y)):
                xfers.append((block(src, ax, 2 * cx + cy, d[2]), out.at[k], (cx, cy, c), out.at[k]))
        return xfers

    parts = _comm("reduce_chips", chip_sum, [jax.ShapeDtypeStruct((3,) + d, BF16) for d in dims], {}, plan_b)
    mine = [add_own_block(h, p, chip, ax, name="sum_chips") for h, p, ax in zip(chip_sum, parts, axes)]

    def plan_c(in_refs, out_refs, x, y, c, count_only):
        if count_only:
            return len(grads)
        return [(src, out, (x, y, 1 - c), out) for src, out in zip(in_refs, out_refs)]

    theirs = _comm("share_cores", mine, [jax.ShapeDtypeStruct(d, F32) for d in dims], {}, plan_c)
    return list(zip(mine, theirs))


def gather_slabs(vec, *, name):
    def plan(in_refs, out_refs, x, y, c, count_only):
        if count_only:
            return 8
        me = 4 * x + 2 * y + c
        xfers = [(in_refs[0], out_refs[0].at[me], None, None)]
        for k in range(1, 8):
            px, py, pc = x ^ (k >> 2), y ^ ((k >> 1) & 1), c ^ (k & 1)
            xfers.append((in_refs[0], out_refs[0].at[me], (px, py, pc), out_refs[0].at[4 * px + 2 * py + pc]))
        return xfers

    return _comm(name, [vec], [jax.ShapeDtypeStruct((8,) + vec.shape, F32)], {}, plan)[0]


def allreduce_small(vec):
    return sum_slots(gather_slabs(vec, name="gather_small"), name="sum_small")


def gather_conv_w(block):
    l, taps, c = block.shape
    flat = block.reshape(-1)
    slabs = gather_slabs(jnp.pad(flat, (0, -flat.size % 1024)).reshape(-1, 128), name="gather_conv_w")
    per_chip = slabs[0::2].reshape(N_CHIPS, -1)[:, :flat.size].reshape(N_CHIPS, l, taps, c)
    return per_chip.transpose(1, 2, 0, 3).reshape(l, taps, N_CHIPS * c)


def _rope_tables(positions):
    inv_freq = ROPE_THETA ** (-jnp.arange(0, ROPE_DIM, 2, dtype=F32) / ROPE_DIM)
    ang = positions.astype(F32)[:, None] * inv_freq
    ang = jnp.concatenate([ang, ang], axis=-1)
    pad = ((0, 0), (0, HEAD - ROPE_DIM))
    return jnp.pad(jnp.cos(ang), pad), jnp.pad(jnp.sin(ang), pad)


def _uq_layout(w, heads):
    ql = w.shape[0]
    w = w.reshape(ql, heads, HEAD + ROPE_DIM)
    rot = jnp.pad(w[:, :, HEAD:], ((0, 0), (0, 0), (0, HEAD - ROPE_DIM)))
    return jnp.concatenate([w[:, :, :HEAD].reshape(ql, heads * HEAD), rot.reshape(ql, heads * HEAD)], axis=1)


def _uq_layout_inv(dw, heads):
    ql = dw.shape[0]
    nope = dw[:, :heads * HEAD].reshape(ql, heads, HEAD)
    rot = dw[:, heads * HEAD:].reshape(ql, heads, HEAD)[:, :, :ROPE_DIM]
    return jnp.concatenate([nope, rot], axis=-1).reshape(ql, heads * (HEAD + ROPE_DIM))


def kernel(x, mem, positions, norm_mix, norm_mem_q, norm_mem_kv, norm_ffn, norm_final, mla_w_down, mla_q_norm, mla_w_uq, mla_kv_norm, mla_w_ukv, mla_w_o, sb_w_qkv, sb_w_o, mem_w_q, mem_w_kv, mem_w_o, ffn_w_in, ffn_conv_w, ffn_conv_b, ffn_w_out, loss_target, m_norm_mix, m_norm_mem_q, m_norm_mem_kv, m_norm_ffn, m_norm_final, m_mla_w_down, m_mla_q_norm, m_mla_w_uq, m_mla_kv_norm, m_mla_w_ukv, m_mla_w_o, m_sb_w_qkv, m_sb_w_o, m_mem_w_q, m_mem_w_kv, m_mem_w_o, m_ffn_w_in, m_ffn_conv_w, m_ffn_conv_b, m_ffn_w_out, v_norm_mix, v_norm_mem_q, v_norm_mem_kv, v_norm_ffn, v_norm_final, v_mla_w_down, v_mla_q_norm, v_mla_w_uq, v_mla_kv_norm, v_mla_w_ukv, v_mla_w_o, v_sb_w_qkv, v_sb_w_o, v_mem_w_q, v_mem_w_kv, v_mem_w_o, v_ffn_w_in, v_ffn_conv_w, v_ffn_conv_b, v_ffn_w_out):
    args = dict(locals())
    big = ["mla_w_down", "mla_w_uq", "mla_w_ukv", "mla_w_o", "sb_w_qkv", "sb_w_o", "mem_w_q", "mem_w_kv",
           "mem_w_o", "ffn_w_in", "ffn_w_out"]
    col_cut = {"mla_w_uq", "mla_w_ukv", "sb_w_qkv", "mem_w_o", "ffn_w_in"}
    axes = [1 if n in col_cut else 0 for n in big]
    small = ["norm_mix", "norm_mem_q", "norm_mem_kv", "norm_ffn", "norm_final", "mla_q_norm", "mla_kv_norm",
             "ffn_conv_b", "ffn_conv_w"]
    order = ["norm_mix", "norm_mem_q", "norm_mem_kv", "norm_ffn", "norm_final", "mla_w_down", "mla_q_norm",
             "mla_w_uq", "mla_kv_norm", "mla_w_ukv", "mla_w_o", "sb_w_qkv", "sb_w_o", "mem_w_q", "mem_w_kv",
             "mem_w_o", "ffn_w_in", "ffn_conv_w", "ffn_conv_b", "ffn_w_out"]

    xs, mems, target = x[0], mem[0], loss_target[0]
    s, d = xs.shape
    depth = norm_mix.shape[0]
    ql, kvl = mla_q_norm.shape[1], mla_kv_norm.shape[1]
    mla_heads = N_CHIPS * mla_w_uq.shape[2] // (HEAD + ROPE_DIM)
    sb_heads = N_CHIPS * sb_w_qkv.shape[2] // (3 * HEAD)
    mem_heads = mem_w_q.shape[2] // HEAD
    ff = N_CHIPS * ffn_w_out.shape[1]
    chip = 2 * lax.axis_index("x") + lax.axis_index("y")
    chip_op = jnp.reshape(chip, (1,)).astype(jnp.int32)
    core_op = jnp.reshape(lax.axis_index("c"), (1,)).astype(jnp.int32)

    full = dict(zip(big, gather_weights([args[n].astype(BF16) for n in big], axes)))
    w_uq = [_uq_layout(full["mla_w_uq"][j], mla_heads) for j in range(full["mla_w_uq"].shape[0])]
    cos, sin = _rope_tables(positions[0])
    _, sb_tk = _sb_tiles(s)
    tri = (jnp.arange(sb_tk)[:, None] > jnp.arange(sb_tk)[None, :]).astype(BF16)
    conv_w = gather_conv_w(ffn_conv_w)

    saved = []
    xa = xs
    for i in range(depth):
        j = i // 2
        lay = {}
        lay["xa"] = xa
        h1 = rmsnorm_fwd(xa, norm_mix[i], name="norm_fwd")
        lay["h1"] = h1
        if i % 2 == 0:
            down = mm(h1, full["mla_w_down"][j], out_dtype=F32, name="mm_down")
            cq = rmsnorm_fwd(down[:, :ql], mla_q_norm[j], name="norm_lora_fwd")
            ckv = rmsnorm_fwd(down[:, ql:ql + kvl], mla_kv_norm[j], name="norm_lora_fwd")
            kr_raw = jnp.pad(down[:, ql + kvl:], ((0, 0), (0, HEAD - ROPE_DIM)))
            kr = rope(kr_raw, cos, sin, col0=0, n=1, inverse=False, name="rope_k")
            qall = mm(cq, w_uq[j], name="mm_uq")
            qr = rope(qall, cos, sin, col0=mla_heads, n=mla_heads, inverse=False, name="rope_q")
            kv = mm(ckv, full["mla_w_ukv"][j], name="mm_ukv")
            o, lse = mla_fwd(qall, qr, kv, kr, mla_heads, name="mla_fwd")
            xb = mm(o, full["mla_w_o"][j], add=xa, out_dtype=F32, name="mm_out_res")
            lay.update(down=down, cq=cq, ckv=ckv, qall=qall, qr=qr, kv=kv, kr=kr, o=o, lse=lse)
        else:
            qkv = mm(h1, full["sb_w_qkv"][j], name="mm_qkv")
            o, cmat = sb_fwd(qkv, tri, sb_heads, name="sb_fwd")
            xb = mm(o, full["sb_w_o"][j], add=xa, out_dtype=F32, name="mm_out_res")
            lay.update(qkv=qkv, o=o, cmat=cmat)
        h2 = rmsnorm_fwd(xb, norm_mem_q[i], name="norm_fwd")
        hm = rmsnorm_fwd(mems, norm_mem_kv[i], name="norm_mem_fwd")
        qm = mm(h2, full["mem_w_q"][i], name="mm_mem_q")
        kvm = mm(hm, full["mem_w_kv"][i], name="mm_mem_kv")
        om = mem_fwd(qm, kvm, mem_heads, name="mem_fwd")
        xc = mm(om, full["mem_w_o"][i], add=xb, out_dtype=F32, name="mm_mem_out_res")
        h3 = rmsnorm_fwd(xc, norm_ffn[i], name="norm_fwd")
        u = mm(h3, full["ffn_w_in"][i], name="mm_ffn_in")
        act = conv_gate_fwd(u, conv_w[i], ffn_conv_b[i][None, :], name="conv_gate_fwd")
        xd = mm(act, full["ffn_w_out"][i], add=xc, out_dtype=F32, name="mm_ffn_out_res")
        lay.update(xb=xb, h2=h2, hm=hm, qm=qm, kvm=kvm, om=om, xc=xc, h3=h3, u=u, act=act)
        saved.append(lay)
        xa = xd

    dx, dxb, g_final, loss_part = final_loss(xa, norm_final, target, name="final_loss")

    gw = {n: [None] * full[n].shape[0] for n in big}
    g_small = {"norm_mix": [None] * depth, "norm_mem_q": [None] * depth, "norm_mem_kv": [None] * depth,
               "norm_ffn": [None] * depth, "mla_q_norm": [None] * (depth - depth // 2),
               "mla_kv_norm": [None] * (depth - depth // 2), "conv": [None] * depth}
    for i in reversed(range(depth)):
        j = i // 2
        lay = saved[i]
        gw["ffn_w_out"][i] = mm(lay["act"], dxb, mode="tn", out_dtype=F32, name="mm_ffn_out_wgrad")
        da = mm(dxb, full["ffn_w_out"][i], mode="nt", name="mm_ffn_out_dgrad")
        dcg, dcu, g_small["conv"][i] = conv_gate_bwd(lay["u"], da, conv_w[i], ffn_conv_b[i][None, :], name="conv_gate_bwd")
        du = jnp.concatenate([conv_transpose(dcg, conv_w[i][:, :ff], name="conv_transpose"),
                              conv_transpose(dcu, conv_w[i][:, ff:], name="conv_transpose")], axis=1)
        gw["ffn_w_in"][i] = mm(lay["h3"], du, mode="tn", out_dtype=F32, name="mm_ffn_in_wgrad")
        dh3 = mm(du, full["ffn_w_in"][i], mode="nt", name="mm_ffn_in_dgrad")
        dx, dxb, g_small["norm_ffn"][i] = rmsnorm_bwd(dh3, lay["xc"], norm_ffn[i], dx, name="norm_bwd")
        gw["mem_w_o"][i] = mm(lay["om"], dxb, mode="tn", out_dtype=F32, name="mm_mem_out_wgrad")
        dom = mm(dxb, full["mem_w_o"][i], mode="nt", name="mm_mem_out_dgrad")
        dqm, dkm, dvm = mem_bwd(lay["qm"], lay["kvm"], dom, mem_heads, name="mem_bwd")
        dkvm = jnp.concatenate([dkm, dvm], axis=1)
        gw["mem_w_q"][i] = mm(lay["h2"], dqm, mode="tn", out_dtype=F32, name="mm_mem_q_wgrad")
        gw["mem_w_kv"][i] = mm(lay["hm"], dkvm, mode="tn", out_dtype=F32, name="mm_mem_kv_wgrad")
        dh2 = mm(dqm, full["mem_w_q"][i], mode="nt", name="mm_mem_q_dgrad")
        dhm = mm(dkvm, full["mem_w_kv"][i], mode="nt", name="mm_mem_kv_dgrad")
        _, _, g_small["norm_mem_kv"][i] = rmsnorm_bwd(dhm, mems, norm_mem_kv[i], name="norm_mem_bwd")
        dx, dxb, g_small["norm_mem_q"][i] = rmsnorm_bwd(dh2, lay["xb"], norm_mem_q[i], dx, name="norm_bwd")
        if i % 2 == 0:
            gw["mla_w_o"][j] = mm(lay["o"], dxb, mode="tn", out_dtype=F32, name="mm_out_wgrad")
            do = mm(dxb, full["mla_w_o"][j], mode="nt", name="mm_out_dgrad")
            dqn, dqr, dkv, dkr_heads = mla_bwd(lay["qall"], lay["qr"], lay["kv"], lay["kr"], lay["o"], do, lay["lse"],
                                               mla_heads, name="mla_bwd")
            dkr = sum_lane_tiles(dkr_heads, mla_heads, name="sum_heads")
            dqall = jnp.concatenate([dqn, rope(dqr, cos, sin, col0=0, n=mla_heads, inverse=True, name="rope_q_bwd")], axis=1)
            gw["mla_w_ukv"][j] = mm(lay["ckv"], dkv, mode="tn", out_dtype=F32, name="mm_ukv_wgrad")
            dckv = mm(dkv, full["mla_w_ukv"][j], mode="nt", name="mm_ukv_dgrad")
            gw["mla_w_uq"][j] = _uq_layout_inv(mm(lay["cq"], dqall, mode="tn", out_dtype=F32, name="mm_uq_wgrad"), mla_heads)
            dcq = mm(dqall, w_uq[j], mode="nt", name="mm_uq_dgrad")
            down = lay["down"]
            _, d_q, g_small["mla_q_norm"][j] = rmsnorm_bwd(dcq, down[:, :ql], mla_q_norm[j], name="norm_lora_bwd")
            _, d_kv, g_small["mla_kv_norm"][j] = rmsnorm_bwd(dckv, down[:, ql:ql + kvl], mla_kv_norm[j], name="norm_lora_bwd")
            d_kr = rope(dkr, cos, sin, col0=0, n=1, inverse=True, name="rope_k_bwd")[:, :ROPE_DIM]
            ddown = jnp.concatenate([d_q, d_kv, d_kr], axis=1)
            gw["mla_w_down"][j] = mm(lay["h1"], ddown, mode="tn", out_dtype=F32, name="mm_down_wgrad")
            dh1 = mm(ddown, full["mla_w_down"][j], mode="nt", name="mm_down_dgrad")
        else:
            gw["sb_w_o"][j] = mm(lay["o"], dxb, mode="tn", out_dtype=F32, name="mm_out_wgrad")
            do = mm(dxb, full["sb_w_o"][j], mode="nt", name="mm_out_dgrad")
            dq, dk, dv = sb_bwd(lay["qkv"], do, lay["cmat"], tri, sb_heads, name="sb_bwd")
            dqkv = jnp.concatenate([dq, dk, dv], axis=1)
            gw["sb_w_qkv"][j] = mm(lay["h1"], dqkv, mode="tn", out_dtype=F32, name="mm_qkv_wgrad")
            dh1 = mm(dqkv, full["sb_w_qkv"][j], mode="nt", name="mm_qkv_dgrad")
        dx, dxb, g_small["norm_mix"][i] = rmsnorm_bwd(dh1, lay["xa"], norm_mix[i], dx, name="norm_bwd")

    halves = dict(zip(big, reduce_scatter_grads([jnp.stack(gw[n]) for n in big], axes, core_op, chip_op)))
    conv = jnp.stack(g_small["conv"])
    parts = [jnp.concatenate(g_small[n], axis=0) for n in ("norm_mix", "norm_mem_q", "norm_mem_kv", "norm_ffn")]
    parts += [g_final, jnp.concatenate(g_small["mla_q_norm"], axis=0), jnp.concatenate(g_small["mla_kv_norm"], axis=0),
              conv[:, 3, :], conv[:, :3, :], loss_part[:1, :1]]
    sizes = [p.size for p in parts]
    packed = jnp.concatenate([p.reshape(-1) for p in parts])
    packed = jnp.pad(packed, (0, -packed.size % 1024)).reshape(-1, 128)
    total = allreduce_small(packed).reshape(-1)
    g_rep, at = {}, 0
    for n, p, size in zip(small + ["loss"], parts, sizes):
        g_rep[n] = total[at:at + size].reshape(p.shape)
        at += size
    loss = g_rep.pop("loss").reshape(())
    g_rep["norm_final"] = g_rep["norm_final"].reshape(norm_final.shape)
    width = ffn_conv_w.shape[2]
    g_rep["ffn_conv_w"] = lax.dynamic_slice_in_dim(g_rep["ffn_conv_w"], chip * width, width, axis=2)

    grads, delta, new_m, new_v = {}, {}, {}, {}
    for n in order:
        if n in halves:
            mine, theirs = halves[n]
            grads[n], delta[n], new_m[n], new_v[n] = adamw_halves(
                args[n], mine, theirs, args["m_" + n], args["v_" + n], core_op, name="adamw_big")
        else:
            grads[n] = g_rep[n]
            delta[n], new_m[n], new_v[n] = adamw(args[n], g_rep[n], args["m_" + n], args["v_" + n], name="adamw")
    return (loss, dx[None], *[grads[n] for n in order], *[delta[n] for n in order],
            *[new_m[n] for n in order], *[new_v[n] for n in order])
```

```python
import math

import jax
import jax.numpy as jnp
import numpy as np
from jax import lax
from jax.experimental import pallas as pl
from jax.experimental.pallas import tpu as pltpu

F32 = jnp.float32
BF16 = jnp.bfloat16
MESH = pl.DeviceIdType.MESH

EPS = 1e-6
LOG2E = 1.4426950408889634
CHUNK_SHIFT = 6
HEAD = 128
ROPE_DIM = 64
ROPE_THETA = 10000.0
N_CHIPS = 4
ADAM_LR, ADAM_B1, ADAM_B2, ADAM_EPS, ADAM_WD, ADAM_STEP = 0.001, 0.9, 0.999, 1e-08, 0.01, 10

VMEM_LIMIT_BYTES = 48 * 1024 * 1024
MM_TM, MM_TN, MM_TK = (1024, 512, 256, 128), (1024, 1408, 512, 256, 128), (1024, 512, 256, 128)
MM_WHOLE_K, MM_WHOLE_TM = 2048, (512, 256, 128)
ROW_TILE = (256, 128, 64, 32, 16, 8)
ATT_T = 512
MLA_HEADS_PER_STEP, MLA_BWD_HEADS_PER_STEP = 2, 2
SB_TQ, SB_TK = 512, 256
SB_HEADS_PER_STEP = 2
MEM_TQ = 512
CONV_TM, CONV_TN = 512, (512, 256, 128)
HALO = 16


def _pick(dim, prefs):
    for p in prefs:
        if dim % p == 0:
            return p
    return dim


def _dot(a, b, ca, cb):
    return lax.dot_general(a, b, (((ca,), (cb,)), ((), ())), preferred_element_type=F32)


def _params(sem):
    return pltpu.CompilerParams(dimension_semantics=sem, vmem_limit_bytes=VMEM_LIMIT_BYTES)


def _head_groups(heads, per_step):
    hp = per_step if heads % per_step == 0 else 1
    return hp, heads // hp, hp * HEAD


def _tables(pairs):
    arr = np.asarray(pairs, dtype=np.int32)
    return jnp.asarray(arr[:, 0]), jnp.asarray(arr[:, 1])


def mm(a, b, *, mode="nn", add=None, out_dtype=BF16, name):
    if mode == "nn":
        (m, k), (k2, n) = a.shape, b.shape
    elif mode == "nt":
        (m, k), (n, k2) = a.shape, b.shape
    else:
        (k, m), (k2, n) = a.shape, b.shape
    assert k == k2, (a.shape, b.shape, mode)
    ca, cb = {"nn": (1, 0), "nt": (1, 1), "tn": (0, 0)}[mode]
    if k <= MM_WHOLE_K:
        return _mm_whole_k(a, b, add, (m, n, k), (ca, cb), mode, out_dtype, name)
    tm, tn, tk = _pick(m, MM_TM), _pick(n, MM_TN), _pick(k, MM_TK)
    nk = k // tk

    def body(a_ref, b_ref, *rest):
        if add is None:
            o_ref, acc_ref = rest
        else:
            add_ref, o_ref, acc_ref = rest
        kk = pl.program_id(2)

        @pl.when(kk == 0)
        def _():
            acc_ref[...] = jnp.zeros_like(acc_ref)

        acc_ref[...] += _dot(a_ref[...].astype(BF16), b_ref[...].astype(BF16), ca, cb)

        @pl.when(kk == nk - 1)
        def _():
            r = acc_ref[...]
            if add is not None:
                r = r + add_ref[...]
            o_ref[...] = r.astype(o_ref.dtype)

    if mode == "tn":
        a_spec = pl.BlockSpec((tk, tm), lambda i, j, kk: (kk, i))
    else:
        a_spec = pl.BlockSpec((tm, tk), lambda i, j, kk: (i, kk))
    if mode == "nt":
        b_spec = pl.BlockSpec((tn, tk), lambda i, j, kk: (j, kk))
    else:
        b_spec = pl.BlockSpec((tk, tn), lambda i, j, kk: (kk, j))
    o_spec = pl.BlockSpec((tm, tn), lambda i, j, kk: (i, j))
    in_specs, args = [a_spec, b_spec], [a, b]
    if add is not None:
        in_specs.append(o_spec)
        args.append(add)
    return pl.pallas_call(
        body, name=name, grid=(m // tm, n // tn, nk), in_specs=in_specs, out_specs=o_spec,
        out_shape=jax.ShapeDtypeStruct((m, n), out_dtype), scratch_shapes=[pltpu.VMEM((tm, tn), F32)],
        compiler_params=_params(("parallel", "parallel", "arbitrary")),
    )(*args)


def _mm_whole_k(a, b, add, mnk, contract, mode, out_dtype, name):
    m, n, k = mnk
    tm, tn = _pick(m, MM_WHOLE_TM), _pick(n, MM_TN)

    def body(a_ref, b_ref, *rest):
        r = _dot(a_ref[...].astype(BF16), b_ref[...].astype(BF16), *contract)
        if add is not None:
            r = r + rest[0][...]
        rest[-1][...] = r.astype(rest[-1].dtype)

    a_spec = pl.BlockSpec((k, tm), lambda j, i: (0, i)) if mode == "tn" else pl.BlockSpec((tm, k), lambda j, i: (i, 0))
    b_spec = pl.BlockSpec((tn, k), lambda j, i: (j, 0)) if mode == "nt" else pl.BlockSpec((k, tn), lambda j, i: (0, j))
    o_spec = pl.BlockSpec((tm, tn), lambda j, i: (i, j))
    in_specs, args = [a_spec, b_spec], [a, b]
    if add is not None:
        in_specs.append(o_spec)
        args.append(add)
    return pl.pallas_call(
        body, name=name, grid=(n // tn, m // tm), in_specs=in_specs, out_specs=o_spec,
        out_shape=jax.ShapeDtypeStruct((m, n), out_dtype), compiler_params=_params(("parallel", "parallel")),
    )(*args)


def rmsnorm_fwd(x, g, *, name):
    m, d = x.shape
    tm = _pick(m, ROW_TILE)

    def body(x_ref, g_ref, o_ref):
        xv = x_ref[...]
        r = lax.rsqrt(jnp.mean(xv * xv, axis=-1, keepdims=True) + EPS)
        o_ref[...] = (xv * r * g_ref[...]).astype(o_ref.dtype)

    return pl.pallas_call(
        body, name=name, grid=(m // tm,),
        in_specs=[pl.BlockSpec((tm, d), lambda i: (i, 0)), pl.BlockSpec((1, d), lambda i: (0, 0))],
        out_specs=pl.BlockSpec((tm, d), lambda i: (i, 0)), out_shape=jax.ShapeDtypeStruct((m, d), BF16),
        compiler_params=_params(("parallel",)),
    )(x, g.reshape(1, d))


def rmsnorm_bwd(dh, x, g, res=None, *, name):
    m, d = x.shape
    tm = _pick(m, ROW_TILE)

    def body(dh_ref, x_ref, g_ref, *rest):
        if res is None:
            dx_ref, dxb_ref, dg_ref = rest
        else:
            res_ref, dx_ref, dxb_ref, dg_ref = rest

        @pl.when(pl.program_id(0) == 0)
        def _():
            dg_ref[...] = jnp.zeros_like(dg_ref)

        xv = x_ref[...]
        dhv = dh_ref[...].astype(F32)
        r = lax.rsqrt(jnp.mean(xv * xv, axis=-1, keepdims=True) + EPS)
        y = xv * r
        dhg = dhv * g_ref[...]
        dx = r * (dhg - y * jnp.mean(dhg * y, axis=-1, keepdims=True))
        if res is not None:
            dx = dx + res_ref[...]
        dx_ref[...] = dx
        dxb_ref[...] = dx.astype(BF16)
        dg_ref[...] += jnp.sum(dhv * y, axis=0, keepdims=True)

    row = pl.BlockSpec((tm, d), lambda i: (i, 0))
    vec = pl.BlockSpec((1, d), lambda i: (0, 0))
    in_specs, args = [row, row, vec], [dh, x, g.reshape(1, d)]
    if res is not None:
        in_specs.append(row)
        args.append(res)
    return pl.pallas_call(
        body, name=name, grid=(m // tm,), in_specs=in_specs, out_specs=[row, row, vec],
        out_shape=[jax.ShapeDtypeStruct((m, d), F32), jax.ShapeDtypeStruct((m, d), BF16),
                   jax.ShapeDtypeStruct((1, d), F32)],
        compiler_params=_params(("arbitrary",)),
    )(*args)


def final_loss(x, g, target, *, name):
    m, d = x.shape
    tm = _pick(m, ROW_TILE)

    def body(x_ref, g_ref, t_ref, dx_ref, dxb_ref, dg_ref, loss_ref):
        @pl.when(pl.program_id(0) == 0)
        def _():
            dg_ref[...] = jnp.zeros_like(dg_ref)
            loss_ref[...] = jnp.zeros_like(loss_ref)

        xv = x_ref[...]
        gv = g_ref[...]
        r = lax.rsqrt(jnp.mean(xv * xv, axis=-1, keepdims=True) + EPS)
        y = xv * r
        err = y * gv - t_ref[...]
        loss_ref[...] += 0.5 * jnp.sum(jnp.mean(err * err, axis=-1, keepdims=True))
        dy = err * (1.0 / d)
        dyg = dy * gv
        dx = r * (dyg - y * jnp.mean(dyg * y, axis=-1, keepdims=True))
        dx_ref[...] = dx
        dxb_ref[...] = dx.astype(BF16)
        dg_ref[...] += jnp.sum(dy * y, axis=0, keepdims=True)

    row = pl.BlockSpec((tm, d), lambda i: (i, 0))
    vec = pl.BlockSpec((1, d), lambda i: (0, 0))
    return pl.pallas_call(
        body, name=name, grid=(m // tm,), in_specs=[row, vec, row],
        out_specs=[row, row, vec, pl.BlockSpec((8, 128), lambda i: (0, 0))],
        out_shape=[jax.ShapeDtypeStruct((m, d), F32), jax.ShapeDtypeStruct((m, d), BF16),
                   jax.ShapeDtypeStruct((1, d), F32), jax.ShapeDtypeStruct((8, 128), F32)],
        compiler_params=_params(("arbitrary",)),
    )(x, g.reshape(1, d), target)


def rope(xin, cos, sin, *, col0, n, inverse, name):
    s = xin.shape[0]
    tm = _pick(s, ROW_TILE)
    half = ROPE_DIM // 2

    def body(x_ref, c_ref, s_ref, o_ref):
        xv = x_ref[...].astype(F32)
        lane = lax.broadcasted_iota(jnp.int32, xv.shape, 1)
        rot = jnp.where(lane < half, -pltpu.roll(xv, HEAD - half, 1), pltpu.roll(xv, half, 1))
        sv = s_ref[...]
        if inverse:
            sv = -sv
        o_ref[...] = (xv * c_ref[...] + rot * sv).astype(o_ref.dtype)

    tab = pl.BlockSpec((tm, HEAD), lambda i, j: (i, 0))
    return pl.pallas_call(
        body, name=name, grid=(s // tm, n),
        in_specs=[pl.BlockSpec((tm, HEAD), lambda i, j: (i, j + col0)), tab, tab],
        out_specs=pl.BlockSpec((tm, HEAD), lambda i, j: (i, j)),
        out_shape=jax.ShapeDtypeStruct((s, n * HEAD), BF16),
        compiler_params=_params(("parallel", "parallel")),
    )(xin, cos, sin)


def sum_lane_tiles(xin, n, *, name):
    s = xin.shape[0]
    tm = _pick(s, ROW_TILE)

    def body(x_ref, o_ref):
        acc = x_ref[:, :HEAD]
        for k in range(1, n):
            acc = acc + x_ref[:, k * HEAD:(k + 1) * HEAD]
        o_ref[...] = acc

    return pl.pallas_call(
        body, name=name, grid=(s // tm,), in_specs=[pl.BlockSpec((tm, n * HEAD), lambda i: (i, 0))],
        out_specs=pl.BlockSpec((tm, HEAD), lambda i: (i, 0)), out_shape=jax.ShapeDtypeStruct((s, HEAD), F32),
        compiler_params=_params(("parallel",)),
    )(xin)


def _mla_scores(qn, qr, kn, kr, scale, diagonal):
    sc = (_dot(qn, kn, 1, 1) + _dot(qr, kr, 1, 1)) * scale
    if not diagonal:
        return sc
    qchunk = jnp.right_shift(lax.broadcasted_iota(jnp.int32, sc.shape, 0), CHUNK_SHIFT)
    kchunk = jnp.right_shift(lax.broadcasted_iota(jnp.int32, sc.shape, 1), CHUNK_SHIFT)
    return jnp.where(kchunk <= qchunk, sc, -jnp.inf)


def mla_fwd(qall, qr, kv, kr, heads, *, name):
    s = qr.shape[0]
    t = min(ATT_T, s)
    nq = s // t
    scale = 1.0 / math.sqrt(HEAD + ROPE_DIM)
    pairs = [(qi, kj) for qi in range(nq) for kj in range(qi + 1)]
    qtab, ktab = _tables(pairs)
    hp, groups, wide = _head_groups(heads, MLA_HEADS_PER_STEP)

    def body(qt_ref, kt_ref, qn_ref, qr_ref, kv_ref, kr_ref, o_ref, lse_ref, m_ref, l_ref, acc_ref):
        st = pl.program_id(1)
        qi, kj = qt_ref[st], kt_ref[st]

        @pl.when(kj == 0)
        def _():
            m_ref[...] = jnp.full_like(m_ref, -jnp.inf)
            l_ref[...] = jnp.zeros_like(l_ref)
            acc_ref[...] = jnp.zeros_like(acc_ref)

        def step(diagonal):
            for hh in range(hp):
                cols = slice(hh * HEAD, (hh + 1) * HEAD)
                kn = kv_ref[:, 2 * hh * HEAD:(2 * hh + 1) * HEAD]
                v = kv_ref[:, (2 * hh + 1) * HEAD:(2 * hh + 2) * HEAD]
                sc = _mla_scores(qn_ref[:, cols], qr_ref[:, cols], kn, kr_ref[...], scale, diagonal)
                m_prev = m_ref[hh]
                m_new = jnp.maximum(m_prev, jnp.max(sc, axis=-1, keepdims=True))
                p = jnp.exp(sc - m_new)
                alpha = jnp.exp(m_prev - m_new)
                l_ref[hh] = alpha * l_ref[hh] + jnp.sum(p, axis=-1, keepdims=True)
                acc_ref[:, cols] = alpha * acc_ref[:, cols] + _dot(p.astype(BF16), v, 1, 0)
                m_ref[hh] = m_new

        @pl.when(kj < qi)
        def _():
            step(False)

        @pl.when(kj == qi)
        def _():
            step(True)
            for hh in range(hp):
                cols = slice(hh * HEAD, (hh + 1) * HEAD)
                o_ref[:, cols] = (acc_ref[:, cols] / l_ref[hh]).astype(o_ref.dtype)
                lse_ref[:, cols] = jnp.broadcast_to(m_ref[hh] + jnp.log(l_ref[hh]), (t, HEAD))

    qspec = pl.BlockSpec((t, wide), lambda h, st, qt, kt: (qt[st], h))
    grid_spec = pltpu.PrefetchScalarGridSpec(
        num_scalar_prefetch=2, grid=(groups, len(pairs)),
        in_specs=[qspec, qspec,
                  pl.BlockSpec((t, 2 * wide), lambda h, st, qt, kt: (kt[st], h)),
                  pl.BlockSpec((t, HEAD), lambda h, st, qt, kt: (kt[st], 0))],
        out_specs=[qspec, qspec],
        scratch_shapes=[pltpu.VMEM((hp, t, 1), F32), pltpu.VMEM((hp, t, 1), F32), pltpu.VMEM((t, wide), F32)])
    return pl.pallas_call(
        body, name=name, grid_spec=grid_spec,
        out_shape=[jax.ShapeDtypeStruct((s, heads * HEAD), BF16), jax.ShapeDtypeStruct((s, heads * HEAD), F32)],
        compiler_params=_params(("parallel", "arbitrary")),
    )(qtab, ktab, qall, qr, kv, kr)


def mla_bwd(qall, qr, kv, kr, o, do, lse, heads, *, name):
    s = qr.shape[0]
    t = min(ATT_T, s)
    nq = s // t
    scale = 1.0 / math.sqrt(HEAD + ROPE_DIM)
    pairs = [(kj, qi) for kj in range(nq) for qi in range(kj, nq)]
    ktab, qtab = _tables(pairs)
    last = len(pairs) - 1
    hp, groups, wide = _head_groups(heads, MLA_BWD_HEADS_PER_STEP)

    def body(kt_ref, qt_ref, qn_ref, qr_ref, kv_ref, kr_ref, o_ref, do_ref, lse_ref,
             dqn_ref, dqr_ref, dkv_ref, dkr_ref, fn_ref, fr_ref, akn_ref, av_ref, akr_ref):
        st = pl.program_id(1)
        kj, qi = kt_ref[st], qt_ref[st]

        @pl.when(st == 0)
        def _():
            fn_ref[...] = jnp.zeros_like(fn_ref)
            fr_ref[...] = jnp.zeros_like(fr_ref)

        @pl.when(qi == kj)
        def _():
            akn_ref[...] = jnp.zeros_like(akn_ref)
            av_ref[...] = jnp.zeros_like(av_ref)
            akr_ref[...] = jnp.zeros_like(akr_ref)

        def step(diagonal):
            rows = pl.ds(pl.multiple_of(qi * t, t), t)
            krv = kr_ref[...]
            for hh in range(hp):
                cols = slice(hh * HEAD, (hh + 1) * HEAD)
                qn, qrv, dov = qn_ref[:, cols], qr_ref[:, cols], do_ref[:, cols]
                kn = kv_ref[:, 2 * hh * HEAD:(2 * hh + 1) * HEAD]
                v = kv_ref[:, (2 * hh + 1) * HEAD:(2 * hh + 2) * HEAD]
                sc = _mla_scores(qn, qrv, kn, krv, scale, diagonal)
                p = jnp.exp(sc - lse_ref[:, hh * HEAD:hh * HEAD + 1])
                delta = jnp.sum(dov.astype(F32) * o_ref[:, cols].astype(F32), axis=-1, keepdims=True)
                dp = _dot(dov, v, 1, 1)
                ds = (p * (dp - delta) * scale).astype(BF16)
                av_ref[:, cols] += _dot(p.astype(BF16), dov, 0, 0)
                akn_ref[:, cols] += _dot(ds, qn, 0, 0)
                akr_ref[:, cols] += _dot(ds, qrv, 0, 0)
                fn_ref[rows, cols] += _dot(ds, kn, 1, 0)
                fr_ref[rows, cols] += _dot(ds, krv, 1, 0)

        @pl.when(qi == kj)
        def _():
            step(True)

        @pl.when(qi > kj)
        def _():
            step(False)

        @pl.when(qi == nq - 1)
        def _():
            for hh in range(hp):
                cols = slice(hh * HEAD, (hh + 1) * HEAD)
                dkv_ref[:, 2 * hh * HEAD:(2 * hh + 1) * HEAD] = akn_ref[:, cols].astype(dkv_ref.dtype)
                dkv_ref[:, (2 * hh + 1) * HEAD:(2 * hh + 2) * HEAD] = av_ref[:, cols].astype(dkv_ref.dtype)
            dkr_ref[...] = akr_ref[...]

        @pl.when(st == last)
        def _():
            dqn_ref[...] = fn_ref[...].astype(dqn_ref.dtype)
            dqr_ref[...] = fr_ref[...].astype(dqr_ref.dtype)

    qspec = pl.BlockSpec((t, wide), lambda h, st, kt, qt: (qt[st], h))
    kvspec = pl.BlockSpec((t, 2 * wide), lambda h, st, kt, qt: (kt[st], h))
    krspec = pl.BlockSpec((t, HEAD), lambda h, st, kt, qt: (kt[st], 0))
    headspec = pl.BlockSpec((s, wide), lambda h, st, kt, qt: (0, h))
    grid_spec = pltpu.PrefetchScalarGridSpec(
        num_scalar_prefetch=2, grid=(groups, len(pairs)),
        in_specs=[qspec, qspec, kvspec, krspec, qspec, qspec, qspec],
        out_specs=[headspec, headspec, kvspec, pl.BlockSpec((t, wide), lambda h, st, kt, qt: (kt[st], h))],
        scratch_shapes=[pltpu.VMEM((s, wide), F32), pltpu.VMEM((s, wide), F32), pltpu.VMEM((t, wide), F32),
                        pltpu.VMEM((t, wide), F32), pltpu.VMEM((t, wide), F32)])
    return pl.pallas_call(
        body, name=name, grid_spec=grid_spec,
        out_shape=[jax.ShapeDtypeStruct((s, heads * HEAD), BF16), jax.ShapeDtypeStruct((s, heads * HEAD), BF16),
                   jax.ShapeDtypeStruct((s, heads * 2 * HEAD), BF16), jax.ShapeDtypeStruct((s, heads * HEAD), F32)],
        compiler_params=_params(("parallel", "arbitrary")),
    )(ktab, qtab, qall, qr, kv, kr, o, do, lse)


def _split_dot(val, tri, cb):
    hi = val.astype(BF16)
    lo = (val - hi.astype(F32)).astype(BF16)
    return _dot(hi, tri, 1, cb) + _dot(lo, tri, 1, cb)


def _sb_logs(q, k, offset, scale, masked):
    z = _dot(q, k, 1, 1) * (scale * LOG2E)
    sp = jnp.log2(1.0 + jnp.exp2(-jnp.abs(z)))
    ls = jnp.minimum(z, 0.0) - sp
    lk = ls - z
    if not masked:
        return None, ls, lk
    strict = (lax.broadcasted_iota(jnp.int32, z.shape, 1) + offset) < lax.broadcasted_iota(jnp.int32, z.shape, 0)
    return strict, ls, jnp.where(strict, lk, 0.0)


def _lane_pick(blk, idx):
    lane = lax.broadcasted_iota(jnp.int32, blk.shape, 1)
    return jnp.sum(jnp.where(lane == idx, blk, 0.0), axis=-1, keepdims=True)


def _lane_put(blk, idx, col):
    lane = lax.broadcasted_iota(jnp.int32, blk.shape, 1)
    return jnp.where(lane == idx, col, blk)


def _sb_tiles(s):
    tq = min(SB_TQ, s)
    tk = min(SB_TK, tq)
    assert s // tk <= HEAD
    return tq, tk


def sb_fwd(qkv, tri, heads, *, name):
    s = qkv.shape[0]
    tq, tk = _sb_tiles(s)
    nq, ratio = s // tq, tq // tk
    scale = 1.0 / math.sqrt(HEAD)
    pairs = [(qi, kj) for qi in range(nq) for kj in range((qi + 1) * ratio - 1, -1, -1)]
    qtab, ktab = _tables(pairs)
    hp, groups, wide = _head_groups(heads, SB_HEADS_PER_STEP)

    def body(qt_ref, kt_ref, q_ref, k_ref, v_ref, tri_ref, o_ref, c_ref, carry_ref, acc_ref):
        st = pl.program_id(1)
        qi, kj = qt_ref[st], kt_ref[st]

        @pl.when(kj == (qi + 1) * ratio - 1)
        def _():
            carry_ref[...] = jnp.zeros_like(carry_ref)
            acc_ref[...] = jnp.zeros_like(acc_ref)
            c_ref[...] = jnp.zeros_like(c_ref)

        def step(masked):
            for hh in range(hp):
                cols = slice(hh * HEAD, (hh + 1) * HEAD)
                strict, ls, lk = _sb_logs(q_ref[:, cols], k_ref[:, cols], kj * tk - qi * tq, scale, masked)
                carry = carry_ref[hh]
                a = jnp.exp2(ls + _split_dot(lk, tri_ref[...], 0) + carry)
                if masked:
                    a = jnp.where(strict, a, 0.0)
                acc_ref[:, cols] += _dot(a.astype(BF16), v_ref[:, cols], 1, 0)
                c_ref[:, cols] = _lane_put(c_ref[:, cols], kj, carry)
                carry_ref[hh] = carry + jnp.sum(lk, axis=-1, keepdims=True)

        @pl.when((kj + 1) * tk > qi * tq)
        def _():
            step(True)

        @pl.when((kj + 1) * tk <= qi * tq)
        def _():
            step(False)

        @pl.when(kj == 0)
        def _():
            o_ref[...] = acc_ref[...].astype(o_ref.dtype)

    qspec = pl.BlockSpec((tq, wide), lambda h, st, qt, kt: (qt[st], h))
    grid_spec = pltpu.PrefetchScalarGridSpec(
        num_scalar_prefetch=2, grid=(groups, len(pairs)),
        in_specs=[qspec,
                  pl.BlockSpec((tk, wide), lambda h, st, qt, kt: (kt[st], groups + h)),
                  pl.BlockSpec((tk, wide), lambda h, st, qt, kt: (kt[st], 2 * groups + h)),
                  pl.BlockSpec((tk, tk), lambda h, st, qt, kt: (0, 0))],
        out_specs=[qspec, qspec],
        scratch_shapes=[pltpu.VMEM((hp, tq, 1), F32), pltpu.VMEM((tq, wide), F32)])
    return pl.pallas_call(
        body, name=name, grid_spec=grid_spec,
        out_shape=[jax.ShapeDtypeStruct((s, heads * HEAD), BF16), jax.ShapeDtypeStruct((s, heads * HEAD), F32)],
        compiler_params=_params(("parallel", "arbitrary")),
    )(qtab, ktab, qkv, qkv, qkv, tri)


def sb_bwd(qkv, do, cmat, tri, heads, *, name):
    s = qkv.shape[0]
    tq, tk = _sb_tiles(s)
    nq, nkb, ratio = s // tq, s // tk, tq // tk
    scale = 1.0 / math.sqrt(HEAD)
    pairs = [(kj, qi) for kj in range(nkb) for qi in range(kj // ratio, nq)]
    ktab, qtab = _tables(pairs)
    last = len(pairs) - 1
    hp, groups, wide = _head_groups(heads, SB_HEADS_PER_STEP)

    def body(kt_ref, qt_ref, q_ref, k_ref, v_ref, do_ref, c_ref, tri_ref, dq_ref, dk_ref, dv_ref,
             dqf_ref, gsum_ref, ak_ref, av_ref):
        st = pl.program_id(1)
        kj, qi = kt_ref[st], qt_ref[st]

        @pl.when(st == 0)
        def _():
            dqf_ref[...] = jnp.zeros_like(dqf_ref)
            gsum_ref[...] = jnp.zeros_like(gsum_ref)

        @pl.when(qi == kj // ratio)
        def _():
            ak_ref[...] = jnp.zeros_like(ak_ref)
            av_ref[...] = jnp.zeros_like(av_ref)

        def step(masked):
            rows = pl.ds(pl.multiple_of(qi * tq, tq), tq)
            tri_v = tri_ref[...]
            for hh in range(hp):
                cols = slice(hh * HEAD, (hh + 1) * HEAD)
                qv, kblk, dov = q_ref[:, cols], k_ref[:, cols], do_ref[:, cols]
                strict, ls, lk = _sb_logs(qv, kblk, kj * tk - qi * tq, scale, masked)
                a = jnp.exp2(ls + _split_dot(lk, tri_v, 0) + _lane_pick(c_ref[:, cols], kj))
                if masked:
                    a = jnp.where(strict, a, 0.0)
                g = _dot(dov, v_ref[:, cols], 1, 1) * a
                before_all = gsum_ref[hh, rows, :]
                before = _split_dot(g, tri_v, 1) + before_all[:, :1]
                beta = jnp.exp2(ls)
                dz = g * (1.0 - beta) - before * beta
                if masked:
                    dz = jnp.where(strict, dz, 0.0)
                dzb = dz.astype(BF16)
                av_ref[:, cols] += _dot(a.astype(BF16), dov, 0, 0)
                ak_ref[:, cols] += _dot(dzb, qv, 0, 0)
                dqf_ref[rows, cols] += _dot(dzb, kblk, 1, 0)
                gsum_ref[hh, rows, :] = before_all + jnp.sum(g, axis=-1, keepdims=True)

        @pl.when((kj + 1) * tk > qi * tq)
        def _():
            step(True)

        @pl.when((kj + 1) * tk <= qi * tq)
        def _():
            step(False)

        @pl.when(qi == nq - 1)
        def _():
            dk_ref[...] = (ak_ref[...] * scale).astype(dk_ref.dtype)
            dv_ref[...] = av_ref[...].astype(dv_ref.dtype)

        @pl.when(st == last)
        def _():
            dq_ref[...] = (dqf_ref[...] * scale).astype(dq_ref.dtype)

    qspec = pl.BlockSpec((tq, wide), lambda h, st, kt, qt: (qt[st], h))
    ospec = pl.BlockSpec((tk, wide), lambda h, st, kt, qt: (kt[st], h))
    grid_spec = pltpu.PrefetchScalarGridSpec(
        num_scalar_prefetch=2, grid=(groups, len(pairs)),
        in_specs=[qspec,
                  pl.BlockSpec((tk, wide), lambda h, st, kt, qt: (kt[st], groups + h)),
                  pl.BlockSpec((tk, wide), lambda h, st, kt, qt: (kt[st], 2 * groups + h)),
                  qspec, qspec, pl.BlockSpec((tk, tk), lambda h, st, kt, qt: (0, 0))],
        out_specs=[pl.BlockSpec((s, wide), lambda h, st, kt, qt: (0, h)), ospec, ospec],
        scratch_shapes=[pltpu.VMEM((s, wide), F32), pltpu.VMEM((hp, s, HEAD), F32), pltpu.VMEM((tk, wide), F32),
                        pltpu.VMEM((tk, wide), F32)])
    return pl.pallas_call(
        body, name=name, grid_spec=grid_spec,
        out_shape=[jax.ShapeDtypeStruct((s, heads * HEAD), BF16)] * 3,
        compiler_params=_params(("parallel", "arbitrary")),
    )(ktab, qtab, qkv, qkv, qkv, do, cmat, tri)


def _mem_probs(q, k, scale):
    sc = _dot(q, k, 1, 1) * scale
    e = jnp.exp(sc - jnp.max(sc, axis=-1, keepdims=True))
    return e / jnp.sum(e, axis=-1, keepdims=True)


def mem_fwd(q, kvm, heads, *, name):
    s, nm = q.shape[0], kvm.shape[0]
    tq = min(MEM_TQ, s)
    scale = 1.0 / math.sqrt(HEAD)

    def body(q_ref, k_ref, v_ref, o_ref):
        p = _mem_probs(q_ref[...], k_ref[...], scale)
        o_ref[...] = _dot(p.astype(BF16), v_ref[...], 1, 0).astype(o_ref.dtype)

    qspec = pl.BlockSpec((tq, HEAD), lambda h, qi: (qi, h))
    return pl.pallas_call(
        body, name=name, grid=(heads, s // tq),
        in_specs=[qspec, pl.BlockSpec((nm, HEAD), lambda h, qi: (0, h)),
                  pl.BlockSpec((nm, HEAD), lambda h, qi: (0, heads + h))],
        out_specs=qspec, out_shape=jax.ShapeDtypeStruct((s, heads * HEAD), BF16),
        compiler_params=_params(("parallel", "parallel")),
    )(q, kvm, kvm)


def mem_bwd(q, kvm, do, heads, *, name):
    s, nm = q.shape[0], kvm.shape[0]
    tq = min(MEM_TQ, s)
    scale = 1.0 / math.sqrt(HEAD)

    def body(q_ref, k_ref, v_ref, do_ref, dq_ref, dk_ref, dv_ref):
        @pl.when(pl.program_id(1) == 0)
        def _():
            dk_ref[...] = jnp.zeros_like(dk_ref)
            dv_ref[...] = jnp.zeros_like(dv_ref)

        qv, kvv, dov = q_ref[...], k_ref[...], do_ref[...]
        p = _mem_probs(qv, kvv, scale)
        dp = _dot(dov, v_ref[...], 1, 1)
        ds = (p * (dp - jnp.sum(dp * p, axis=-1, keepdims=True)) * scale).astype(BF16)
        dq_ref[...] = _dot(ds, kvv, 1, 0).astype(dq_ref.dtype)
        dk_ref[...] += _dot(ds, qv, 0, 0)
        dv_ref[...] += _dot(p.astype(BF16), dov, 0, 0)

    qspec = pl.BlockSpec((tq, HEAD), lambda h, qi: (qi, h))
    kspec = pl.BlockSpec((nm, HEAD), lambda h, qi: (0, h))
    return pl.pallas_call(
        body, name=name, grid=(heads, s // tq),
        in_specs=[qspec, kspec, pl.BlockSpec((nm, HEAD), lambda h, qi: (0, heads + h)), qspec],
        out_specs=[qspec, kspec, kspec],
        out_shape=[jax.ShapeDtypeStruct((s, heads * HEAD), BF16), jax.ShapeDtypeStruct((nm, heads * HEAD), F32),
                   jax.ShapeDtypeStruct((nm, heads * HEAD), F32)],
        compiler_params=_params(("parallel", "arbitrary")),
    )(q, kvm, kvm, do)


def _conv3(u, halo, w, b):
    tm = u.shape[0]
    row = lax.broadcasted_iota(jnp.int32, u.shape, 0)
    h1, h2 = halo[HALO - 1:HALO, :], halo[HALO - 2:HALO - 1, :]
    u1 = jnp.where(row == 0, h1, pltpu.roll(u, 1, 0))
    u2 = jnp.where(row == 0, h2, jnp.where(row == 1, h1, pltpu.roll(u, 2 % tm, 0)))
    return b + w[0:1, :] * u2 + w[1:2, :] * u1 + w[2:3, :] * u, u1, u2


def _conv_specs(s, f):
    tm, tn = min(CONV_TM, s), _pick(f, CONV_TN)
    return tm, tn, s // tm, f // tn


def _silu_parts(g):
    sg = 1.0 / (1.0 + jnp.exp(-g))
    return g * sg, sg


def conv_gate_fwd(u, cw, cb, *, name):
    s, f = u.shape[0], u.shape[1] // 2
    tm, tn, ni, nj = _conv_specs(s, f)
    hb = tm // HALO

    def body(ug_ref, uu_ref, hg_ref, hu_ref, wg_ref, wu_ref, bg_ref, bu_ref, a_ref):
        keep = (pl.program_id(1) > 0).astype(F32)
        gate, _, _ = _conv3(ug_ref[...].astype(F32), hg_ref[...].astype(F32) * keep, wg_ref[...], bg_ref[...])
        up, _, _ = _conv3(uu_ref[...].astype(F32), hu_ref[...].astype(F32) * keep, wu_ref[...], bu_ref[...])
        a_ref[...] = (_silu_parts(gate)[0] * up).astype(a_ref.dtype)

    def main(off):
        return pl.BlockSpec((tm, tn), lambda j, i: (i, j + off))

    def halo(off):
        return pl.BlockSpec((HALO, tn), lambda j, i: (jnp.maximum(i * hb - 1, 0), j + off))

    def par(rows, off):
        return pl.BlockSpec((rows, tn), lambda j, i: (0, j + off))

    return pl.pallas_call(
        body, name=name, grid=(nj, ni),
        in_specs=[main(0), main(nj), halo(0), halo(nj), par(3, 0), par(3, nj), par(1, 0), par(1, nj)],
        out_specs=main(0), out_shape=jax.ShapeDtypeStruct((s, f), BF16),
        compiler_params=_params(("parallel", "parallel")),
    )(u, u, u, u, cw, cw, cb, cb)


def conv_gate_bwd(u, da, cw, cb, *, name):
    s, f = u.shape[0], u.shape[1] // 2
    tm, tn, ni, nj = _conv_specs(s, f)
    hb = tm // HALO

    def body(ug_ref, uu_ref, hg_ref, hu_ref, da_ref, wg_ref, wu_ref, bg_ref, bu_ref, dg_ref, du_ref, pg_ref, pu_ref):
        @pl.when(pl.program_id(1) == 0)
        def _():
            pg_ref[...] = jnp.zeros_like(pg_ref)
            pu_ref[...] = jnp.zeros_like(pu_ref)

        keep = (pl.program_id(1) > 0).astype(F32)
        ug, uu = ug_ref[...].astype(F32), uu_ref[...].astype(F32)
        gate, ug1, ug2 = _conv3(ug, hg_ref[...].astype(F32) * keep, wg_ref[...], bg_ref[...])
        up, uu1, uu2 = _conv3(uu, hu_ref[...].astype(F32) * keep, wu_ref[...], bu_ref[...])
        act, sg = _silu_parts(gate)
        dav = da_ref[...].astype(F32)
        d_gate = dav * up * (sg * (1.0 + gate * (1.0 - sg)))
        d_up = dav * act
        dg_ref[...] = d_gate.astype(dg_ref.dtype)
        du_ref[...] = d_up.astype(du_ref.dtype)
        for p_ref, dc, taps in ((pg_ref, d_gate, (ug2, ug1, ug)), (pu_ref, d_up, (uu2, uu1, uu))):
            for r, tap in enumerate(taps):
                p_ref[r:r + 1, :] += jnp.sum(dc * tap, axis=0, keepdims=True)
            p_ref[3:4, :] += jnp.sum(dc, axis=0, keepdims=True)

    def main(off):
        return pl.BlockSpec((tm, tn), lambda j, i: (i, j + off))

    def halo(off):
        return pl.BlockSpec((HALO, tn), lambda j, i: (jnp.maximum(i * hb - 1, 0), j + off))

    def par(rows, off):
        return pl.BlockSpec((rows, tn), lambda j, i: (0, j + off))

    dg, du, pg, pu = pl.pallas_call(
        body, name=name, grid=(nj, ni),
        in_specs=[main(0), main(nj), halo(0), halo(nj), main(0), par(3, 0), par(3, nj), par(1, 0), par(1, nj)],
        out_specs=[main(0), main(0), par(8, 0), par(8, 0)],
        out_shape=[jax.ShapeDtypeStruct((s, f), BF16), jax.ShapeDtypeStruct((s, f), BF16),
                   jax.ShapeDtypeStruct((8, f), F32), jax.ShapeDtypeStruct((8, f), F32)],
        compiler_params=_params(("parallel", "arbitrary")),
    )(u, u, u, u, da, cw, cw, cb, cb)
    return dg, du, jnp.concatenate([pg, pu], axis=1)


def conv_transpose(dc, w, *, name):
    s, f = dc.shape
    tm, tn, ni, nj = _conv_specs(s, f)
    hb = tm // HALO

    def body(d_ref, h_ref, w_ref, o_ref):
        d = d_ref[...].astype(F32)
        halo = h_ref[...].astype(F32) * (pl.program_id(1) < ni - 1).astype(F32)
        row = lax.broadcasted_iota(jnp.int32, d.shape, 0)
        n0, n1 = halo[0:1, :], halo[1:2, :]
        d1 = jnp.where(row == tm - 1, n0, pltpu.roll(d, tm - 1, 0))
        d2 = jnp.where(row == tm - 2, n0, jnp.where(row == tm - 1, n1, pltpu.roll(d, tm - 2, 0)))
        wv = w_ref[...]
        o_ref[...] = (wv[2:3, :] * d + wv[1:2, :] * d1 + wv[0:1, :] * d2).astype(o_ref.dtype)

    main = pl.BlockSpec((tm, tn), lambda j, i: (i, j))
    return pl.pallas_call(
        body, name=name, grid=(nj, ni),
        in_specs=[main, pl.BlockSpec((HALO, tn), lambda j, i: (jnp.minimum((i + 1) * hb, s // HALO - 1), j)),
                  pl.BlockSpec((3, tn), lambda j, i: (0, j))],
        out_specs=main, out_shape=jax.ShapeDtypeStruct((s, f), BF16),
        compiler_params=_params(("parallel", "parallel")),
    )(dc, dc, w)


def _tile2d(r, c):
    return _pick(r, ROW_TILE), _pick(c, (2048, 1024, 512, 256, 128))


def sum_slots(buf, *, name):
    n, shape = buf.shape[0], buf.shape[1:]
    r, c = math.prod(shape[:-1]), shape[-1]
    tm, tn = _tile2d(r, c)

    def body(b_ref, o_ref):
        acc = b_ref[0]
        for k in range(1, n):
            acc = acc + b_ref[k]
        o_ref[...] = acc

    out = pl.pallas_call(
        body, name=name, grid=(r // tm, c // tn),
        in_specs=[pl.BlockSpec((n, tm, tn), lambda i, j: (0, i, j))],
        out_specs=pl.BlockSpec((tm, tn), lambda i, j: (i, j)), out_shape=jax.ShapeDtypeStruct((r, c), F32),
        compiler_params=_params(("parallel", "parallel")),
    )(buf.reshape(n, r, c))
    return out.reshape(shape)


def add_half(layers, got, core, *, name):
    n_l = len(layers)
    a, _, r, c = layers[0].shape
    tm, tn = _tile2d(r, c)

    def body(core_ref, *refs):
        got_ref, o_ref = refs[n_l:]
        for l in range(n_l):
            @pl.when(pl.program_id(0) == l)
            def _(l=l):
                o_ref[...] = (refs[l][...] + got_ref[...]).astype(o_ref.dtype)

    def layer_spec(l):
        def index(b, q, i, j, core_ref):
            on = b == l
            return (jnp.where(on, q, 0), core_ref[0], jnp.where(on, i, 0), jnp.where(on, j, 0))
        return pl.BlockSpec((None, None, tm, tn), index)

    part = pl.BlockSpec((None, None, tm, tn), lambda b, q, i, j, core_ref: (b, q, i, j))
    grid_spec = pltpu.PrefetchScalarGridSpec(
        num_scalar_prefetch=1, grid=(n_l, a, r // tm, c // tn),
        in_specs=[layer_spec(l) for l in range(n_l)] + [part], out_specs=part)
    return pl.pallas_call(
        body, name=name, grid_spec=grid_spec, out_shape=jax.ShapeDtypeStruct((n_l, a, r, c), BF16),
        compiler_params=_params(("arbitrary", "parallel", "parallel", "parallel")),
    )(core, *layers, got)


def add_own_block(h, got, chip, axis, *, name):
    _, l, r, c = got.shape
    tm, tn = _tile2d(r, c)
    ncb = c // tn

    def body(chip_ref, h_ref, got_ref, o_ref):
        o_ref[...] = ((h_ref[...].astype(F32) + got_ref[0].astype(F32)) + got_ref[1].astype(F32)) + got_ref[2].astype(F32)

    if axis == 0:
        h_spec = pl.BlockSpec((None, None, tm, tn), lambda b, i, j, chip_ref: (b, chip_ref[0], i, j))
    else:
        h_spec = pl.BlockSpec((None, tm, tn), lambda b, i, j, chip_ref: (b, i, chip_ref[0] * ncb + j))
    grid_spec = pltpu.PrefetchScalarGridSpec(
        num_scalar_prefetch=1, grid=(l, r // tm, ncb),
        in_specs=[h_spec, pl.BlockSpec((3, None, tm, tn), lambda b, i, j, chip_ref: (0, b, i, j))],
        out_specs=pl.BlockSpec((None, tm, tn), lambda b, i, j, chip_ref: (b, i, j)))
    return pl.pallas_call(
        body, name=name, grid_spec=grid_spec, out_shape=jax.ShapeDtypeStruct((l, r, c), F32),
        compiler_params=_params(("parallel", "parallel", "parallel")),
    )(chip, h, got)


def _adam_update(w, g, m, v):
    bc1, bc2 = 1.0 - ADAM_B1 ** ADAM_STEP, 1.0 - ADAM_B2 ** ADAM_STEP
    mn = ADAM_B1 * m + (1.0 - ADAM_B1) * g
    vn = ADAM_B2 * v + (1.0 - ADAM_B2) * (g * g)
    return -ADAM_LR * ((mn / bc1) / (jnp.sqrt(vn / bc2) + ADAM_EPS) + ADAM_WD * w), mn, vn


def adamw(w, g, m, v, *, name):
    shape = w.shape
    c = shape[-1]
    r = math.prod(shape[:-1]) if len(shape) > 1 else 1
    tm, tn = _tile2d(r, c)

    def body(w_ref, g_ref, m_ref, v_ref, d_ref, mo_ref, vo_ref):
        d_ref[...], mo_ref[...], vo_ref[...] = _adam_update(w_ref[...], g_ref[...], m_ref[...], v_ref[...])

    spec = pl.BlockSpec((tm, tn), lambda i, j: (i, j))
    outs = pl.pallas_call(
        body, name=name, grid=(r // tm, c // tn), in_specs=[spec] * 4, out_specs=[spec] * 3,
        out_shape=[jax.ShapeDtypeStruct((r, c), F32)] * 3, compiler_params=_params(("parallel", "parallel")),
    )(*(t.reshape(r, c) for t in (w, g, m, v)))
    return tuple(o.reshape(shape) for o in outs)


def adamw_halves(w, mine, got, m, v, core, *, name):
    l, r, c = w.shape
    rh = r // 2
    tm, tn = _tile2d(rh, c)

    def body(core_ref, w_ref, mine_ref, got_ref, m_ref, v_ref, g_ref, d_ref, mo_ref, vo_ref):
        g = jnp.where(pl.program_id(1) == core_ref[0], mine_ref[...], got_ref[...])
        g_ref[...] = g
        d_ref[...], mo_ref[...], vo_ref[...] = _adam_update(w_ref[...], g, m_ref[...], v_ref[...])

    full = pl.BlockSpec((None, None, tm, tn), lambda b, h, i, j, core_ref: (b, h, i, j))
    half = pl.BlockSpec((None, tm, tn), lambda b, h, i, j, core_ref: (b, i, j))
    grid_spec = pltpu.PrefetchScalarGridSpec(
        num_scalar_prefetch=1, grid=(l, 2, rh // tm, c // tn),
        in_specs=[full, half, half, full, full], out_specs=[full] * 4)
    outs = pl.pallas_call(
        body, name=name, grid_spec=grid_spec, out_shape=[jax.ShapeDtypeStruct((l, 2, rh, c), F32)] * 4,
        compiler_params=_params(("parallel", "parallel", "parallel", "parallel")),
    )(core, w.reshape(l, 2, rh, c), mine, got, m.reshape(l, 2, rh, c), v.reshape(l, 2, rh, c))
    return tuple(o.reshape(l, r, c) for o in outs)


def _comm(name, inputs, out_shapes, aliases, plan):
    n_in, n_out = len(inputs), len(out_shapes)
    probe = plan([None] * n_in, [None] * n_out, 0, 0, 0, count_only=True)

    def body(*refs):
        in_refs, out_refs = refs[:n_in], refs[n_in:n_in + n_out]
        send_sems, recv_sems = refs[n_in + n_out:]
        x, y, c = lax.axis_index("x"), lax.axis_index("y"), lax.axis_index("c")
        xfers = plan(in_refs, out_refs, x, y, c, count_only=False)
        started = []
        for n, (src, dst, peer, _) in enumerate(xfers):
            if peer is None:
                cp = pltpu.make_async_copy(src, dst, send_sems.at[n])
            else:
                cp = pltpu.make_async_remote_copy(src_ref=src, dst_ref=dst, send_sem=send_sems.at[n],
                                                  recv_sem=recv_sems.at[n], device_id=peer, device_id_type=MESH)
            cp.start()
            started.append(cp)
        for n, (src, dst, peer, got) in enumerate(xfers):
            if peer is None:
                started[n].wait()
            else:
                pltpu.make_async_remote_copy(src_ref=got, dst_ref=got, send_sem=send_sems.at[n],
                                             recv_sem=recv_sems.at[n], device_id=peer,
                                             device_id_type=MESH).wait_recv()
                started[n].wait_send()

    hbm = pl.BlockSpec(memory_space=pl.ANY)
    return pl.pallas_call(
        body, name=name, in_specs=[hbm] * n_in, out_specs=[hbm] * n_out, out_shape=out_shapes,
        input_output_aliases=aliases,
        scratch_shapes=[pltpu.SemaphoreType.DMA((probe,)), pltpu.SemaphoreType.DMA((probe,))],
    )(*inputs)


def _other_chips(x, y):
    return [(1 - x, y), (x, 1 - y), (1 - x, 1 - y)]


def gather_weights(shards, axes):
    def out_shape(sh, ax):
        l, r, c = sh.shape
        return jax.ShapeDtypeStruct((l, N_CHIPS, 2, r // 2, c) if ax == 0 else (l, 2, r // 2, N_CHIPS * c), BF16)

    def piece(ref, ax, chip, half, width):
        if ax == 0:
            return ref.at[:, chip, half]
        return ref.at[:, half, :, pl.ds(chip * width, width)]

    def plan1(in_refs, out_refs, x, y, c, count_only):
        if count_only:
            return 4 * len(shards)
        me, xfers = 2 * x + y, []
        for sh, ax, src, out in zip(shards, axes, in_refs, out_refs):
            rows, width = sh.shape[1] // 2, sh.shape[2]
            mine = src.at[:, pl.ds(c * rows, rows), :]
            xfers.append((mine, piece(out, ax, me, c, width), None, None))
            for cx, cy in _other_chips(x, y):
                xfers.append((mine, piece(out, ax, me, c, width), (cx, cy, c), piece(out, ax, 2 * cx + cy, c, width)))
        return xfers

    def half_of(ref, ax, half):
        return ref.at[:, :, half] if ax == 0 else ref.at[:, half]

    def plan2(in_refs, out_refs, x, y, c, count_only):
        if count_only:
            return len(shards)
        return [(half_of(out, ax, c), half_of(out, ax, c), (x, y, 1 - c), half_of(out, ax, 1 - c))
                for ax, out in zip(axes, out_refs)]

    shapes = [out_shape(sh, ax) for sh, ax in zip(shards, axes)]
    part = _comm("gather_chips", list(shards), shapes, {}, plan1)
    full = _comm("gather_cores", list(part), shapes, {n: n for n in range(len(shards))}, plan2)
    return [f.reshape(sh.shape[0], N_CHIPS * sh.shape[1], sh.shape[2]) if ax == 0
            else f.reshape(sh.shape[0], sh.shape[1], N_CHIPS * sh.shape[2])
            for f, sh, ax in zip(full, shards, axes)]


def reduce_scatter_grads(grads, axes, core, chip):
    dims, views = [], []
    for layers, ax in zip(grads, axes):
        r, c = layers[0].shape
        if ax == 0:
            dims.append((len(layers), r // N_CHIPS // 2, c))
            views.append([g.reshape(N_CHIPS, 2, r // N_CHIPS // 2, c) for g in layers])
        else:
            dims.append((len(layers), r // 2, c // N_CHIPS))
            views.append([g.reshape(1, 2, r // 2, c) for g in layers])

    def half_shape(ax, d):
        return (d[0], N_CHIPS, d[1], d[2]) if ax == 0 else (d[0], d[1], N_CHIPS * d[2])

    def plan_a(in_refs, out_refs, x, y, c, count_only):
        if count_only:
            return sum(d[0] for d in dims)
        xfers, n = [], 0
        for d, out in zip(dims, out_refs):
            for l in range(d[0]):
                xfers.append((in_refs[n].at[:, 1 - c], out.at[l], (x, y, 1 - c), out.at[l]))
                n += 1
        return xfers

    got = _comm("reduce_cores", [v for layers in views for v in layers],
                [jax.ShapeDtypeStruct((d[0], N_CHIPS if ax == 0 else 1, d[1], d[2] if ax == 0 else N_CHIPS * d[2]), F32)
                 for ax, d in zip(axes, dims)], {}, plan_a)
    chip_sum = [add_half(layers, g, core, name="sum_cores").reshape(half_shape(ax, d))
                for ax, d, layers, g in zip(axes, dims, views, got)]

    def block(ref, ax, which, width):
        return ref.at[:, which] if ax == 0 else ref.at[:, :, pl.ds(which * width, width)]

    def plan_b(in_refs, out_refs, x, y, c, count_only):
        if count_only:
            return 3 * len(grads)
        xfers = []
        for ax, d, src, out in zip(axes, dims, in_refs, out_refs):
            for k, (cx, cy) in enumerate(_other_chips(x, y)):
                xfers.append((block(src, ax, 2 * cx + cy, d[2]), out.at[k], (cx, cy, c), out.at[k]))
        return xfers

    parts = _comm("reduce_chips", chip_sum, [jax.ShapeDtypeStruct((3,) + d, BF16) for d in dims], {}, plan_b)
    mine = [add_own_block(h, p, chip, ax, name="sum_chips") for h, p, ax in zip(chip_sum, parts, axes)]

    def plan_c(in_refs, out_refs, x, y, c, count_only):
        if count_only:
            return len(grads)
        return [(src, out, (x, y, 1 - c), out) for src, out in zip(in_refs, out_refs)]

    theirs = _comm("share_cores", mine, [jax.ShapeDtypeStruct(d, F32) for d in dims], {}, plan_c)
    return list(zip(mine, theirs))


def gather_slabs(vec, *, name):
    def plan(in_refs, out_refs, x, y, c, count_only):
        if count_only:
            return 8
        me = 4 * x + 2 * y + c
        xfers = [(in_refs[0], out_refs[0].at[me], None, None)]
        for k in range(1, 8):
            px, py, pc = x ^ (k >> 2), y ^ ((k >> 1) & 1), c ^ (k & 1)
            xfers.append((in_refs[0], out_refs[0].at[me], (px, py, pc), out_refs[0].at[4 * px + 2 * py + pc]))
        return xfers

    return _comm(name, [vec], [jax.ShapeDtypeStruct((8,) + vec.shape, F32)], {}, plan)[0]


def allreduce_small(vec):
    return sum_slots(gather_slabs(vec, name="gather_small"), name="sum_small")


def gather_conv_w(block):
    l, taps, c = block.shape
    flat = block.reshape(-1)
    slabs = gather_slabs(jnp.pad(flat, (0, -flat.size % 1024)).reshape(-1, 128), name="gather_conv_w")
    per_chip = slabs[0::2].reshape(N_CHIPS, -1)[:, :flat.size].reshape(N_CHIPS, l, taps, c)
    return per_chip.transpose(1, 2, 0, 3).reshape(l, taps, N_CHIPS * c)


def _rope_tables(positions):
    inv_freq = ROPE_THETA ** (-jnp.arange(0, ROPE_DIM, 2, dtype=F32) / ROPE_DIM)
    ang = positions.astype(F32)[:, None] * inv_freq
    ang = jnp.concatenate([ang, ang], axis=-1)
    pad = ((0, 0), (0, HEAD - ROPE_DIM))
    return jnp.pad(jnp.cos(ang), pad), jnp.pad(jnp.sin(ang), pad)


def _uq_layout(w, heads):
    ql = w.shape[0]
    w = w.reshape(ql, heads, HEAD + ROPE_DIM)
    rot = jnp.pad(w[:, :, HEAD:], ((0, 0), (0, 0), (0, HEAD - ROPE_DIM)))
    return jnp.concatenate([w[:, :, :HEAD].reshape(ql, heads * HEAD), rot.reshape(ql, heads * HEAD)], axis=1)


def _uq_layout_inv(dw, heads):
    ql = dw.shape[0]
    nope = dw[:, :heads * HEAD].reshape(ql, heads, HEAD)
    rot = dw[:, heads * HEAD:].reshape(ql, heads, HEAD)[:, :, :ROPE_DIM]
    return jnp.concatenate([nope, rot], axis=-1).reshape(ql, heads * (HEAD + ROPE_DIM))


def kernel(x, mem, positions, norm_mix, norm_mem_q, norm_mem_kv, norm_ffn, norm_final, mla_w_down, mla_q_norm, mla_w_uq, mla_kv_norm, mla_w_ukv, mla_w_o, sb_w_qkv, sb_w_o, mem_w_q, mem_w_kv, mem_w_o, ffn_w_in, ffn_conv_w, ffn_conv_b, ffn_w_out, loss_target, m_norm_mix, m_norm_mem_q, m_norm_mem_kv, m_norm_ffn, m_norm_final, m_mla_w_down, m_mla_q_norm, m_mla_w_uq, m_mla_kv_norm, m_mla_w_ukv, m_mla_w_o, m_sb_w_qkv, m_sb_w_o, m_mem_w_q, m_mem_w_kv, m_mem_w_o, m_ffn_w_in, m_ffn_conv_w, m_ffn_conv_b, m_ffn_w_out, v_norm_mix, v_norm_mem_q, v_norm_mem_kv, v_norm_ffn, v_norm_final, v_mla_w_down, v_mla_q_norm, v_mla_w_uq, v_mla_kv_norm, v_mla_w_ukv, v_mla_w_o, v_sb_w_qkv, v_sb_w_o, v_mem_w_q, v_mem_w_kv, v_mem_w_o, v_ffn_w_in, v_ffn_conv_w, v_ffn_conv_b, v_ffn_w_out):
    args = dict(locals())
    big = ["mla_w_down", "mla_w_uq", "mla_w_ukv", "mla_w_o", "sb_w_qkv", "sb_w_o", "mem_w_q", "mem_w_kv",
           "mem_w_o", "ffn_w_in", "ffn_w_out"]
    col_cut = {"mla_w_uq", "mla_w_ukv", "sb_w_qkv", "mem_w_o", "ffn_w_in"}
    axes = [1 if n in col_cut else 0 for n in big]
    small = ["norm_mix", "norm_mem_q", "norm_mem_kv", "norm_ffn", "norm_final", "mla_q_norm", "mla_kv_norm",
             "ffn_conv_b", "ffn_conv_w"]
    order = ["norm_mix", "norm_mem_q", "norm_mem_kv", "norm_ffn", "norm_final", "mla_w_down", "mla_q_norm",
             "mla_w_uq", "mla_kv_norm", "mla_w_ukv", "mla_w_o", "sb_w_qkv", "sb_w_o", "mem_w_q", "mem_w_kv",
             "mem_w_o", "ffn_w_in", "ffn_conv_w", "ffn_conv_b", "ffn_w_out"]

    xs, mems, target = x[0], mem[0], loss_target[0]
    s, d = xs.shape
    depth = norm_mix.shape[0]
    ql, kvl = mla_q_norm.shape[1], mla_kv_norm.shape[1]
    mla_heads = N_CHIPS * mla_w_uq.shape[2] // (HEAD + ROPE_DIM)
    sb_heads = N_CHIPS * sb_w_qkv.shape[2] // (3 * HEAD)
    mem_heads = mem_w_q.shape[2] // HEAD
    ff = N_CHIPS * ffn_w_out.shape[1]
    chip = 2 * lax.axis_index("x") + lax.axis_index("y")
    chip_op = jnp.reshape(chip, (1,)).astype(jnp.int32)
    core_op = jnp.reshape(lax.axis_index("c"), (1,)).astype(jnp.int32)

    full = dict(zip(big, gather_weights([args[n].astype(BF16) for n in big], axes)))
    w_uq = [_uq_layout(full["mla_w_uq"][j], mla_heads) for j in range(full["mla_w_uq"].shape[0])]
    cos, sin = _rope_tables(positions[0])
    _, sb_tk = _sb_tiles(s)
    tri = (jnp.arange(sb_tk)[:, None] > jnp.arange(sb_tk)[None, :]).astype(BF16)
    conv_w = gather_conv_w(ffn_conv_w)

    saved = []
    xa = xs
    for i in range(depth):
        j = i // 2
        lay = {}
        lay["xa"] = xa
        h1 = rmsnorm_fwd(xa, norm_mix[i], name="norm_fwd")
        lay["h1"] = h1
        if i % 2 == 0:
            down = mm(h1, full["mla_w_down"][j], out_dtype=F32, name="mm_down")
            cq = rmsnorm_fwd(down[:, :ql], mla_q_norm[j], name="norm_lora_fwd")
            ckv = rmsnorm_fwd(down[:, ql:ql + kvl], mla_kv_norm[j], name="norm_lora_fwd")
            kr_raw = jnp.pad(down[:, ql + kvl:], ((0, 0), (0, HEAD - ROPE_DIM)))
            kr = rope(kr_raw, cos, sin, col0=0, n=1, inverse=False, name="rope_k")
            qall = mm(cq, w_uq[j], name="mm_uq")
            qr = rope(qall, cos, sin, col0=mla_heads, n=mla_heads, inverse=False, name="rope_q")
            kv = mm(ckv, full["mla_w_ukv"][j], name="mm_ukv")
            o, lse = mla_fwd(qall, qr, kv, kr, mla_heads, name="mla_fwd")
            xb = mm(o, full["mla_w_o"][j], add=xa, out_dtype=F32, name="mm_out_res")
            lay.update(down=down, cq=cq, ckv=ckv, qall=qall, qr=qr, kv=kv, kr=kr, o=o, lse=lse)
        else:
            qkv = mm(h1, full["sb_w_qkv"][j], name="mm_qkv")
            o, cmat = sb_fwd(qkv, tri, sb_heads, name="sb_fwd")
            xb = mm(o, full["sb_w_o"][j], add=xa, out_dtype=F32, name="mm_out_res")
            lay.update(qkv=qkv, o=o, cmat=cmat)
        h2 = rmsnorm_fwd(xb, norm_mem_q[i], name="norm_fwd")
        hm = rmsnorm_fwd(mems, norm_mem_kv[i], name="norm_mem_fwd")
        qm = mm(h2, full["mem_w_q"][i], name="mm_mem_q")
        kvm = mm(hm, full["mem_w_kv"][i], name="mm_mem_kv")
        om = mem_fwd(qm, kvm, mem_heads, name="mem_fwd")
        xc = mm(om, full["mem_w_o"][i], add=xb, out_dtype=F32, name="mm_mem_out_res")
        h3 = rmsnorm_fwd(xc, norm_ffn[i], name="norm_fwd")
        u = mm(h3, full["ffn_w_in"][i], name="mm_ffn_in")
        act = conv_gate_fwd(u, conv_w[i], ffn_conv_b[i][None, :], name="conv_gate_fwd")
        xd = mm(act, full["ffn_w_out"][i], add=xc, out_dtype=F32, name="mm_ffn_out_res")
        lay.update(xb=xb, h2=h2, hm=hm, qm=qm, kvm=kvm, om=om, xc=xc, h3=h3, u=u, act=act)
        saved.append(lay)
        xa = xd

    dx, dxb, g_final, loss_part = final_loss(xa, norm_final, target, name="final_loss")

    gw = {n: [None] * full[n].shape[0] for n in big}
    g_small = {"norm_mix": [None] * depth, "norm_mem_q": [None] * depth, "norm_mem_kv": [None] * depth,
               "norm_ffn": [None] * depth, "mla_q_norm": [None] * (depth - depth // 2),
               "mla_kv_norm": [None] * (depth - depth // 2), "conv": [None] * depth}
    for i in reversed(range(depth)):
        j = i // 2
        lay = saved[i]
        gw["ffn_w_out"][i] = mm(lay["act"], dxb, mode="tn", out_dtype=F32, name="mm_ffn_out_wgrad")
        da = mm(dxb, full["ffn_w_out"][i], mode="nt", name="mm_ffn_out_dgrad")
        dcg, dcu, g_small["conv"][i] = conv_gate_bwd(lay["u"], da, conv_w[i], ffn_conv_b[i][None, :], name="conv_gate_bwd")
        du = jnp.concatenate([conv_transpose(dcg, conv_w[i][:, :ff], name="conv_transpose"),
                              conv_transpose(dcu, conv_w[i][:, ff:], name="conv_transpose")], axis=1)
        gw["ffn_w_in"][i] = mm(lay["h3"], du, mode="tn", out_dtype=F32, name="mm_ffn_in_wgrad")
        dh3 = mm(du, full["ffn_w_in"][i], mode="nt", name="mm_ffn_in_dgrad")
        dx, dxb, g_small["norm_ffn"][i] = rmsnorm_bwd(dh3, lay["xc"], norm_ffn[i], dx, name="norm_bwd")
        gw["mem_w_o"][i] = mm(lay["om"], dxb, mode="tn", out_dtype=F32, name="mm_mem_out_wgrad")
        dom = mm(dxb, full["mem_w_o"][i], mode="nt", name="mm_mem_out_dgrad")
        dqm, dkm, dvm = mem_bwd(lay["qm"], lay["kvm"], dom, mem_heads, name="mem_bwd")
        dkvm = jnp.concatenate([dkm, dvm], axis=1)
        gw["mem_w_q"][i] = mm(lay["h2"], dqm, mode="tn", out_dtype=F32, name="mm_mem_q_wgrad")
        gw["mem_w_kv"][i] = mm(lay["hm"], dkvm, mode="tn", out_dtype=F32, name="mm_mem_kv_wgrad")
        dh2 = mm(dqm, full["mem_w_q"][i], mode="nt", name="mm_mem_q_dgrad")
        dhm = mm(dkvm, full["mem_w_kv"][i], mode="nt", name="mm_mem_kv_dgrad")
        _, _, g_small["norm_mem_kv"][i] = rmsnorm_bwd(dhm, mems, norm_mem_kv[i], name="norm_mem_bwd")
        dx, dxb, g_small["norm_mem_q"][i] = rmsnorm_bwd(dh2, lay["xb"], norm_mem_q[i], dx, name="norm_bwd")
        if i % 2 == 0:
            gw["mla_w_o"][j] = mm(lay["o"], dxb, mode="tn", out_dtype=F32, name="mm_out_wgrad")
            do = mm(dxb, full["mla_w_o"][j], mode="nt", name="mm_out_dgrad")
            dqn, dqr, dkv, dkr_heads = mla_bwd(lay["qall"], lay["qr"], lay["kv"], lay["kr"], lay["o"], do, lay["lse"],
                                               mla_heads, name="mla_bwd")
            dkr = sum_lane_tiles(dkr_heads, mla_heads, name="sum_heads")
            dqall = jnp.concatenate([dqn, rope(dqr, cos, sin, col0=0, n=mla_heads, inverse=True, name="rope_q_bwd")], axis=1)
            gw["mla_w_ukv"][j] = mm(lay["ckv"], dkv, mode="tn", out_dtype=F32, name="mm_ukv_wgrad")
            dckv = mm(dkv, full["mla_w_ukv"][j], mode="nt", name="mm_ukv_dgrad")
            gw["mla_w_uq"][j] = _uq_layout_inv(mm(lay["cq"], dqall, mode="tn", out_dtype=F32, name="mm_uq_wgrad"), mla_heads)
            dcq = mm(dqall, w_uq[j], mode="nt", name="mm_uq_dgrad")
            down = lay["down"]
            _, d_q, g_small["mla_q_norm"][j] = rmsnorm_bwd(dcq, down[:, :ql], mla_q_norm[j], name="norm_lora_bwd")
            _, d_kv, g_small["mla_kv_norm"][j] = rmsnorm_bwd(dckv, down[:, ql:ql + kvl], mla_kv_norm[j], name="norm_lora_bwd")
            d_kr = rope(dkr, cos, sin, col0=0, n=1, inverse=True, name="rope_k_bwd")[:, :ROPE_DIM]
            ddown = jnp.concatenate([d_q, d_kv, d_kr], axis=1)
            gw["mla_w_down"][j] = mm(lay["h1"], ddown, mode="tn", out_dtype=F32, name="mm_down_wgrad")
            dh1 = mm(ddown, full["mla_w_down"][j], mode="nt", name="mm_down_dgrad")
        else:
            gw["sb_w_o"][j] = mm(lay["o"], dxb, mode="tn", out_dtype=F32, name="mm_out_wgrad")
            do = mm(dxb, full["sb_w_o"][j], mode="nt", name="mm_out_dgrad")
            dq, dk, dv = sb_bwd(lay["qkv"], do, lay["cmat"], tri, sb_heads, name="sb_bwd")
            dqkv = jnp.concatenate([dq, dk, dv], axis=1)
            gw["sb_w_qkv"][j] = mm(lay["h1"], dqkv, mode="tn", out_dtype=F32, name="mm_qkv_wgrad")
            dh1 = mm(dqkv, full["sb_w_qkv"][j], mode="nt", name="mm_qkv_dgrad")
        dx, dxb, g_small["norm_mix"][i] = rmsnorm_bwd(dh1, lay["xa"], norm_mix[i], dx, name="norm_bwd")

    halves = dict(zip(big, reduce_scatter_grads([gw[n] for n in big], axes, core_op, chip_op)))
    conv = jnp.stack(g_small["conv"])
    parts = [jnp.concatenate(g_small[n], axis=0) for n in ("norm_mix", "norm_mem_q", "norm_mem_kv", "norm_ffn")]
    parts += [g_final, jnp.concatenate(g_small["mla_q_norm"], axis=0), jnp.concatenate(g_small["mla_kv_norm"], axis=0),
              conv[:, 3, :], conv[:, :3, :], loss_part[:1, :1]]
    sizes = [p.size for p in parts]
    packed = jnp.concatenate([p.reshape(-1) for p in parts])
    packed = jnp.pad(packed, (0, -packed.size % 1024)).reshape(-1, 128)
    total = allreduce_small(packed).reshape(-1)
    g_rep, at = {}, 0
    for n, p, size in zip(small + ["loss"], parts, sizes):
        g_rep[n] = total[at:at + size].reshape(p.shape)
        at += size
    loss = g_rep.pop("loss").reshape(())
    g_rep["norm_final"] = g_rep["norm_final"].reshape(norm_final.shape)
    width = ffn_conv_w.shape[2]
    g_rep["ffn_conv_w"] = lax.dynamic_slice_in_dim(g_rep["ffn_conv_w"], chip * width, width, axis=2)

    grads, delta, new_m, new_v = {}, {}, {}, {}
    for n in order:
        if n in halves:
            mine, theirs = halves[n]
            grads[n], delta[n], new_m[n], new_v[n] = adamw_halves(
                args[n], mine, theirs, args["m_" + n], args["v_" + n], core_op, name="adamw_big")
        else:
            grads[n] = g_rep[n]
            delta[n], new_m[n], new_v[n] = adamw(args[n], g_rep[n], args["m_" + n], args["v_" + n], name="adamw")
    return (loss, dx[None], *[grads[n] for n in order], *[delta[n] for n in order],
            *[new_m[n] for n in order], *[new_v[n] for n in order])
```

```python
import math

import jax
import jax.numpy as jnp
import numpy as np
from jax import lax
from jax.experimental import pallas as pl
from jax.experimental.pallas import tpu as pltpu

F32 = jnp.float32
BF16 = jnp.bfloat16
MESH = pl.DeviceIdType.MESH

EPS = 1e-6
LOG2E = 1.4426950408889634
CHUNK_SHIFT = 6
HEAD = 128
ROPE_DIM = 64
ROPE_THETA = 10000.0
N_CHIPS = 4
ADAM_LR, ADAM_B1, ADAM_B2, ADAM_EPS, ADAM_WD, ADAM_STEP = 0.001, 0.9, 0.999, 1e-08, 0.01, 10

VMEM_LIMIT_BYTES = 48 * 1024 * 1024
MM_TM, MM_TN, MM_TK = (1024, 512, 256, 128), (1024, 1408, 512, 256, 128), (1024, 512, 256, 128)
MM_WHOLE_K, MM_WHOLE_TM = 2048, (512, 256, 128)
ROW_TILE = (256, 128, 64, 32, 16, 8)
ATT_T = 512
MLA_HEADS_PER_STEP, MLA_BWD_HEADS_PER_STEP = 2, 2
SB_TQ, SB_TK = 512, 256
SB_HEADS_PER_STEP = 2
SB_DEAD_LOG2 = -200.0
SB_UNVISITED = -1e30
MEM_TQ = 512
CONV_TM, CONV_TN = 512, (512, 256, 128)
HALO = 16


def _pick(dim, prefs):
    for p in prefs:
        if dim % p == 0:
            return p
    return dim


def _dot(a, b, ca, cb):
    return lax.dot_general(a, b, (((ca,), (cb,)), ((), ())), preferred_element_type=F32)


def _params(sem):
    return pltpu.CompilerParams(dimension_semantics=sem, vmem_limit_bytes=VMEM_LIMIT_BYTES)


def _head_groups(heads, per_step):
    hp = per_step if heads % per_step == 0 else 1
    return hp, heads // hp, hp * HEAD


def _tables(pairs):
    arr = np.asarray(pairs, dtype=np.int32)
    return jnp.asarray(arr[:, 0]), jnp.asarray(arr[:, 1])


def mm(a, b, *, mode="nn", add=None, out_dtype=BF16, name):
    if mode == "nn":
        (m, k), (k2, n) = a.shape, b.shape
    elif mode == "nt":
        (m, k), (n, k2) = a.shape, b.shape
    else:
        (k, m), (k2, n) = a.shape, b.shape
    assert k == k2, (a.shape, b.shape, mode)
    ca, cb = {"nn": (1, 0), "nt": (1, 1), "tn": (0, 0)}[mode]
    if k <= MM_WHOLE_K:
        return _mm_whole_k(a, b, add, (m, n, k), (ca, cb), mode, out_dtype, name)
    tm, tn, tk = _pick(m, MM_TM), _pick(n, MM_TN), _pick(k, MM_TK)
    nk = k // tk

    def body(a_ref, b_ref, *rest):
        if add is None:
            o_ref, acc_ref = rest
        else:
            add_ref, o_ref, acc_ref = rest
        kk = pl.program_id(2)

        @pl.when(kk == 0)
        def _():
            acc_ref[...] = jnp.zeros_like(acc_ref)

        acc_ref[...] += _dot(a_ref[...].astype(BF16), b_ref[...].astype(BF16), ca, cb)

        @pl.when(kk == nk - 1)
        def _():
            r = acc_ref[...]
            if add is not None:
                r = r + add_ref[...]
            o_ref[...] = r.astype(o_ref.dtype)

    if mode == "tn":
        a_spec = pl.BlockSpec((tk, tm), lambda i, j, kk: (kk, i))
    else:
        a_spec = pl.BlockSpec((tm, tk), lambda i, j, kk: (i, kk))
    if mode == "nt":
        b_spec = pl.BlockSpec((tn, tk), lambda i, j, kk: (j, kk))
    else:
        b_spec = pl.BlockSpec((tk, tn), lambda i, j, kk: (kk, j))
    o_spec = pl.BlockSpec((tm, tn), lambda i, j, kk: (i, j))
    in_specs, args = [a_spec, b_spec], [a, b]
    if add is not None:
        in_specs.append(o_spec)
        args.append(add)
    return pl.pallas_call(
        body, name=name, grid=(m // tm, n // tn, nk), in_specs=in_specs, out_specs=o_spec,
        out_shape=jax.ShapeDtypeStruct((m, n), out_dtype), scratch_shapes=[pltpu.VMEM((tm, tn), F32)],
        compiler_params=_params(("parallel", "parallel", "arbitrary")),
    )(*args)


def _mm_whole_k(a, b, add, mnk, contract, mode, out_dtype, name):
    m, n, k = mnk
    tm, tn = _pick(m, MM_WHOLE_TM), _pick(n, MM_TN)

    def body(a_ref, b_ref, *rest):
        r = _dot(a_ref[...].astype(BF16), b_ref[...].astype(BF16), *contract)
        if add is not None:
            r = r + rest[0][...]
        rest[-1][...] = r.astype(rest[-1].dtype)

    a_spec = pl.BlockSpec((k, tm), lambda j, i: (0, i)) if mode == "tn" else pl.BlockSpec((tm, k), lambda j, i: (i, 0))
    b_spec = pl.BlockSpec((tn, k), lambda j, i: (j, 0)) if mode == "nt" else pl.BlockSpec((k, tn), lambda j, i: (0, j))
    o_spec = pl.BlockSpec((tm, tn), lambda j, i: (i, j))
    in_specs, args = [a_spec, b_spec], [a, b]
    if add is not None:
        in_specs.append(o_spec)
        args.append(add)
    return pl.pallas_call(
        body, name=name, grid=(n // tn, m // tm), in_specs=in_specs, out_specs=o_spec,
        out_shape=jax.ShapeDtypeStruct((m, n), out_dtype), compiler_params=_params(("parallel", "parallel")),
    )(*args)


def rmsnorm_fwd(x, g, *, name):
    m, d = x.shape
    tm = _pick(m, ROW_TILE)

    def body(x_ref, g_ref, o_ref):
        xv = x_ref[...]
        r = lax.rsqrt(jnp.mean(xv * xv, axis=-1, keepdims=True) + EPS)
        o_ref[...] = (xv * r * g_ref[...]).astype(o_ref.dtype)

    return pl.pallas_call(
        body, name=name, grid=(m // tm,),
        in_specs=[pl.BlockSpec((tm, d), lambda i: (i, 0)), pl.BlockSpec((1, d), lambda i: (0, 0))],
        out_specs=pl.BlockSpec((tm, d), lambda i: (i, 0)), out_shape=jax.ShapeDtypeStruct((m, d), BF16),
        compiler_params=_params(("parallel",)),
    )(x, g.reshape(1, d))


def rmsnorm_bwd(dh, x, g, res=None, *, name):
    m, d = x.shape
    tm = _pick(m, ROW_TILE)

    def body(dh_ref, x_ref, g_ref, *rest):
        if res is None:
            dx_ref, dxb_ref, dg_ref = rest
        else:
            res_ref, dx_ref, dxb_ref, dg_ref = rest

        @pl.when(pl.program_id(0) == 0)
        def _():
            dg_ref[...] = jnp.zeros_like(dg_ref)

        xv = x_ref[...]
        dhv = dh_ref[...].astype(F32)
        r = lax.rsqrt(jnp.mean(xv * xv, axis=-1, keepdims=True) + EPS)
        y = xv * r
        dhg = dhv * g_ref[...]
        dx = r * (dhg - y * jnp.mean(dhg * y, axis=-1, keepdims=True))
        if res is not None:
            dx = dx + res_ref[...]
        dx_ref[...] = dx
        dxb_ref[...] = dx.astype(BF16)
        dg_ref[...] += jnp.sum(dhv * y, axis=0, keepdims=True)

    row = pl.BlockSpec((tm, d), lambda i: (i, 0))
    vec = pl.BlockSpec((1, d), lambda i: (0, 0))
    in_specs, args = [row, row, vec], [dh, x, g.reshape(1, d)]
    if res is not None:
        in_specs.append(row)
        args.append(res)
    return pl.pallas_call(
        body, name=name, grid=(m // tm,), in_specs=in_specs, out_specs=[row, row, vec],
        out_shape=[jax.ShapeDtypeStruct((m, d), F32), jax.ShapeDtypeStruct((m, d), BF16),
                   jax.ShapeDtypeStruct((1, d), F32)],
        compiler_params=_params(("arbitrary",)),
    )(*args)


def final_loss(x, g, target, *, name):
    m, d = x.shape
    tm = _pick(m, ROW_TILE)

    def body(x_ref, g_ref, t_ref, dx_ref, dxb_ref, dg_ref, loss_ref):
        @pl.when(pl.program_id(0) == 0)
        def _():
            dg_ref[...] = jnp.zeros_like(dg_ref)
            loss_ref[...] = jnp.zeros_like(loss_ref)

        xv = x_ref[...]
        gv = g_ref[...]
        r = lax.rsqrt(jnp.mean(xv * xv, axis=-1, keepdims=True) + EPS)
        y = xv * r
        err = y * gv - t_ref[...]
        loss_ref[...] += 0.5 * jnp.sum(jnp.mean(err * err, axis=-1, keepdims=True))
        dy = err * (1.0 / d)
        dyg = dy * gv
        dx = r * (dyg - y * jnp.mean(dyg * y, axis=-1, keepdims=True))
        dx_ref[...] = dx
        dxb_ref[...] = dx.astype(BF16)
        dg_ref[...] += jnp.sum(dy * y, axis=0, keepdims=True)

    row = pl.BlockSpec((tm, d), lambda i: (i, 0))
    vec = pl.BlockSpec((1, d), lambda i: (0, 0))
    return pl.pallas_call(
        body, name=name, grid=(m // tm,), in_specs=[row, vec, row],
        out_specs=[row, row, vec, pl.BlockSpec((8, 128), lambda i: (0, 0))],
        out_shape=[jax.ShapeDtypeStruct((m, d), F32), jax.ShapeDtypeStruct((m, d), BF16),
                   jax.ShapeDtypeStruct((1, d), F32), jax.ShapeDtypeStruct((8, 128), F32)],
        compiler_params=_params(("arbitrary",)),
    )(x, g.reshape(1, d), target)


def rope(xin, cos, sin, *, col0, n, inverse, name):
    s = xin.shape[0]
    tm = _pick(s, ROW_TILE)
    half = ROPE_DIM // 2

    def body(x_ref, c_ref, s_ref, o_ref):
        xv = x_ref[...].astype(F32)
        lane = lax.broadcasted_iota(jnp.int32, xv.shape, 1)
        rot = jnp.where(lane < half, -pltpu.roll(xv, HEAD - half, 1), pltpu.roll(xv, half, 1))
        sv = s_ref[...]
        if inverse:
            sv = -sv
        o_ref[...] = (xv * c_ref[...] + rot * sv).astype(o_ref.dtype)

    tab = pl.BlockSpec((tm, HEAD), lambda i, j: (i, 0))
    return pl.pallas_call(
        body, name=name, grid=(s // tm, n),
        in_specs=[pl.BlockSpec((tm, HEAD), lambda i, j: (i, j + col0)), tab, tab],
        out_specs=pl.BlockSpec((tm, HEAD), lambda i, j: (i, j)),
        out_shape=jax.ShapeDtypeStruct((s, n * HEAD), BF16),
        compiler_params=_params(("parallel", "parallel")),
    )(xin, cos, sin)


def sum_lane_tiles(xin, n, *, name):
    s = xin.shape[0]
    tm = _pick(s, ROW_TILE)

    def body(x_ref, o_ref):
        acc = x_ref[:, :HEAD]
        for k in range(1, n):
            acc = acc + x_ref[:, k * HEAD:(k + 1) * HEAD]
        o_ref[...] = acc

    return pl.pallas_call(
        body, name=name, grid=(s // tm,), in_specs=[pl.BlockSpec((tm, n * HEAD), lambda i: (i, 0))],
        out_specs=pl.BlockSpec((tm, HEAD), lambda i: (i, 0)), out_shape=jax.ShapeDtypeStruct((s, HEAD), F32),
        compiler_params=_params(("parallel",)),
    )(xin)


def _mla_scores(qn, qr, kn, kr, scale, diagonal):
    sc = (_dot(qn, kn, 1, 1) + _dot(qr, kr, 1, 1)) * scale
    if not diagonal:
        return sc
    qchunk = jnp.right_shift(lax.broadcasted_iota(jnp.int32, sc.shape, 0), CHUNK_SHIFT)
    kchunk = jnp.right_shift(lax.broadcasted_iota(jnp.int32, sc.shape, 1), CHUNK_SHIFT)
    return jnp.where(kchunk <= qchunk, sc, -jnp.inf)


def mla_fwd(qall, qr, kv, kr, heads, *, name):
    s = qr.shape[0]
    t = min(ATT_T, s)
    nq = s // t
    scale = 1.0 / math.sqrt(HEAD + ROPE_DIM)
    pairs = [(qi, kj) for qi in range(nq) for kj in range(qi + 1)]
    qtab, ktab = _tables(pairs)
    hp, groups, wide = _head_groups(heads, MLA_HEADS_PER_STEP)

    def body(qt_ref, kt_ref, qn_ref, qr_ref, kv_ref, kr_ref, o_ref, lse_ref, m_ref, l_ref, acc_ref):
        st = pl.program_id(1)
        qi, kj = qt_ref[st], kt_ref[st]

        @pl.when(kj == 0)
        def _():
            m_ref[...] = jnp.full_like(m_ref, -jnp.inf)
            l_ref[...] = jnp.zeros_like(l_ref)
            acc_ref[...] = jnp.zeros_like(acc_ref)

        def step(diagonal):
            for hh in range(hp):
                cols = slice(hh * HEAD, (hh + 1) * HEAD)
                kn = kv_ref[:, 2 * hh * HEAD:(2 * hh + 1) * HEAD]
                v = kv_ref[:, (2 * hh + 1) * HEAD:(2 * hh + 2) * HEAD]
                sc = _mla_scores(qn_ref[:, cols], qr_ref[:, cols], kn, kr_ref[...], scale, diagonal)
                m_prev = m_ref[hh]
                m_new = jnp.maximum(m_prev, jnp.max(sc, axis=-1, keepdims=True))
                p = jnp.exp(sc - m_new)
                alpha = jnp.exp(m_prev - m_new)
                l_ref[hh] = alpha * l_ref[hh] + jnp.sum(p, axis=-1, keepdims=True)
                acc_ref[:, cols] = alpha * acc_ref[:, cols] + _dot(p.astype(BF16), v, 1, 0)
                m_ref[hh] = m_new

        @pl.when(kj < qi)
        def _():
            step(False)

        @pl.when(kj == qi)
        def _():
            step(True)
            for hh in range(hp):
                cols = slice(hh * HEAD, (hh + 1) * HEAD)
                o_ref[:, cols] = (acc_ref[:, cols] / l_ref[hh]).astype(o_ref.dtype)
                lse_ref[:, cols] = jnp.broadcast_to(m_ref[hh] + jnp.log(l_ref[hh]), (t, HEAD))

    qspec = pl.BlockSpec((t, wide), lambda h, st, qt, kt: (qt[st], h))
    grid_spec = pltpu.PrefetchScalarGridSpec(
        num_scalar_prefetch=2, grid=(groups, len(pairs)),
        in_specs=[qspec, qspec,
                  pl.BlockSpec((t, 2 * wide), lambda h, st, qt, kt: (kt[st], h)),
                  pl.BlockSpec((t, HEAD), lambda h, st, qt, kt: (kt[st], 0))],
        out_specs=[qspec, qspec],
        scratch_shapes=[pltpu.VMEM((hp, t, 1), F32), pltpu.VMEM((hp, t, 1), F32), pltpu.VMEM((t, wide), F32)])
    return pl.pallas_call(
        body, name=name, grid_spec=grid_spec,
        out_shape=[jax.ShapeDtypeStruct((s, heads * HEAD), BF16), jax.ShapeDtypeStruct((s, heads * HEAD), F32)],
        compiler_params=_params(("parallel", "arbitrary")),
    )(qtab, ktab, qall, qr, kv, kr)


def mla_bwd(qall, qr, kv, kr, o, do, lse, heads, *, name):
    s = qr.shape[0]
    t = min(ATT_T, s)
    nq = s // t
    scale = 1.0 / math.sqrt(HEAD + ROPE_DIM)
    pairs = [(kj, qi) for kj in range(nq) for qi in range(kj, nq)]
    ktab, qtab = _tables(pairs)
    last = len(pairs) - 1
    hp, groups, wide = _head_groups(heads, MLA_BWD_HEADS_PER_STEP)

    def body(kt_ref, qt_ref, qn_ref, qr_ref, kv_ref, kr_ref, o_ref, do_ref, lse_ref,
             dqn_ref, dqr_ref, dkv_ref, dkr_ref, fn_ref, fr_ref, akn_ref, av_ref, akr_ref):
        st = pl.program_id(1)
        kj, qi = kt_ref[st], qt_ref[st]

        @pl.when(st == 0)
        def _():
            fn_ref[...] = jnp.zeros_like(fn_ref)
            fr_ref[...] = jnp.zeros_like(fr_ref)

        @pl.when(qi == kj)
        def _():
            akn_ref[...] = jnp.zeros_like(akn_ref)
            av_ref[...] = jnp.zeros_like(av_ref)
            akr_ref[...] = jnp.zeros_like(akr_ref)

        def step(diagonal):
            rows = pl.ds(pl.multiple_of(qi * t, t), t)
            krv = kr_ref[...]
            for hh in range(hp):
                cols = slice(hh * HEAD, (hh + 1) * HEAD)
                qn, qrv, dov = qn_ref[:, cols], qr_ref[:, cols], do_ref[:, cols]
                kn = kv_ref[:, 2 * hh * HEAD:(2 * hh + 1) * HEAD]
                v = kv_ref[:, (2 * hh + 1) * HEAD:(2 * hh + 2) * HEAD]
                sc = _mla_scores(qn, qrv, kn, krv, scale, diagonal)
                p = jnp.exp(sc - lse_ref[:, hh * HEAD:hh * HEAD + 1])
                delta = jnp.sum(dov.astype(F32) * o_ref[:, cols].astype(F32), axis=-1, keepdims=True)
                dp = _dot(dov, v, 1, 1)
                ds = (p * (dp - delta) * scale).astype(BF16)
                av_ref[:, cols] += _dot(p.astype(BF16), dov, 0, 0)
                akn_ref[:, cols] += _dot(ds, qn, 0, 0)
                akr_ref[:, cols] += _dot(ds, qrv, 0, 0)
                fn_ref[rows, cols] += _dot(ds, kn, 1, 0)
                fr_ref[rows, cols] += _dot(ds, krv, 1, 0)

        @pl.when(qi == kj)
        def _():
            step(True)

        @pl.when(qi > kj)
        def _():
            step(False)

        @pl.when(qi == nq - 1)
        def _():
            for hh in range(hp):
                cols = slice(hh * HEAD, (hh + 1) * HEAD)
                dkv_ref[:, 2 * hh * HEAD:(2 * hh + 1) * HEAD] = akn_ref[:, cols].astype(dkv_ref.dtype)
                dkv_ref[:, (2 * hh + 1) * HEAD:(2 * hh + 2) * HEAD] = av_ref[:, cols].astype(dkv_ref.dtype)
            dkr_ref[...] = akr_ref[...]

        @pl.when(st == last)
        def _():
            dqn_ref[...] = fn_ref[...].astype(dqn_ref.dtype)
            dqr_ref[...] = fr_ref[...].astype(dqr_ref.dtype)

    qspec = pl.BlockSpec((t, wide), lambda h, st, kt, qt: (qt[st], h))
    kvspec = pl.BlockSpec((t, 2 * wide), lambda h, st, kt, qt: (kt[st], h))
    krspec = pl.BlockSpec((t, HEAD), lambda h, st, kt, qt: (kt[st], 0))
    headspec = pl.BlockSpec((s, wide), lambda h, st, kt, qt: (0, h))
    grid_spec = pltpu.PrefetchScalarGridSpec(
        num_scalar_prefetch=2, grid=(groups, len(pairs)),
        in_specs=[qspec, qspec, kvspec, krspec, qspec, qspec, qspec],
        out_specs=[headspec, headspec, kvspec, pl.BlockSpec((t, wide), lambda h, st, kt, qt: (kt[st], h))],
        scratch_shapes=[pltpu.VMEM((s, wide), F32), pltpu.VMEM((s, wide), F32), pltpu.VMEM((t, wide), F32),
                        pltpu.VMEM((t, wide), F32), pltpu.VMEM((t, wide), F32)])
    return pl.pallas_call(
        body, name=name, grid_spec=grid_spec,
        out_shape=[jax.ShapeDtypeStruct((s, heads * HEAD), BF16), jax.ShapeDtypeStruct((s, heads * HEAD), BF16),
                   jax.ShapeDtypeStruct((s, heads * 2 * HEAD), BF16), jax.ShapeDtypeStruct((s, heads * HEAD), F32)],
        compiler_params=_params(("parallel", "arbitrary")),
    )(ktab, qtab, qall, qr, kv, kr, o, do, lse)


def _split_dot(val, tri, cb):
    hi = val.astype(BF16)
    lo = (val - hi.astype(F32)).astype(BF16)
    return _dot(hi, tri, 1, cb) + _dot(lo, tri, 1, cb)


def _sb_logs(q, k, offset, scale, masked):
    z = _dot(q, k, 1, 1) * (scale * LOG2E)
    sp = jnp.log2(1.0 + jnp.exp2(-jnp.abs(z)))
    ls = jnp.minimum(z, 0.0) - sp
    lk = ls - z
    if not masked:
        return None, ls, lk
    strict = (lax.broadcasted_iota(jnp.int32, z.shape, 1) + offset) < lax.broadcasted_iota(jnp.int32, z.shape, 0)
    return strict, ls, jnp.where(strict, lk, 0.0)


def _lane_pick(blk, idx):
    lane = lax.broadcasted_iota(jnp.int32, blk.shape, 1)
    return jnp.sum(jnp.where(lane == idx, blk, 0.0), axis=-1, keepdims=True)


def _lane_put(blk, idx, col):
    lane = lax.broadcasted_iota(jnp.int32, blk.shape, 1)
    return jnp.where(lane == idx, col, blk)


def _sb_tiles(s):
    tq = min(SB_TQ, s)
    tk = min(SB_TK, tq)
    assert s // tk <= HEAD
    return tq, tk


def sb_fwd(qkv, tri, heads, *, name):
    s = qkv.shape[0]
    tq, tk = _sb_tiles(s)
    nq, ratio = s // tq, tq // tk
    scale = 1.0 / math.sqrt(HEAD)
    pairs = [(qi, kj) for qi in range(nq) for kj in range((qi + 1) * ratio - 1, -1, -1)]
    qtab, ktab = _tables(pairs)
    hp, groups, wide = _head_groups(heads, SB_HEADS_PER_STEP)

    def body(qt_ref, kt_ref, q_ref, k_ref, v_ref, tri_ref, o_ref, c_ref, carry_ref, acc_ref):
        st = pl.program_id(1)
        qi, kj = qt_ref[st], kt_ref[st]

        @pl.when(kj == (qi + 1) * ratio - 1)
        def _():
            carry_ref[...] = jnp.zeros_like(carry_ref)
            acc_ref[...] = jnp.zeros_like(acc_ref)
            c_ref[...] = jnp.full_like(c_ref, SB_UNVISITED)

        alive = jnp.max(carry_ref[...]) > SB_DEAD_LOG2

        def step(masked):
            for hh in range(hp):
                cols = slice(hh * HEAD, (hh + 1) * HEAD)
                strict, ls, lk = _sb_logs(q_ref[:, cols], k_ref[:, cols], kj * tk - qi * tq, scale, masked)
                carry = carry_ref[hh]
                a = jnp.exp2(ls + _split_dot(lk, tri_ref[...], 0) + carry)
                if masked:
                    a = jnp.where(strict, a, 0.0)
                acc_ref[:, cols] += _dot(a.astype(BF16), v_ref[:, cols], 1, 0)
                c_ref[:, cols] = _lane_put(c_ref[:, cols], kj, carry)
                carry_ref[hh] = carry + jnp.sum(lk, axis=-1, keepdims=True)

        @pl.when(alive & ((kj + 1) * tk > qi * tq))
        def _():
            step(True)

        @pl.when(alive & ((kj + 1) * tk <= qi * tq))
        def _():
            step(False)

        @pl.when(kj == 0)
        def _():
            o_ref[...] = acc_ref[...].astype(o_ref.dtype)

    qspec = pl.BlockSpec((tq, wide), lambda h, st, qt, kt: (qt[st], h))
    grid_spec = pltpu.PrefetchScalarGridSpec(
        num_scalar_prefetch=2, grid=(groups, len(pairs)),
        in_specs=[qspec,
                  pl.BlockSpec((tk, wide), lambda h, st, qt, kt: (kt[st], groups + h)),
                  pl.BlockSpec((tk, wide), lambda h, st, qt, kt: (kt[st], 2 * groups + h)),
                  pl.BlockSpec((tk, tk), lambda h, st, qt, kt: (0, 0))],
        out_specs=[qspec, qspec],
        scratch_shapes=[pltpu.VMEM((hp, tq, 1), F32), pltpu.VMEM((tq, wide), F32)])
    return pl.pallas_call(
        body, name=name, grid_spec=grid_spec,
        out_shape=[jax.ShapeDtypeStruct((s, heads * HEAD), BF16), jax.ShapeDtypeStruct((s, heads * HEAD), F32)],
        compiler_params=_params(("parallel", "arbitrary")),
    )(qtab, ktab, qkv, qkv, qkv, tri)


def sb_bwd(qkv, do, cmat, tri, heads, *, name):
    s = qkv.shape[0]
    tq, tk = _sb_tiles(s)
    nq, nkb, ratio = s // tq, s // tk, tq // tk
    scale = 1.0 / math.sqrt(HEAD)
    pairs = [(kj, qi) for kj in range(nkb) for qi in range(kj // ratio, nq)]
    ktab, qtab = _tables(pairs)
    last = len(pairs) - 1
    hp, groups, wide = _head_groups(heads, SB_HEADS_PER_STEP)

    def body(kt_ref, qt_ref, q_ref, k_ref, v_ref, do_ref, c_ref, tri_ref, dq_ref, dk_ref, dv_ref,
             dqf_ref, gsum_ref, ak_ref, av_ref):
        st = pl.program_id(1)
        kj, qi = kt_ref[st], qt_ref[st]

        @pl.when(st == 0)
        def _():
            dqf_ref[...] = jnp.zeros_like(dqf_ref)
            gsum_ref[...] = jnp.zeros_like(gsum_ref)

        @pl.when(qi == kj // ratio)
        def _():
            ak_ref[...] = jnp.zeros_like(ak_ref)
            av_ref[...] = jnp.zeros_like(av_ref)

        def step(masked):
            rows = pl.ds(pl.multiple_of(qi * tq, tq), tq)
            tri_v = tri_ref[...]
            for hh in range(hp):
                cols = slice(hh * HEAD, (hh + 1) * HEAD)
                qv, kblk, dov = q_ref[:, cols], k_ref[:, cols], do_ref[:, cols]
                strict, ls, lk = _sb_logs(qv, kblk, kj * tk - qi * tq, scale, masked)
                a = jnp.exp2(ls + _split_dot(lk, tri_v, 0) + _lane_pick(c_ref[:, cols], kj))
                if masked:
                    a = jnp.where(strict, a, 0.0)
                g = _dot(dov, v_ref[:, cols], 1, 1) * a
                before_all = gsum_ref[hh, rows, :]
                before = _split_dot(g, tri_v, 1) + before_all[:, :1]
                beta = jnp.exp2(ls)
                dz = g * (1.0 - beta) - before * beta
                if masked:
                    dz = jnp.where(strict, dz, 0.0)
                dzb = dz.astype(BF16)
                av_ref[:, cols] += _dot(a.astype(BF16), dov, 0, 0)
                ak_ref[:, cols] += _dot(dzb, qv, 0, 0)
                dqf_ref[rows, cols] += _dot(dzb, kblk, 1, 0)
                gsum_ref[hh, rows, :] = before_all + jnp.sum(g, axis=-1, keepdims=True)

        lane = lax.broadcasted_iota(jnp.int32, (tq, wide), 1)
        alive = jnp.max(jnp.where((lane & (HEAD - 1)) == kj, c_ref[...], SB_UNVISITED)) > SB_DEAD_LOG2

        @pl.when(alive & ((kj + 1) * tk > qi * tq))
        def _():
            step(True)

        @pl.when(alive & ((kj + 1) * tk <= qi * tq))
        def _():
            step(False)

        @pl.when(qi == nq - 1)
        def _():
            dk_ref[...] = (ak_ref[...] * scale).astype(dk_ref.dtype)
            dv_ref[...] = av_ref[...].astype(dv_ref.dtype)

        @pl.when(st == last)
        def _():
            dq_ref[...] = (dqf_ref[...] * scale).astype(dq_ref.dtype)

    qspec = pl.BlockSpec((tq, wide), lambda h, st, kt, qt: (qt[st], h))
    ospec = pl.BlockSpec((tk, wide), lambda h, st, kt, qt: (kt[st], h))
    grid_spec = pltpu.PrefetchScalarGridSpec(
        num_scalar_prefetch=2, grid=(groups, len(pairs)),
        in_specs=[qspec,
                  pl.BlockSpec((tk, wide), lambda h, st, kt, qt: (kt[st], groups + h)),
                  pl.BlockSpec((tk, wide), lambda h, st, kt, qt: (kt[st], 2 * groups + h)),
                  qspec, qspec, pl.BlockSpec((tk, tk), lambda h, st, kt, qt: (0, 0))],
        out_specs=[pl.BlockSpec((s, wide), lambda h, st, kt, qt: (0, h)), ospec, ospec],
        scratch_shapes=[pltpu.VMEM((s, wide), F32), pltpu.VMEM((hp, s, HEAD), F32), pltpu.VMEM((tk, wide), F32),
                        pltpu.VMEM((tk, wide), F32)])
    return pl.pallas_call(
        body, name=name, grid_spec=grid_spec,
        out_shape=[jax.ShapeDtypeStruct((s, heads * HEAD), BF16)] * 3,
        compiler_params=_params(("parallel", "arbitrary")),
    )(ktab, qtab, qkv, qkv, qkv, do, cmat, tri)


def _mem_probs(q, k, scale):
    sc = _dot(q, k, 1, 1) * scale
    e = jnp.exp(sc - jnp.max(sc, axis=-1, keepdims=True))
    return e / jnp.sum(e, axis=-1, keepdims=True)


def mem_fwd(q, kvm, heads, *, name):
    s, nm = q.shape[0], kvm.shape[0]
    tq = min(MEM_TQ, s)
    scale = 1.0 / math.sqrt(HEAD)

    def body(q_ref, k_ref, v_ref, o_ref):
        p = _mem_probs(q_ref[...], k_ref[...], scale)
        o_ref[...] = _dot(p.astype(BF16), v_ref[...], 1, 0).astype(o_ref.dtype)

    qspec = pl.BlockSpec((tq, HEAD), lambda h, qi: (qi, h))
    return pl.pallas_call(
        body, name=name, grid=(heads, s // tq),
        in_specs=[qspec, pl.BlockSpec((nm, HEAD), lambda h, qi: (0, h)),
                  pl.BlockSpec((nm, HEAD), lambda h, qi: (0, heads + h))],
        out_specs=qspec, out_shape=jax.ShapeDtypeStruct((s, heads * HEAD), BF16),
        compiler_params=_params(("parallel", "parallel")),
    )(q, kvm, kvm)


def mem_bwd(q, kvm, do, heads, *, name):
    s, nm = q.shape[0], kvm.shape[0]
    tq = min(MEM_TQ, s)
    scale = 1.0 / math.sqrt(HEAD)

    def body(q_ref, k_ref, v_ref, do_ref, dq_ref, dk_ref, dv_ref):
        @pl.when(pl.program_id(1) == 0)
        def _():
            dk_ref[...] = jnp.zeros_like(dk_ref)
            dv_ref[...] = jnp.zeros_like(dv_ref)

        qv, kvv, dov = q_ref[...], k_ref[...], do_ref[...]
        p = _mem_probs(qv, kvv, scale)
        dp = _dot(dov, v_ref[...], 1, 1)
        ds = (p * (dp - jnp.sum(dp * p, axis=-1, keepdims=True)) * scale).astype(BF16)
        dq_ref[...] = _dot(ds, kvv, 1, 0).astype(dq_ref.dtype)
        dk_ref[...] += _dot(ds, qv, 0, 0)
        dv_ref[...] += _dot(p.astype(BF16), dov, 0, 0)

    qspec = pl.BlockSpec((tq, HEAD), lambda h, qi: (qi, h))
    kspec = pl.BlockSpec((nm, HEAD), lambda h, qi: (0, h))
    return pl.pallas_call(
        body, name=name, grid=(heads, s // tq),
        in_specs=[qspec, kspec, pl.BlockSpec((nm, HEAD), lambda h, qi: (0, heads + h)), qspec],
        out_specs=[qspec, kspec, kspec],
        out_shape=[jax.ShapeDtypeStruct((s, heads * HEAD), BF16), jax.ShapeDtypeStruct((nm, heads * HEAD), F32),
                   jax.ShapeDtypeStruct((nm, heads * HEAD), F32)],
        compiler_params=_params(("parallel", "arbitrary")),
    )(q, kvm, kvm, do)


def _conv3(u, halo, w, b):
    tm = u.shape[0]
    row = lax.broadcasted_iota(jnp.int32, u.shape, 0)
    h1, h2 = halo[HALO - 1:HALO, :], halo[HALO - 2:HALO - 1, :]
    u1 = jnp.where(row == 0, h1, pltpu.roll(u, 1, 0))
    u2 = jnp.where(row == 0, h2, jnp.where(row == 1, h1, pltpu.roll(u, 2 % tm, 0)))
    return b + w[0:1, :] * u2 + w[1:2, :] * u1 + w[2:3, :] * u, u1, u2


def _conv_specs(s, f):
    tm, tn = min(CONV_TM, s), _pick(f, CONV_TN)
    return tm, tn, s // tm, f // tn


def _silu_parts(g):
    sg = 1.0 / (1.0 + jnp.exp(-g))
    return g * sg, sg


def conv_gate_fwd(u, cw, cb, *, name):
    s, f = u.shape[0], u.shape[1] // 2
    tm, tn, ni, nj = _conv_specs(s, f)
    hb = tm // HALO

    def body(ug_ref, uu_ref, hg_ref, hu_ref, wg_ref, wu_ref, bg_ref, bu_ref, a_ref):
        keep = (pl.program_id(1) > 0).astype(F32)
        gate, _, _ = _conv3(ug_ref[...].astype(F32), hg_ref[...].astype(F32) * keep, wg_ref[...], bg_ref[...])
        up, _, _ = _conv3(uu_ref[...].astype(F32), hu_ref[...].astype(F32) * keep, wu_ref[...], bu_ref[...])
        a_ref[...] = (_silu_parts(gate)[0] * up).astype(a_ref.dtype)

    def main(off):
        return pl.BlockSpec((tm, tn), lambda j, i: (i, j + off))

    def halo(off):
        return pl.BlockSpec((HALO, tn), lambda j, i: (jnp.maximum(i * hb - 1, 0), j + off))

    def par(rows, off):
        return pl.BlockSpec((rows, tn), lambda j, i: (0, j + off))

    return pl.pallas_call(
        body, name=name, grid=(nj, ni),
        in_specs=[main(0), main(nj), halo(0), halo(nj), par(3, 0), par(3, nj), par(1, 0), par(1, nj)],
        out_specs=main(0), out_shape=jax.ShapeDtypeStruct((s, f), BF16),
        compiler_params=_params(("parallel", "parallel")),
    )(u, u, u, u, cw, cw, cb, cb)


def conv_gate_bwd(u, da, cw, cb, *, name):
    s, f = u.shape[0], u.shape[1] // 2
    tm, tn, ni, nj = _conv_specs(s, f)
    hb = tm // HALO

    def body(ug_ref, uu_ref, hg_ref, hu_ref, da_ref, wg_ref, wu_ref, bg_ref, bu_ref, dg_ref, du_ref, pg_ref, pu_ref):
        @pl.when(pl.program_id(1) == 0)
        def _():
            pg_ref[...] = jnp.zeros_like(pg_ref)
            pu_ref[...] = jnp.zeros_like(pu_ref)

        keep = (pl.program_id(1) > 0).astype(F32)
        ug, uu = ug_ref[...].astype(F32), uu_ref[...].astype(F32)
        gate, ug1, ug2 = _conv3(ug, hg_ref[...].astype(F32) * keep, wg_ref[...], bg_ref[...])
        up, uu1, uu2 = _conv3(uu, hu_ref[...].astype(F32) * keep, wu_ref[...], bu_ref[...])
        act, sg = _silu_parts(gate)
        dav = da_ref[...].astype(F32)
        d_gate = dav * up * (sg * (1.0 + gate * (1.0 - sg)))
        d_up = dav * act
        dg_ref[...] = d_gate.astype(dg_ref.dtype)
        du_ref[...] = d_up.astype(du_ref.dtype)
        for p_ref, dc, taps in ((pg_ref, d_gate, (ug2, ug1, ug)), (pu_ref, d_up, (uu2, uu1, uu))):
            for r, tap in enumerate(taps):
                p_ref[r:r + 1, :] += jnp.sum(dc * tap, axis=0, keepdims=True)
            p_ref[3:4, :] += jnp.sum(dc, axis=0, keepdims=True)

    def main(off):
        return pl.BlockSpec((tm, tn), lambda j, i: (i, j + off))

    def halo(off):
        return pl.BlockSpec((HALO, tn), lambda j, i: (jnp.maximum(i * hb - 1, 0), j + off))

    def par(rows, off):
        return pl.BlockSpec((rows, tn), lambda j, i: (0, j + off))

    dg, du, pg, pu = pl.pallas_call(
        body, name=name, grid=(nj, ni),
        in_specs=[main(0), main(nj), halo(0), halo(nj), main(0), par(3, 0), par(3, nj), par(1, 0), par(1, nj)],
        out_specs=[main(0), main(0), par(8, 0), par(8, 0)],
        out_shape=[jax.ShapeDtypeStruct((s, f), BF16), jax.ShapeDtypeStruct((s, f), BF16),
                   jax.ShapeDtypeStruct((8, f), F32), jax.ShapeDtypeStruct((8, f), F32)],
        compiler_params=_params(("parallel", "arbitrary")),
    )(u, u, u, u, da, cw, cw, cb, cb)
    return dg, du, jnp.concatenate([pg, pu], axis=1)


def conv_transpose(dc, w, *, name):
    s, f = dc.shape
    tm, tn, ni, nj = _conv_specs(s, f)
    hb = tm // HALO

    def body(d_ref, h_ref, w_ref, o_ref):
        d = d_ref[...].astype(F32)
        halo = h_ref[...].astype(F32) * (pl.program_id(1) < ni - 1).astype(F32)
        row = lax.broadcasted_iota(jnp.int32, d.shape, 0)
        n0, n1 = halo[0:1, :], halo[1:2, :]
        d1 = jnp.where(row == tm - 1, n0, pltpu.roll(d, tm - 1, 0))
        d2 = jnp.where(row == tm - 2, n0, jnp.where(row == tm - 1, n1, pltpu.roll(d, tm - 2, 0)))
        wv = w_ref[...]
        o_ref[...] = (wv[2:3, :] * d + wv[1:2, :] * d1 + wv[0:1, :] * d2).astype(o_ref.dtype)

    main = pl.BlockSpec((tm, tn), lambda j, i: (i, j))
    return pl.pallas_call(
        body, name=name, grid=(nj, ni),
        in_specs=[main, pl.BlockSpec((HALO, tn), lambda j, i: (jnp.minimum((i + 1) * hb, s // HALO - 1), j)),
                  pl.BlockSpec((3, tn), lambda j, i: (0, j))],
        out_specs=main, out_shape=jax.ShapeDtypeStruct((s, f), BF16),
        compiler_params=_params(("parallel", "parallel")),
    )(dc, dc, w)


def _tile2d(r, c):
    return _pick(r, ROW_TILE), _pick(c, (2048, 1024, 512, 256, 128))


def sum_slots(buf, *, name):
    n, shape = buf.shape[0], buf.shape[1:]
    r, c = math.prod(shape[:-1]), shape[-1]
    tm, tn = _tile2d(r, c)

    def body(b_ref, o_ref):
        acc = b_ref[0]
        for k in range(1, n):
            acc = acc + b_ref[k]
        o_ref[...] = acc

    out = pl.pallas_call(
        body, name=name, grid=(r // tm, c // tn),
        in_specs=[pl.BlockSpec((n, tm, tn), lambda i, j: (0, i, j))],
        out_specs=pl.BlockSpec((tm, tn), lambda i, j: (i, j)), out_shape=jax.ShapeDtypeStruct((r, c), F32),
        compiler_params=_params(("parallel", "parallel")),
    )(buf.reshape(n, r, c))
    return out.reshape(shape)


def add_half(layers, got, core, *, name):
    n_l = len(layers)
    a, _, r, c = layers[0].shape
    tm, tn = _tile2d(r, c)

    def body(core_ref, *refs):
        got_ref, o_ref = refs[n_l:]
        for l in range(n_l):
            @pl.when(pl.program_id(0) == l)
            def _(l=l):
                o_ref[...] = (refs[l][...] + got_ref[...]).astype(o_ref.dtype)

    def layer_spec(l):
        def index(b, q, i, j, core_ref):
            on = b == l
            return (jnp.where(on, q, 0), core_ref[0], jnp.where(on, i, 0), jnp.where(on, j, 0))
        return pl.BlockSpec((None, None, tm, tn), index)

    part = pl.BlockSpec((None, None, tm, tn), lambda b, q, i, j, core_ref: (b, q, i, j))
    grid_spec = pltpu.PrefetchScalarGridSpec(
        num_scalar_prefetch=1, grid=(n_l, a, r // tm, c // tn),
        in_specs=[layer_spec(l) for l in range(n_l)] + [part], out_specs=part)
    return pl.pallas_call(
        body, name=name, grid_spec=grid_spec, out_shape=jax.ShapeDtypeStruct((n_l, a, r, c), BF16),
        compiler_params=_params(("arbitrary", "parallel", "parallel", "parallel")),
    )(core, *layers, got)


def add_own_block(h, got, chip, axis, *, name):
    _, l, r, c = got.shape
    tm, tn = _tile2d(r, c)
    ncb = c // tn

    def body(chip_ref, h_ref, got_ref, o_ref):
        o_ref[...] = ((h_ref[...].astype(F32) + got_ref[0].astype(F32)) + got_ref[1].astype(F32)) + got_ref[2].astype(F32)

    if axis == 0:
        h_spec = pl.BlockSpec((None, None, tm, tn), lambda b, i, j, chip_ref: (b, chip_ref[0], i, j))
    else:
        h_spec = pl.BlockSpec((None, tm, tn), lambda b, i, j, chip_ref: (b, i, chip_ref[0] * ncb + j))
    grid_spec = pltpu.PrefetchScalarGridSpec(
        num_scalar_prefetch=1, grid=(l, r // tm, ncb),
        in_specs=[h_spec, pl.BlockSpec((3, None, tm, tn), lambda b, i, j, chip_ref: (0, b, i, j))],
        out_specs=pl.BlockSpec((None, tm, tn), lambda b, i, j, chip_ref: (b, i, j)))
    return pl.pallas_call(
        body, name=name, grid_spec=grid_spec, out_shape=jax.ShapeDtypeStruct((l, r, c), F32),
        compiler_params=_params(("parallel", "parallel", "parallel")),
    )(chip, h, got)


def _adam_update(w, g, m, v):
    bc1, bc2 = 1.0 - ADAM_B1 ** ADAM_STEP, 1.0 - ADAM_B2 ** ADAM_STEP
    mn = ADAM_B1 * m + (1.0 - ADAM_B1) * g
    vn = ADAM_B2 * v + (1.0 - ADAM_B2) * (g * g)
    return -ADAM_LR * ((mn / bc1) / (jnp.sqrt(vn / bc2) + ADAM_EPS) + ADAM_WD * w), mn, vn


def adamw(w, g, m, v, *, name):
    shape = w.shape
    c = shape[-1]
    r = math.prod(shape[:-1]) if len(shape) > 1 else 1
    tm, tn = _tile2d(r, c)

    def body(w_ref, g_ref, m_ref, v_ref, d_ref, mo_ref, vo_ref):
        d_ref[...], mo_ref[...], vo_ref[...] = _adam_update(w_ref[...], g_ref[...], m_ref[...], v_ref[...])

    spec = pl.BlockSpec((tm, tn), lambda i, j: (i, j))
    outs = pl.pallas_call(
        body, name=name, grid=(r // tm, c // tn), in_specs=[spec] * 4, out_specs=[spec] * 3,
        out_shape=[jax.ShapeDtypeStruct((r, c), F32)] * 3, compiler_params=_params(("parallel", "parallel")),
    )(*(t.reshape(r, c) for t in (w, g, m, v)))
    return tuple(o.reshape(shape) for o in outs)


def adamw_halves(w, mine, got, m, v, core, *, name):
    l, r, c = w.shape
    rh = r // 2
    tm, tn = _tile2d(rh, c)

    def body(core_ref, w_ref, mine_ref, got_ref, m_ref, v_ref, g_ref, d_ref, mo_ref, vo_ref):
        g = jnp.where(pl.program_id(1) == core_ref[0], mine_ref[...], got_ref[...])
        g_ref[...] = g
        d_ref[...], mo_ref[...], vo_ref[...] = _adam_update(w_ref[...], g, m_ref[...], v_ref[...])

    full = pl.BlockSpec((None, None, tm, tn), lambda b, h, i, j, core_ref: (b, h, i, j))
    half = pl.BlockSpec((None, tm, tn), lambda b, h, i, j, core_ref: (b, i, j))
    grid_spec = pltpu.PrefetchScalarGridSpec(
        num_scalar_prefetch=1, grid=(l, 2, rh // tm, c // tn),
        in_specs=[full, half, half, full, full], out_specs=[full] * 4)
    outs = pl.pallas_call(
        body, name=name, grid_spec=grid_spec, out_shape=[jax.ShapeDtypeStruct((l, 2, rh, c), F32)] * 4,
        compiler_params=_params(("parallel", "parallel", "parallel", "parallel")),
    )(core, w.reshape(l, 2, rh, c), mine, got, m.reshape(l, 2, rh, c), v.reshape(l, 2, rh, c))
    return tuple(o.reshape(l, r, c) for o in outs)


def _comm(name, inputs, out_shapes, aliases, plan):
    n_in, n_out = len(inputs), len(out_shapes)
    probe = plan([None] * n_in, [None] * n_out, 0, 0, 0, count_only=True)

    def body(*refs):
        in_refs, out_refs = refs[:n_in], refs[n_in:n_in + n_out]
        send_sems, recv_sems = refs[n_in + n_out:]
        x, y, c = lax.axis_index("x"), lax.axis_index("y"), lax.axis_index("c")
        xfers = plan(in_refs, out_refs, x, y, c, count_only=False)
        started = []
        for n, (src, dst, peer, _) in enumerate(xfers):
            if peer is None:
                cp = pltpu.make_async_copy(src, dst, send_sems.at[n])
            else:
                cp = pltpu.make_async_remote_copy(src_ref=src, dst_ref=dst, send_sem=send_sems.at[n],
                                                  recv_sem=recv_sems.at[n], device_id=peer, device_id_type=MESH)
            cp.start()
            started.append(cp)
        for n, (src, dst, peer, got) in enumerate(xfers):
            if peer is None:
                started[n].wait()
            else:
                pltpu.make_async_remote_copy(src_ref=got, dst_ref=got, send_sem=send_sems.at[n],
                                             recv_sem=recv_sems.at[n], device_id=peer,
                                             device_id_type=MESH).wait_recv()
                started[n].wait_send()

    hbm = pl.BlockSpec(memory_space=pl.ANY)
    return pl.pallas_call(
        body, name=name, in_specs=[hbm] * n_in, out_specs=[hbm] * n_out, out_shape=out_shapes,
        input_output_aliases=aliases,
        scratch_shapes=[pltpu.SemaphoreType.DMA((probe,)), pltpu.SemaphoreType.DMA((probe,))],
    )(*inputs)


def _other_chips(x, y):
    return [(1 - x, y), (x, 1 - y), (1 - x, 1 - y)]


def gather_weights(shards, axes):
    def out_shape(sh, ax):
        l, r, c = sh.shape
        return jax.ShapeDtypeStruct((l, N_CHIPS, 2, r // 2, c) if ax == 0 else (l, 2, r // 2, N_CHIPS * c), BF16)

    def piece(ref, ax, chip, half, width):
        if ax == 0:
            return ref.at[:, chip, half]
        return ref.at[:, half, :, pl.ds(chip * width, width)]

    def plan1(in_refs, out_refs, x, y, c, count_only):
        if count_only:
            return 4 * len(shards)
        me, xfers = 2 * x + y, []
        for sh, ax, src, out in zip(shards, axes, in_refs, out_refs):
            rows, width = sh.shape[1] // 2, sh.shape[2]
            mine = src.at[:, pl.ds(c * rows, rows), :]
            xfers.append((mine, piece(out, ax, me, c, width), None, None))
            for cx, cy in _other_chips(x, y):
                xfers.append((mine, piece(out, ax, me, c, width), (cx, cy, c), piece(out, ax, 2 * cx + cy, c, width)))
        return xfers

    def half_of(ref, ax, half):
        return ref.at[:, :, half] if ax == 0 else ref.at[:, half]

    def plan2(in_refs, out_refs, x, y, c, count_only):
        if count_only:
            return len(shards)
        return [(half_of(out, ax, c), half_of(out, ax, c), (x, y, 1 - c), half_of(out, ax, 1 - c))
                for ax, out in zip(axes, out_refs)]

    shapes = [out_shape(sh, ax) for sh, ax in zip(shards, axes)]
    part = _comm("gather_chips", list(shards), shapes, {}, plan1)
    full = _comm("gather_cores", list(part), shapes, {n: n for n in range(len(shards))}, plan2)
    return [f.reshape(sh.shape[0], N_CHIPS * sh.shape[1], sh.shape[2]) if ax == 0
            else f.reshape(sh.shape[0], sh.shape[1], N_CHIPS * sh.shape[2])
            for f, sh, ax in zip(full, shards, axes)]


def reduce_scatter_grads(grads, axes, core, chip):
    dims, views = [], []
    for layers, ax in zip(grads, axes):
        r, c = layers[0].shape
        if ax == 0:
            dims.append((len(layers), r // N_CHIPS // 2, c))
            views.append([g.reshape(N_CHIPS, 2, r // N_CHIPS // 2, c) for g in layers])
        else:
            dims.append((len(layers), r // 2, c // N_CHIPS))
            views.append([g.reshape(1, 2, r // 2, c) for g in layers])

    def half_shape(ax, d):
        return (d[0], N_CHIPS, d[1], d[2]) if ax == 0 else (d[0], d[1], N_CHIPS * d[2])

    def plan_a(in_refs, out_refs, x, y, c, count_only):
        if count_only:
            return sum(d[0] for d in dims)
        xfers, n = [], 0
        for d, out in zip(dims, out_refs):
            for l in range(d[0]):
                xfers.append((in_refs[n].at[:, 1 - c], out.at[l], (x, y, 1 - c), out.at[l]))
                n += 1
        return xfers

    got = _comm("reduce_cores", [v for layers in views for v in layers],
                [jax.ShapeDtypeStruct((d[0], N_CHIPS if ax == 0 else 1, d[1], d[2] if ax == 0 else N_CHIPS * d[2]), F32)
                 for ax, d in zip(axes, dims)], {}, plan_a)
    chip_sum = [add_half(layers, g, core, name="sum_cores").reshape(half_shape(ax, d))
                for ax, d, layers, g in zip(axes, dims, views, got)]

    def block(ref, ax, which, width):
        return ref.at[:, which] if ax == 0 else ref.at[:, :, pl.ds(which * width, width)]

    def plan_b(in_refs, out_refs, x, y, c, count_only):
        if count_only:
            return 3 * len(grads)
        xfers = []
        for ax, d, src, out in zip(axes, dims, in_refs, out_refs):
            for k, (cx, cy) in enumerate(_other_chips(x, y)):
                xfers.append((block(src, ax, 2 * cx + cy, d[2]), out.at[k], (cx, cy, c), out.at[k]))
        return xfers

    parts = _comm("reduce_chips", chip_sum, [jax.ShapeDtypeStruct((3,) + d, BF16) for d in dims], {}, plan_b)
    mine = [add_own_block(h, p, chip, ax, name="sum_chips") for h, p, ax in zip(chip_sum, parts, axes)]

    def plan_c(in_refs, out_refs, x, y, c, count_only):
        if count_only:
            return len(grads)
        return [(src, out, (x, y, 1 - c), out) for src, out in zip(in_refs, out_refs)]

    theirs = _comm("share_cores", mine, [jax.ShapeDtypeStruct(d, F32) for d in dims], {}, plan_c)
    return list(zip(mine, theirs))


def gather_slabs(vec, *, name):
    def plan(in_refs, out_refs, x, y, c, count_only):
        if count_only:
            return 8
        me = 4 * x + 2 * y + c
        xfers = [(in_refs[0], out_refs[0].at[me], None, None)]
        for k in range(1, 8):
            px, py, pc = x ^ (k >> 2), y ^ ((k >> 1) & 1), c ^ (k & 1)
            xfers.append((in_refs[0], out_refs[0].at[me], (px, py, pc), out_refs[0].at[4 * px + 2 * py + pc]))
        return xfers

    return _comm(name, [vec], [jax.ShapeDtypeStruct((8,) + vec.shape, F32)], {}, plan)[0]


def allreduce_small(vec):
    return sum_slots(gather_slabs(vec, name="gather_small"), name="sum_small")


def gather_conv_w(block):
    l, taps, c = block.shape
    flat = block.reshape(-1)
    slabs = gather_slabs(jnp.pad(flat, (0, -flat.size % 1024)).reshape(-1, 128), name="gather_conv_w")
    per_chip = slabs[0::2].reshape(N_CHIPS, -1)[:, :flat.size].reshape(N_CHIPS, l, taps, c)
    return per_chip.transpose(1, 2, 0, 3).reshape(l, taps, N_CHIPS * c)


def _rope_tables(positions):
    inv_freq = ROPE_THETA ** (-jnp.arange(0, ROPE_DIM, 2, dtype=F32) / ROPE_DIM)
    ang = positions.astype(F32)[:, None] * inv_freq
    ang = jnp.concatenate([ang, ang], axis=-1)
    pad = ((0, 0), (0, HEAD - ROPE_DIM))
    return jnp.pad(jnp.cos(ang), pad), jnp.pad(jnp.sin(ang), pad)


def _uq_layout(w, heads):
    ql = w.shape[0]
    w = w.reshape(ql, heads, HEAD + ROPE_DIM)
    rot = jnp.pad(w[:, :, HEAD:], ((0, 0), (0, 0), (0, HEAD - ROPE_DIM)))
    return jnp.concatenate([w[:, :, :HEAD].reshape(ql, heads * HEAD), rot.reshape(ql, heads * HEAD)], axis=1)


def _uq_layout_inv(dw, heads):
    ql = dw.shape[0]
    nope = dw[:, :heads * HEAD].reshape(ql, heads, HEAD)
    rot = dw[:, heads * HEAD:].reshape(ql, heads, HEAD)[:, :, :ROPE_DIM]
    return jnp.concatenate([nope, rot], axis=-1).reshape(ql, heads * (HEAD + ROPE_DIM))


def kernel(x, mem, positions, norm_mix, norm_mem_q, norm_mem_kv, norm_ffn, norm_final, mla_w_down, mla_q_norm, mla_w_uq, mla_kv_norm, mla_w_ukv, mla_w_o, sb_w_qkv, sb_w_o, mem_w_q, mem_w_kv, mem_w_o, ffn_w_in, ffn_conv_w, ffn_conv_b, ffn_w_out, loss_target, m_norm_mix, m_norm_mem_q, m_norm_mem_kv, m_norm_ffn, m_norm_final, m_mla_w_down, m_mla_q_norm, m_mla_w_uq, m_mla_kv_norm, m_mla_w_ukv, m_mla_w_o, m_sb_w_qkv, m_sb_w_o, m_mem_w_q, m_mem_w_kv, m_mem_w_o, m_ffn_w_in, m_ffn_conv_w, m_ffn_conv_b, m_ffn_w_out, v_norm_mix, v_norm_mem_q, v_norm_mem_kv, v_norm_ffn, v_norm_final, v_mla_w_down, v_mla_q_norm, v_mla_w_uq, v_mla_kv_norm, v_mla_w_ukv, v_mla_w_o, v_sb_w_qkv, v_sb_w_o, v_mem_w_q, v_mem_w_kv, v_mem_w_o, v_ffn_w_in, v_ffn_conv_w, v_ffn_conv_b, v_ffn_w_out):
    args = dict(locals())
    big = ["mla_w_down", "mla_w_uq", "mla_w_ukv", "mla_w_o", "sb_w_qkv", "sb_w_o", "mem_w_q", "mem_w_kv",
           "mem_w_o", "ffn_w_in", "ffn_w_out"]
    col_cut = {"mla_w_uq", "mla_w_ukv", "sb_w_qkv", "mem_w_o", "ffn_w_in"}
    axes = [1 if n in col_cut else 0 for n in big]
    small = ["norm_mix", "norm_mem_q", "norm_mem_kv", "norm_ffn", "norm_final", "mla_q_norm", "mla_kv_norm",
             "ffn_conv_b", "ffn_conv_w"]
    order = ["norm_mix", "norm_mem_q", "norm_mem_kv", "norm_ffn", "norm_final", "mla_w_down", "mla_q_norm",
             "mla_w_uq", "mla_kv_norm", "mla_w_ukv", "mla_w_o", "sb_w_qkv", "sb_w_o", "mem_w_q", "mem_w_kv",
             "mem_w_o", "ffn_w_in", "ffn_conv_w", "ffn_conv_b", "ffn_w_out"]

    xs, mems, target = x[0], mem[0], loss_target[0]
    s, d = xs.shape
    depth = norm_mix.shape[0]
    ql, kvl = mla_q_norm.shape[1], mla_kv_norm.shape[1]
    mla_heads = N_CHIPS * mla_w_uq.shape[2] // (HEAD + ROPE_DIM)
    sb_heads = N_CHIPS * sb_w_qkv.shape[2] // (3 * HEAD)
    mem_heads = mem_w_q.shape[2] // HEAD
    ff = N_CHIPS * ffn_w_out.shape[1]
    chip = 2 * lax.axis_index("x") + lax.axis_index("y")
    chip_op = jnp.reshape(chip, (1,)).astype(jnp.int32)
    core_op = jnp.reshape(lax.axis_index("c"), (1,)).astype(jnp.int32)

    full = dict(zip(big, gather_weights([args[n].astype(BF16) for n in big], axes)))
    w_uq = [_uq_layout(full["mla_w_uq"][j], mla_heads) for j in range(full["mla_w_uq"].shape[0])]
    cos, sin = _rope_tables(positions[0])
    _, sb_tk = _sb_tiles(s)
    tri = (jnp.arange(sb_tk)[:, None] > jnp.arange(sb_tk)[None, :]).astype(BF16)
    conv_w = gather_conv_w(ffn_conv_w)

    saved = []
    xa = xs
    for i in range(depth):
        j = i // 2
        lay = {}
        lay["xa"] = xa
        h1 = rmsnorm_fwd(xa, norm_mix[i], name="norm_fwd")
        lay["h1"] = h1
        if i % 2 == 0:
            down = mm(h1, full["mla_w_down"][j], out_dtype=F32, name="mm_down")
            cq = rmsnorm_fwd(down[:, :ql], mla_q_norm[j], name="norm_lora_fwd")
            ckv = rmsnorm_fwd(down[:, ql:ql + kvl], mla_kv_norm[j], name="norm_lora_fwd")
            kr_raw = jnp.pad(down[:, ql + kvl:], ((0, 0), (0, HEAD - ROPE_DIM)))
            kr = rope(kr_raw, cos, sin, col0=0, n=1, inverse=False, name="rope_k")
            qall = mm(cq, w_uq[j], name="mm_uq")
            qr = rope(qall, cos, sin, col0=mla_heads, n=mla_heads, inverse=False, name="rope_q")
            kv = mm(ckv, full["mla_w_ukv"][j], name="mm_ukv")
            o, lse = mla_fwd(qall, qr, kv, kr, mla_heads, name="mla_fwd")
            xb = mm(o, full["mla_w_o"][j], add=xa, out_dtype=F32, name="mm_out_res")
            lay.update(down=down, cq=cq, ckv=ckv, qall=qall, qr=qr, kv=kv, kr=kr, o=o, lse=lse)
        else:
            qkv = mm(h1, full["sb_w_qkv"][j], name="mm_qkv")
            o, cmat = sb_fwd(qkv, tri, sb_heads, name="sb_fwd")
            xb = mm(o, full["sb_w_o"][j], add=xa, out_dtype=F32, name="mm_out_res")
            lay.update(qkv=qkv, o=o, cmat=cmat)
        h2 = rmsnorm_fwd(xb, norm_mem_q[i], name="norm_fwd")
        hm = rmsnorm_fwd(mems, norm_mem_kv[i], name="norm_mem_fwd")
        qm = mm(h2, full["mem_w_q"][i], name="mm_mem_q")
        kvm = mm(hm, full["mem_w_kv"][i], name="mm_mem_kv")
        om = mem_fwd(qm, kvm, mem_heads, name="mem_fwd")
        xc = mm(om, full["mem_w_o"][i], add=xb, out_dtype=F32, name="mm_mem_out_res")
        h3 = rmsnorm_fwd(xc, norm_ffn[i], name="norm_fwd")
        u = mm(h3, full["ffn_w_in"][i], name="mm_ffn_in")
        act = conv_gate_fwd(u, conv_w[i], ffn_conv_b[i][None, :], name="conv_gate_fwd")
        xd = mm(act, full["ffn_w_out"][i], add=xc, out_dtype=F32, name="mm_ffn_out_res")
        lay.update(xb=xb, h2=h2, hm=hm, qm=qm, kvm=kvm, om=om, xc=xc, h3=h3, u=u, act=act)
        saved.append(lay)
        xa = xd

    dx, dxb, g_final, loss_part = final_loss(xa, norm_final, target, name="final_loss")

    gw = {n: [None] * full[n].shape[0] for n in big}
    g_small = {"norm_mix": [None] * depth, "norm_mem_q": [None] * depth, "norm_mem_kv": [None] * depth,
               "norm_ffn": [None] * depth, "mla_q_norm": [None] * (depth - depth // 2),
               "mla_kv_norm": [None] * (depth - depth // 2), "conv": [None] * depth}
    for i in reversed(range(depth)):
        j = i // 2
        lay = saved[i]
        gw["ffn_w_out"][i] = mm(lay["act"], dxb, mode="tn", out_dtype=F32, name="mm_ffn_out_wgrad")
        da = mm(dxb, full["ffn_w_out"][i], mode="nt", name="mm_ffn_out_dgrad")
        dcg, dcu, g_small["conv"][i] = conv_gate_bwd(lay["u"], da, conv_w[i], ffn_conv_b[i][None, :], name="conv_gate_bwd")
        du = jnp.concatenate([conv_transpose(dcg, conv_w[i][:, :ff], name="conv_transpose"),
                              conv_transpose(dcu, conv_w[i][:, ff:], name="conv_transpose")], axis=1)
        gw["ffn_w_in"][i] = mm(lay["h3"], du, mode="tn", out_dtype=F32, name="mm_ffn_in_wgrad")
        dh3 = mm(du, full["ffn_w_in"][i], mode="nt", name="mm_ffn_in_dgrad")
        dx, dxb, g_small["norm_ffn"][i] = rmsnorm_bwd(dh3, lay["xc"], norm_ffn[i], dx, name="norm_bwd")
        gw["mem_w_o"][i] = mm(lay["om"], dxb, mode="tn", out_dtype=F32, name="mm_mem_out_wgrad")
        dom = mm(dxb, full["mem_w_o"][i], mode="nt", name="mm_mem_out_dgrad")
        dqm, dkm, dvm = mem_bwd(lay["qm"], lay["kvm"], dom, mem_heads, name="mem_bwd")
        dkvm = jnp.concatenate([dkm, dvm], axis=1)
        gw["mem_w_q"][i] = mm(lay["h2"], dqm, mode="tn", out_dtype=F32, name="mm_mem_q_wgrad")
        gw["mem_w_kv"][i] = mm(lay["hm"], dkvm, mode="tn", out_dtype=F32, name="mm_mem_kv_wgrad")
        dh2 = mm(dqm, full["mem_w_q"][i], mode="nt", name="mm_mem_q_dgrad")
        dhm = mm(dkvm, full["mem_w_kv"][i], mode="nt", name="mm_mem_kv_dgrad")
        _, _, g_small["norm_mem_kv"][i] = rmsnorm_bwd(dhm, mems, norm_mem_kv[i], name="norm_mem_bwd")
        dx, dxb, g_small["norm_mem_q"][i] = rmsnorm_bwd(dh2, lay["xb"], norm_mem_q[i], dx, name="norm_bwd")
        if i % 2 == 0:
            gw["mla_w_o"][j] = mm(lay["o"], dxb, mode="tn", out_dtype=F32, name="mm_out_wgrad")
            do = mm(dxb, full["mla_w_o"][j], mode="nt", name="mm_out_dgrad")
            dqn, dqr, dkv, dkr_heads = mla_bwd(lay["qall"], lay["qr"], lay["kv"], lay["kr"], lay["o"], do, lay["lse"],
                                               mla_heads, name="mla_bwd")
            dkr = sum_lane_tiles(dkr_heads, mla_heads, name="sum_heads")
            dqall = jnp.concatenate([dqn, rope(dqr, cos, sin, col0=0, n=mla_heads, inverse=True, name="rope_q_bwd")], axis=1)
            gw["mla_w_ukv"][j] = mm(lay["ckv"], dkv, mode="tn", out_dtype=F32, name="mm_ukv_wgrad")
            dckv = mm(dkv, full["mla_w_ukv"][j], mode="nt", name="mm_ukv_dgrad")
            gw["mla_w_uq"][j] = _uq_layout_inv(mm(lay["cq"], dqall, mode="tn", out_dtype=F32, name="mm_uq_wgrad"), mla_heads)
            dcq = mm(dqall, w_uq[j], mode="nt", name="mm_uq_dgrad")
            down = lay["down"]
            _, d_q, g_small["mla_q_norm"][j] = rmsnorm_bwd(dcq, down[:, :ql], mla_q_norm[j], name="norm_lora_bwd")
            _, d_kv, g_small["mla_kv_norm"][j] = rmsnorm_bwd(dckv, down[:, ql:ql + kvl], mla_kv_norm[j], name="norm_lora_bwd")
            d_kr = rope(dkr, cos, sin, col0=0, n=1, inverse=True, name="rope_k_bwd")[:, :ROPE_DIM]
            ddown = jnp.concatenate([d_q, d_kv, d_kr], axis=1)
            gw["mla_w_down"][j] = mm(lay["h1"], ddown, mode="tn", out_dtype=F32, name="mm_down_wgrad")
            dh1 = mm(ddown, full["mla_w_down"][j], mode="nt", name="mm_down_dgrad")
        else:
            gw["sb_w_o"][j] = mm(lay["o"], dxb, mode="tn", out_dtype=F32, name="mm_out_wgrad")
            do = mm(dxb, full["sb_w_o"][j], mode="nt", name="mm_out_dgrad")
            dq, dk, dv = sb_bwd(lay["qkv"], do, lay["cmat"], tri, sb_heads, name="sb_bwd")
            dqkv = jnp.concatenate([dq, dk, dv], axis=1)
            gw["sb_w_qkv"][j] = mm(lay["h1"], dqkv, mode="tn", out_dtype=F32, name="mm_qkv_wgrad")
            dh1 = mm(dqkv, full["sb_w_qkv"][j], mode="nt", name="mm_qkv_dgrad")
        dx, dxb, g_small["norm_mix"][i] = rmsnorm_bwd(dh1, lay["xa"], norm_mix[i], dx, name="norm_bwd")

    halves = dict(zip(big, reduce_scatter_grads([gw[n] for n in big], axes, core_op, chip_op)))
    conv = jnp.stack(g_small["conv"])
    parts = [jnp.concatenate(g_small[n], axis=0) for n in ("norm_mix", "norm_mem_q", "norm_mem_kv", "norm_ffn")]
    parts += [g_final, jnp.concatenate(g_small["mla_q_norm"], axis=0), jnp.concatenate(g_small["mla_kv_norm"], axis=0),
              conv[:, 3, :], conv[:, :3, :], loss_part[:1, :1]]
    sizes = [p.size for p in parts]
    packed = jnp.concatenate([p.reshape(-1) for p in parts])
    packed = jnp.pad(packed, (0, -packed.size % 1024)).reshape(-1, 128)
    total = allreduce_small(packed).reshape(-1)
    g_rep, at = {}, 0
    for n, p, size in zip(small + ["loss"], parts, sizes):
        g_rep[n] = total[at:at + size].reshape(p.shape)
        at += size
    loss = g_rep.pop("loss").reshape(())
    g_rep["norm_final"] = g_rep["norm_final"].reshape(norm_final.shape)
    width = ffn_conv_w.shape[2]
    g_rep["ffn_conv_w"] = lax.dynamic_slice_in_dim(g_rep["ffn_conv_w"], chip * width, width, axis=2)

    grads, delta, new_m, new_v = {}, {}, {}, {}
    for n in order:
        if n in halves:
            mine, theirs = halves[n]
            grads[n], delta[n], new_m[n], new_v[n] = adamw_halves(
                args[n], mine, theirs, args["m_" + n], args["v_" + n], core_op, name="adamw_big")
        else:
            grads[n] = g_rep[n]
            delta[n], new_m[n], new_v[n] = adamw(args[n], g_rep[n], args["m_" + n], args["v_" + n], name="adamw")
    return (loss, dx[None], *[grads[n] for n in order], *[delta[n] for n in order],
            *[new_m[n] for n in order], *[new_v[n] for n in order])
```

```python
import math

import jax
import jax.numpy as jnp
import numpy as np
from jax import lax
from jax.experimental import pallas as pl
from jax.experimental.pallas import tpu as pltpu

F32 = jnp.float32
BF16 = jnp.bfloat16
MESH = pl.DeviceIdType.MESH

EPS = 1e-6
LOG2E = 1.4426950408889634
CHUNK_SHIFT = 6
HEAD = 128
ROPE_DIM = 64
ROPE_THETA = 10000.0
N_CHIPS = 4
ADAM_LR, ADAM_B1, ADAM_B2, ADAM_EPS, ADAM_WD, ADAM_STEP = 0.001, 0.9, 0.999, 1e-08, 0.01, 10

VMEM_LIMIT_BYTES = 48 * 1024 * 1024
MM_TM, MM_TN, MM_TK = (1024, 512, 256, 128), (1024, 1408, 512, 256, 128), (1024, 512, 256, 128)
MM_WHOLE_K, MM_WHOLE_TM = 2048, (512, 256, 128)
ROW_TILE = (256, 128, 64, 32, 16, 8)
ATT_T = 512
MLA_HEADS_PER_STEP, MLA_BWD_HEADS_PER_STEP = 2, 2
SB_TQ, SB_TK = 512, 256
SB_HEADS_PER_STEP = 2
SB_SUBS_PER_STEP = 4
SB_DEAD_LOG2 = -200.0
SB_UNVISITED = -1e30
MEM_TQ = 512
CONV_TM, CONV_TN = 512, (512, 256, 128)
HALO = 16


def _pick(dim, prefs):
    for p in prefs:
        if dim % p == 0:
            return p
    return dim


def _dot(a, b, ca, cb):
    return lax.dot_general(a, b, (((ca,), (cb,)), ((), ())), preferred_element_type=F32)


def _params(sem):
    return pltpu.CompilerParams(dimension_semantics=sem, vmem_limit_bytes=VMEM_LIMIT_BYTES)


def _head_groups(heads, per_step):
    hp = per_step if heads % per_step == 0 else 1
    return hp, heads // hp, hp * HEAD


def _tables(pairs):
    arr = np.asarray(pairs, dtype=np.int32)
    return jnp.asarray(arr[:, 0]), jnp.asarray(arr[:, 1])


def mm(a, b, *, mode="nn", add=None, out_dtype=BF16, name):
    if mode == "nn":
        (m, k), (k2, n) = a.shape, b.shape
    elif mode == "nt":
        (m, k), (n, k2) = a.shape, b.shape
    else:
        (k, m), (k2, n) = a.shape, b.shape
    assert k == k2, (a.shape, b.shape, mode)
    ca, cb = {"nn": (1, 0), "nt": (1, 1), "tn": (0, 0)}[mode]
    if k <= MM_WHOLE_K:
        return _mm_whole_k(a, b, add, (m, n, k), (ca, cb), mode, out_dtype, name)
    tm, tn, tk = _pick(m, MM_TM), _pick(n, MM_TN), _pick(k, MM_TK)
    nk = k // tk

    def body(a_ref, b_ref, *rest):
        if add is None:
            o_ref, acc_ref = rest
        else:
            add_ref, o_ref, acc_ref = rest
        kk = pl.program_id(2)

        @pl.when(kk == 0)
        def _():
            acc_ref[...] = jnp.zeros_like(acc_ref)

        acc_ref[...] += _dot(a_ref[...].astype(BF16), b_ref[...].astype(BF16), ca, cb)

        @pl.when(kk == nk - 1)
        def _():
            r = acc_ref[...]
            if add is not None:
                r = r + add_ref[...]
            o_ref[...] = r.astype(o_ref.dtype)

    if mode == "tn":
        a_spec = pl.BlockSpec((tk, tm), lambda i, j, kk: (kk, i))
    else:
        a_spec = pl.BlockSpec((tm, tk), lambda i, j, kk: (i, kk))
    if mode == "nt":
        b_spec = pl.BlockSpec((tn, tk), lambda i, j, kk: (j, kk))
    else:
        b_spec = pl.BlockSpec((tk, tn), lambda i, j, kk: (kk, j))
    o_spec = pl.BlockSpec((tm, tn), lambda i, j, kk: (i, j))
    in_specs, args = [a_spec, b_spec], [a, b]
    if add is not None:
        in_specs.append(o_spec)
        args.append(add)
    return pl.pallas_call(
        body, name=name, grid=(m // tm, n // tn, nk), in_specs=in_specs, out_specs=o_spec,
        out_shape=jax.ShapeDtypeStruct((m, n), out_dtype), scratch_shapes=[pltpu.VMEM((tm, tn), F32)],
        compiler_params=_params(("parallel", "parallel", "arbitrary")),
    )(*args)


def _mm_whole_k(a, b, add, mnk, contract, mode, out_dtype, name):
    m, n, k = mnk
    tm, tn = _pick(m, MM_WHOLE_TM), _pick(n, MM_TN)

    def body(a_ref, b_ref, *rest):
        r = _dot(a_ref[...].astype(BF16), b_ref[...].astype(BF16), *contract)
        if add is not None:
            r = r + rest[0][...]
        rest[-1][...] = r.astype(rest[-1].dtype)

    a_spec = pl.BlockSpec((k, tm), lambda j, i: (0, i)) if mode == "tn" else pl.BlockSpec((tm, k), lambda j, i: (i, 0))
    b_spec = pl.BlockSpec((tn, k), lambda j, i: (j, 0)) if mode == "nt" else pl.BlockSpec((k, tn), lambda j, i: (0, j))
    o_spec = pl.BlockSpec((tm, tn), lambda j, i: (i, j))
    in_specs, args = [a_spec, b_spec], [a, b]
    if add is not None:
        in_specs.append(o_spec)
        args.append(add)
    return pl.pallas_call(
        body, name=name, grid=(n // tn, m // tm), in_specs=in_specs, out_specs=o_spec,
        out_shape=jax.ShapeDtypeStruct((m, n), out_dtype), compiler_params=_params(("parallel", "parallel")),
    )(*args)


def rmsnorm_fwd(x, g, *, name):
    m, d = x.shape
    tm = _pick(m, ROW_TILE)

    def body(x_ref, g_ref, o_ref):
        xv = x_ref[...]
        r = lax.rsqrt(jnp.mean(xv * xv, axis=-1, keepdims=True) + EPS)
        o_ref[...] = (xv * r * g_ref[...]).astype(o_ref.dtype)

    return pl.pallas_call(
        body, name=name, grid=(m // tm,),
        in_specs=[pl.BlockSpec((tm, d), lambda i: (i, 0)), pl.BlockSpec((1, d), lambda i: (0, 0))],
        out_specs=pl.BlockSpec((tm, d), lambda i: (i, 0)), out_shape=jax.ShapeDtypeStruct((m, d), BF16),
        compiler_params=_params(("parallel",)),
    )(x, g.reshape(1, d))


def rmsnorm_bwd(dh, x, g, res=None, *, name):
    m, d = x.shape
    tm = _pick(m, ROW_TILE)

    def body(dh_ref, x_ref, g_ref, *rest):
        if res is None:
            dx_ref, dxb_ref, dg_ref = rest
        else:
            res_ref, dx_ref, dxb_ref, dg_ref = rest

        @pl.when(pl.program_id(0) == 0)
        def _():
            dg_ref[...] = jnp.zeros_like(dg_ref)

        xv = x_ref[...]
        dhv = dh_ref[...].astype(F32)
        r = lax.rsqrt(jnp.mean(xv * xv, axis=-1, keepdims=True) + EPS)
        y = xv * r
        dhg = dhv * g_ref[...]
        dx = r * (dhg - y * jnp.mean(dhg * y, axis=-1, keepdims=True))
        if res is not None:
            dx = dx + res_ref[...]
        dx_ref[...] = dx
        dxb_ref[...] = dx.astype(BF16)
        dg_ref[...] += jnp.sum(dhv * y, axis=0, keepdims=True)

    row = pl.BlockSpec((tm, d), lambda i: (i, 0))
    vec = pl.BlockSpec((1, d), lambda i: (0, 0))
    in_specs, args = [row, row, vec], [dh, x, g.reshape(1, d)]
    if res is not None:
        in_specs.append(row)
        args.append(res)
    return pl.pallas_call(
        body, name=name, grid=(m // tm,), in_specs=in_specs, out_specs=[row, row, vec],
        out_shape=[jax.ShapeDtypeStruct((m, d), F32), jax.ShapeDtypeStruct((m, d), BF16),
                   jax.ShapeDtypeStruct((1, d), F32)],
        compiler_params=_params(("arbitrary",)),
    )(*args)


def final_loss(x, g, target, *, name):
    m, d = x.shape
    tm = _pick(m, ROW_TILE)

    def body(x_ref, g_ref, t_ref, dx_ref, dxb_ref, dg_ref, loss_ref):
        @pl.when(pl.program_id(0) == 0)
        def _():
            dg_ref[...] = jnp.zeros_like(dg_ref)
            loss_ref[...] = jnp.zeros_like(loss_ref)

        xv = x_ref[...]
        gv = g_ref[...]
        r = lax.rsqrt(jnp.mean(xv * xv, axis=-1, keepdims=True) + EPS)
        y = xv * r
        err = y * gv - t_ref[...]
        loss_ref[...] += 0.5 * jnp.sum(jnp.mean(err * err, axis=-1, keepdims=True))
        dy = err * (1.0 / d)
        dyg = dy * gv
        dx = r * (dyg - y * jnp.mean(dyg * y, axis=-1, keepdims=True))
        dx_ref[...] = dx
        dxb_ref[...] = dx.astype(BF16)
        dg_ref[...] += jnp.sum(dy * y, axis=0, keepdims=True)

    row = pl.BlockSpec((tm, d), lambda i: (i, 0))
    vec = pl.BlockSpec((1, d), lambda i: (0, 0))
    return pl.pallas_call(
        body, name=name, grid=(m // tm,), in_specs=[row, vec, row],
        out_specs=[row, row, vec, pl.BlockSpec((8, 128), lambda i: (0, 0))],
        out_shape=[jax.ShapeDtypeStruct((m, d), F32), jax.ShapeDtypeStruct((m, d), BF16),
                   jax.ShapeDtypeStruct((1, d), F32), jax.ShapeDtypeStruct((8, 128), F32)],
        compiler_params=_params(("arbitrary",)),
    )(x, g.reshape(1, d), target)


def rope(xin, cos, sin, *, col0, n, inverse, name):
    s = xin.shape[0]
    tm = _pick(s, ROW_TILE)
    half = ROPE_DIM // 2

    def body(x_ref, c_ref, s_ref, o_ref):
        xv = x_ref[...].astype(F32)
        lane = lax.broadcasted_iota(jnp.int32, xv.shape, 1)
        rot = jnp.where(lane < half, -pltpu.roll(xv, HEAD - half, 1), pltpu.roll(xv, half, 1))
        sv = s_ref[...]
        if inverse:
            sv = -sv
        o_ref[...] = (xv * c_ref[...] + rot * sv).astype(o_ref.dtype)

    tab = pl.BlockSpec((tm, HEAD), lambda i, j: (i, 0))
    return pl.pallas_call(
        body, name=name, grid=(s // tm, n),
        in_specs=[pl.BlockSpec((tm, HEAD), lambda i, j: (i, j + col0)), tab, tab],
        out_specs=pl.BlockSpec((tm, HEAD), lambda i, j: (i, j)),
        out_shape=jax.ShapeDtypeStruct((s, n * HEAD), BF16),
        compiler_params=_params(("parallel", "parallel")),
    )(xin, cos, sin)


def sum_lane_tiles(xin, n, *, name):
    s = xin.shape[0]
    tm = _pick(s, ROW_TILE)

    def body(x_ref, o_ref):
        acc = x_ref[:, :HEAD]
        for k in range(1, n):
            acc = acc + x_ref[:, k * HEAD:(k + 1) * HEAD]
        o_ref[...] = acc

    return pl.pallas_call(
        body, name=name, grid=(s // tm,), in_specs=[pl.BlockSpec((tm, n * HEAD), lambda i: (i, 0))],
        out_specs=pl.BlockSpec((tm, HEAD), lambda i: (i, 0)), out_shape=jax.ShapeDtypeStruct((s, HEAD), F32),
        compiler_params=_params(("parallel",)),
    )(xin)


def _mla_scores(q, k, scale, diagonal):
    sc = _dot(q, k, 1, 1) * scale
    if not diagonal:
        return sc
    qchunk = jnp.right_shift(lax.broadcasted_iota(jnp.int32, sc.shape, 0), CHUNK_SHIFT)
    kchunk = jnp.right_shift(lax.broadcasted_iota(jnp.int32, sc.shape, 1), CHUNK_SHIFT)
    return jnp.where(kchunk <= qchunk, sc, -jnp.inf)


def mla_fwd(qall, qr, kv, kr, heads, *, name):
    s = qr.shape[0]
    t = min(ATT_T, s)
    nq = s // t
    scale = 1.0 / math.sqrt(HEAD + ROPE_DIM)
    pairs = [(qi, kj) for qi in range(nq) for kj in range(qi + 1)]
    qtab, ktab = _tables(pairs)
    hp, groups, wide = _head_groups(heads, MLA_HEADS_PER_STEP)

    def body(qt_ref, kt_ref, qn_ref, qr_ref, kv_ref, kr_ref, o_ref, lse_ref, m_ref, l_ref, acc_ref):
        st = pl.program_id(1)
        qi, kj = qt_ref[st], kt_ref[st]

        @pl.when(kj == 0)
        def _():
            m_ref[...] = jnp.full_like(m_ref, -jnp.inf)
            l_ref[...] = jnp.zeros_like(l_ref)
            acc_ref[...] = jnp.zeros_like(acc_ref)

        def step(diagonal):
            for hh in range(hp):
                cols = slice(hh * HEAD, (hh + 1) * HEAD)
                kn = kv_ref[:, 2 * hh * HEAD:(2 * hh + 1) * HEAD]
                v = kv_ref[:, (2 * hh + 1) * HEAD:(2 * hh + 2) * HEAD]
                sc = _mla_scores(jnp.concatenate([qn_ref[:, cols], qr_ref[:, cols]], axis=1),
                                 jnp.concatenate([kn, kr_ref[...]], axis=1), scale, diagonal)
                m_prev = m_ref[hh]
                m_new = jnp.maximum(m_prev, jnp.max(sc, axis=-1, keepdims=True))
                p = jnp.exp(sc - m_new)
                alpha = jnp.exp(m_prev - m_new)
                pv = _dot(p.astype(BF16), jnp.concatenate([v, jnp.ones_like(v)], axis=1), 1, 0)
                l_ref[hh] = alpha * l_ref[hh] + pv[:, HEAD:HEAD + 1]
                acc_ref[:, cols] = alpha * acc_ref[:, cols] + pv[:, :HEAD]
                m_ref[hh] = m_new

        @pl.when(kj < qi)
        def _():
            step(False)

        @pl.when(kj == qi)
        def _():
            step(True)
            for hh in range(hp):
                cols = slice(hh * HEAD, (hh + 1) * HEAD)
                o_ref[:, cols] = (acc_ref[:, cols] / l_ref[hh]).astype(o_ref.dtype)
                lse_ref[:, cols] = jnp.broadcast_to(m_ref[hh] + jnp.log(l_ref[hh]), (t, HEAD))

    qspec = pl.BlockSpec((t, wide), lambda h, st, qt, kt: (qt[st], h))
    grid_spec = pltpu.PrefetchScalarGridSpec(
        num_scalar_prefetch=2, grid=(groups, len(pairs)),
        in_specs=[qspec, qspec,
                  pl.BlockSpec((t, 2 * wide), lambda h, st, qt, kt: (kt[st], h)),
                  pl.BlockSpec((t, HEAD), lambda h, st, qt, kt: (kt[st], 0))],
        out_specs=[qspec, qspec],
        scratch_shapes=[pltpu.VMEM((hp, t, 1), F32), pltpu.VMEM((hp, t, 1), F32), pltpu.VMEM((t, wide), F32)])
    return pl.pallas_call(
        body, name=name, grid_spec=grid_spec,
        out_shape=[jax.ShapeDtypeStruct((s, heads * HEAD), BF16), jax.ShapeDtypeStruct((s, heads * HEAD), F32)],
        compiler_params=_params(("parallel", "arbitrary")),
    )(qtab, ktab, qall, qr, kv, kr)


def mla_bwd(qall, qr, kv, kr, o, do, lse, heads, *, name):
    s = qr.shape[0]
    t = min(ATT_T, s)
    nq = s // t
    scale = 1.0 / math.sqrt(HEAD + ROPE_DIM)
    pairs = [(kj, qi) for kj in range(nq) for qi in range(kj, nq)]
    ktab, qtab = _tables(pairs)
    last = len(pairs) - 1
    hp, groups, wide = _head_groups(heads, MLA_BWD_HEADS_PER_STEP)

    def body(kt_ref, qt_ref, qn_ref, qr_ref, kv_ref, kr_ref, o_ref, do_ref, lse_ref,
             dqn_ref, dqr_ref, dkv_ref, dkr_ref, fn_ref, fr_ref, akn_ref, av_ref, akr_ref):
        st = pl.program_id(1)
        kj, qi = kt_ref[st], qt_ref[st]

        @pl.when(st == 0)
        def _():
            fn_ref[...] = jnp.zeros_like(fn_ref)
            fr_ref[...] = jnp.zeros_like(fr_ref)

        @pl.when(qi == kj)
        def _():
            akn_ref[...] = jnp.zeros_like(akn_ref)
            av_ref[...] = jnp.zeros_like(av_ref)
            akr_ref[...] = jnp.zeros_like(akr_ref)

        def step(diagonal):
            rows = pl.ds(pl.multiple_of(qi * t, t), t)
            krv = kr_ref[...]
            for hh in range(hp):
                cols = slice(hh * HEAD, (hh + 1) * HEAD)
                dov = do_ref[:, cols]
                q2 = jnp.concatenate([qn_ref[:, cols], qr_ref[:, cols]], axis=1)
                k2 = jnp.concatenate([kv_ref[:, 2 * hh * HEAD:(2 * hh + 1) * HEAD], krv], axis=1)
                v = kv_ref[:, (2 * hh + 1) * HEAD:(2 * hh + 2) * HEAD]
                sc = _mla_scores(q2, k2, scale, diagonal)
                p = jnp.exp(sc - lse_ref[:, hh * HEAD:hh * HEAD + 1])
                delta = jnp.sum(dov.astype(F32) * o_ref[:, cols].astype(F32), axis=-1, keepdims=True)
                dp = _dot(dov, v, 1, 1)
                ds = (p * (dp - delta) * scale).astype(BF16)
                av_ref[:, cols] += _dot(p.astype(BF16), dov, 0, 0)
                dk2 = _dot(ds, q2, 0, 0)
                akn_ref[:, cols] += dk2[:, :HEAD]
                akr_ref[:, cols] += dk2[:, HEAD:]
                dq2 = _dot(ds, k2, 1, 0)
                fn_ref[rows, cols] += dq2[:, :HEAD]
                fr_ref[rows, cols] += dq2[:, HEAD:]

        @pl.when(qi == kj)
        def _():
            step(True)

        @pl.when(qi > kj)
        def _():
            step(False)

        @pl.when(qi == nq - 1)
        def _():
            for hh in range(hp):
                cols = slice(hh * HEAD, (hh + 1) * HEAD)
                dkv_ref[:, 2 * hh * HEAD:(2 * hh + 1) * HEAD] = akn_ref[:, cols].astype(dkv_ref.dtype)
                dkv_ref[:, (2 * hh + 1) * HEAD:(2 * hh + 2) * HEAD] = av_ref[:, cols].astype(dkv_ref.dtype)
            dkr_ref[...] = akr_ref[...]

        @pl.when(st == last)
        def _():
            dqn_ref[...] = fn_ref[...].astype(dqn_ref.dtype)
            dqr_ref[...] = fr_ref[...].astype(dqr_ref.dtype)

    qspec = pl.BlockSpec((t, wide), lambda h, st, kt, qt: (qt[st], h))
    kvspec = pl.BlockSpec((t, 2 * wide), lambda h, st, kt, qt: (kt[st], h))
    krspec = pl.BlockSpec((t, HEAD), lambda h, st, kt, qt: (kt[st], 0))
    headspec = pl.BlockSpec((s, wide), lambda h, st, kt, qt: (0, h))
    grid_spec = pltpu.PrefetchScalarGridSpec(
        num_scalar_prefetch=2, grid=(groups, len(pairs)),
        in_specs=[qspec, qspec, kvspec, krspec, qspec, qspec, qspec],
        out_specs=[headspec, headspec, kvspec, pl.BlockSpec((t, wide), lambda h, st, kt, qt: (kt[st], h))],
        scratch_shapes=[pltpu.VMEM((s, wide), F32), pltpu.VMEM((s, wide), F32), pltpu.VMEM((t, wide), F32),
                        pltpu.VMEM((t, wide), F32), pltpu.VMEM((t, wide), F32)])
    return pl.pallas_call(
        body, name=name, grid_spec=grid_spec,
        out_shape=[jax.ShapeDtypeStruct((s, heads * HEAD), BF16), jax.ShapeDtypeStruct((s, heads * HEAD), BF16),
                   jax.ShapeDtypeStruct((s, heads * 2 * HEAD), BF16), jax.ShapeDtypeStruct((s, heads * HEAD), F32)],
        compiler_params=_params(("parallel", "arbitrary")),
    )(ktab, qtab, qall, qr, kv, kr, o, do, lse)


def _split_dot(val, tri, cb):
    hi = val.astype(BF16)
    lo = (val - hi.astype(F32)).astype(BF16)
    return _dot(hi, tri, 1, cb) + _dot(lo, tri, 1, cb)


def _sb_logs(q, k, offset, scale, masked):
    z = _dot(q, k, 1, 1) * (scale * LOG2E)
    sp = jnp.log2(1.0 + jnp.exp2(-jnp.abs(z)))
    ls = jnp.minimum(z, 0.0) - sp
    lk = ls - z
    if not masked:
        return None, ls, lk
    strict = (lax.broadcasted_iota(jnp.int32, z.shape, 1) + offset) < lax.broadcasted_iota(jnp.int32, z.shape, 0)
    return strict, ls, jnp.where(strict, lk, 0.0)


def _lane_pick(blk, idx):
    lane = lax.broadcasted_iota(jnp.int32, blk.shape, 1)
    return jnp.sum(jnp.where(lane == idx, blk, 0.0), axis=-1, keepdims=True)


def _lane_put(blk, idx, col):
    lane = lax.broadcasted_iota(jnp.int32, blk.shape, 1)
    return jnp.where(lane == idx, col, blk)


def _sb_tiles(s):
    tq = min(SB_TQ, s)
    tk = min(SB_TK, tq)
    assert s // tk <= HEAD
    return tq, tk


def sb_fwd(qkv, tri, heads, *, name):
    s = qkv.shape[0]
    tq, tk = _sb_tiles(s)
    nq = s // tq
    nsub = min(SB_SUBS_PER_STEP, s // tk)
    span = nsub * tk
    scale = 1.0 / math.sqrt(HEAD)
    pairs = [(qi, sb) for qi in range(nq) for sb in range(((qi + 1) * tq - 1) // span, -1, -1)]
    qtab, ktab = _tables(pairs)
    hp, groups, wide = _head_groups(heads, SB_HEADS_PER_STEP)

    def body(qt_ref, kt_ref, q_ref, k_ref, v_ref, tri_ref, o_ref, c_ref, carry_ref, acc_ref):
        st = pl.program_id(1)
        qi, sb = qt_ref[st], kt_ref[st]

        @pl.when(sb == ((qi + 1) * tq - 1) // span)
        def _():
            carry_ref[...] = jnp.zeros_like(carry_ref)
            acc_ref[...] = jnp.zeros_like(acc_ref)
            c_ref[...] = jnp.full_like(c_ref, SB_UNVISITED)

        def sub_block(i, _):
            sub = nsub - 1 - i
            kj = sb * nsub + sub
            keys = pl.ds(pl.multiple_of(sub * tk, tk), tk)
            alive = (kj * tk < (qi + 1) * tq) & (jnp.max(carry_ref[...]) > SB_DEAD_LOG2)

            def step(masked):
                for hh in range(hp):
                    cols = slice(hh * HEAD, (hh + 1) * HEAD)
                    strict, ls, lk = _sb_logs(q_ref[:, cols], k_ref[keys, cols], kj * tk - qi * tq, scale, masked)
                    carry = carry_ref[hh]
                    a = jnp.exp2(ls + _split_dot(lk, tri_ref[...], 0) + carry)
                    if masked:
                        a = jnp.where(strict, a, 0.0)
                    acc_ref[:, cols] += _dot(a.astype(BF16), v_ref[keys, cols], 1, 0)
                    c_ref[:, cols] = _lane_put(c_ref[:, cols], kj, carry)
                    carry_ref[hh] = carry + jnp.sum(lk, axis=-1, keepdims=True)

            @pl.when(alive & ((kj + 1) * tk > qi * tq))
            def _():
                step(True)

            @pl.when(alive & ((kj + 1) * tk <= qi * tq))
            def _():
                step(False)

            return 0

        lax.fori_loop(0, nsub, sub_block, 0)

        @pl.when(sb == 0)
        def _():
            o_ref[...] = acc_ref[...].astype(o_ref.dtype)

    qspec = pl.BlockSpec((tq, wide), lambda h, st, qt, kt: (qt[st], h))
    grid_spec = pltpu.PrefetchScalarGridSpec(
        num_scalar_prefetch=2, grid=(groups, len(pairs)),
        in_specs=[qspec,
                  pl.BlockSpec((span, wide), lambda h, st, qt, kt: (kt[st], groups + h)),
                  pl.BlockSpec((span, wide), lambda h, st, qt, kt: (kt[st], 2 * groups + h)),
                  pl.BlockSpec((tk, tk), lambda h, st, qt, kt: (0, 0))],
        out_specs=[qspec, qspec],
        scratch_shapes=[pltpu.VMEM((hp, tq, 1), F32), pltpu.VMEM((tq, wide), F32)])
    return pl.pallas_call(
        body, name=name, grid_spec=grid_spec,
        out_shape=[jax.ShapeDtypeStruct((s, heads * HEAD), BF16), jax.ShapeDtypeStruct((s, heads * HEAD), F32)],
        compiler_params=_params(("parallel", "arbitrary")),
    )(qtab, ktab, qkv, qkv, qkv, tri)


def sb_bwd(qkv, do, cmat, tri, heads, *, name):
    s = qkv.shape[0]
    tq, tk = _sb_tiles(s)
    nq = s // tq
    nsub = min(SB_SUBS_PER_STEP, s // tk)
    span = nsub * tk
    scale = 1.0 / math.sqrt(HEAD)
    pairs = [(sb, qi) for sb in range(s // span) for qi in range(sb * span // tq, nq)]
    ktab, qtab = _tables(pairs)
    last = len(pairs) - 1
    hp, groups, wide = _head_groups(heads, SB_HEADS_PER_STEP)

    def body(kt_ref, qt_ref, q_ref, k_ref, v_ref, do_ref, c_ref, tri_ref, dq_ref, dk_ref, dv_ref,
             dqf_ref, gsum_ref, ak_ref, av_ref):
        st = pl.program_id(1)
        sb, qi = kt_ref[st], qt_ref[st]

        @pl.when(st == 0)
        def _():
            dqf_ref[...] = jnp.zeros_like(dqf_ref)
            gsum_ref[...] = jnp.zeros_like(gsum_ref)

        @pl.when(qi == sb * span // tq)
        def _():
            ak_ref[...] = jnp.zeros_like(ak_ref)
            av_ref[...] = jnp.zeros_like(av_ref)

        def sub_block(sub, _):
            kj = sb * nsub + sub
            keys = pl.ds(pl.multiple_of(sub * tk, tk), tk)

            def step(masked):
                rows = pl.ds(pl.multiple_of(qi * tq, tq), tq)
                tri_v = tri_ref[...]
                for hh in range(hp):
                    cols = slice(hh * HEAD, (hh + 1) * HEAD)
                    qv, kblk, dov = q_ref[:, cols], k_ref[keys, cols], do_ref[:, cols]
                    strict, ls, lk = _sb_logs(qv, kblk, kj * tk - qi * tq, scale, masked)
                    a = jnp.exp2(ls + _split_dot(lk, tri_v, 0) + _lane_pick(c_ref[:, cols], kj))
                    if masked:
                        a = jnp.where(strict, a, 0.0)
                    g = _dot(dov, v_ref[keys, cols], 1, 1) * a
                    before_all = gsum_ref[hh, rows, :]
                    before = _split_dot(g, tri_v, 1) + before_all[:, :1]
                    beta = jnp.exp2(ls)
                    dz = g * (1.0 - beta) - before * beta
                    if masked:
                        dz = jnp.where(strict, dz, 0.0)
                    dzb = dz.astype(BF16)
                    av_ref[keys, cols] += _dot(a.astype(BF16), dov, 0, 0)
                    ak_ref[keys, cols] += _dot(dzb, qv, 0, 0)
                    dqf_ref[rows, cols] += _dot(dzb, kblk, 1, 0)
                    gsum_ref[hh, rows, :] = before_all + jnp.sum(g, axis=-1, keepdims=True)

            lane = lax.broadcasted_iota(jnp.int32, (tq, wide), 1)
            seen = jnp.max(jnp.where((lane & (HEAD - 1)) == kj, c_ref[...], SB_UNVISITED)) > SB_DEAD_LOG2
            alive = (kj * tk < (qi + 1) * tq) & seen

            @pl.when(alive & ((kj + 1) * tk > qi * tq))
            def _():
                step(True)

            @pl.when(alive & ((kj + 1) * tk <= qi * tq))
            def _():
                step(False)

            return 0

        lax.fori_loop(0, nsub, sub_block, 0)

        @pl.when(qi == nq - 1)
        def _():
            dk_ref[...] = (ak_ref[...] * scale).astype(dk_ref.dtype)
            dv_ref[...] = av_ref[...].astype(dv_ref.dtype)

        @pl.when(st == last)
        def _():
            dq_ref[...] = (dqf_ref[...] * scale).astype(dq_ref.dtype)

    qspec = pl.BlockSpec((tq, wide), lambda h, st, kt, qt: (qt[st], h))
    ospec = pl.BlockSpec((span, wide), lambda h, st, kt, qt: (kt[st], h))
    grid_spec = pltpu.PrefetchScalarGridSpec(
        num_scalar_prefetch=2, grid=(groups, len(pairs)),
        in_specs=[qspec,
                  pl.BlockSpec((span, wide), lambda h, st, kt, qt: (kt[st], groups + h)),
                  pl.BlockSpec((span, wide), lambda h, st, kt, qt: (kt[st], 2 * groups + h)),
                  qspec, qspec, pl.BlockSpec((tk, tk), lambda h, st, kt, qt: (0, 0))],
        out_specs=[pl.BlockSpec((s, wide), lambda h, st, kt, qt: (0, h)), ospec, ospec],
        scratch_shapes=[pltpu.VMEM((s, wide), F32), pltpu.VMEM((hp, s, HEAD), F32), pltpu.VMEM((span, wide), F32),
                        pltpu.VMEM((span, wide), F32)])
    return pl.pallas_call(
        body, name=name, grid_spec=grid_spec,
        out_shape=[jax.ShapeDtypeStruct((s, heads * HEAD), BF16)] * 3,
        compiler_params=_params(("parallel", "arbitrary")),
    )(ktab, qtab, qkv, qkv, qkv, do, cmat, tri)


def _mem_probs(q, k, scale):
    sc = _dot(q, k, 1, 1) * scale
    e = jnp.exp(sc - jnp.max(sc, axis=-1, keepdims=True))
    return e / jnp.sum(e, axis=-1, keepdims=True)


def mem_fwd(q, kvm, heads, *, name):
    s, nm = q.shape[0], kvm.shape[0]
    tq = min(MEM_TQ, s)
    scale = 1.0 / math.sqrt(HEAD)

    def body(q_ref, k_ref, v_ref, o_ref):
        p = _mem_probs(q_ref[...], k_ref[...], scale)
        o_ref[...] = _dot(p.astype(BF16), v_ref[...], 1, 0).astype(o_ref.dtype)

    qspec = pl.BlockSpec((tq, HEAD), lambda h, qi: (qi, h))
    return pl.pallas_call(
        body, name=name, grid=(heads, s // tq),
        in_specs=[qspec, pl.BlockSpec((nm, HEAD), lambda h, qi: (0, h)),
                  pl.BlockSpec((nm, HEAD), lambda h, qi: (0, heads + h))],
        out_specs=qspec, out_shape=jax.ShapeDtypeStruct((s, heads * HEAD), BF16),
        compiler_params=_params(("parallel", "parallel")),
    )(q, kvm, kvm)


def mem_bwd(q, kvm, do, heads, *, name):
    s, nm = q.shape[0], kvm.shape[0]
    tq = min(MEM_TQ, s)
    scale = 1.0 / math.sqrt(HEAD)

    def body(q_ref, k_ref, v_ref, do_ref, dq_ref, dk_ref, dv_ref):
        @pl.when(pl.program_id(1) == 0)
        def _():
            dk_ref[...] = jnp.zeros_like(dk_ref)
            dv_ref[...] = jnp.zeros_like(dv_ref)

        qv, kvv, dov = q_ref[...], k_ref[...], do_ref[...]
        p = _mem_probs(qv, kvv, scale)
        dp = _dot(dov, v_ref[...], 1, 1)
        ds = (p * (dp - jnp.sum(dp * p, axis=-1, keepdims=True)) * scale).astype(BF16)
        dq_ref[...] = _dot(ds, kvv, 1, 0).astype(dq_ref.dtype)
        dk_ref[...] += _dot(ds, qv, 0, 0)
        dv_ref[...] += _dot(p.astype(BF16), dov, 0, 0)

    qspec = pl.BlockSpec((tq, HEAD), lambda h, qi: (qi, h))
    kspec = pl.BlockSpec((nm, HEAD), lambda h, qi: (0, h))
    return pl.pallas_call(
        body, name=name, grid=(heads, s // tq),
        in_specs=[qspec, kspec, pl.BlockSpec((nm, HEAD), lambda h, qi: (0, heads + h)), qspec],
        out_specs=[qspec, kspec, kspec],
        out_shape=[jax.ShapeDtypeStruct((s, heads * HEAD), BF16), jax.ShapeDtypeStruct((nm, heads * HEAD), F32),
                   jax.ShapeDtypeStruct((nm, heads * HEAD), F32)],
        compiler_params=_params(("parallel", "arbitrary")),
    )(q, kvm, kvm, do)


def _conv3(u, halo, w, b):
    tm = u.shape[0]
    row = lax.broadcasted_iota(jnp.int32, u.shape, 0)
    h1, h2 = halo[HALO - 1:HALO, :], halo[HALO - 2:HALO - 1, :]
    u1 = jnp.where(row == 0, h1, pltpu.roll(u, 1, 0))
    u2 = jnp.where(row == 0, h2, jnp.where(row == 1, h1, pltpu.roll(u, 2 % tm, 0)))
    return b + w[0:1, :] * u2 + w[1:2, :] * u1 + w[2:3, :] * u, u1, u2


def _conv_specs(s, f):
    tm, tn = min(CONV_TM, s), _pick(f, CONV_TN)
    return tm, tn, s // tm, f // tn


def _silu_parts(g):
    sg = 1.0 / (1.0 + jnp.exp(-g))
    return g * sg, sg


def conv_gate_fwd(u, cw, cb, *, name):
    s, f = u.shape[0], u.shape[1] // 2
    tm, tn, ni, nj = _conv_specs(s, f)
    hb = tm // HALO

    def body(ug_ref, uu_ref, hg_ref, hu_ref, wg_ref, wu_ref, bg_ref, bu_ref, a_ref):
        keep = (pl.program_id(1) > 0).astype(F32)
        gate, _, _ = _conv3(ug_ref[...].astype(F32), hg_ref[...].astype(F32) * keep, wg_ref[...], bg_ref[...])
        up, _, _ = _conv3(uu_ref[...].astype(F32), hu_ref[...].astype(F32) * keep, wu_ref[...], bu_ref[...])
        a_ref[...] = (_silu_parts(gate)[0] * up).astype(a_ref.dtype)

    def main(off):
        return pl.BlockSpec((tm, tn), lambda j, i: (i, j + off))

    def halo(off):
        return pl.BlockSpec((HALO, tn), lambda j, i: (jnp.maximum(i * hb - 1, 0), j + off))

    def par(rows, off):
        return pl.BlockSpec((rows, tn), lambda j, i: (0, j + off))

    return pl.pallas_call(
        body, name=name, grid=(nj, ni),
        in_specs=[main(0), main(nj), halo(0), halo(nj), par(3, 0), par(3, nj), par(1, 0), par(1, nj)],
        out_specs=main(0), out_shape=jax.ShapeDtypeStruct((s, f), BF16),
        compiler_params=_params(("parallel", "parallel")),
    )(u, u, u, u, cw, cw, cb, cb)


def conv_gate_bwd(u, da, cw, cb, *, name):
    s, f = u.shape[0], u.shape[1] // 2
    tm, tn, ni, nj = _conv_specs(s, f)
    hb = tm // HALO

    def body(ug_ref, uu_ref, hg_ref, hu_ref, da_ref, wg_ref, wu_ref, bg_ref, bu_ref, dg_ref, du_ref, pg_ref, pu_ref):
        @pl.when(pl.program_id(1) == 0)
        def _():
            pg_ref[...] = jnp.zeros_like(pg_ref)
            pu_ref[...] = jnp.zeros_like(pu_ref)

        keep = (pl.program_id(1) > 0).astype(F32)
        ug, uu = ug_ref[...].astype(F32), uu_ref[...].astype(F32)
        gate, ug1, ug2 = _conv3(ug, hg_ref[...].astype(F32) * keep, wg_ref[...], bg_ref[...])
        up, uu1, uu2 = _conv3(uu, hu_ref[...].astype(F32) * keep, wu_ref[...], bu_ref[...])
        act, sg = _silu_parts(gate)
        dav = da_ref[...].astype(F32)
        d_gate = dav * up * (sg * (1.0 + gate * (1.0 - sg)))
        d_up = dav * act
        dg_ref[...] = d_gate.astype(dg_ref.dtype)
        du_ref[...] = d_up.astype(du_ref.dtype)
        for p_ref, dc, taps in ((pg_ref, d_gate, (ug2, ug1, ug)), (pu_ref, d_up, (uu2, uu1, uu))):
            for r, tap in enumerate(taps):
                p_ref[r:r + 1, :] += jnp.sum(dc * tap, axis=0, keepdims=True)
            p_ref[3:4, :] += jnp.sum(dc, axis=0, keepdims=True)

    def main(off):
        return pl.BlockSpec((tm, tn), lambda j, i: (i, j + off))

    def halo(off):
        return pl.BlockSpec((HALO, tn), lambda j, i: (jnp.maximum(i * hb - 1, 0), j + off))

    def par(rows, off):
        return pl.BlockSpec((rows, tn), lambda j, i: (0, j + off))

    dg, du, pg, pu = pl.pallas_call(
        body, name=name, grid=(nj, ni),
        in_specs=[main(0), main(nj), halo(0), halo(nj), main(0), par(3, 0), par(3, nj), par(1, 0), par(1, nj)],
        out_specs=[main(0), main(0), par(8, 0), par(8, 0)],
        out_shape=[jax.ShapeDtypeStruct((s, f), BF16), jax.ShapeDtypeStruct((s, f), BF16),
                   jax.ShapeDtypeStruct((8, f), F32), jax.ShapeDtypeStruct((8, f), F32)],
        compiler_params=_params(("parallel", "arbitrary")),
    )(u, u, u, u, da, cw, cw, cb, cb)
    return dg, du, jnp.concatenate([pg, pu], axis=1)


def conv_transpose(dc, w, *, name):
    s, f = dc.shape
    tm, tn, ni, nj = _conv_specs(s, f)
    hb = tm // HALO

    def body(d_ref, h_ref, w_ref, o_ref):
        d = d_ref[...].astype(F32)
        halo = h_ref[...].astype(F32) * (pl.program_id(1) < ni - 1).astype(F32)
        row = lax.broadcasted_iota(jnp.int32, d.shape, 0)
        n0, n1 = halo[0:1, :], halo[1:2, :]
        d1 = jnp.where(row == tm - 1, n0, pltpu.roll(d, tm - 1, 0))
        d2 = jnp.where(row == tm - 2, n0, jnp.where(row == tm - 1, n1, pltpu.roll(d, tm - 2, 0)))
        wv = w_ref[...]
        o_ref[...] = (wv[2:3, :] * d + wv[1:2, :] * d1 + wv[0:1, :] * d2).astype(o_ref.dtype)

    main = pl.BlockSpec((tm, tn), lambda j, i: (i, j))
    return pl.pallas_call(
        body, name=name, grid=(nj, ni),
        in_specs=[main, pl.BlockSpec((HALO, tn), lambda j, i: (jnp.minimum((i + 1) * hb, s // HALO - 1), j)),
                  pl.BlockSpec((3, tn), lambda j, i: (0, j))],
        out_specs=main, out_shape=jax.ShapeDtypeStruct((s, f), BF16),
        compiler_params=_params(("parallel", "parallel")),
    )(dc, dc, w)


def _tile2d(r, c):
    return _pick(r, ROW_TILE), _pick(c, (2048, 1024, 512, 256, 128))


def sum_slots(buf, *, name):
    n, shape = buf.shape[0], buf.shape[1:]
    r, c = math.prod(shape[:-1]), shape[-1]
    tm, tn = _tile2d(r, c)

    def body(b_ref, o_ref):
        acc = b_ref[0]
        for k in range(1, n):
            acc = acc + b_ref[k]
        o_ref[...] = acc

    out = pl.pallas_call(
        body, name=name, grid=(r // tm, c // tn),
        in_specs=[pl.BlockSpec((n, tm, tn), lambda i, j: (0, i, j))],
        out_specs=pl.BlockSpec((tm, tn), lambda i, j: (i, j)), out_shape=jax.ShapeDtypeStruct((r, c), F32),
        compiler_params=_params(("parallel", "parallel")),
    )(buf.reshape(n, r, c))
    return out.reshape(shape)


def add_half(layers, got, core, *, name):
    n_l = len(layers)
    a, _, r, c = layers[0].shape
    tm, tn = _tile2d(r, c)

    def body(core_ref, *refs):
        got_ref, o_ref = refs[n_l:]
        for l in range(n_l):
            @pl.when(pl.program_id(0) == l)
            def _(l=l):
                o_ref[...] = (refs[l][...] + got_ref[...]).astype(o_ref.dtype)

    def layer_spec(l):
        def index(b, q, i, j, core_ref):
            on = b == l
            return (jnp.where(on, q, 0), core_ref[0], jnp.where(on, i, 0), jnp.where(on, j, 0))
        return pl.BlockSpec((None, None, tm, tn), index)

    part = pl.BlockSpec((None, None, tm, tn), lambda b, q, i, j, core_ref: (b, q, i, j))
    grid_spec = pltpu.PrefetchScalarGridSpec(
        num_scalar_prefetch=1, grid=(n_l, a, r // tm, c // tn),
        in_specs=[layer_spec(l) for l in range(n_l)] + [part], out_specs=part)
    return pl.pallas_call(
        body, name=name, grid_spec=grid_spec, out_shape=jax.ShapeDtypeStruct((n_l, a, r, c), BF16),
        compiler_params=_params(("arbitrary", "parallel", "parallel", "parallel")),
    )(core, *layers, got)


def add_own_block(h, got, chip, axis, *, name):
    _, l, r, c = got.shape
    tm, tn = _tile2d(r, c)
    ncb = c // tn

    def body(chip_ref, h_ref, got_ref, o_ref):
        o_ref[...] = ((h_ref[...].astype(F32) + got_ref[0].astype(F32)) + got_ref[1].astype(F32)) + got_ref[2].astype(F32)

    if axis == 0:
        h_spec = pl.BlockSpec((None, None, tm, tn), lambda b, i, j, chip_ref: (b, chip_ref[0], i, j))
    else:
        h_spec = pl.BlockSpec((None, tm, tn), lambda b, i, j, chip_ref: (b, i, chip_ref[0] * ncb + j))
    grid_spec = pltpu.PrefetchScalarGridSpec(
        num_scalar_prefetch=1, grid=(l, r // tm, ncb),
        in_specs=[h_spec, pl.BlockSpec((3, None, tm, tn), lambda b, i, j, chip_ref: (0, b, i, j))],
        out_specs=pl.BlockSpec((None, tm, tn), lambda b, i, j, chip_ref: (b, i, j)))
    return pl.pallas_call(
        body, name=name, grid_spec=grid_spec, out_shape=jax.ShapeDtypeStruct((l, r, c), F32),
        compiler_params=_params(("parallel", "parallel", "parallel")),
    )(chip, h, got)


def _adam_update(w, g, m, v):
    bc1, bc2 = 1.0 - ADAM_B1 ** ADAM_STEP, 1.0 - ADAM_B2 ** ADAM_STEP
    mn = ADAM_B1 * m + (1.0 - ADAM_B1) * g
    vn = ADAM_B2 * v + (1.0 - ADAM_B2) * (g * g)
    return -ADAM_LR * ((mn / bc1) / (jnp.sqrt(vn / bc2) + ADAM_EPS) + ADAM_WD * w), mn, vn


def adamw(w, g, m, v, *, name):
    shape = w.shape
    c = shape[-1]
    r = math.prod(shape[:-1]) if len(shape) > 1 else 1
    tm, tn = _tile2d(r, c)

    def body(w_ref, g_ref, m_ref, v_ref, d_ref, mo_ref, vo_ref):
        d_ref[...], mo_ref[...], vo_ref[...] = _adam_update(w_ref[...], g_ref[...], m_ref[...], v_ref[...])

    spec = pl.BlockSpec((tm, tn), lambda i, j: (i, j))
    outs = pl.pallas_call(
        body, name=name, grid=(r // tm, c // tn), in_specs=[spec] * 4, out_specs=[spec] * 3,
        out_shape=[jax.ShapeDtypeStruct((r, c), F32)] * 3, compiler_params=_params(("parallel", "parallel")),
    )(*(t.reshape(r, c) for t in (w, g, m, v)))
    return tuple(o.reshape(shape) for o in outs)


def adamw_halves(w, mine, got, m, v, core, *, name):
    l, r, c = w.shape
    rh = r // 2
    tm, tn = _tile2d(rh, c)

    def body(core_ref, w_ref, mine_ref, got_ref, m_ref, v_ref, g_ref, d_ref, mo_ref, vo_ref):
        g = jnp.where(pl.program_id(1) == core_ref[0], mine_ref[...], got_ref[...])
        g_ref[...] = g
        d_ref[...], mo_ref[...], vo_ref[...] = _adam_update(w_ref[...], g, m_ref[...], v_ref[...])

    full = pl.BlockSpec((None, None, tm, tn), lambda b, h, i, j, core_ref: (b, h, i, j))
    half = pl.BlockSpec((None, tm, tn), lambda b, h, i, j, core_ref: (b, i, j))
    grid_spec = pltpu.PrefetchScalarGridSpec(
        num_scalar_prefetch=1, grid=(l, 2, rh // tm, c // tn),
        in_specs=[full, half, half, full, full], out_specs=[full] * 4)
    outs = pl.pallas_call(
        body, name=name, grid_spec=grid_spec, out_shape=[jax.ShapeDtypeStruct((l, 2, rh, c), F32)] * 4,
        compiler_params=_params(("parallel", "parallel", "parallel", "parallel")),
    )(core, w.reshape(l, 2, rh, c), mine, got, m.reshape(l, 2, rh, c), v.reshape(l, 2, rh, c))
    return tuple(o.reshape(l, r, c) for o in outs)


def _comm(name, inputs, out_shapes, aliases, plan):
    n_in, n_out = len(inputs), len(out_shapes)
    probe = plan([None] * n_in, [None] * n_out, 0, 0, 0, count_only=True)

    def body(*refs):
        in_refs, out_refs = refs[:n_in], refs[n_in:n_in + n_out]
        send_sems, recv_sems = refs[n_in + n_out:]
        x, y, c = lax.axis_index("x"), lax.axis_index("y"), lax.axis_index("c")
        xfers = plan(in_refs, out_refs, x, y, c, count_only=False)
        started = []
        for n, (src, dst, peer, _) in enumerate(xfers):
            if peer is None:
                cp = pltpu.make_async_copy(src, dst, send_sems.at[n])
            else:
                cp = pltpu.make_async_remote_copy(src_ref=src, dst_ref=dst, send_sem=send_sems.at[n],
                                                  recv_sem=recv_sems.at[n], device_id=peer, device_id_type=MESH)
            cp.start()
            started.append(cp)
        for n, (src, dst, peer, got) in enumerate(xfers):
            if peer is None:
                started[n].wait()
            else:
                pltpu.make_async_remote_copy(src_ref=got, dst_ref=got, send_sem=send_sems.at[n],
                                             recv_sem=recv_sems.at[n], device_id=peer,
                                             device_id_type=MESH).wait_recv()
                started[n].wait_send()

    hbm = pl.BlockSpec(memory_space=pl.ANY)
    return pl.pallas_call(
        body, name=name, in_specs=[hbm] * n_in, out_specs=[hbm] * n_out, out_shape=out_shapes,
        input_output_aliases=aliases,
        scratch_shapes=[pltpu.SemaphoreType.DMA((probe,)), pltpu.SemaphoreType.DMA((probe,))],
    )(*inputs)


def _other_chips(x, y):
    return [(1 - x, y), (x, 1 - y), (1 - x, 1 - y)]


def gather_weights(shards, axes):
    def out_shape(sh, ax):
        l, r, c = sh.shape
        return jax.ShapeDtypeStruct((l, N_CHIPS, 2, r // 2, c) if ax == 0 else (l, 2, r // 2, N_CHIPS * c), BF16)

    def piece(ref, ax, chip, half, width):
        if ax == 0:
            return ref.at[:, chip, half]
        return ref.at[:, half, :, pl.ds(chip * width, width)]

    def plan1(in_refs, out_refs, x, y, c, count_only):
        if count_only:
            return 4 * len(shards)
        me, xfers = 2 * x + y, []
        for sh, ax, src, out in zip(shards, axes, in_refs, out_refs):
            rows, width = sh.shape[1] // 2, sh.shape[2]
            mine = src.at[:, pl.ds(c * rows, rows), :]
            xfers.append((mine, piece(out, ax, me, c, width), None, None))
            for cx, cy in _other_chips(x, y):
                xfers.append((mine, piece(out, ax, me, c, width), (cx, cy, c), piece(out, ax, 2 * cx + cy, c, width)))
        return xfers

    def half_of(ref, ax, half):
        return ref.at[:, :, half] if ax == 0 else ref.at[:, half]

    def plan2(in_refs, out_refs, x, y, c, count_only):
        if count_only:
            return len(shards)
        return [(half_of(out, ax, c), half_of(out, ax, c), (x, y, 1 - c), half_of(out, ax, 1 - c))
                for ax, out in zip(axes, out_refs)]

    shapes = [out_shape(sh, ax) for sh, ax in zip(shards, axes)]
    part = _comm("gather_chips", list(shards), shapes, {}, plan1)
    full = _comm("gather_cores", list(part), shapes, {n: n for n in range(len(shards))}, plan2)
    return [f.reshape(sh.shape[0], N_CHIPS * sh.shape[1], sh.shape[2]) if ax == 0
            else f.reshape(sh.shape[0], sh.shape[1], N_CHIPS * sh.shape[2])
            for f, sh, ax in zip(full, shards, axes)]


def reduce_scatter_grads(grads, axes, core, chip):
    dims, views = [], []
    for layers, ax in zip(grads, axes):
        r, c = layers[0].shape
        if ax == 0:
            dims.append((len(layers), r // N_CHIPS // 2, c))
            views.append([g.reshape(N_CHIPS, 2, r // N_CHIPS // 2, c) for g in layers])
        else:
            dims.append((len(layers), r // 2, c // N_CHIPS))
            views.append([g.reshape(1, 2, r // 2, c) for g in layers])

    def half_shape(ax, d):
        return (d[0], N_CHIPS, d[1], d[2]) if ax == 0 else (d[0], d[1], N_CHIPS * d[2])

    def plan_a(in_refs, out_refs, x, y, c, count_only):
        if count_only:
            return sum(d[0] for d in dims)
        xfers, n = [], 0
        for d, out in zip(dims, out_refs):
            for l in range(d[0]):
                xfers.append((in_refs[n].at[:, 1 - c], out.at[l], (x, y, 1 - c), out.at[l]))
                n += 1
        return xfers

    got = _comm("reduce_cores", [v for layers in views for v in layers],
                [jax.ShapeDtypeStruct((d[0], N_CHIPS if ax == 0 else 1, d[1], d[2] if ax == 0 else N_CHIPS * d[2]), F32)
                 for ax, d in zip(axes, dims)], {}, plan_a)
    chip_sum = [add_half(layers, g, core, name="sum_cores").reshape(half_shape(ax, d))
                for ax, d, layers, g in zip(axes, dims, views, got)]

    def block(ref, ax, which, width):
        return ref.at[:, which] if ax == 0 else ref.at[:, :, pl.ds(which * width, width)]

    def plan_b(in_refs, out_refs, x, y, c, count_only):
        if count_only:
            return 3 * len(grads)
        xfers = []
        for ax, d, src, out in zip(axes, dims, in_refs, out_refs):
            for k, (cx, cy) in enumerate(_other_chips(x, y)):
                xfers.append((block(src, ax, 2 * cx + cy, d[2]), out.at[k], (cx, cy, c), out.at[k]))
        return xfers

    parts = _comm("reduce_chips", chip_sum, [jax.ShapeDtypeStruct((3,) + d, BF16) for d in dims], {}, plan_b)
    mine = [add_own_block(h, p, chip, ax, name="sum_chips") for h, p, ax in zip(chip_sum, parts, axes)]

    def plan_c(in_refs, out_refs, x, y, c, count_only):
        if count_only:
            return len(grads)
        return [(src, out, (x, y, 1 - c), out) for src, out in zip(in_refs, out_refs)]

    theirs = _comm("share_cores", mine, [jax.ShapeDtypeStruct(d, F32) for d in dims], {}, plan_c)
    return list(zip(mine, theirs))


def gather_slabs(vec, *, name):
    def plan(in_refs, out_refs, x, y, c, count_only):
        if count_only:
            return 8
        me = 4 * x + 2 * y + c
        xfers = [(in_refs[0], out_refs[0].at[me], None, None)]
        for k in range(1, 8):
            px, py, pc = x ^ (k >> 2), y ^ ((k >> 1) & 1), c ^ (k & 1)
            xfers.append((in_refs[0], out_refs[0].at[me], (px, py, pc), out_refs[0].at[4 * px + 2 * py + pc]))
        return xfers

    return _comm(name, [vec], [jax.ShapeDtypeStruct((8,) + vec.shape, F32)], {}, plan)[0]


def allreduce_small(vec):
    return sum_slots(gather_slabs(vec, name="gather_small"), name="sum_small")


def gather_conv_w(block):
    l, taps, c = block.shape
    flat = block.reshape(-1)
    slabs = gather_slabs(jnp.pad(flat, (0, -flat.size % 1024)).reshape(-1, 128), name="gather_conv_w")
    per_chip = slabs[0::2].reshape(N_CHIPS, -1)[:, :flat.size].reshape(N_CHIPS, l, taps, c)
    return per_chip.transpose(1, 2, 0, 3).reshape(l, taps, N_CHIPS * c)


def _rope_tables(positions):
    inv_freq = ROPE_THETA ** (-jnp.arange(0, ROPE_DIM, 2, dtype=F32) / ROPE_DIM)
    ang = positions.astype(F32)[:, None] * inv_freq
    ang = jnp.concatenate([ang, ang], axis=-1)
    pad = ((0, 0), (0, HEAD - ROPE_DIM))
    return jnp.pad(jnp.cos(ang), pad), jnp.pad(jnp.sin(ang), pad)


def _uq_layout(w, heads):
    ql = w.shape[0]
    w = w.reshape(ql, heads, HEAD + ROPE_DIM)
    rot = jnp.pad(w[:, :, HEAD:], ((0, 0), (0, 0), (0, HEAD - ROPE_DIM)))
    return jnp.concatenate([w[:, :, :HEAD].reshape(ql, heads * HEAD), rot.reshape(ql, heads * HEAD)], axis=1)


def _uq_layout_inv(dw, heads):
    ql = dw.shape[0]
    nope = dw[:, :heads * HEAD].reshape(ql, heads, HEAD)
    rot = dw[:, heads * HEAD:].reshape(ql, heads, HEAD)[:, :, :ROPE_DIM]
    return jnp.concatenate([nope, rot], axis=-1).reshape(ql, heads * (HEAD + ROPE_DIM))


def kernel(x, mem, positions, norm_mix, norm_mem_q, norm_mem_kv, norm_ffn, norm_final, mla_w_down, mla_q_norm, mla_w_uq, mla_kv_norm, mla_w_ukv, mla_w_o, sb_w_qkv, sb_w_o, mem_w_q, mem_w_kv, mem_w_o, ffn_w_in, ffn_conv_w, ffn_conv_b, ffn_w_out, loss_target, m_norm_mix, m_norm_mem_q, m_norm_mem_kv, m_norm_ffn, m_norm_final, m_mla_w_down, m_mla_q_norm, m_mla_w_uq, m_mla_kv_norm, m_mla_w_ukv, m_mla_w_o, m_sb_w_qkv, m_sb_w_o, m_mem_w_q, m_mem_w_kv, m_mem_w_o, m_ffn_w_in, m_ffn_conv_w, m_ffn_conv_b, m_ffn_w_out, v_norm_mix, v_norm_mem_q, v_norm_mem_kv, v_norm_ffn, v_norm_final, v_mla_w_down, v_mla_q_norm, v_mla_w_uq, v_mla_kv_norm, v_mla_w_ukv, v_mla_w_o, v_sb_w_qkv, v_sb_w_o, v_mem_w_q, v_mem_w_kv, v_mem_w_o, v_ffn_w_in, v_ffn_conv_w, v_ffn_conv_b, v_ffn_w_out):
    args = dict(locals())
    big = ["mla_w_down", "mla_w_uq", "mla_w_ukv", "mla_w_o", "sb_w_qkv", "sb_w_o", "mem_w_q", "mem_w_kv",
           "mem_w_o", "ffn_w_in", "ffn_w_out"]
    col_cut = {"mla_w_uq", "mla_w_ukv", "sb_w_qkv", "mem_w_o", "ffn_w_in"}
    axes = [1 if n in col_cut else 0 for n in big]
    small = ["norm_mix", "norm_mem_q", "norm_mem_kv", "norm_ffn", "norm_final", "mla_q_norm", "mla_kv_norm",
             "ffn_conv_b", "ffn_conv_w"]
    order = ["norm_mix", "norm_mem_q", "norm_mem_kv", "norm_ffn", "norm_final", "mla_w_down", "mla_q_norm",
             "mla_w_uq", "mla_kv_norm", "mla_w_ukv", "mla_w_o", "sb_w_qkv", "sb_w_o", "mem_w_q", "mem_w_kv",
             "mem_w_o", "ffn_w_in", "ffn_conv_w", "ffn_conv_b", "ffn_w_out"]

    xs, mems, target = x[0], mem[0], loss_target[0]
    s, d = xs.shape
    depth = norm_mix.shape[0]
    ql, kvl = mla_q_norm.shape[1], mla_kv_norm.shape[1]
    mla_heads = N_CHIPS * mla_w_uq.shape[2] // (HEAD + ROPE_DIM)
    sb_heads = N_CHIPS * sb_w_qkv.shape[2] // (3 * HEAD)
    mem_heads = mem_w_q.shape[2] // HEAD
    ff = N_CHIPS * ffn_w_out.shape[1]
    chip = 2 * lax.axis_index("x") + lax.axis_index("y")
    chip_op = jnp.reshape(chip, (1,)).astype(jnp.int32)
    core_op = jnp.reshape(lax.axis_index("c"), (1,)).astype(jnp.int32)

    full = dict(zip(big, gather_weights([args[n].astype(BF16) for n in big], axes)))
    w_uq = [_uq_layout(full["mla_w_uq"][j], mla_heads) for j in range(full["mla_w_uq"].shape[0])]
    cos, sin = _rope_tables(positions[0])
    _, sb_tk = _sb_tiles(s)
    tri = (jnp.arange(sb_tk)[:, None] > jnp.arange(sb_tk)[None, :]).astype(BF16)
    conv_w = gather_conv_w(ffn_conv_w)

    saved = []
    xa = xs
    for i in range(depth):
        j = i // 2
        lay = {}
        lay["xa"] = xa
        h1 = rmsnorm_fwd(xa, norm_mix[i], name="norm_fwd")
        lay["h1"] = h1
        if i % 2 == 0:
            down = mm(h1, full["mla_w_down"][j], out_dtype=F32, name="mm_down")
            cq = rmsnorm_fwd(down[:, :ql], mla_q_norm[j], name="norm_lora_fwd")
            ckv = rmsnorm_fwd(down[:, ql:ql + kvl], mla_kv_norm[j], name="norm_lora_fwd")
            kr_raw = jnp.pad(down[:, ql + kvl:], ((0, 0), (0, HEAD - ROPE_DIM)))
            kr = rope(kr_raw, cos, sin, col0=0, n=1, inverse=False, name="rope_k")
            qall = mm(cq, w_uq[j], name="mm_uq")
            qr = rope(qall, cos, sin, col0=mla_heads, n=mla_heads, inverse=False, name="rope_q")
            kv = mm(ckv, full["mla_w_ukv"][j], name="mm_ukv")
            o, lse = mla_fwd(qall, qr, kv, kr, mla_heads, name="mla_fwd")
            xb = mm(o, full["mla_w_o"][j], add=xa, out_dtype=F32, name="mm_out_res")
            lay.update(down=down, cq=cq, ckv=ckv, qall=qall, qr=qr, kv=kv, kr=kr, o=o, lse=lse)
        else:
            qkv = mm(h1, full["sb_w_qkv"][j], name="mm_qkv")
            o, cmat = sb_fwd(qkv, tri, sb_heads, name="sb_fwd")
            xb = mm(o, full["sb_w_o"][j], add=xa, out_dtype=F32, name="mm_out_res")
            lay.update(qkv=qkv, o=o, cmat=cmat)
        h2 = rmsnorm_fwd(xb, norm_mem_q[i], name="norm_fwd")
        hm = rmsnorm_fwd(mems, norm_mem_kv[i], name="norm_mem_fwd")
        qm = mm(h2, full["mem_w_q"][i], name="mm_mem_q")
        kvm = mm(hm, full["mem_w_kv"][i], name="mm_mem_kv")
        om = mem_fwd(qm, kvm, mem_heads, name="mem_fwd")
        xc = mm(om, full["mem_w_o"][i], add=xb, out_dtype=F32, name="mm_mem_out_res")
        h3 = rmsnorm_fwd(xc, norm_ffn[i], name="norm_fwd")
        u = mm(h3, full["ffn_w_in"][i], name="mm_ffn_in")
        act = conv_gate_fwd(u, conv_w[i], ffn_conv_b[i][None, :], name="conv_gate_fwd")
        xd = mm(act, full["ffn_w_out"][i], add=xc, out_dtype=F32, name="mm_ffn_out_res")
        lay.update(xb=xb, h2=h2, hm=hm, qm=qm, kvm=kvm, om=om, xc=xc, h3=h3, u=u, act=act)
        saved.append(lay)
        xa = xd

    dx, dxb, g_final, loss_part = final_loss(xa, norm_final, target, name="final_loss")

    gw = {n: [None] * full[n].shape[0] for n in big}
    g_small = {"norm_mix": [None] * depth, "norm_mem_q": [None] * depth, "norm_mem_kv": [None] * depth,
               "norm_ffn": [None] * depth, "mla_q_norm": [None] * (depth - depth // 2),
               "mla_kv_norm": [None] * (depth - depth // 2), "conv": [None] * depth}
    for i in reversed(range(depth)):
        j = i // 2
        lay = saved[i]
        gw["ffn_w_out"][i] = mm(lay["act"], dxb, mode="tn", out_dtype=F32, name="mm_ffn_out_wgrad")
        da = mm(dxb, full["ffn_w_out"][i], mode="nt", name="mm_ffn_out_dgrad")
        dcg, dcu, g_small["conv"][i] = conv_gate_bwd(lay["u"], da, conv_w[i], ffn_conv_b[i][None, :], name="conv_gate_bwd")
        du = jnp.concatenate([conv_transpose(dcg, conv_w[i][:, :ff], name="conv_transpose"),
                              conv_transpose(dcu, conv_w[i][:, ff:], name="conv_transpose")], axis=1)
        gw["ffn_w_in"][i] = mm(lay["h3"], du, mode="tn", out_dtype=F32, name="mm_ffn_in_wgrad")
        dh3 = mm(du, full["ffn_w_in"][i], mode="nt", name="mm_ffn_in_dgrad")
        dx, dxb, g_small["norm_ffn"][i] = rmsnorm_bwd(dh3, lay["xc"], norm_ffn[i], dx, name="norm_bwd")
        gw["mem_w_o"][i] = mm(lay["om"], dxb, mode="tn", out_dtype=F32, name="mm_mem_out_wgrad")
        dom = mm(dxb, full["mem_w_o"][i], mode="nt", name="mm_mem_out_dgrad")
        dqm, dkm, dvm = mem_bwd(lay["qm"], lay["kvm"], dom, mem_heads, name="mem_bwd")
        dkvm = jnp.concatenate([dkm, dvm], axis=1)
        gw["mem_w_q"][i] = mm(lay["h2"], dqm, mode="tn", out_dtype=F32, name="mm_mem_q_wgrad")
        gw["mem_w_kv"][i] = mm(lay["hm"], dkvm, mode="tn", out_dtype=F32, name="mm_mem_kv_wgrad")
        dh2 = mm(dqm, full["mem_w_q"][i], mode="nt", name="mm_mem_q_dgrad")
        dhm = mm(dkvm, full["mem_w_kv"][i], mode="nt", name="mm_mem_kv_dgrad")
        _, _, g_small["norm_mem_kv"][i] = rmsnorm_bwd(dhm, mems, norm_mem_kv[i], name="norm_mem_bwd")
        dx, dxb, g_small["norm_mem_q"][i] = rmsnorm_bwd(dh2, lay["xb"], norm_mem_q[i], dx, name="norm_bwd")
        if i % 2 == 0:
            gw["mla_w_o"][j] = mm(lay["o"], dxb, mode="tn", out_dtype=F32, name="mm_out_wgrad")
            do = mm(dxb, full["mla_w_o"][j], mode="nt", name="mm_out_dgrad")
            dqn, dqr, dkv, dkr_heads = mla_bwd(lay["qall"], lay["qr"], lay["kv"], lay["kr"], lay["o"], do, lay["lse"],
                                               mla_heads, name="mla_bwd")
            dkr = sum_lane_tiles(dkr_heads, mla_heads, name="sum_heads")
            dqall = jnp.concatenate([dqn, rope(dqr, cos, sin, col0=0, n=mla_heads, inverse=True, name="rope_q_bwd")], axis=1)
            gw["mla_w_ukv"][j] = mm(lay["ckv"], dkv, mode="tn", out_dtype=F32, name="mm_ukv_wgrad")
            dckv = mm(dkv, full["mla_w_ukv"][j], mode="nt", name="mm_ukv_dgrad")
            gw["mla_w_uq"][j] = _uq_layout_inv(mm(lay["cq"], dqall, mode="tn", out_dtype=F32, name="mm_uq_wgrad"), mla_heads)
            dcq = mm(dqall, w_uq[j], mode="nt", name="mm_uq_dgrad")
            down = lay["down"]
            _, d_q, g_small["mla_q_norm"][j] = rmsnorm_bwd(dcq, down[:, :ql], mla_q_norm[j], name="norm_lora_bwd")
            _, d_kv, g_small["mla_kv_norm"][j] = rmsnorm_bwd(dckv, down[:, ql:ql + kvl], mla_kv_norm[j], name="norm_lora_bwd")
            d_kr = rope(dkr, cos, sin, col0=0, n=1, inverse=True, name="rope_k_bwd")[:, :ROPE_DIM]
            ddown = jnp.concatenate([d_q, d_kv, d_kr], axis=1)
            gw["mla_w_down"][j] = mm(lay["h1"], ddown, mode="tn", out_dtype=F32, name="mm_down_wgrad")
            dh1 = mm(ddown, full["mla_w_down"][j], mode="nt", name="mm_down_dgrad")
        else:
            gw["sb_w_o"][j] = mm(lay["o"], dxb, mode="tn", out_dtype=F32, name="mm_out_wgrad")
            do = mm(dxb, full["sb_w_o"][j], mode="nt", name="mm_out_dgrad")
            dq, dk, dv = sb_bwd(lay["qkv"], do, lay["cmat"], tri, sb_heads, name="sb_bwd")
            dqkv = jnp.concatenate([dq, dk, dv], axis=1)
            gw["sb_w_qkv"][j] = mm(lay["h1"], dqkv, mode="tn", out_dtype=F32, name="mm_qkv_wgrad")
            dh1 = mm(dqkv, full["sb_w_qkv"][j], mode="nt", name="mm_qkv_dgrad")
        dx, dxb, g_small["norm_mix"][i] = rmsnorm_bwd(dh1, lay["xa"], norm_mix[i], dx, name="norm_bwd")

    halves = dict(zip(big, reduce_scatter_grads([gw[n] for n in big], axes, core_op, chip_op)))
    conv = jnp.stack(g_small["conv"])
    parts = [jnp.concatenate(g_small[n], axis=0) for n in ("norm_mix", "norm_mem_q", "norm_mem_kv", "norm_ffn")]
    parts += [g_final, jnp.concatenate(g_small["mla_q_norm"], axis=0), jnp.concatenate(g_small["mla_kv_norm"], axis=0),
              conv[:, 3, :], conv[:, :3, :], loss_part[:1, :1]]
    sizes = [p.size for p in parts]
    packed = jnp.concatenate([p.reshape(-1) for p in parts])
    packed = jnp.pad(packed, (0, -packed.size % 1024)).reshape(-1, 128)
    total = allreduce_small(packed).reshape(-1)
    g_rep, at = {}, 0
    for n, p, size in zip(small + ["loss"], parts, sizes):
        g_rep[n] = total[at:at + size].reshape(p.shape)
        at += size
    loss = g_rep.pop("loss").reshape(())
    g_rep["norm_final"] = g_rep["norm_final"].reshape(norm_final.shape)
    width = ffn_conv_w.shape[2]
    g_rep["ffn_conv_w"] = lax.dynamic_slice_in_dim(g_rep["ffn_conv_w"], chip * width, width, axis=2)

    grads, delta, new_m, new_v = {}, {}, {}, {}
    for n in order:
        if n in halves:
            mine, theirs = halves[n]
            grads[n], delta[n], new_m[n], new_v[n] = adamw_halves(
                args[n], mine, theirs, args["m_" + n], args["v_" + n], core_op, name="adamw_big")
        else:
            grads[n] = g_rep[n]
            delta[n], new_m[n], new_v[n] = adamw(args[n], g_rep[n], args["m_" + n], args["v_" + n], name="adamw")
    return (loss, dx[None], *[grads[n] for n in order], *[delta[n] for n in order],
            *[new_m[n] for n in order], *[new_v[n] for n in order])
```

```python
import math

import jax
import jax.numpy as jnp
import numpy as np
from jax import lax
from jax.experimental import pallas as pl
from jax.experimental.pallas import tpu as pltpu

F32 = jnp.float32
BF16 = jnp.bfloat16
MESH = pl.DeviceIdType.MESH

EPS = 1e-6
LOG2E = 1.4426950408889634
CHUNK_SHIFT = 6
HEAD = 128
ROPE_DIM = 64
ROPE_THETA = 10000.0
N_CHIPS = 4
ADAM_LR, ADAM_B1, ADAM_B2, ADAM_EPS, ADAM_WD, ADAM_STEP = 0.001, 0.9, 0.999, 1e-08, 0.01, 10

VMEM_LIMIT_BYTES = 48 * 1024 * 1024
MM_TM, MM_TN, MM_TK = (1024, 1408, 512, 256, 128), (1024, 1408, 512, 256, 128), (1024, 512, 256, 128)
MM_WHOLE_K, MM_WHOLE_TM = 2048, (512, 256, 128)
ROW_TILE = (256, 128, 64, 32, 16, 8)
ATT_T = 512
MLA_HEADS_PER_STEP, MLA_BWD_HEADS_PER_STEP = 4, 2
SB_TQ, SB_TK = 512, 256
SB_HEADS_PER_STEP = 2
SB_SUBS_PER_STEP = 4
SB_DEAD_LOG2 = -200.0
SB_UNVISITED = -1e30
MEM_TQ = 512
CONV_TM, CONV_TN = 512, (512, 256, 128)
HALO = 16


def _pick(dim, prefs):
    for p in prefs:
        if dim % p == 0:
            return p
    return dim


def _dot(a, b, ca, cb):
    return lax.dot_general(a, b, (((ca,), (cb,)), ((), ())), preferred_element_type=F32)


def _params(sem):
    return pltpu.CompilerParams(dimension_semantics=sem, vmem_limit_bytes=VMEM_LIMIT_BYTES)


def _head_groups(heads, per_step):
    hp = per_step if heads % per_step == 0 else 1
    return hp, heads // hp, hp * HEAD


def _tables(pairs):
    arr = np.asarray(pairs, dtype=np.int32)
    return jnp.asarray(arr[:, 0]), jnp.asarray(arr[:, 1])


def mm(a, b, *, mode="nn", add=None, out_dtype=BF16, name):
    if mode == "nn":
        (m, k), (k2, n) = a.shape, b.shape
    elif mode == "nt":
        (m, k), (n, k2) = a.shape, b.shape
    else:
        (k, m), (k2, n) = a.shape, b.shape
    assert k == k2, (a.shape, b.shape, mode)
    ca, cb = {"nn": (1, 0), "nt": (1, 1), "tn": (0, 0)}[mode]
    if k <= MM_WHOLE_K:
        return _mm_whole_k(a, b, add, (m, n, k), (ca, cb), mode, out_dtype, name)
    tm, tn, tk = _pick(m, MM_TM), _pick(n, MM_TN), _pick(k, MM_TK)
    nk = k // tk

    def body(a_ref, b_ref, *rest):
        if add is None:
            o_ref, acc_ref = rest
        else:
            add_ref, o_ref, acc_ref = rest
        kk = pl.program_id(2)

        @pl.when(kk == 0)
        def _():
            acc_ref[...] = jnp.zeros_like(acc_ref)

        acc_ref[...] += _dot(a_ref[...].astype(BF16), b_ref[...].astype(BF16), ca, cb)

        @pl.when(kk == nk - 1)
        def _():
            r = acc_ref[...]
            if add is not None:
                r = r + add_ref[...]
            o_ref[...] = r.astype(o_ref.dtype)

    if mode == "tn":
        a_spec = pl.BlockSpec((tk, tm), lambda i, j, kk: (kk, i))
    else:
        a_spec = pl.BlockSpec((tm, tk), lambda i, j, kk: (i, kk))
    if mode == "nt":
        b_spec = pl.BlockSpec((tn, tk), lambda i, j, kk: (j, kk))
    else:
        b_spec = pl.BlockSpec((tk, tn), lambda i, j, kk: (kk, j))
    o_spec = pl.BlockSpec((tm, tn), lambda i, j, kk: (i, j))
    in_specs, args = [a_spec, b_spec], [a, b]
    if add is not None:
        in_specs.append(o_spec)
        args.append(add)
    return pl.pallas_call(
        body, name=name, grid=(m // tm, n // tn, nk), in_specs=in_specs, out_specs=o_spec,
        out_shape=jax.ShapeDtypeStruct((m, n), out_dtype), scratch_shapes=[pltpu.VMEM((tm, tn), F32)],
        compiler_params=_params(("parallel", "parallel", "arbitrary")),
    )(*args)


def _mm_whole_k(a, b, add, mnk, contract, mode, out_dtype, name):
    m, n, k = mnk
    tm, tn = _pick(m, MM_WHOLE_TM), _pick(n, MM_TN)

    def body(a_ref, b_ref, *rest):
        r = _dot(a_ref[...].astype(BF16), b_ref[...].astype(BF16), *contract)
        if add is not None:
            r = r + rest[0][...]
        rest[-1][...] = r.astype(rest[-1].dtype)

    a_spec = pl.BlockSpec((k, tm), lambda j, i: (0, i)) if mode == "tn" else pl.BlockSpec((tm, k), lambda j, i: (i, 0))
    b_spec = pl.BlockSpec((tn, k), lambda j, i: (j, 0)) if mode == "nt" else pl.BlockSpec((k, tn), lambda j, i: (0, j))
    o_spec = pl.BlockSpec((tm, tn), lambda j, i: (i, j))
    in_specs, args = [a_spec, b_spec], [a, b]
    if add is not None:
        in_specs.append(o_spec)
        args.append(add)
    return pl.pallas_call(
        body, name=name, grid=(n // tn, m // tm), in_specs=in_specs, out_specs=o_spec,
        out_shape=jax.ShapeDtypeStruct((m, n), out_dtype), compiler_params=_params(("parallel", "parallel")),
    )(*args)


def rmsnorm_fwd(x, g, *, name):
    m, d = x.shape
    tm = _pick(m, ROW_TILE)

    def body(x_ref, g_ref, o_ref):
        xv = x_ref[...]
        r = lax.rsqrt(jnp.mean(xv * xv, axis=-1, keepdims=True) + EPS)
        o_ref[...] = (xv * r * g_ref[...]).astype(o_ref.dtype)

    return pl.pallas_call(
        body, name=name, grid=(m // tm,),
        in_specs=[pl.BlockSpec((tm, d), lambda i: (i, 0)), pl.BlockSpec((1, d), lambda i: (0, 0))],
        out_specs=pl.BlockSpec((tm, d), lambda i: (i, 0)), out_shape=jax.ShapeDtypeStruct((m, d), BF16),
        compiler_params=_params(("parallel",)),
    )(x, g.reshape(1, d))


def rmsnorm_bwd(dh, x, g, res=None, *, name):
    m, d = x.shape
    tm = _pick(m, ROW_TILE)

    def body(dh_ref, x_ref, g_ref, *rest):
        if res is None:
            dx_ref, dxb_ref, dg_ref = rest
        else:
            res_ref, dx_ref, dxb_ref, dg_ref = rest

        @pl.when(pl.program_id(0) == 0)
        def _():
            dg_ref[...] = jnp.zeros_like(dg_ref)

        xv = x_ref[...]
        dhv = dh_ref[...].astype(F32)
        r = lax.rsqrt(jnp.mean(xv * xv, axis=-1, keepdims=True) + EPS)
        y = xv * r
        dhg = dhv * g_ref[...]
        dx = r * (dhg - y * jnp.mean(dhg * y, axis=-1, keepdims=True))
        if res is not None:
            dx = dx + res_ref[...]
        dx_ref[...] = dx
        dxb_ref[...] = dx.astype(BF16)
        dg_ref[...] += jnp.sum(dhv * y, axis=0, keepdims=True)

    row = pl.BlockSpec((tm, d), lambda i: (i, 0))
    vec = pl.BlockSpec((1, d), lambda i: (0, 0))
    in_specs, args = [row, row, vec], [dh, x, g.reshape(1, d)]
    if res is not None:
        in_specs.append(row)
        args.append(res)
    return pl.pallas_call(
        body, name=name, grid=(m // tm,), in_specs=in_specs, out_specs=[row, row, vec],
        out_shape=[jax.ShapeDtypeStruct((m, d), F32), jax.ShapeDtypeStruct((m, d), BF16),
                   jax.ShapeDtypeStruct((1, d), F32)],
        compiler_params=_params(("arbitrary",)),
    )(*args)


def final_loss(x, g, target, *, name):
    m, d = x.shape
    tm = _pick(m, ROW_TILE)

    def body(x_ref, g_ref, t_ref, dx_ref, dxb_ref, dg_ref, loss_ref):
        @pl.when(pl.program_id(0) == 0)
        def _():
            dg_ref[...] = jnp.zeros_like(dg_ref)
            loss_ref[...] = jnp.zeros_like(loss_ref)

        xv = x_ref[...]
        gv = g_ref[...]
        r = lax.rsqrt(jnp.mean(xv * xv, axis=-1, keepdims=True) + EPS)
        y = xv * r
        err = y * gv - t_ref[...]
        loss_ref[...] += 0.5 * jnp.sum(jnp.mean(err * err, axis=-1, keepdims=True))
        dy = err * (1.0 / d)
        dyg = dy * gv
        dx = r * (dyg - y * jnp.mean(dyg * y, axis=-1, keepdims=True))
        dx_ref[...] = dx
        dxb_ref[...] = dx.astype(BF16)
        dg_ref[...] += jnp.sum(dy * y, axis=0, keepdims=True)

    row = pl.BlockSpec((tm, d), lambda i: (i, 0))
    vec = pl.BlockSpec((1, d), lambda i: (0, 0))
    return pl.pallas_call(
        body, name=name, grid=(m // tm,), in_specs=[row, vec, row],
        out_specs=[row, row, vec, pl.BlockSpec((8, 128), lambda i: (0, 0))],
        out_shape=[jax.ShapeDtypeStruct((m, d), F32), jax.ShapeDtypeStruct((m, d), BF16),
                   jax.ShapeDtypeStruct((1, d), F32), jax.ShapeDtypeStruct((8, 128), F32)],
        compiler_params=_params(("arbitrary",)),
    )(x, g.reshape(1, d), target)


def rope(xin, cos, sin, *, col0, n, inverse, name):
    s = xin.shape[0]
    tm = _pick(s, ROW_TILE)
    half = ROPE_DIM // 2

    def body(x_ref, c_ref, s_ref, o_ref):
        xv = x_ref[...].astype(F32)
        lane = lax.broadcasted_iota(jnp.int32, xv.shape, 1)
        rot = jnp.where(lane < half, -pltpu.roll(xv, HEAD - half, 1), pltpu.roll(xv, half, 1))
        sv = s_ref[...]
        if inverse:
            sv = -sv
        o_ref[...] = (xv * c_ref[...] + rot * sv).astype(o_ref.dtype)

    tab = pl.BlockSpec((tm, HEAD), lambda i, j: (i, 0))
    return pl.pallas_call(
        body, name=name, grid=(s // tm, n),
        in_specs=[pl.BlockSpec((tm, HEAD), lambda i, j: (i, j + col0)), tab, tab],
        out_specs=pl.BlockSpec((tm, HEAD), lambda i, j: (i, j)),
        out_shape=jax.ShapeDtypeStruct((s, n * HEAD), BF16),
        compiler_params=_params(("parallel", "parallel")),
    )(xin, cos, sin)


def sum_lane_tiles(xin, n, *, name):
    s = xin.shape[0]
    tm = _pick(s, ROW_TILE)

    def body(x_ref, o_ref):
        acc = x_ref[:, :HEAD]
        for k in range(1, n):
            acc = acc + x_ref[:, k * HEAD:(k + 1) * HEAD]
        o_ref[...] = acc

    return pl.pallas_call(
        body, name=name, grid=(s // tm,), in_specs=[pl.BlockSpec((tm, n * HEAD), lambda i: (i, 0))],
        out_specs=pl.BlockSpec((tm, HEAD), lambda i: (i, 0)), out_shape=jax.ShapeDtypeStruct((s, HEAD), F32),
        compiler_params=_params(("parallel",)),
    )(xin)


def _mla_scores(q, k, scale, diagonal):
    sc = _dot(q, k, 1, 1) * scale
    if not diagonal:
        return sc
    qchunk = jnp.right_shift(lax.broadcasted_iota(jnp.int32, sc.shape, 0), CHUNK_SHIFT)
    kchunk = jnp.right_shift(lax.broadcasted_iota(jnp.int32, sc.shape, 1), CHUNK_SHIFT)
    return jnp.where(kchunk <= qchunk, sc, -jnp.inf)


def mla_fwd(qall, qr, kv, kr, heads, *, name, ride=None):
    s = qr.shape[0]
    t = min(ATT_T, s)
    nq = s // t
    scale = 1.0 / math.sqrt(HEAD + ROPE_DIM)
    pairs = [(qi, kj) for qi in range(nq) for kj in range(qi + 1)]
    qtab, ktab = _tables(pairs)
    hp, groups, wide = _head_groups(heads, MLA_HEADS_PER_STEP)
    r_in, r_in_specs, r_out, r_out_specs, r_sems = _ride_specs(ride)

    def body(qt_ref, kt_ref, qn_ref, qr_ref, kv_ref, kr_ref, *refs):
        o_ref, lse_ref = refs[len(r_in):len(r_in) + 2]
        m_ref, l_ref, acc_ref = refs[len(r_in) + 2 + len(r_out):len(r_in) + 5 + len(r_out)]
        st = pl.program_id(1)
        qi, kj = qt_ref[st], kt_ref[st]
        _ride_along(ride, refs[:len(r_in)], refs[len(r_in) + 2:len(r_in) + 2 + len(r_out)], refs[len(refs) - 2:],
                    (pl.program_id(0) == 0) & (st == 0), (pl.program_id(0) == groups - 1) & (st == len(pairs) - 1))

        @pl.when(kj == 0)
        def _():
            m_ref[...] = jnp.full_like(m_ref, -jnp.inf)
            l_ref[...] = jnp.zeros_like(l_ref)
            acc_ref[...] = jnp.zeros_like(acc_ref)

        def step(diagonal):
            for hh in range(hp):
                cols = slice(hh * HEAD, (hh + 1) * HEAD)
                kn = kv_ref[:, 2 * hh * HEAD:(2 * hh + 1) * HEAD]
                v = kv_ref[:, (2 * hh + 1) * HEAD:(2 * hh + 2) * HEAD]
                sc = _mla_scores(jnp.concatenate([qn_ref[:, cols], qr_ref[:, cols]], axis=1),
                                 jnp.concatenate([kn, kr_ref[...]], axis=1), scale, diagonal)
                m_prev = m_ref[hh]
                m_new = jnp.maximum(m_prev, jnp.max(sc, axis=-1, keepdims=True))
                p = jnp.exp(sc - m_new)
                alpha = jnp.exp(m_prev - m_new)
                pv = _dot(p.astype(BF16), jnp.concatenate([v, jnp.ones_like(v)], axis=1), 1, 0)
                l_ref[hh] = alpha * l_ref[hh] + pv[:, HEAD:HEAD + 1]
                acc_ref[:, cols] = alpha * acc_ref[:, cols] + pv[:, :HEAD]
                m_ref[hh] = m_new

        @pl.when(kj < qi)
        def _():
            step(False)

        @pl.when(kj == qi)
        def _():
            step(True)
            for hh in range(hp):
                cols = slice(hh * HEAD, (hh + 1) * HEAD)
                o_ref[:, cols] = (acc_ref[:, cols] / l_ref[hh]).astype(o_ref.dtype)
                lse_ref[:, cols] = jnp.broadcast_to(m_ref[hh] + jnp.log(l_ref[hh]), (t, HEAD))

    qspec = pl.BlockSpec((t, wide), lambda h, st, qt, kt: (qt[st], h))
    grid_spec = pltpu.PrefetchScalarGridSpec(
        num_scalar_prefetch=2, grid=(groups, len(pairs)),
        in_specs=[qspec, qspec,
                  pl.BlockSpec((t, 2 * wide), lambda h, st, qt, kt: (kt[st], h)),
                  pl.BlockSpec((t, HEAD), lambda h, st, qt, kt: (kt[st], 0))] + r_in_specs,
        out_specs=[qspec, qspec] + r_out_specs,
        scratch_shapes=[pltpu.VMEM((hp, t, 1), F32), pltpu.VMEM((hp, t, 1), F32), pltpu.VMEM((t, wide), F32)] + r_sems)
    outs = pl.pallas_call(
        body, name=name, grid_spec=grid_spec,
        out_shape=[jax.ShapeDtypeStruct((s, heads * HEAD), BF16), jax.ShapeDtypeStruct((s, heads * HEAD), F32)] + r_out,
        compiler_params=_params(("arbitrary", "arbitrary") if ride else ("parallel", "arbitrary")),
    )(qtab, ktab, qall, qr, kv, kr, *r_in)
    return outs if ride is None else (outs[0], outs[1], list(outs[2:]))


def mla_bwd(qall, qr, kv, kr, o, do, lse, heads, *, name, ride=None):
    s = qr.shape[0]
    t = min(ATT_T, s)
    nq = s // t
    scale = 1.0 / math.sqrt(HEAD + ROPE_DIM)
    pairs = [(kj, qi) for kj in range(nq) for qi in range(kj, nq)]
    ktab, qtab = _tables(pairs)
    last = len(pairs) - 1
    hp, groups, wide = _head_groups(heads, MLA_BWD_HEADS_PER_STEP)
    r_in, r_in_specs, r_out, r_out_specs, r_sems = _ride_specs(ride)

    def body(kt_ref, qt_ref, qn_ref, qr_ref, kv_ref, kr_ref, o_ref, do_ref, lse_ref, *refs):
        n_i, n_o = len(r_in), len(r_out)
        dqn_ref, dqr_ref, dkv_ref, dkr_ref = refs[n_i:n_i + 4]
        fn_ref, fr_ref, akn_ref, av_ref, akr_ref = refs[n_i + 4 + n_o:n_i + 9 + n_o]
        st = pl.program_id(1)
        kj, qi = kt_ref[st], qt_ref[st]
        _ride_along(ride, refs[:n_i], refs[n_i + 4:n_i + 4 + n_o], refs[len(refs) - 2:],
                    (pl.program_id(0) == 0) & (st == 0), (pl.program_id(0) == groups - 1) & (st == last))

        @pl.when(st == 0)
        def _():
            fn_ref[...] = jnp.zeros_like(fn_ref)
            fr_ref[...] = jnp.zeros_like(fr_ref)

        @pl.when(qi == kj)
        def _():
            akn_ref[...] = jnp.zeros_like(akn_ref)
            av_ref[...] = jnp.zeros_like(av_ref)
            akr_ref[...] = jnp.zeros_like(akr_ref)

        def step(diagonal):
            rows = pl.ds(pl.multiple_of(qi * t, t), t)
            krv = kr_ref[...]
            for hh in range(hp):
                cols = slice(hh * HEAD, (hh + 1) * HEAD)
                dov = do_ref[:, cols]
                q2 = jnp.concatenate([qn_ref[:, cols], qr_ref[:, cols]], axis=1)
                k2 = jnp.concatenate([kv_ref[:, 2 * hh * HEAD:(2 * hh + 1) * HEAD], krv], axis=1)
                v = kv_ref[:, (2 * hh + 1) * HEAD:(2 * hh + 2) * HEAD]
                sc = _mla_scores(q2, k2, scale, diagonal)
                p = jnp.exp(sc - lse_ref[:, hh * HEAD:hh * HEAD + 1])
                delta = jnp.sum(dov.astype(F32) * o_ref[:, cols].astype(F32), axis=-1, keepdims=True)
                dp = _dot(dov, v, 1, 1)
                ds = (p * (dp - delta) * scale).astype(BF16)
                av_ref[:, cols] += _dot(p.astype(BF16), dov, 0, 0)
                dk2 = _dot(ds, q2, 0, 0)
                akn_ref[:, cols] += dk2[:, :HEAD]
                akr_ref[:, cols] += dk2[:, HEAD:]
                dq2 = _dot(ds, k2, 1, 0)
                fn_ref[rows, cols] += dq2[:, :HEAD]
                fr_ref[rows, cols] += dq2[:, HEAD:]

        @pl.when(qi == kj)
        def _():
            step(True)

        @pl.when(qi > kj)
        def _():
            step(False)

        @pl.when(qi == nq - 1)
        def _():
            for hh in range(hp):
                cols = slice(hh * HEAD, (hh + 1) * HEAD)
                dkv_ref[:, 2 * hh * HEAD:(2 * hh + 1) * HEAD] = akn_ref[:, cols].astype(dkv_ref.dtype)
                dkv_ref[:, (2 * hh + 1) * HEAD:(2 * hh + 2) * HEAD] = av_ref[:, cols].astype(dkv_ref.dtype)
            dkr_ref[...] = akr_ref[...]

        @pl.when(st == last)
        def _():
            dqn_ref[...] = fn_ref[...].astype(dqn_ref.dtype)
            dqr_ref[...] = fr_ref[...].astype(dqr_ref.dtype)

    qspec = pl.BlockSpec((t, wide), lambda h, st, kt, qt: (qt[st], h))
    kvspec = pl.BlockSpec((t, 2 * wide), lambda h, st, kt, qt: (kt[st], h))
    krspec = pl.BlockSpec((t, HEAD), lambda h, st, kt, qt: (kt[st], 0))
    headspec = pl.BlockSpec((s, wide), lambda h, st, kt, qt: (0, h))
    grid_spec = pltpu.PrefetchScalarGridSpec(
        num_scalar_prefetch=2, grid=(groups, len(pairs)),
        in_specs=[qspec, qspec, kvspec, krspec, qspec, qspec, qspec] + r_in_specs,
        out_specs=[headspec, headspec, kvspec, pl.BlockSpec((t, wide), lambda h, st, kt, qt: (kt[st], h))] + r_out_specs,
        scratch_shapes=[pltpu.VMEM((s, wide), F32), pltpu.VMEM((s, wide), F32), pltpu.VMEM((t, wide), F32),
                        pltpu.VMEM((t, wide), F32), pltpu.VMEM((t, wide), F32)] + r_sems)
    outs = pl.pallas_call(
        body, name=name, grid_spec=grid_spec,
        out_shape=[jax.ShapeDtypeStruct((s, heads * HEAD), BF16), jax.ShapeDtypeStruct((s, heads * HEAD), BF16),
                   jax.ShapeDtypeStruct((s, heads * 2 * HEAD), BF16), jax.ShapeDtypeStruct((s, heads * HEAD), F32)] + r_out,
        compiler_params=_params(("arbitrary", "arbitrary") if ride else ("parallel", "arbitrary")),
    )(ktab, qtab, qall, qr, kv, kr, o, do, lse, *r_in)
    return outs if ride is None else (*outs[:4], list(outs[4:]))


def _split_dot(val, tri, cb):
    hi = val.astype(BF16)
    lo = (val - hi.astype(F32)).astype(BF16)
    return _dot(hi, tri, 1, cb) + _dot(lo, tri, 1, cb)


def _sb_logs(q, k, offset, scale, masked):
    z = _dot(q, k, 1, 1) * (scale * LOG2E)
    sp = jnp.log2(1.0 + jnp.exp2(-jnp.abs(z)))
    ls = jnp.minimum(z, 0.0) - sp
    lk = ls - z
    if not masked:
        return None, ls, lk
    strict = (lax.broadcasted_iota(jnp.int32, z.shape, 1) + offset) < lax.broadcasted_iota(jnp.int32, z.shape, 0)
    return strict, ls, jnp.where(strict, lk, 0.0)


def _lane_pick(blk, idx):
    lane = lax.broadcasted_iota(jnp.int32, blk.shape, 1)
    return jnp.sum(jnp.where(lane == idx, blk, 0.0), axis=-1, keepdims=True)


def _lane_put(blk, idx, col):
    lane = lax.broadcasted_iota(jnp.int32, blk.shape, 1)
    return jnp.where(lane == idx, col, blk)


def _sb_tiles(s):
    tq = min(SB_TQ, s)
    tk = min(SB_TK, tq)
    assert s // tk <= HEAD
    return tq, tk


def sb_fwd(qkv, tri, heads, *, name):
    s = qkv.shape[0]
    tq, tk = _sb_tiles(s)
    nq = s // tq
    nsub = min(SB_SUBS_PER_STEP, s // tk)
    span = nsub * tk
    scale = 1.0 / math.sqrt(HEAD)
    pairs = [(qi, sb) for qi in range(nq) for sb in range(((qi + 1) * tq - 1) // span, -1, -1)]
    qtab, ktab = _tables(pairs)
    hp, groups, wide = _head_groups(heads, SB_HEADS_PER_STEP)

    def body(qt_ref, kt_ref, q_ref, k_ref, v_ref, tri_ref, o_ref, c_ref, carry_ref, acc_ref):
        st = pl.program_id(1)
        qi, sb = qt_ref[st], kt_ref[st]

        @pl.when(sb == ((qi + 1) * tq - 1) // span)
        def _():
            carry_ref[...] = jnp.zeros_like(carry_ref)
            acc_ref[...] = jnp.zeros_like(acc_ref)
            c_ref[...] = jnp.full_like(c_ref, SB_UNVISITED)

        def sub_block(i, _):
            sub = nsub - 1 - i
            kj = sb * nsub + sub
            keys = pl.ds(pl.multiple_of(sub * tk, tk), tk)
            alive = (kj * tk < (qi + 1) * tq) & (jnp.max(carry_ref[...]) > SB_DEAD_LOG2)

            def step(masked):
                for hh in range(hp):
                    cols = slice(hh * HEAD, (hh + 1) * HEAD)
                    strict, ls, lk = _sb_logs(q_ref[:, cols], k_ref[keys, cols], kj * tk - qi * tq, scale, masked)
                    carry = carry_ref[hh]
                    a = jnp.exp2(ls + _split_dot(lk, tri_ref[...], 0) + carry)
                    if masked:
                        a = jnp.where(strict, a, 0.0)
                    acc_ref[:, cols] += _dot(a.astype(BF16), v_ref[keys, cols], 1, 0)
                    c_ref[:, cols] = _lane_put(c_ref[:, cols], kj, carry)
                    carry_ref[hh] = carry + jnp.sum(lk, axis=-1, keepdims=True)

            @pl.when(alive & ((kj + 1) * tk > qi * tq))
            def _():
                step(True)

            @pl.when(alive & ((kj + 1) * tk <= qi * tq))
            def _():
                step(False)

            return 0

        lax.fori_loop(0, nsub, sub_block, 0)

        @pl.when(sb == 0)
        def _():
            o_ref[...] = acc_ref[...].astype(o_ref.dtype)

    qspec = pl.BlockSpec((tq, wide), lambda h, st, qt, kt: (qt[st], h))
    grid_spec = pltpu.PrefetchScalarGridSpec(
        num_scalar_prefetch=2, grid=(groups, len(pairs)),
        in_specs=[qspec,
                  pl.BlockSpec((span, wide), lambda h, st, qt, kt: (kt[st], groups + h)),
                  pl.BlockSpec((span, wide), lambda h, st, qt, kt: (kt[st], 2 * groups + h)),
                  pl.BlockSpec((tk, tk), lambda h, st, qt, kt: (0, 0))],
        out_specs=[qspec, qspec],
        scratch_shapes=[pltpu.VMEM((hp, tq, 1), F32), pltpu.VMEM((tq, wide), F32)])
    return pl.pallas_call(
        body, name=name, grid_spec=grid_spec,
        out_shape=[jax.ShapeDtypeStruct((s, heads * HEAD), BF16), jax.ShapeDtypeStruct((s, heads * HEAD), F32)],
        compiler_params=_params(("parallel", "arbitrary")),
    )(qtab, ktab, qkv, qkv, qkv, tri)


def sb_bwd(qkv, do, cmat, tri, heads, *, name):
    s = qkv.shape[0]
    tq, tk = _sb_tiles(s)
    nq = s // tq
    nsub = min(SB_SUBS_PER_STEP, s // tk)
    span = nsub * tk
    scale = 1.0 / math.sqrt(HEAD)
    pairs = [(sb, qi) for sb in range(s // span) for qi in range(sb * span // tq, nq)]
    ktab, qtab = _tables(pairs)
    last = len(pairs) - 1
    hp, groups, wide = _head_groups(heads, SB_HEADS_PER_STEP)

    def body(kt_ref, qt_ref, q_ref, k_ref, v_ref, do_ref, c_ref, tri_ref, dq_ref, dk_ref, dv_ref,
             dqf_ref, gsum_ref, ak_ref, av_ref):
        st = pl.program_id(1)
        sb, qi = kt_ref[st], qt_ref[st]

        @pl.when(st == 0)
        def _():
            dqf_ref[...] = jnp.zeros_like(dqf_ref)
            gsum_ref[...] = jnp.zeros_like(gsum_ref)

        @pl.when(qi == sb * span // tq)
        def _():
            ak_ref[...] = jnp.zeros_like(ak_ref)
            av_ref[...] = jnp.zeros_like(av_ref)

        def sub_block(sub, _):
            kj = sb * nsub + sub
            keys = pl.ds(pl.multiple_of(sub * tk, tk), tk)

            def step(masked):
                rows = pl.ds(pl.multiple_of(qi * tq, tq), tq)
                tri_v = tri_ref[...]
                for hh in range(hp):
                    cols = slice(hh * HEAD, (hh + 1) * HEAD)
                    qv, kblk, dov = q_ref[:, cols], k_ref[keys, cols], do_ref[:, cols]
                    strict, ls, lk = _sb_logs(qv, kblk, kj * tk - qi * tq, scale, masked)
                    a = jnp.exp2(ls + _split_dot(lk, tri_v, 0) + _lane_pick(c_ref[:, cols], kj))
                    if masked:
                        a = jnp.where(strict, a, 0.0)
                    g = _dot(dov, v_ref[keys, cols], 1, 1) * a
                    before_all = gsum_ref[hh, rows, :]
                    before = _split_dot(g, tri_v, 1) + before_all[:, :1]
                    beta = jnp.exp2(ls)
                    dz = g * (1.0 - beta) - before * beta
                    if masked:
                        dz = jnp.where(strict, dz, 0.0)
                    dzb = dz.astype(BF16)
                    av_ref[keys, cols] += _dot(a.astype(BF16), dov, 0, 0)
                    ak_ref[keys, cols] += _dot(dzb, qv, 0, 0)
                    dqf_ref[rows, cols] += _dot(dzb, kblk, 1, 0)
                    gsum_ref[hh, rows, :] = before_all + jnp.sum(g, axis=-1, keepdims=True)

            lane = lax.broadcasted_iota(jnp.int32, (tq, wide), 1)
            seen = jnp.max(jnp.where((lane & (HEAD - 1)) == kj, c_ref[...], SB_UNVISITED)) > SB_DEAD_LOG2
            alive = (kj * tk < (qi + 1) * tq) & seen

            @pl.when(alive & ((kj + 1) * tk > qi * tq))
            def _():
                step(True)

            @pl.when(alive & ((kj + 1) * tk <= qi * tq))
            def _():
                step(False)

            return 0

        lax.fori_loop(0, nsub, sub_block, 0)

        @pl.when(qi == nq - 1)
        def _():
            dk_ref[...] = (ak_ref[...] * scale).astype(dk_ref.dtype)
            dv_ref[...] = av_ref[...].astype(dv_ref.dtype)

        @pl.when(st == last)
        def _():
            dq_ref[...] = (dqf_ref[...] * scale).astype(dq_ref.dtype)

    qspec = pl.BlockSpec((tq, wide), lambda h, st, kt, qt: (qt[st], h))
    ospec = pl.BlockSpec((span, wide), lambda h, st, kt, qt: (kt[st], h))
    grid_spec = pltpu.PrefetchScalarGridSpec(
        num_scalar_prefetch=2, grid=(groups, len(pairs)),
        in_specs=[qspec,
                  pl.BlockSpec((span, wide), lambda h, st, kt, qt: (kt[st], groups + h)),
                  pl.BlockSpec((span, wide), lambda h, st, kt, qt: (kt[st], 2 * groups + h)),
                  qspec, qspec, pl.BlockSpec((tk, tk), lambda h, st, kt, qt: (0, 0))],
        out_specs=[pl.BlockSpec((s, wide), lambda h, st, kt, qt: (0, h)), ospec, ospec],
        scratch_shapes=[pltpu.VMEM((s, wide), F32), pltpu.VMEM((hp, s, HEAD), F32), pltpu.VMEM((span, wide), F32),
                        pltpu.VMEM((span, wide), F32)])
    return pl.pallas_call(
        body, name=name, grid_spec=grid_spec,
        out_shape=[jax.ShapeDtypeStruct((s, heads * HEAD), BF16)] * 3,
        compiler_params=_params(("parallel", "arbitrary")),
    )(ktab, qtab, qkv, qkv, qkv, do, cmat, tri)


def _mem_probs(q, k, scale):
    sc = _dot(q, k, 1, 1) * scale
    e = jnp.exp(sc - jnp.max(sc, axis=-1, keepdims=True))
    return e / jnp.sum(e, axis=-1, keepdims=True)


def mem_fwd(q, kvm, heads, *, name):
    s, nm = q.shape[0], kvm.shape[0]
    tq = min(MEM_TQ, s)
    scale = 1.0 / math.sqrt(HEAD)

    def body(q_ref, k_ref, v_ref, o_ref):
        p = _mem_probs(q_ref[...], k_ref[...], scale)
        o_ref[...] = _dot(p.astype(BF16), v_ref[...], 1, 0).astype(o_ref.dtype)

    qspec = pl.BlockSpec((tq, HEAD), lambda h, qi: (qi, h))
    return pl.pallas_call(
        body, name=name, grid=(heads, s // tq),
        in_specs=[qspec, pl.BlockSpec((nm, HEAD), lambda h, qi: (0, h)),
                  pl.BlockSpec((nm, HEAD), lambda h, qi: (0, heads + h))],
        out_specs=qspec, out_shape=jax.ShapeDtypeStruct((s, heads * HEAD), BF16),
        compiler_params=_params(("parallel", "parallel")),
    )(q, kvm, kvm)


def mem_bwd(q, kvm, do, heads, *, name):
    s, nm = q.shape[0], kvm.shape[0]
    tq = min(MEM_TQ, s)
    scale = 1.0 / math.sqrt(HEAD)

    def body(q_ref, k_ref, v_ref, do_ref, dq_ref, dk_ref, dv_ref):
        @pl.when(pl.program_id(1) == 0)
        def _():
            dk_ref[...] = jnp.zeros_like(dk_ref)
            dv_ref[...] = jnp.zeros_like(dv_ref)

        qv, kvv, dov = q_ref[...], k_ref[...], do_ref[...]
        p = _mem_probs(qv, kvv, scale)
        dp = _dot(dov, v_ref[...], 1, 1)
        ds = (p * (dp - jnp.sum(dp * p, axis=-1, keepdims=True)) * scale).astype(BF16)
        dq_ref[...] = _dot(ds, kvv, 1, 0).astype(dq_ref.dtype)
        dk_ref[...] += _dot(ds, qv, 0, 0)
        dv_ref[...] += _dot(p.astype(BF16), dov, 0, 0)

    qspec = pl.BlockSpec((tq, HEAD), lambda h, qi: (qi, h))
    kspec = pl.BlockSpec((nm, HEAD), lambda h, qi: (0, h))
    return pl.pallas_call(
        body, name=name, grid=(heads, s // tq),
        in_specs=[qspec, kspec, pl.BlockSpec((nm, HEAD), lambda h, qi: (0, heads + h)), qspec],
        out_specs=[qspec, kspec, kspec],
        out_shape=[jax.ShapeDtypeStruct((s, heads * HEAD), BF16), jax.ShapeDtypeStruct((nm, heads * HEAD), F32),
                   jax.ShapeDtypeStruct((nm, heads * HEAD), F32)],
        compiler_params=_params(("parallel", "arbitrary")),
    )(q, kvm, kvm, do)


def _conv3(u, halo, w, b):
    tm = u.shape[0]
    row = lax.broadcasted_iota(jnp.int32, u.shape, 0)
    h1, h2 = halo[HALO - 1:HALO, :], halo[HALO - 2:HALO - 1, :]
    u1 = jnp.where(row == 0, h1, pltpu.roll(u, 1, 0))
    u2 = jnp.where(row == 0, h2, jnp.where(row == 1, h1, pltpu.roll(u, 2 % tm, 0)))
    return b + w[0:1, :] * u2 + w[1:2, :] * u1 + w[2:3, :] * u, u1, u2


def _conv_specs(s, f):
    tm, tn = min(CONV_TM, s), _pick(f, CONV_TN)
    return tm, tn, s // tm, f // tn


def _silu_parts(g):
    sg = 1.0 / (1.0 + jnp.exp(-g))
    return g * sg, sg


def conv_gate_fwd(u, cw, cb, *, name):
    s, f = u.shape[0], u.shape[1] // 2
    tm, tn, ni, nj = _conv_specs(s, f)
    hb = tm // HALO

    def body(ug_ref, uu_ref, hg_ref, hu_ref, wg_ref, wu_ref, bg_ref, bu_ref, a_ref):
        keep = (pl.program_id(1) > 0).astype(F32)
        gate, _, _ = _conv3(ug_ref[...].astype(F32), hg_ref[...].astype(F32) * keep, wg_ref[...], bg_ref[...])
        up, _, _ = _conv3(uu_ref[...].astype(F32), hu_ref[...].astype(F32) * keep, wu_ref[...], bu_ref[...])
        a_ref[...] = (_silu_parts(gate)[0] * up).astype(a_ref.dtype)

    def main(off):
        return pl.BlockSpec((tm, tn), lambda j, i: (i, j + off))

    def halo(off):
        return pl.BlockSpec((HALO, tn), lambda j, i: (jnp.maximum(i * hb - 1, 0), j + off))

    def par(rows, off):
        return pl.BlockSpec((rows, tn), lambda j, i: (0, j + off))

    return pl.pallas_call(
        body, name=name, grid=(nj, ni),
        in_specs=[main(0), main(nj), halo(0), halo(nj), par(3, 0), par(3, nj), par(1, 0), par(1, nj)],
        out_specs=main(0), out_shape=jax.ShapeDtypeStruct((s, f), BF16),
        compiler_params=_params(("parallel", "parallel")),
    )(u, u, u, u, cw, cw, cb, cb)


def conv_gate_bwd(u, da, cw, cb, *, name):
    s, f = u.shape[0], u.shape[1] // 2
    tm, tn, ni, nj = _conv_specs(s, f)
    hb = tm // HALO

    def body(ug_ref, uu_ref, hg_ref, hu_ref, da_ref, wg_ref, wu_ref, bg_ref, bu_ref, dg_ref, du_ref, pg_ref, pu_ref):
        @pl.when(pl.program_id(1) == 0)
        def _():
            pg_ref[...] = jnp.zeros_like(pg_ref)
            pu_ref[...] = jnp.zeros_like(pu_ref)

        keep = (pl.program_id(1) > 0).astype(F32)
        ug, uu = ug_ref[...].astype(F32), uu_ref[...].astype(F32)
        gate, ug1, ug2 = _conv3(ug, hg_ref[...].astype(F32) * keep, wg_ref[...], bg_ref[...])
        up, uu1, uu2 = _conv3(uu, hu_ref[...].astype(F32) * keep, wu_ref[...], bu_ref[...])
        act, sg = _silu_parts(gate)
        dav = da_ref[...].astype(F32)
        d_gate = dav * up * (sg * (1.0 + gate * (1.0 - sg)))
        d_up = dav * act
        dg_ref[...] = d_gate.astype(dg_ref.dtype)
        du_ref[...] = d_up.astype(du_ref.dtype)
        for p_ref, dc, taps in ((pg_ref, d_gate, (ug2, ug1, ug)), (pu_ref, d_up, (uu2, uu1, uu))):
            for r, tap in enumerate(taps):
                p_ref[r:r + 1, :] += jnp.sum(dc * tap, axis=0, keepdims=True)
            p_ref[3:4, :] += jnp.sum(dc, axis=0, keepdims=True)

    def main(off):
        return pl.BlockSpec((tm, tn), lambda j, i: (i, j + off))

    def halo(off):
        return pl.BlockSpec((HALO, tn), lambda j, i: (jnp.maximum(i * hb - 1, 0), j + off))

    def par(rows, off):
        return pl.BlockSpec((rows, tn), lambda j, i: (0, j + off))

    dg, du, pg, pu = pl.pallas_call(
        body, name=name, grid=(nj, ni),
        in_specs=[main(0), main(nj), halo(0), halo(nj), main(0), par(3, 0), par(3, nj), par(1, 0), par(1, nj)],
        out_specs=[main(0), main(0), par(8, 0), par(8, 0)],
        out_shape=[jax.ShapeDtypeStruct((s, f), BF16), jax.ShapeDtypeStruct((s, f), BF16),
                   jax.ShapeDtypeStruct((8, f), F32), jax.ShapeDtypeStruct((8, f), F32)],
        compiler_params=_params(("parallel", "arbitrary")),
    )(u, u, u, u, da, cw, cw, cb, cb)
    return dg, du, jnp.concatenate([pg, pu], axis=1)


def conv_transpose(dc, w, *, name):
    s, f = dc.shape
    tm, tn, ni, nj = _conv_specs(s, f)
    hb = tm // HALO

    def body(d_ref, h_ref, w_ref, o_ref):
        d = d_ref[...].astype(F32)
        halo = h_ref[...].astype(F32) * (pl.program_id(1) < ni - 1).astype(F32)
        row = lax.broadcasted_iota(jnp.int32, d.shape, 0)
        n0, n1 = halo[0:1, :], halo[1:2, :]
        d1 = jnp.where(row == tm - 1, n0, pltpu.roll(d, tm - 1, 0))
        d2 = jnp.where(row == tm - 2, n0, jnp.where(row == tm - 1, n1, pltpu.roll(d, tm - 2, 0)))
        wv = w_ref[...]
        o_ref[...] = (wv[2:3, :] * d + wv[1:2, :] * d1 + wv[0:1, :] * d2).astype(o_ref.dtype)

    main = pl.BlockSpec((tm, tn), lambda j, i: (i, j))
    return pl.pallas_call(
        body, name=name, grid=(nj, ni),
        in_specs=[main, pl.BlockSpec((HALO, tn), lambda j, i: (jnp.minimum((i + 1) * hb, s // HALO - 1), j)),
                  pl.BlockSpec((3, tn), lambda j, i: (0, j))],
        out_specs=main, out_shape=jax.ShapeDtypeStruct((s, f), BF16),
        compiler_params=_params(("parallel", "parallel")),
    )(dc, dc, w)


def _tile2d(r, c):
    return _pick(r, ROW_TILE), _pick(c, (2048, 1024, 512, 256, 128))


def sum_slots(buf, *, name):
    n, shape = buf.shape[0], buf.shape[1:]
    r, c = math.prod(shape[:-1]), shape[-1]
    tm, tn = _tile2d(r, c)

    def body(b_ref, o_ref):
        acc = b_ref[0]
        for k in range(1, n):
            acc = acc + b_ref[k]
        o_ref[...] = acc

    out = pl.pallas_call(
        body, name=name, grid=(r // tm, c // tn),
        in_specs=[pl.BlockSpec((n, tm, tn), lambda i, j: (0, i, j))],
        out_specs=pl.BlockSpec((tm, tn), lambda i, j: (i, j)), out_shape=jax.ShapeDtypeStruct((r, c), F32),
        compiler_params=_params(("parallel", "parallel")),
    )(buf.reshape(n, r, c))
    return out.reshape(shape)


def add_half(layers, got, core, *, name):
    n_l = len(layers)
    a, _, r, c = layers[0].shape
    tm, tn = _tile2d(r, c)

    def body(core_ref, *refs):
        got_ref, o_ref = refs[n_l:]
        for l in range(n_l):
            @pl.when(pl.program_id(0) == l)
            def _(l=l):
                o_ref[...] = (refs[l][...] + got_ref[...]).astype(o_ref.dtype)

    def layer_spec(l):
        def index(b, q, i, j, core_ref):
            on = b == l
            return (jnp.where(on, q, 0), core_ref[0], jnp.where(on, i, 0), jnp.where(on, j, 0))
        return pl.BlockSpec((None, None, tm, tn), index)

    part = pl.BlockSpec((None, None, tm, tn), lambda b, q, i, j, core_ref: (b, q, i, j))
    grid_spec = pltpu.PrefetchScalarGridSpec(
        num_scalar_prefetch=1, grid=(n_l, a, r // tm, c // tn),
        in_specs=[layer_spec(l) for l in range(n_l)] + [part], out_specs=part)
    return pl.pallas_call(
        body, name=name, grid_spec=grid_spec, out_shape=jax.ShapeDtypeStruct((n_l, a, r, c), BF16),
        compiler_params=_params(("arbitrary", "parallel", "parallel", "parallel")),
    )(core, *layers, got)


def add_own_block(h, got, chip, axis, *, name):
    _, l, r, c = got.shape
    tm, tn = _tile2d(r, c)
    ncb = c // tn

    def body(chip_ref, h_ref, got_ref, o_ref):
        o_ref[...] = ((h_ref[...].astype(F32) + got_ref[0].astype(F32)) + got_ref[1].astype(F32)) + got_ref[2].astype(F32)

    if axis == 0:
        h_spec = pl.BlockSpec((None, None, tm, tn), lambda b, i, j, chip_ref: (b, chip_ref[0], i, j))
    else:
        h_spec = pl.BlockSpec((None, tm, tn), lambda b, i, j, chip_ref: (b, i, chip_ref[0] * ncb + j))
    grid_spec = pltpu.PrefetchScalarGridSpec(
        num_scalar_prefetch=1, grid=(l, r // tm, ncb),
        in_specs=[h_spec, pl.BlockSpec((3, None, tm, tn), lambda b, i, j, chip_ref: (0, b, i, j))],
        out_specs=pl.BlockSpec((None, tm, tn), lambda b, i, j, chip_ref: (b, i, j)))
    return pl.pallas_call(
        body, name=name, grid_spec=grid_spec, out_shape=jax.ShapeDtypeStruct((l, r, c), F32),
        compiler_params=_params(("parallel", "parallel", "parallel")),
    )(chip, h, got)


def _adam_update(w, g, m, v):
    bc1, bc2 = 1.0 - ADAM_B1 ** ADAM_STEP, 1.0 - ADAM_B2 ** ADAM_STEP
    mn = ADAM_B1 * m + (1.0 - ADAM_B1) * g
    vn = ADAM_B2 * v + (1.0 - ADAM_B2) * (g * g)
    return -ADAM_LR * ((mn / bc1) / (jnp.sqrt(vn / bc2) + ADAM_EPS) + ADAM_WD * w), mn, vn


def adamw(w, g, m, v, *, name):
    shape = w.shape
    c = shape[-1]
    r = math.prod(shape[:-1]) if len(shape) > 1 else 1
    tm, tn = _tile2d(r, c)

    def body(w_ref, g_ref, m_ref, v_ref, d_ref, mo_ref, vo_ref):
        d_ref[...], mo_ref[...], vo_ref[...] = _adam_update(w_ref[...], g_ref[...], m_ref[...], v_ref[...])

    spec = pl.BlockSpec((tm, tn), lambda i, j: (i, j))
    outs = pl.pallas_call(
        body, name=name, grid=(r // tm, c // tn), in_specs=[spec] * 4, out_specs=[spec] * 3,
        out_shape=[jax.ShapeDtypeStruct((r, c), F32)] * 3, compiler_params=_params(("parallel", "parallel")),
    )(*(t.reshape(r, c) for t in (w, g, m, v)))
    return tuple(o.reshape(shape) for o in outs)


def adamw_halves(w, mine, got, m, v, core, *, name):
    l, r, c = w.shape
    rh = r // 2
    tm, tn = _tile2d(rh, c)

    def body(core_ref, w_ref, mine_ref, got_ref, m_ref, v_ref, g_ref, d_ref, mo_ref, vo_ref):
        g = jnp.where(pl.program_id(1) == core_ref[0], mine_ref[...], got_ref[...])
        g_ref[...] = g
        d_ref[...], mo_ref[...], vo_ref[...] = _adam_update(w_ref[...], g, m_ref[...], v_ref[...])

    full = pl.BlockSpec((None, None, tm, tn), lambda b, h, i, j, core_ref: (b, h, i, j))
    half = pl.BlockSpec((None, tm, tn), lambda b, h, i, j, core_ref: (b, i, j))
    grid_spec = pltpu.PrefetchScalarGridSpec(
        num_scalar_prefetch=1, grid=(l, 2, rh // tm, c // tn),
        in_specs=[full, half, half, full, full], out_specs=[full] * 4)
    outs = pl.pallas_call(
        body, name=name, grid_spec=grid_spec, out_shape=[jax.ShapeDtypeStruct((l, 2, rh, c), F32)] * 4,
        compiler_params=_params(("parallel", "parallel", "parallel", "parallel")),
    )(core, w.reshape(l, 2, rh, c), mine, got, m.reshape(l, 2, rh, c), v.reshape(l, 2, rh, c))
    return tuple(o.reshape(l, r, c) for o in outs)


def _copies(plan, in_refs, out_refs, send_sems, recv_sems):
    x, y, c = lax.axis_index("x"), lax.axis_index("y"), lax.axis_index("c")
    out = []
    for n, (src, dst, peer, got) in enumerate(plan(in_refs, out_refs, x, y, c, count_only=False)):
        if peer is None:
            out.append((pltpu.make_async_copy(src, dst, send_sems.at[n]), None))
        else:
            mk = lambda s, d: pltpu.make_async_remote_copy(
                src_ref=s, dst_ref=d, send_sem=send_sems.at[n], recv_sem=recv_sems.at[n], device_id=peer,
                device_id_type=MESH)
            out.append((mk(src, dst), mk(got, got)))
    return out


def _start_all(copies):
    for mine, _ in copies:
        mine.start()


def _wait_all(copies):
    for mine, theirs in copies:
        if theirs is None:
            mine.wait()
        else:
            theirs.wait_recv()
            mine.wait_send()


def _ride_specs(ride):
    if ride is None:
        return [], [], [], [], []
    inputs, out_shapes, plan = ride
    count = plan([None] * len(inputs), [None] * len(out_shapes), 0, 0, 0, count_only=True)
    hbm = pl.BlockSpec(memory_space=pl.ANY)
    sems = [pltpu.SemaphoreType.DMA((count,)), pltpu.SemaphoreType.DMA((count,))]
    return list(inputs), [hbm] * len(inputs), list(out_shapes), [hbm] * len(out_shapes), sems


def _ride_along(ride, in_refs, out_refs, sems, first, last):
    if ride is None:
        return

    @pl.when(first)
    def _():
        _start_all(_copies(ride[2], in_refs, out_refs, *sems))

    @pl.when(last)
    def _():
        _wait_all(_copies(ride[2], in_refs, out_refs, *sems))


def _comm(name, inputs, out_shapes, aliases, plan):
    n_in, n_out = len(inputs), len(out_shapes)
    probe = plan([None] * n_in, [None] * n_out, 0, 0, 0, count_only=True)

    def body(*refs):
        copies = _copies(plan, refs[:n_in], refs[n_in:n_in + n_out], *refs[n_in + n_out:])
        _start_all(copies)
        _wait_all(copies)

    hbm = pl.BlockSpec(memory_space=pl.ANY)
    return pl.pallas_call(
        body, name=name, in_specs=[hbm] * n_in, out_specs=[hbm] * n_out, out_shape=out_shapes,
        input_output_aliases=aliases,
        scratch_shapes=[pltpu.SemaphoreType.DMA((probe,)), pltpu.SemaphoreType.DMA((probe,))],
    )(*inputs)


def _other_chips(x, y):
    return [(1 - x, y), (x, 1 - y), (1 - x, 1 - y)]


def gather_weights(shards, axes, carrier=None):
    def out_shape(sh, ax):
        l, r, c = sh.shape
        return jax.ShapeDtypeStruct((l, N_CHIPS, 2, r // 2, c) if ax == 0 else (l, 2, r // 2, N_CHIPS * c), BF16)

    def piece(ref, ax, chip, half, width):
        if ax == 0:
            return ref.at[:, chip, half]
        return ref.at[:, half, :, pl.ds(chip * width, width)]

    def plan1(in_refs, out_refs, x, y, c, count_only):
        if count_only:
            return 4 * len(shards)
        me, xfers = 2 * x + y, []
        for sh, ax, src, out in zip(shards, axes, in_refs, out_refs):
            rows, width = sh.shape[1] // 2, sh.shape[2]
            mine = src.at[:, pl.ds(c * rows, rows), :]
            xfers.append((mine, piece(out, ax, me, c, width), None, None))
            for cx, cy in _other_chips(x, y):
                xfers.append((mine, piece(out, ax, me, c, width), (cx, cy, c), piece(out, ax, 2 * cx + cy, c, width)))
        return xfers

    def half_of(ref, ax, half):
        return ref.at[:, :, half] if ax == 0 else ref.at[:, half]

    def plan2(in_refs, out_refs, x, y, c, count_only):
        if count_only:
            return len(shards)
        return [(half_of(out, ax, c), half_of(out, ax, c), (x, y, 1 - c), half_of(out, ax, 1 - c))
                for ax, out in zip(axes, out_refs)]

    shapes = [out_shape(sh, ax) for sh, ax in zip(shards, axes)]
    if carrier is None:
        carried, part = None, _comm("gather_chips", list(shards), shapes, {}, plan1)
    else:
        carried, part = carrier((list(shards), shapes, plan1))
    full = _comm("gather_cores", list(part), shapes, {n: n for n in range(len(shards))}, plan2)
    return [f.reshape(sh.shape[0], N_CHIPS * sh.shape[1], sh.shape[2]) if ax == 0
            else f.reshape(sh.shape[0], sh.shape[1], N_CHIPS * sh.shape[2])
            for f, sh, ax in zip(full, shards, axes)], carried


def reduce_scatter_grads(grads, axes, core, chip, carrier=None):
    dims, views = [], []
    for layers, ax in zip(grads, axes):
        r, c = layers[0].shape
        if ax == 0:
            dims.append((len(layers), r // N_CHIPS // 2, c))
            views.append([g.reshape(N_CHIPS, 2, r // N_CHIPS // 2, c) for g in layers])
        else:
            dims.append((len(layers), r // 2, c // N_CHIPS))
            views.append([g.reshape(1, 2, r // 2, c) for g in layers])

    def half_shape(ax, d):
        return (d[0], N_CHIPS, d[1], d[2]) if ax == 0 else (d[0], d[1], N_CHIPS * d[2])

    def plan_a(in_refs, out_refs, x, y, c, count_only):
        if count_only:
            return sum(d[0] for d in dims)
        xfers, n = [], 0
        for d, out in zip(dims, out_refs):
            for l in range(d[0]):
                xfers.append((in_refs[n].at[:, 1 - c], out.at[l], (x, y, 1 - c), out.at[l]))
                n += 1
        return xfers

    got = _comm("reduce_cores", [v for layers in views for v in layers],
                [jax.ShapeDtypeStruct((d[0], N_CHIPS if ax == 0 else 1, d[1], d[2] if ax == 0 else N_CHIPS * d[2]), F32)
                 for ax, d in zip(axes, dims)], {}, plan_a)
    chip_sum = [add_half(layers, g, core, name="sum_cores").reshape(half_shape(ax, d))
                for ax, d, layers, g in zip(axes, dims, views, got)]

    def block(ref, ax, which, width):
        return ref.at[:, which] if ax == 0 else ref.at[:, :, pl.ds(which * width, width)]

    def plan_b(in_refs, out_refs, x, y, c, count_only):
        if count_only:
            return 3 * len(grads)
        xfers = []
        for ax, d, src, out in zip(axes, dims, in_refs, out_refs):
            for k, (cx, cy) in enumerate(_other_chips(x, y)):
                xfers.append((block(src, ax, 2 * cx + cy, d[2]), out.at[k], (cx, cy, c), out.at[k]))
        return xfers

    part_shapes = [jax.ShapeDtypeStruct((3,) + d, BF16) for d in dims]
    if carrier is None:
        carried, parts = None, _comm("reduce_chips", chip_sum, part_shapes, {}, plan_b)
    else:
        carried, parts = carrier((chip_sum, part_shapes, plan_b))
    mine = [add_own_block(h, p, chip, ax, name="sum_chips") for h, p, ax in zip(chip_sum, parts, axes)]

    def plan_c(in_refs, out_refs, x, y, c, count_only):
        if count_only:
            return len(grads)
        return [(src, out, (x, y, 1 - c), out) for src, out in zip(in_refs, out_refs)]

    theirs = _comm("share_cores", mine, [jax.ShapeDtypeStruct(d, F32) for d in dims], {}, plan_c)
    return list(zip(mine, theirs)), carried


def gather_slabs(vec, *, name):
    def plan(in_refs, out_refs, x, y, c, count_only):
        if count_only:
            return 8
        me = 4 * x + 2 * y + c
        xfers = [(in_refs[0], out_refs[0].at[me], None, None)]
        for k in range(1, 8):
            px, py, pc = x ^ (k >> 2), y ^ ((k >> 1) & 1), c ^ (k & 1)
            xfers.append((in_refs[0], out_refs[0].at[me], (px, py, pc), out_refs[0].at[4 * px + 2 * py + pc]))
        return xfers

    return _comm(name, [vec], [jax.ShapeDtypeStruct((8,) + vec.shape, F32)], {}, plan)[0]


def allreduce_small(vec):
    return sum_slots(gather_slabs(vec, name="gather_small"), name="sum_small")


def gather_conv_w(block):
    l, taps, c = block.shape
    flat = block.reshape(-1)
    slabs = gather_slabs(jnp.pad(flat, (0, -flat.size % 1024)).reshape(-1, 128), name="gather_conv_w")
    per_chip = slabs[0::2].reshape(N_CHIPS, -1)[:, :flat.size].reshape(N_CHIPS, l, taps, c)
    return per_chip.transpose(1, 2, 0, 3).reshape(l, taps, N_CHIPS * c)


def _rope_tables(positions):
    inv_freq = ROPE_THETA ** (-jnp.arange(0, ROPE_DIM, 2, dtype=F32) / ROPE_DIM)
    ang = positions.astype(F32)[:, None] * inv_freq
    ang = jnp.concatenate([ang, ang], axis=-1)
    pad = ((0, 0), (0, HEAD - ROPE_DIM))
    return jnp.pad(jnp.cos(ang), pad), jnp.pad(jnp.sin(ang), pad)


def _uq_layout(w, heads):
    ql = w.shape[0]
    w = w.reshape(ql, heads, HEAD + ROPE_DIM)
    rot = jnp.pad(w[:, :, HEAD:], ((0, 0), (0, 0), (0, HEAD - ROPE_DIM)))
    return jnp.concatenate([w[:, :, :HEAD].reshape(ql, heads * HEAD), rot.reshape(ql, heads * HEAD)], axis=1)


def _uq_layout_inv(dw, heads):
    ql = dw.shape[0]
    nope = dw[:, :heads * HEAD].reshape(ql, heads, HEAD)
    rot = dw[:, heads * HEAD:].reshape(ql, heads, HEAD)[:, :, :ROPE_DIM]
    return jnp.concatenate([nope, rot], axis=-1).reshape(ql, heads * (HEAD + ROPE_DIM))


def kernel(x, mem, positions, norm_mix, norm_mem_q, norm_mem_kv, norm_ffn, norm_final, mla_w_down, mla_q_norm, mla_w_uq, mla_kv_norm, mla_w_ukv, mla_w_o, sb_w_qkv, sb_w_o, mem_w_q, mem_w_kv, mem_w_o, ffn_w_in, ffn_conv_w, ffn_conv_b, ffn_w_out, loss_target, m_norm_mix, m_norm_mem_q, m_norm_mem_kv, m_norm_ffn, m_norm_final, m_mla_w_down, m_mla_q_norm, m_mla_w_uq, m_mla_kv_norm, m_mla_w_ukv, m_mla_w_o, m_sb_w_qkv, m_sb_w_o, m_mem_w_q, m_mem_w_kv, m_mem_w_o, m_ffn_w_in, m_ffn_conv_w, m_ffn_conv_b, m_ffn_w_out, v_norm_mix, v_norm_mem_q, v_norm_mem_kv, v_norm_ffn, v_norm_final, v_mla_w_down, v_mla_q_norm, v_mla_w_uq, v_mla_kv_norm, v_mla_w_ukv, v_mla_w_o, v_sb_w_qkv, v_sb_w_o, v_mem_w_q, v_mem_w_kv, v_mem_w_o, v_ffn_w_in, v_ffn_conv_w, v_ffn_conv_b, v_ffn_w_out):
    args = dict(locals())
    big = ["mla_w_down", "mla_w_uq", "mla_w_ukv", "mla_w_o", "sb_w_qkv", "sb_w_o", "mem_w_q", "mem_w_kv",
           "mem_w_o", "ffn_w_in", "ffn_w_out"]
    col_cut = {"mla_w_uq", "mla_w_ukv", "sb_w_qkv", "mem_w_o", "ffn_w_in"}
    axes = [1 if n in col_cut else 0 for n in big]
    small = ["norm_mix", "norm_mem_q", "norm_mem_kv", "norm_ffn", "norm_final", "mla_q_norm", "mla_kv_norm",
             "ffn_conv_b", "ffn_conv_w"]
    order = ["norm_mix", "norm_mem_q", "norm_mem_kv", "norm_ffn", "norm_final", "mla_w_down", "mla_q_norm",
             "mla_w_uq", "mla_kv_norm", "mla_w_ukv", "mla_w_o", "sb_w_qkv", "sb_w_o", "mem_w_q", "mem_w_kv",
             "mem_w_o", "ffn_w_in", "ffn_conv_w", "ffn_conv_b", "ffn_w_out"]

    xs, mems, target = x[0], mem[0], loss_target[0]
    s, d = xs.shape
    depth = norm_mix.shape[0]
    ql, kvl = mla_q_norm.shape[1], mla_kv_norm.shape[1]
    mla_heads = N_CHIPS * mla_w_uq.shape[2] // (HEAD + ROPE_DIM)
    sb_heads = N_CHIPS * sb_w_qkv.shape[2] // (3 * HEAD)
    mem_heads = mem_w_q.shape[2] // HEAD
    ff = N_CHIPS * ffn_w_out.shape[1]
    chip = 2 * lax.axis_index("x") + lax.axis_index("y")
    chip_op = jnp.reshape(chip, (1,)).astype(jnp.int32)
    core_op = jnp.reshape(lax.axis_index("c"), (1,)).astype(jnp.int32)

    axis_of = dict(zip(big, axes))
    early = [n for n in big if not n.startswith("sb_")]
    late_layers = {n: list(range(0 if n.startswith("sb_") else 1, args[n].shape[0])) for n in big}
    whole = {n: {} for n in big}
    early_full, _ = gather_weights([args[n][0:1].astype(BF16) for n in early], [axis_of[n] for n in early])
    for n, f in zip(early, early_full):
        whole[n][0] = f[0]
    w_uq = {0: _uq_layout(whole["mla_w_uq"][0], mla_heads)}
    cos, sin = _rope_tables(positions[0])
    _, sb_tk = _sb_tiles(s)
    tri = (jnp.arange(sb_tk)[:, None] > jnp.arange(sb_tk)[None, :]).astype(BF16)
    conv_w = gather_conv_w(ffn_conv_w)

    saved = []
    xa = xs
    for i in range(depth):
        j = i // 2
        lay = {}
        lay["xa"] = xa
        h1 = rmsnorm_fwd(xa, norm_mix[i], name="norm_fwd")
        lay["h1"] = h1
        if i % 2 == 0:
            down = mm(h1, whole["mla_w_down"][j], out_dtype=F32, name="mm_down")
            cq = rmsnorm_fwd(down[:, :ql], mla_q_norm[j], name="norm_lora_fwd")
            ckv = rmsnorm_fwd(down[:, ql:ql + kvl], mla_kv_norm[j], name="norm_lora_fwd")
            kr_raw = jnp.pad(down[:, ql + kvl:], ((0, 0), (0, HEAD - ROPE_DIM)))
            kr = rope(kr_raw, cos, sin, col0=0, n=1, inverse=False, name="rope_k")
            qall = mm(cq, w_uq[j], name="mm_uq")
            qr = rope(qall, cos, sin, col0=mla_heads, n=mla_heads, inverse=False, name="rope_q")
            kv = mm(ckv, whole["mla_w_ukv"][j], name="mm_ukv")
            if i == 0:
                def attend_and_gather(ride, att=(qall, qr, kv, kr)):
                    o_, lse_, part = mla_fwd(*att, mla_heads, name="mla_fwd_gather", ride=ride)
                    return (o_, lse_), part

                late_full, (o, lse) = gather_weights(
                    [args[n][late_layers[n][0]:].astype(BF16) for n in big], axes, attend_and_gather)
                for n, f in zip(big, late_full):
                    for at, l in enumerate(late_layers[n]):
                        whole[n][l] = f[at]
                for l in late_layers["mla_w_uq"]:
                    w_uq[l] = _uq_layout(whole["mla_w_uq"][l], mla_heads)
            else:
                o, lse = mla_fwd(qall, qr, kv, kr, mla_heads, name="mla_fwd")
            xb = mm(o, whole["mla_w_o"][j], add=xa, out_dtype=F32, name="mm_out_res")
            lay.update(down=down, cq=cq, ckv=ckv, qall=qall, qr=qr, kv=kv, kr=kr, o=o, lse=lse)
        else:
            qkv = mm(h1, whole["sb_w_qkv"][j], name="mm_qkv")
            o, cmat = sb_fwd(qkv, tri, sb_heads, name="sb_fwd")
            xb = mm(o, whole["sb_w_o"][j], add=xa, out_dtype=F32, name="mm_out_res")
            lay.update(qkv=qkv, o=o, cmat=cmat)
        h2 = rmsnorm_fwd(xb, norm_mem_q[i], name="norm_fwd")
        hm = rmsnorm_fwd(mems, norm_mem_kv[i], name="norm_mem_fwd")
        qm = mm(h2, whole["mem_w_q"][i], name="mm_mem_q")
        kvm = mm(hm, whole["mem_w_kv"][i], name="mm_mem_kv")
        om = mem_fwd(qm, kvm, mem_heads, name="mem_fwd")
        xc = mm(om, whole["mem_w_o"][i], add=xb, out_dtype=F32, name="mm_mem_out_res")
        h3 = rmsnorm_fwd(xc, norm_ffn[i], name="norm_fwd")
        u = mm(h3, whole["ffn_w_in"][i], name="mm_ffn_in")
        act = conv_gate_fwd(u, conv_w[i], ffn_conv_b[i][None, :], name="conv_gate_fwd")
        xd = mm(act, whole["ffn_w_out"][i], add=xc, out_dtype=F32, name="mm_ffn_out_res")
        lay.update(xb=xb, h2=h2, hm=hm, qm=qm, kvm=kvm, om=om, xc=xc, h3=h3, u=u, act=act)
        saved.append(lay)
        xa = xd

    dx, dxb, g_final, loss_part = final_loss(xa, norm_final, target, name="final_loss")

    gw = {n: [None] * args[n].shape[0] for n in big}
    g_small = {"norm_mix": [None] * depth, "norm_mem_q": [None] * depth, "norm_mem_kv": [None] * depth,
               "norm_ffn": [None] * depth, "mla_q_norm": [None] * (depth - depth // 2),
               "mla_kv_norm": [None] * (depth - depth // 2), "conv": [None] * depth}
    for i in reversed(range(depth)):
        j = i // 2
        lay = saved[i]
        gw["ffn_w_out"][i] = mm(lay["act"], dxb, mode="tn", out_dtype=F32, name="mm_ffn_out_wgrad")
        da = mm(dxb, whole["ffn_w_out"][i], mode="nt", name="mm_ffn_out_dgrad")
        dcg, dcu, g_small["conv"][i] = conv_gate_bwd(lay["u"], da, conv_w[i], ffn_conv_b[i][None, :], name="conv_gate_bwd")
        du = jnp.concatenate([conv_transpose(dcg, conv_w[i][:, :ff], name="conv_transpose"),
                              conv_transpose(dcu, conv_w[i][:, ff:], name="conv_transpose")], axis=1)
        gw["ffn_w_in"][i] = mm(lay["h3"], du, mode="tn", out_dtype=F32, name="mm_ffn_in_wgrad")
        dh3 = mm(du, whole["ffn_w_in"][i], mode="nt", name="mm_ffn_in_dgrad")
        dx, dxb, g_small["norm_ffn"][i] = rmsnorm_bwd(dh3, lay["xc"], norm_ffn[i], dx, name="norm_bwd")
        gw["mem_w_o"][i] = mm(lay["om"], dxb, mode="tn", out_dtype=F32, name="mm_mem_out_wgrad")
        dom = mm(dxb, whole["mem_w_o"][i], mode="nt", name="mm_mem_out_dgrad")
        dqm, dkm, dvm = mem_bwd(lay["qm"], lay["kvm"], dom, mem_heads, name="mem_bwd")
        dkvm = jnp.concatenate([dkm, dvm], axis=1)
        gw["mem_w_q"][i] = mm(lay["h2"], dqm, mode="tn", out_dtype=F32, name="mm_mem_q_wgrad")
        gw["mem_w_kv"][i] = mm(lay["hm"], dkvm, mode="tn", out_dtype=F32, name="mm_mem_kv_wgrad")
        dh2 = mm(dqm, whole["mem_w_q"][i], mode="nt", name="mm_mem_q_dgrad")
        dhm = mm(dkvm, whole["mem_w_kv"][i], mode="nt", name="mm_mem_kv_dgrad")
        _, _, g_small["norm_mem_kv"][i] = rmsnorm_bwd(dhm, mems, norm_mem_kv[i], name="norm_mem_bwd")
        dx, dxb, g_small["norm_mem_q"][i] = rmsnorm_bwd(dh2, lay["xb"], norm_mem_q[i], dx, name="norm_bwd")
        if i % 2 == 0:
            gw["mla_w_o"][j] = mm(lay["o"], dxb, mode="tn", out_dtype=F32, name="mm_out_wgrad")
            do = mm(dxb, whole["mla_w_o"][j], mode="nt", name="mm_out_dgrad")
            att = (lay["qall"], lay["qr"], lay["kv"], lay["kr"], lay["o"], do, lay["lse"], mla_heads)
            if i == 0:
                def attend_and_reduce(ride, att=att):
                    *dqkv_, parts_ = mla_bwd(*att, name="mla_bwd_reduce", ride=ride)
                    return dqkv_, parts_

                halves_late, (dqn, dqr, dkv, dkr_heads) = reduce_scatter_grads(
                    [gw[n][1:] if n.startswith("mla_") else gw[n] for n in big], axes, core_op, chip_op,
                    attend_and_reduce)
            else:
                dqn, dqr, dkv, dkr_heads = mla_bwd(*att, name="mla_bwd")
            dkr = sum_lane_tiles(dkr_heads, mla_heads, name="sum_heads")
            dqall = jnp.concatenate([dqn, rope(dqr, cos, sin, col0=0, n=mla_heads, inverse=True, name="rope_q_bwd")], axis=1)
            gw["mla_w_ukv"][j] = mm(lay["ckv"], dkv, mode="tn", out_dtype=F32, name="mm_ukv_wgrad")
            dckv = mm(dkv, whole["mla_w_ukv"][j], mode="nt", name="mm_ukv_dgrad")
            gw["mla_w_uq"][j] = _uq_layout_inv(mm(lay["cq"], dqall, mode="tn", out_dtype=F32, name="mm_uq_wgrad"), mla_heads)
            dcq = mm(dqall, w_uq[j], mode="nt", name="mm_uq_dgrad")
            down = lay["down"]
            _, d_q, g_small["mla_q_norm"][j] = rmsnorm_bwd(dcq, down[:, :ql], mla_q_norm[j], name="norm_lora_bwd")
            _, d_kv, g_small["mla_kv_norm"][j] = rmsnorm_bwd(dckv, down[:, ql:ql + kvl], mla_kv_norm[j], name="norm_lora_bwd")
            d_kr = rope(dkr, cos, sin, col0=0, n=1, inverse=True, name="rope_k_bwd")[:, :ROPE_DIM]
            ddown = jnp.concatenate([d_q, d_kv, d_kr], axis=1)
            gw["mla_w_down"][j] = mm(lay["h1"], ddown, mode="tn", out_dtype=F32, name="mm_down_wgrad")
            dh1 = mm(ddown, whole["mla_w_down"][j], mode="nt", name="mm_down_dgrad")
        else:
            gw["sb_w_o"][j] = mm(lay["o"], dxb, mode="tn", out_dtype=F32, name="mm_out_wgrad")
            do = mm(dxb, whole["sb_w_o"][j], mode="nt", name="mm_out_dgrad")
            dq, dk, dv = sb_bwd(lay["qkv"], do, lay["cmat"], tri, sb_heads, name="sb_bwd")
            dqkv = jnp.concatenate([dq, dk, dv], axis=1)
            gw["sb_w_qkv"][j] = mm(lay["h1"], dqkv, mode="tn", out_dtype=F32, name="mm_qkv_wgrad")
            dh1 = mm(dqkv, whole["sb_w_qkv"][j], mode="nt", name="mm_qkv_dgrad")
        dx, dxb, g_small["norm_mix"][i] = rmsnorm_bwd(dh1, lay["xa"], norm_mix[i], dx, name="norm_bwd")

    halves = dict(zip(big, halves_late))
    mla_names = [n for n in big if n.startswith("mla_")]
    halves_first, _ = reduce_scatter_grads([gw[n][:1] for n in mla_names], [axis_of[n] for n in mla_names],
                                           core_op, chip_op)
    for n, first in zip(mla_names, halves_first):
        halves[n] = tuple(jnp.concatenate([a, b], axis=0) for a, b in zip(first, halves[n]))
    conv = jnp.stack(g_small["conv"])
    parts = [jnp.concatenate(g_small[n], axis=0) for n in ("norm_mix", "norm_mem_q", "norm_mem_kv", "norm_ffn")]
    parts += [g_final, jnp.concatenate(g_small["mla_q_norm"], axis=0), jnp.concatenate(g_small["mla_kv_norm"], axis=0),
              conv[:, 3, :], conv[:, :3, :], loss_part[:1, :1]]
    sizes = [p.size for p in parts]
    packed = jnp.concatenate([p.reshape(-1) for p in parts])
    packed = jnp.pad(packed, (0, -packed.size % 1024)).reshape(-1, 128)
    total = allreduce_small(packed).reshape(-1)
    g_rep, at = {}, 0
    for n, p, size in zip(small + ["loss"], parts, sizes):
        g_rep[n] = total[at:at + size].reshape(p.shape)
        at += size
    loss = g_rep.pop("loss").reshape(())
    g_rep["norm_final"] = g_rep["norm_final"].reshape(norm_final.shape)
    width = ffn_conv_w.shape[2]
    g_rep["ffn_conv_w"] = lax.dynamic_slice_in_dim(g_rep["ffn_conv_w"], chip * width, width, axis=2)

    grads, delta, new_m, new_v = {}, {}, {}, {}
    for n in order:
        if n in halves:
            mine, theirs = halves[n]
            grads[n], delta[n], new_m[n], new_v[n] = adamw_halves(
                args[n], mine, theirs, args["m_" + n], args["v_" + n], core_op, name="adamw_big")
        else:
            grads[n] = g_rep[n]
            delta[n], new_m[n], new_v[n] = adamw(args[n], g_rep[n], args["m_" + n], args["v_" + n], name="adamw")
    return (loss, dx[None], *[grads[n] for n in order], *[delta[n] for n in order],
            *[new_m[n] for n in order], *[new_v[n] for n in order])
```

```python
import math

import jax
import jax.numpy as jnp
import numpy as np
from jax import lax
from jax.experimental import pallas as pl
from jax.experimental.pallas import tpu as pltpu

F32 = jnp.float32
BF16 = jnp.bfloat16
MESH = pl.DeviceIdType.MESH

EPS = 1e-6
LOG2E = 1.4426950408889634
CHUNK_SHIFT = 6
HEAD = 128
ROPE_DIM = 64
ROPE_THETA = 10000.0
N_CHIPS = 4
ADAM_LR, ADAM_B1, ADAM_B2, ADAM_EPS, ADAM_WD, ADAM_STEP = 0.001, 0.9, 0.999, 1e-08, 0.01, 10

VMEM_LIMIT_BYTES = 48 * 1024 * 1024
MM_TM, MM_TN, MM_TK = (1024, 1408, 512, 256, 128), (1024, 1408, 512, 256, 128), (1024, 512, 256, 128)
MM_WHOLE_K, MM_WHOLE_TM = 2048, (512, 256, 128)
ROW_TILE = (256, 128, 64, 32, 16, 8)
ATT_T = 512
MLA_HEADS_PER_STEP, MLA_BWD_HEADS_PER_STEP = 4, 2
SB_TQ, SB_TK = 512, 256
SB_HEADS_PER_STEP = 2
SB_SUBS_PER_STEP = 4
SB_DEAD_LOG2 = -200.0
SB_UNVISITED = -1e30
MEM_TQ = 512
CONV_TM, CONV_TN = 512, (512, 256, 128)
HALO = 16


def _pick(dim, prefs):
    for p in prefs:
        if dim % p == 0:
            return p
    return dim


def _dot(a, b, ca, cb):
    return lax.dot_general(a, b, (((ca,), (cb,)), ((), ())), preferred_element_type=F32)


def _params(sem):
    return pltpu.CompilerParams(dimension_semantics=sem, vmem_limit_bytes=VMEM_LIMIT_BYTES)


def _head_groups(heads, per_step):
    hp = per_step if heads % per_step == 0 else 1
    return hp, heads // hp, hp * HEAD


def _tables(pairs):
    arr = np.asarray(pairs, dtype=np.int32)
    return jnp.asarray(arr[:, 0]), jnp.asarray(arr[:, 1])


def mm(a, b, *, mode="nn", add=None, out_dtype=BF16, name):
    if mode == "nn":
        (m, k), (k2, n) = a.shape, b.shape
    elif mode == "nt":
        (m, k), (n, k2) = a.shape, b.shape
    else:
        (k, m), (k2, n) = a.shape, b.shape
    assert k == k2, (a.shape, b.shape, mode)
    ca, cb = {"nn": (1, 0), "nt": (1, 1), "tn": (0, 0)}[mode]
    if k <= MM_WHOLE_K:
        return _mm_whole_k(a, b, add, (m, n, k), (ca, cb), mode, out_dtype, name)
    tm, tn, tk = _pick(m, MM_TM), _pick(n, MM_TN), _pick(k, MM_TK)
    nk = k // tk

    def body(a_ref, b_ref, *rest):
        if add is None:
            o_ref, acc_ref = rest
        else:
            add_ref, o_ref, acc_ref = rest
        kk = pl.program_id(2)

        @pl.when(kk == 0)
        def _():
            acc_ref[...] = jnp.zeros_like(acc_ref)

        acc_ref[...] += _dot(a_ref[...].astype(BF16), b_ref[...].astype(BF16), ca, cb)

        @pl.when(kk == nk - 1)
        def _():
            r = acc_ref[...]
            if add is not None:
                r = r + add_ref[...]
            o_ref[...] = r.astype(o_ref.dtype)

    if mode == "tn":
        a_spec = pl.BlockSpec((tk, tm), lambda i, j, kk: (kk, i))
    else:
        a_spec = pl.BlockSpec((tm, tk), lambda i, j, kk: (i, kk))
    if mode == "nt":
        b_spec = pl.BlockSpec((tn, tk), lambda i, j, kk: (j, kk))
    else:
        b_spec = pl.BlockSpec((tk, tn), lambda i, j, kk: (kk, j))
    o_spec = pl.BlockSpec((tm, tn), lambda i, j, kk: (i, j))
    in_specs, args = [a_spec, b_spec], [a, b]
    if add is not None:
        in_specs.append(o_spec)
        args.append(add)
    return pl.pallas_call(
        body, name=name, grid=(m // tm, n // tn, nk), in_specs=in_specs, out_specs=o_spec,
        out_shape=jax.ShapeDtypeStruct((m, n), out_dtype), scratch_shapes=[pltpu.VMEM((tm, tn), F32)],
        compiler_params=_params(("parallel", "parallel", "arbitrary")),
    )(*args)


def _mm_whole_k(a, b, add, mnk, contract, mode, out_dtype, name):
    m, n, k = mnk
    tm, tn = _pick(m, MM_WHOLE_TM), _pick(n, MM_TN)

    def body(a_ref, b_ref, *rest):
        r = _dot(a_ref[...].astype(BF16), b_ref[...].astype(BF16), *contract)
        if add is not None:
            r = r + rest[0][...]
        rest[-1][...] = r.astype(rest[-1].dtype)

    a_spec = pl.BlockSpec((k, tm), lambda j, i: (0, i)) if mode == "tn" else pl.BlockSpec((tm, k), lambda j, i: (i, 0))
    b_spec = pl.BlockSpec((tn, k), lambda j, i: (j, 0)) if mode == "nt" else pl.BlockSpec((k, tn), lambda j, i: (0, j))
    o_spec = pl.BlockSpec((tm, tn), lambda j, i: (i, j))
    in_specs, args = [a_spec, b_spec], [a, b]
    if add is not None:
        in_specs.append(o_spec)
        args.append(add)
    return pl.pallas_call(
        body, name=name, grid=(n // tn, m // tm), in_specs=in_specs, out_specs=o_spec,
        out_shape=jax.ShapeDtypeStruct((m, n), out_dtype), compiler_params=_params(("parallel", "parallel")),
    )(*args)


def rmsnorm_fwd(x, g, *, name):
    m, d = x.shape
    tm = _pick(m, ROW_TILE)

    def body(x_ref, g_ref, o_ref):
        xv = x_ref[...]
        r = lax.rsqrt(jnp.mean(xv * xv, axis=-1, keepdims=True) + EPS)
        o_ref[...] = (xv * r * g_ref[...]).astype(o_ref.dtype)

    return pl.pallas_call(
        body, name=name, grid=(m // tm,),
        in_specs=[pl.BlockSpec((tm, d), lambda i: (i, 0)), pl.BlockSpec((1, d), lambda i: (0, 0))],
        out_specs=pl.BlockSpec((tm, d), lambda i: (i, 0)), out_shape=jax.ShapeDtypeStruct((m, d), BF16),
        compiler_params=_params(("parallel",)),
    )(x, g.reshape(1, d))


def rmsnorm_bwd(dh, x, g, res=None, *, name):
    m, d = x.shape
    tm = _pick(m, ROW_TILE)

    def body(dh_ref, x_ref, g_ref, *rest):
        if res is None:
            dx_ref, dxb_ref, dg_ref = rest
        else:
            res_ref, dx_ref, dxb_ref, dg_ref = rest

        @pl.when(pl.program_id(0) == 0)
        def _():
            dg_ref[...] = jnp.zeros_like(dg_ref)

        xv = x_ref[...]
        dhv = dh_ref[...].astype(F32)
        r = lax.rsqrt(jnp.mean(xv * xv, axis=-1, keepdims=True) + EPS)
        y = xv * r
        dhg = dhv * g_ref[...]
        dx = r * (dhg - y * jnp.mean(dhg * y, axis=-1, keepdims=True))
        if res is not None:
            dx = dx + res_ref[...]
        dx_ref[...] = dx
        dxb_ref[...] = dx.astype(BF16)
        dg_ref[...] += jnp.sum(dhv * y, axis=0, keepdims=True)

    row = pl.BlockSpec((tm, d), lambda i: (i, 0))
    vec = pl.BlockSpec((1, d), lambda i: (0, 0))
    in_specs, args = [row, row, vec], [dh, x, g.reshape(1, d)]
    if res is not None:
        in_specs.append(row)
        args.append(res)
    return pl.pallas_call(
        body, name=name, grid=(m // tm,), in_specs=in_specs, out_specs=[row, row, vec],
        out_shape=[jax.ShapeDtypeStruct((m, d), F32), jax.ShapeDtypeStruct((m, d), BF16),
                   jax.ShapeDtypeStruct((1, d), F32)],
        compiler_params=_params(("arbitrary",)),
    )(*args)


def final_loss(x, g, target, *, name):
    m, d = x.shape
    tm = _pick(m, ROW_TILE)

    def body(x_ref, g_ref, t_ref, dx_ref, dxb_ref, dg_ref, loss_ref):
        @pl.when(pl.program_id(0) == 0)
        def _():
            dg_ref[...] = jnp.zeros_like(dg_ref)
            loss_ref[...] = jnp.zeros_like(loss_ref)

        xv = x_ref[...]
        gv = g_ref[...]
        r = lax.rsqrt(jnp.mean(xv * xv, axis=-1, keepdims=True) + EPS)
        y = xv * r
        err = y * gv - t_ref[...]
        loss_ref[...] += 0.5 * jnp.sum(jnp.mean(err * err, axis=-1, keepdims=True))
        dy = err * (1.0 / d)
        dyg = dy * gv
        dx = r * (dyg - y * jnp.mean(dyg * y, axis=-1, keepdims=True))
        dx_ref[...] = dx
        dxb_ref[...] = dx.astype(BF16)
        dg_ref[...] += jnp.sum(dy * y, axis=0, keepdims=True)

    row = pl.BlockSpec((tm, d), lambda i: (i, 0))
    vec = pl.BlockSpec((1, d), lambda i: (0, 0))
    return pl.pallas_call(
        body, name=name, grid=(m // tm,), in_specs=[row, vec, row],
        out_specs=[row, row, vec, pl.BlockSpec((8, 128), lambda i: (0, 0))],
        out_shape=[jax.ShapeDtypeStruct((m, d), F32), jax.ShapeDtypeStruct((m, d), BF16),
                   jax.ShapeDtypeStruct((1, d), F32), jax.ShapeDtypeStruct((8, 128), F32)],
        compiler_params=_params(("arbitrary",)),
    )(x, g.reshape(1, d), target)


def rope(xin, cos, sin, *, col0, n, inverse, name):
    s = xin.shape[0]
    tm = _pick(s, ROW_TILE)
    half = ROPE_DIM // 2

    wide = n * HEAD
    assert col0 % n == 0 and wide & (wide - 1) == 0

    def body(x_ref, c_ref, s_ref, o_ref):
        xv = x_ref[...].astype(F32)
        lane = lax.broadcasted_iota(jnp.int32, xv.shape, 1) & (HEAD - 1)
        rot = jnp.where(lane < half, -pltpu.roll(xv, wide - half, 1), pltpu.roll(xv, half, 1))
        sv = jnp.tile(s_ref[...], (1, n))
        if inverse:
            sv = -sv
        o_ref[...] = (xv * jnp.tile(c_ref[...], (1, n)) + rot * sv).astype(o_ref.dtype)

    tab = pl.BlockSpec((tm, HEAD), lambda i: (i, 0))
    return pl.pallas_call(
        body, name=name, grid=(s // tm,),
        in_specs=[pl.BlockSpec((tm, wide), lambda i: (i, col0 // n)), tab, tab],
        out_specs=pl.BlockSpec((tm, wide), lambda i: (i, 0)),
        out_shape=jax.ShapeDtypeStruct((s, wide), BF16),
        compiler_params=_params(("parallel",)),
    )(xin, cos, sin)


def sum_lane_tiles(xin, n, *, name):
    s = xin.shape[0]
    tm = _pick(s, ROW_TILE)

    def body(x_ref, o_ref):
        acc = x_ref[:, :HEAD]
        for k in range(1, n):
            acc = acc + x_ref[:, k * HEAD:(k + 1) * HEAD]
        o_ref[...] = acc

    return pl.pallas_call(
        body, name=name, grid=(s // tm,), in_specs=[pl.BlockSpec((tm, n * HEAD), lambda i: (i, 0))],
        out_specs=pl.BlockSpec((tm, HEAD), lambda i: (i, 0)), out_shape=jax.ShapeDtypeStruct((s, HEAD), F32),
        compiler_params=_params(("parallel",)),
    )(xin)


def _mla_scores(q, k, scale, diagonal):
    sc = _dot(q, k, 1, 1) * scale
    if not diagonal:
        return sc
    qchunk = jnp.right_shift(lax.broadcasted_iota(jnp.int32, sc.shape, 0), CHUNK_SHIFT)
    kchunk = jnp.right_shift(lax.broadcasted_iota(jnp.int32, sc.shape, 1), CHUNK_SHIFT)
    return jnp.where(kchunk <= qchunk, sc, -jnp.inf)


def mla_fwd(qall, qr, kv, kr, heads, *, name, ride=None):
    s = qr.shape[0]
    t = min(ATT_T, s)
    nq = s // t
    scale = 1.0 / math.sqrt(HEAD + ROPE_DIM)
    pairs = [(qi, kj) for qi in range(nq) for kj in range(qi + 1)]
    qtab, ktab = _tables(pairs)
    hp, groups, wide = _head_groups(heads, MLA_HEADS_PER_STEP)
    r_in, r_in_specs, r_out, r_out_specs, r_sems = _ride_specs(ride)

    def body(qt_ref, kt_ref, qn_ref, qr_ref, kv_ref, kr_ref, *refs):
        o_ref, lse_ref = refs[len(r_in):len(r_in) + 2]
        m_ref, l_ref, acc_ref = refs[len(r_in) + 2 + len(r_out):len(r_in) + 5 + len(r_out)]
        st = pl.program_id(1)
        qi, kj = qt_ref[st], kt_ref[st]
        _ride_along(ride, refs[:len(r_in)], refs[len(r_in) + 2:len(r_in) + 2 + len(r_out)], refs[len(refs) - 2:],
                    (pl.program_id(0) == 0) & (st == 0), (pl.program_id(0) == groups - 1) & (st == len(pairs) - 1))

        @pl.when(kj == 0)
        def _():
            m_ref[...] = jnp.full_like(m_ref, -jnp.inf)
            l_ref[...] = jnp.zeros_like(l_ref)
            acc_ref[...] = jnp.zeros_like(acc_ref)

        def step(diagonal):
            for hh in range(hp):
                cols = slice(hh * HEAD, (hh + 1) * HEAD)
                kn = kv_ref[:, 2 * hh * HEAD:(2 * hh + 1) * HEAD]
                v = kv_ref[:, (2 * hh + 1) * HEAD:(2 * hh + 2) * HEAD]
                sc = _mla_scores(jnp.concatenate([qn_ref[:, cols], qr_ref[:, cols]], axis=1),
                                 jnp.concatenate([kn, kr_ref[...]], axis=1), scale, diagonal)
                m_prev = m_ref[hh]
                m_new = jnp.maximum(m_prev, jnp.max(sc, axis=-1, keepdims=True))
                p = jnp.exp(sc - m_new)
                alpha = jnp.exp(m_prev - m_new)
                pv = _dot(p.astype(BF16), jnp.concatenate([v, jnp.ones_like(v)], axis=1), 1, 0)
                l_ref[hh] = alpha * l_ref[hh] + pv[:, HEAD:HEAD + 1]
                acc_ref[:, cols] = alpha * acc_ref[:, cols] + pv[:, :HEAD]
                m_ref[hh] = m_new

        @pl.when(kj < qi)
        def _():
            step(False)

        @pl.when(kj == qi)
        def _():
            step(True)
            for hh in range(hp):
                cols = slice(hh * HEAD, (hh + 1) * HEAD)
                o_ref[:, cols] = (acc_ref[:, cols] / l_ref[hh]).astype(o_ref.dtype)
                lse_ref[:, cols] = jnp.broadcast_to(m_ref[hh] + jnp.log(l_ref[hh]), (t, HEAD))

    qspec = pl.BlockSpec((t, wide), lambda h, st, qt, kt: (qt[st], h))
    grid_spec = pltpu.PrefetchScalarGridSpec(
        num_scalar_prefetch=2, grid=(groups, len(pairs)),
        in_specs=[qspec, qspec,
                  pl.BlockSpec((t, 2 * wide), lambda h, st, qt, kt: (kt[st], h)),
                  pl.BlockSpec((t, HEAD), lambda h, st, qt, kt: (kt[st], 0))] + r_in_specs,
        out_specs=[qspec, qspec] + r_out_specs,
        scratch_shapes=[pltpu.VMEM((hp, t, 1), F32), pltpu.VMEM((hp, t, 1), F32), pltpu.VMEM((t, wide), F32)] + r_sems)
    outs = pl.pallas_call(
        body, name=name, grid_spec=grid_spec,
        out_shape=[jax.ShapeDtypeStruct((s, heads * HEAD), BF16), jax.ShapeDtypeStruct((s, heads * HEAD), F32)] + r_out,
        compiler_params=_params(("arbitrary", "arbitrary") if ride else ("parallel", "arbitrary")),
    )(qtab, ktab, qall, qr, kv, kr, *r_in)
    return outs if ride is None else (outs[0], outs[1], list(outs[2:]))


def mla_bwd(qall, qr, kv, kr, o, do, lse, heads, *, name, ride=None):
    s = qr.shape[0]
    t = min(ATT_T, s)
    nq = s // t
    scale = 1.0 / math.sqrt(HEAD + ROPE_DIM)
    pairs = [(kj, qi) for kj in range(nq) for qi in range(kj, nq)]
    ktab, qtab = _tables(pairs)
    last = len(pairs) - 1
    hp, groups, wide = _head_groups(heads, MLA_BWD_HEADS_PER_STEP)
    r_in, r_in_specs, r_out, r_out_specs, r_sems = _ride_specs(ride)

    def body(kt_ref, qt_ref, qn_ref, qr_ref, kv_ref, kr_ref, o_ref, do_ref, lse_ref, *refs):
        n_i, n_o = len(r_in), len(r_out)
        dqn_ref, dqr_ref, dkv_ref, dkr_ref = refs[n_i:n_i + 4]
        fn_ref, fr_ref, akn_ref, av_ref, akr_ref = refs[n_i + 4 + n_o:n_i + 9 + n_o]
        st = pl.program_id(1)
        kj, qi = kt_ref[st], qt_ref[st]
        _ride_along(ride, refs[:n_i], refs[n_i + 4:n_i + 4 + n_o], refs[len(refs) - 2:],
                    (pl.program_id(0) == 0) & (st == 0), (pl.program_id(0) == groups - 1) & (st == last))

        @pl.when(st == 0)
        def _():
            fn_ref[...] = jnp.zeros_like(fn_ref)
            fr_ref[...] = jnp.zeros_like(fr_ref)

        @pl.when(qi == kj)
        def _():
            akn_ref[...] = jnp.zeros_like(akn_ref)
            av_ref[...] = jnp.zeros_like(av_ref)
            akr_ref[...] = jnp.zeros_like(akr_ref)

        def step(diagonal):
            rows = pl.ds(pl.multiple_of(qi * t, t), t)
            krv = kr_ref[...]
            for hh in range(hp):
                cols = slice(hh * HEAD, (hh + 1) * HEAD)
                dov = do_ref[:, cols]
                q2 = jnp.concatenate([qn_ref[:, cols], qr_ref[:, cols]], axis=1)
                k2 = jnp.concatenate([kv_ref[:, 2 * hh * HEAD:(2 * hh + 1) * HEAD], krv], axis=1)
                v = kv_ref[:, (2 * hh + 1) * HEAD:(2 * hh + 2) * HEAD]
                sc = _mla_scores(q2, k2, scale, diagonal)
                p = jnp.exp(sc - lse_ref[:, hh * HEAD:hh * HEAD + 1])
                delta = jnp.sum(dov.astype(F32) * o_ref[:, cols].astype(F32), axis=-1, keepdims=True)
                dp = _dot(dov, v, 1, 1)
                ds = (p * (dp - delta) * scale).astype(BF16)
                av_ref[:, cols] += _dot(p.astype(BF16), dov, 0, 0)
                dk2 = _dot(ds, q2, 0, 0)
                akn_ref[:, cols] += dk2[:, :HEAD]
                akr_ref[:, cols] += dk2[:, HEAD:]
                dq2 = _dot(ds, k2, 1, 0)
                fn_ref[rows, cols] += dq2[:, :HEAD]
                fr_ref[rows, cols] += dq2[:, HEAD:]

        @pl.when(qi == kj)
        def _():
            step(True)

        @pl.when(qi > kj)
        def _():
            step(False)

        @pl.when(qi == nq - 1)
        def _():
            for hh in range(hp):
                cols = slice(hh * HEAD, (hh + 1) * HEAD)
                dkv_ref[:, 2 * hh * HEAD:(2 * hh + 1) * HEAD] = akn_ref[:, cols].astype(dkv_ref.dtype)
                dkv_ref[:, (2 * hh + 1) * HEAD:(2 * hh + 2) * HEAD] = av_ref[:, cols].astype(dkv_ref.dtype)
            dkr_ref[...] = akr_ref[...]

        @pl.when(st == last)
        def _():
            dqn_ref[...] = fn_ref[...].astype(dqn_ref.dtype)
            dqr_ref[...] = fr_ref[...].astype(dqr_ref.dtype)

    qspec = pl.BlockSpec((t, wide), lambda h, st, kt, qt: (qt[st], h))
    kvspec = pl.BlockSpec((t, 2 * wide), lambda h, st, kt, qt: (kt[st], h))
    krspec = pl.BlockSpec((t, HEAD), lambda h, st, kt, qt: (kt[st], 0))
    headspec = pl.BlockSpec((s, wide), lambda h, st, kt, qt: (0, h))
    grid_spec = pltpu.PrefetchScalarGridSpec(
        num_scalar_prefetch=2, grid=(groups, len(pairs)),
        in_specs=[qspec, qspec, kvspec, krspec, qspec, qspec, qspec] + r_in_specs,
        out_specs=[headspec, headspec, kvspec, pl.BlockSpec((t, wide), lambda h, st, kt, qt: (kt[st], h))] + r_out_specs,
        scratch_shapes=[pltpu.VMEM((s, wide), F32), pltpu.VMEM((s, wide), F32), pltpu.VMEM((t, wide), F32),
                        pltpu.VMEM((t, wide), F32), pltpu.VMEM((t, wide), F32)] + r_sems)
    outs = pl.pallas_call(
        body, name=name, grid_spec=grid_spec,
        out_shape=[jax.ShapeDtypeStruct((s, heads * HEAD), BF16), jax.ShapeDtypeStruct((s, heads * HEAD), BF16),
                   jax.ShapeDtypeStruct((s, heads * 2 * HEAD), BF16), jax.ShapeDtypeStruct((s, heads * HEAD), F32)] + r_out,
        compiler_params=_params(("arbitrary", "arbitrary") if ride else ("parallel", "arbitrary")),
    )(ktab, qtab, qall, qr, kv, kr, o, do, lse, *r_in)
    return outs if ride is None else (*outs[:4], list(outs[4:]))


def _split_dot(val, tri, cb):
    hi = val.astype(BF16)
    lo = (val - hi.astype(F32)).astype(BF16)
    return _dot(hi, tri, 1, cb) + _dot(lo, tri, 1, cb)


def _sb_logs(q, k, offset, scale, masked):
    z = _dot(q, k, 1, 1) * (scale * LOG2E)
    sp = jnp.log2(1.0 + jnp.exp2(-jnp.abs(z)))
    ls = jnp.minimum(z, 0.0) - sp
    lk = ls - z
    if not masked:
        return None, ls, lk
    strict = (lax.broadcasted_iota(jnp.int32, z.shape, 1) + offset) < lax.broadcasted_iota(jnp.int32, z.shape, 0)
    return strict, ls, jnp.where(strict, lk, 0.0)


def _lane_pick(blk, idx):
    lane = lax.broadcasted_iota(jnp.int32, blk.shape, 1)
    return jnp.sum(jnp.where(lane == idx, blk, 0.0), axis=-1, keepdims=True)


def _lane_put(blk, idx, col):
    lane = lax.broadcasted_iota(jnp.int32, blk.shape, 1)
    return jnp.where(lane == idx, col, blk)


def _sb_tiles(s):
    tq = min(SB_TQ, s)
    tk = min(SB_TK, tq)
    assert s // tk <= HEAD
    return tq, tk


def sb_fwd(qkv, tri, heads, *, name):
    s = qkv.shape[0]
    tq, tk = _sb_tiles(s)
    nq = s // tq
    nsub = min(SB_SUBS_PER_STEP, s // tk)
    span = nsub * tk
    scale = 1.0 / math.sqrt(HEAD)
    pairs = [(qi, sb) for qi in range(nq) for sb in range(((qi + 1) * tq - 1) // span, -1, -1)]
    qtab, ktab = _tables(pairs)
    hp, groups, wide = _head_groups(heads, SB_HEADS_PER_STEP)

    def body(qt_ref, kt_ref, q_ref, k_ref, v_ref, tri_ref, o_ref, c_ref, carry_ref, acc_ref):
        st = pl.program_id(1)
        qi, sb = qt_ref[st], kt_ref[st]

        @pl.when(sb == ((qi + 1) * tq - 1) // span)
        def _():
            carry_ref[...] = jnp.zeros_like(carry_ref)
            acc_ref[...] = jnp.zeros_like(acc_ref)
            c_ref[...] = jnp.full_like(c_ref, SB_UNVISITED)

        def sub_block(i, _):
            sub = nsub - 1 - i
            kj = sb * nsub + sub
            keys = pl.ds(pl.multiple_of(sub * tk, tk), tk)
            alive = (kj * tk < (qi + 1) * tq) & (jnp.max(carry_ref[...]) > SB_DEAD_LOG2)

            def step(masked):
                for hh in range(hp):
                    cols = slice(hh * HEAD, (hh + 1) * HEAD)
                    strict, ls, lk = _sb_logs(q_ref[:, cols], k_ref[keys, cols], kj * tk - qi * tq, scale, masked)
                    carry = carry_ref[hh]
                    a = jnp.exp2(ls + _split_dot(lk, tri_ref[...], 0) + carry)
                    if masked:
                        a = jnp.where(strict, a, 0.0)
                    acc_ref[:, cols] += _dot(a.astype(BF16), v_ref[keys, cols], 1, 0)
                    c_ref[:, cols] = _lane_put(c_ref[:, cols], kj, carry)
                    carry_ref[hh] = carry + jnp.sum(lk, axis=-1, keepdims=True)

            @pl.when(alive & ((kj + 1) * tk > qi * tq))
            def _():
                step(True)

            @pl.when(alive & ((kj + 1) * tk <= qi * tq))
            def _():
                step(False)

            return 0

        lax.fori_loop(0, nsub, sub_block, 0)

        @pl.when(sb == 0)
        def _():
            o_ref[...] = acc_ref[...].astype(o_ref.dtype)

    qspec = pl.BlockSpec((tq, wide), lambda h, st, qt, kt: (qt[st], h))
    grid_spec = pltpu.PrefetchScalarGridSpec(
        num_scalar_prefetch=2, grid=(groups, len(pairs)),
        in_specs=[qspec,
                  pl.BlockSpec((span, wide), lambda h, st, qt, kt: (kt[st], groups + h)),
                  pl.BlockSpec((span, wide), lambda h, st, qt, kt: (kt[st], 2 * groups + h)),
                  pl.BlockSpec((tk, tk), lambda h, st, qt, kt: (0, 0))],
        out_specs=[qspec, qspec],
        scratch_shapes=[pltpu.VMEM((hp, tq, 1), F32), pltpu.VMEM((tq, wide), F32)])
    return pl.pallas_call(
        body, name=name, grid_spec=grid_spec,
        out_shape=[jax.ShapeDtypeStruct((s, heads * HEAD), BF16), jax.ShapeDtypeStruct((s, heads * HEAD), F32)],
        compiler_params=_params(("parallel", "arbitrary")),
    )(qtab, ktab, qkv, qkv, qkv, tri)


def sb_bwd(qkv, do, cmat, tri, heads, *, name):
    s = qkv.shape[0]
    tq, tk = _sb_tiles(s)
    nq = s // tq
    nsub = min(SB_SUBS_PER_STEP, s // tk)
    span = nsub * tk
    scale = 1.0 / math.sqrt(HEAD)
    pairs = [(sb, qi) for sb in range(s // span) for qi in range(sb * span // tq, nq)]
    ktab, qtab = _tables(pairs)
    last = len(pairs) - 1
    hp, groups, wide = _head_groups(heads, SB_HEADS_PER_STEP)

    def body(kt_ref, qt_ref, q_ref, k_ref, v_ref, do_ref, c_ref, tri_ref, dq_ref, dk_ref, dv_ref,
             dqf_ref, gsum_ref, ak_ref, av_ref):
        st = pl.program_id(1)
        sb, qi = kt_ref[st], qt_ref[st]

        @pl.when(st == 0)
        def _():
            dqf_ref[...] = jnp.zeros_like(dqf_ref)
            gsum_ref[...] = jnp.zeros_like(gsum_ref)

        @pl.when(qi == sb * span // tq)
        def _():
            ak_ref[...] = jnp.zeros_like(ak_ref)
            av_ref[...] = jnp.zeros_like(av_ref)

        def sub_block(sub, _):
            kj = sb * nsub + sub
            keys = pl.ds(pl.multiple_of(sub * tk, tk), tk)

            def step(masked):
                rows = pl.ds(pl.multiple_of(qi * tq, tq), tq)
                tri_v = tri_ref[...]
                for hh in range(hp):
                    cols = slice(hh * HEAD, (hh + 1) * HEAD)
                    qv, kblk, dov = q_ref[:, cols], k_ref[keys, cols], do_ref[:, cols]
                    strict, ls, lk = _sb_logs(qv, kblk, kj * tk - qi * tq, scale, masked)
                    a = jnp.exp2(ls + _split_dot(lk, tri_v, 0) + _lane_pick(c_ref[:, cols], kj))
                    if masked:
                        a = jnp.where(strict, a, 0.0)
                    g = _dot(dov, v_ref[keys, cols], 1, 1) * a
                    before_all = gsum_ref[hh, rows, :]
                    before = _split_dot(g, tri_v, 1) + before_all[:, :1]
                    beta = jnp.exp2(ls)
                    dz = g * (1.0 - beta) - before * beta
                    if masked:
                        dz = jnp.where(strict, dz, 0.0)
                    dzb = dz.astype(BF16)
                    av_ref[keys, cols] += _dot(a.astype(BF16), dov, 0, 0)
                    ak_ref[keys, cols] += _dot(dzb, qv, 0, 0)
                    dqf_ref[rows, cols] += _dot(dzb, kblk, 1, 0)
                    gsum_ref[hh, rows, :] = before_all + jnp.sum(g, axis=-1, keepdims=True)

            lane = lax.broadcasted_iota(jnp.int32, (tq, wide), 1)
            seen = jnp.max(jnp.where((lane & (HEAD - 1)) == kj, c_ref[...], SB_UNVISITED)) > SB_DEAD_LOG2
            alive = (kj * tk < (qi + 1) * tq) & seen

            @pl.when(alive & ((kj + 1) * tk > qi * tq))
            def _():
                step(True)

            @pl.when(alive & ((kj + 1) * tk <= qi * tq))
            def _():
                step(False)

            return 0

        lax.fori_loop(0, nsub, sub_block, 0)

        @pl.when(qi == nq - 1)
        def _():
            dk_ref[...] = (ak_ref[...] * scale).astype(dk_ref.dtype)
            dv_ref[...] = av_ref[...].astype(dv_ref.dtype)

        @pl.when(st == last)
        def _():
            dq_ref[...] = (dqf_ref[...] * scale).astype(dq_ref.dtype)

    qspec = pl.BlockSpec((tq, wide), lambda h, st, kt, qt: (qt[st], h))
    ospec = pl.BlockSpec((span, wide), lambda h, st, kt, qt: (kt[st], h))
    grid_spec = pltpu.PrefetchScalarGridSpec(
        num_scalar_prefetch=2, grid=(groups, len(pairs)),
        in_specs=[qspec,
                  pl.BlockSpec((span, wide), lambda h, st, kt, qt: (kt[st], groups + h)),
                  pl.BlockSpec((span, wide), lambda h, st, kt, qt: (kt[st], 2 * groups + h)),
                  qspec, qspec, pl.BlockSpec((tk, tk), lambda h, st, kt, qt: (0, 0))],
        out_specs=[pl.BlockSpec((s, wide), lambda h, st, kt, qt: (0, h)), ospec, ospec],
        scratch_shapes=[pltpu.VMEM((s, wide), F32), pltpu.VMEM((hp, s, HEAD), F32), pltpu.VMEM((span, wide), F32),
                        pltpu.VMEM((span, wide), F32)])
    return pl.pallas_call(
        body, name=name, grid_spec=grid_spec,
        out_shape=[jax.ShapeDtypeStruct((s, heads * HEAD), BF16)] * 3,
        compiler_params=_params(("parallel", "arbitrary")),
    )(ktab, qtab, qkv, qkv, qkv, do, cmat, tri)


def _mem_probs(q, k, scale):
    sc = _dot(q, k, 1, 1) * scale
    e = jnp.exp(sc - jnp.max(sc, axis=-1, keepdims=True))
    return e / jnp.sum(e, axis=-1, keepdims=True)


def mem_fwd(q, kvm, heads, *, name):
    s, nm = q.shape[0], kvm.shape[0]
    tq = min(MEM_TQ, s)
    scale = 1.0 / math.sqrt(HEAD)

    def body(q_ref, k_ref, v_ref, o_ref):
        p = _mem_probs(q_ref[...], k_ref[...], scale)
        o_ref[...] = _dot(p.astype(BF16), v_ref[...], 1, 0).astype(o_ref.dtype)

    qspec = pl.BlockSpec((tq, HEAD), lambda h, qi: (qi, h))
    return pl.pallas_call(
        body, name=name, grid=(heads, s // tq),
        in_specs=[qspec, pl.BlockSpec((nm, HEAD), lambda h, qi: (0, h)),
                  pl.BlockSpec((nm, HEAD), lambda h, qi: (0, heads + h))],
        out_specs=qspec, out_shape=jax.ShapeDtypeStruct((s, heads * HEAD), BF16),
        compiler_params=_params(("parallel", "parallel")),
    )(q, kvm, kvm)


def mem_bwd(q, kvm, do, heads, *, name):
    s, nm = q.shape[0], kvm.shape[0]
    tq = min(MEM_TQ, s)
    scale = 1.0 / math.sqrt(HEAD)

    def body(q_ref, k_ref, v_ref, do_ref, dq_ref, dk_ref, dv_ref):
        @pl.when(pl.program_id(1) == 0)
        def _():
            dk_ref[...] = jnp.zeros_like(dk_ref)
            dv_ref[...] = jnp.zeros_like(dv_ref)

        qv, kvv, dov = q_ref[...], k_ref[...], do_ref[...]
        p = _mem_probs(qv, kvv, scale)
        dp = _dot(dov, v_ref[...], 1, 1)
        ds = (p * (dp - jnp.sum(dp * p, axis=-1, keepdims=True)) * scale).astype(BF16)
        dq_ref[...] = _dot(ds, kvv, 1, 0).astype(dq_ref.dtype)
        dk_ref[...] += _dot(ds, qv, 0, 0)
        dv_ref[...] += _dot(p.astype(BF16), dov, 0, 0)

    qspec = pl.BlockSpec((tq, HEAD), lambda h, qi: (qi, h))
    kspec = pl.BlockSpec((nm, HEAD), lambda h, qi: (0, h))
    return pl.pallas_call(
        body, name=name, grid=(heads, s // tq),
        in_specs=[qspec, kspec, pl.BlockSpec((nm, HEAD), lambda h, qi: (0, heads + h)), qspec],
        out_specs=[qspec, kspec, kspec],
        out_shape=[jax.ShapeDtypeStruct((s, heads * HEAD), BF16), jax.ShapeDtypeStruct((nm, heads * HEAD), F32),
                   jax.ShapeDtypeStruct((nm, heads * HEAD), F32)],
        compiler_params=_params(("parallel", "arbitrary")),
    )(q, kvm, kvm, do)


def _conv3(u, halo, w, b):
    tm = u.shape[0]
    row = lax.broadcasted_iota(jnp.int32, u.shape, 0)
    h1, h2 = halo[HALO - 1:HALO, :], halo[HALO - 2:HALO - 1, :]
    u1 = jnp.where(row == 0, h1, pltpu.roll(u, 1, 0))
    u2 = jnp.where(row == 0, h2, jnp.where(row == 1, h1, pltpu.roll(u, 2 % tm, 0)))
    return b + w[0:1, :] * u2 + w[1:2, :] * u1 + w[2:3, :] * u, u1, u2


def _conv_specs(s, f):
    tm, tn = min(CONV_TM, s), _pick(f, CONV_TN)
    return tm, tn, s // tm, f // tn


def _silu_parts(g):
    sg = 1.0 / (1.0 + jnp.exp(-g))
    return g * sg, sg


def conv_gate_fwd(u, cw, cb, *, name):
    s, f = u.shape[0], u.shape[1] // 2
    tm, tn, ni, nj = _conv_specs(s, f)
    hb = tm // HALO

    def body(ug_ref, uu_ref, hg_ref, hu_ref, wg_ref, wu_ref, bg_ref, bu_ref, a_ref):
        keep = (pl.program_id(1) > 0).astype(F32)
        gate, _, _ = _conv3(ug_ref[...].astype(F32), hg_ref[...].astype(F32) * keep, wg_ref[...], bg_ref[...])
        up, _, _ = _conv3(uu_ref[...].astype(F32), hu_ref[...].astype(F32) * keep, wu_ref[...], bu_ref[...])
        a_ref[...] = (_silu_parts(gate)[0] * up).astype(a_ref.dtype)

    def main(off):
        return pl.BlockSpec((tm, tn), lambda j, i: (i, j + off))

    def halo(off):
        return pl.BlockSpec((HALO, tn), lambda j, i: (jnp.maximum(i * hb - 1, 0), j + off))

    def par(rows, off):
        return pl.BlockSpec((rows, tn), lambda j, i: (0, j + off))

    return pl.pallas_call(
        body, name=name, grid=(nj, ni),
        in_specs=[main(0), main(nj), halo(0), halo(nj), par(3, 0), par(3, nj), par(1, 0), par(1, nj)],
        out_specs=main(0), out_shape=jax.ShapeDtypeStruct((s, f), BF16),
        compiler_params=_params(("parallel", "parallel")),
    )(u, u, u, u, cw, cw, cb, cb)


def conv_gate_bwd(u, da, cw, cb, *, name):
    s, f = u.shape[0], u.shape[1] // 2
    tm, tn, ni, nj = _conv_specs(s, f)
    hb = tm // HALO

    def body(ug_ref, uu_ref, hg_ref, hu_ref, da_ref, wg_ref, wu_ref, bg_ref, bu_ref, dg_ref, du_ref, pg_ref, pu_ref):
        @pl.when(pl.program_id(1) == 0)
        def _():
            pg_ref[...] = jnp.zeros_like(pg_ref)
            pu_ref[...] = jnp.zeros_like(pu_ref)

        keep = (pl.program_id(1) > 0).astype(F32)
        ug, uu = ug_ref[...].astype(F32), uu_ref[...].astype(F32)
        gate, ug1, ug2 = _conv3(ug, hg_ref[...].astype(F32) * keep, wg_ref[...], bg_ref[...])
        up, uu1, uu2 = _conv3(uu, hu_ref[...].astype(F32) * keep, wu_ref[...], bu_ref[...])
        act, sg = _silu_parts(gate)
        dav = da_ref[...].astype(F32)
        d_gate = dav * up * (sg * (1.0 + gate * (1.0 - sg)))
        d_up = dav * act
        dg_ref[...] = d_gate.astype(dg_ref.dtype)
        du_ref[...] = d_up.astype(du_ref.dtype)
        for p_ref, dc, taps in ((pg_ref, d_gate, (ug2, ug1, ug)), (pu_ref, d_up, (uu2, uu1, uu))):
            for r, tap in enumerate(taps):
                p_ref[r:r + 1, :] += jnp.sum(dc * tap, axis=0, keepdims=True)
            p_ref[3:4, :] += jnp.sum(dc, axis=0, keepdims=True)

    def main(off):
        return pl.BlockSpec((tm, tn), lambda j, i: (i, j + off))

    def halo(off):
        return pl.BlockSpec((HALO, tn), lambda j, i: (jnp.maximum(i * hb - 1, 0), j + off))

    def par(rows, off):
        return pl.BlockSpec((rows, tn), lambda j, i: (0, j + off))

    dg, du, pg, pu = pl.pallas_call(
        body, name=name, grid=(nj, ni),
        in_specs=[main(0), main(nj), halo(0), halo(nj), main(0), par(3, 0), par(3, nj), par(1, 0), par(1, nj)],
        out_specs=[main(0), main(0), par(8, 0), par(8, 0)],
        out_shape=[jax.ShapeDtypeStruct((s, f), BF16), jax.ShapeDtypeStruct((s, f), BF16),
                   jax.ShapeDtypeStruct((8, f), F32), jax.ShapeDtypeStruct((8, f), F32)],
        compiler_params=_params(("parallel", "arbitrary")),
    )(u, u, u, u, da, cw, cw, cb, cb)
    return dg, du, jnp.concatenate([pg, pu], axis=1)


def conv_transpose(dcg, dcu, w, *, name):
    s, f = dcg.shape
    tm, tn, ni, nj = _conv_specs(s, f)
    hb = tm // HALO

    def body(dg_ref, hg_ref, du_ref, hu_ref, w_ref, o_ref):
        inner = (pl.program_id(1) < ni - 1).astype(F32)

        @pl.when(pl.program_id(0) < nj)
        def _():
            one_half(dg_ref, hg_ref, w_ref, o_ref, inner)

        @pl.when(pl.program_id(0) >= nj)
        def _():
            one_half(du_ref, hu_ref, w_ref, o_ref, inner)

    def one_half(d_ref, h_ref, w_ref, o_ref, inner):
        d = d_ref[...].astype(F32)
        halo = h_ref[...].astype(F32) * inner
        row = lax.broadcasted_iota(jnp.int32, d.shape, 0)
        n0, n1 = halo[0:1, :], halo[1:2, :]
        d1 = jnp.where(row == tm - 1, n0, pltpu.roll(d, tm - 1, 0))
        d2 = jnp.where(row == tm - 2, n0, jnp.where(row == tm - 1, n1, pltpu.roll(d, tm - 2, 0)))
        wv = w_ref[...]
        o_ref[...] = (wv[2:3, :] * d + wv[1:2, :] * d1 + wv[0:1, :] * d2).astype(o_ref.dtype)

    def half_specs(lo):
        def col(j):
            return jnp.clip(j - lo, 0, nj - 1)

        def row(j, i):
            return jnp.where((j >= lo) & (j < lo + nj), i, 0)

        return [pl.BlockSpec((tm, tn), lambda j, i: (row(j, i), col(j))),
                pl.BlockSpec((HALO, tn), lambda j, i: (jnp.minimum((row(j, i) + 1) * hb, s // HALO - 1), col(j)))]

    return pl.pallas_call(
        body, name=name, grid=(2 * nj, ni),
        in_specs=half_specs(0) + half_specs(nj) + [pl.BlockSpec((3, tn), lambda j, i: (0, j))],
        out_specs=pl.BlockSpec((tm, tn), lambda j, i: (i, j)), out_shape=jax.ShapeDtypeStruct((s, 2 * f), BF16),
        compiler_params=_params(("parallel", "parallel")),
    )(dcg, dcg, dcu, dcu, w)


def _tile2d(r, c):
    return _pick(r, ROW_TILE), _pick(c, (2048, 1024, 512, 256, 128))


def sum_slots(buf, *, name):
    n, shape = buf.shape[0], buf.shape[1:]
    r, c = math.prod(shape[:-1]), shape[-1]
    tm, tn = _tile2d(r, c)

    def body(b_ref, o_ref):
        acc = b_ref[0]
        for k in range(1, n):
            acc = acc + b_ref[k]
        o_ref[...] = acc

    out = pl.pallas_call(
        body, name=name, grid=(r // tm, c // tn),
        in_specs=[pl.BlockSpec((n, tm, tn), lambda i, j: (0, i, j))],
        out_specs=pl.BlockSpec((tm, tn), lambda i, j: (i, j)), out_shape=jax.ShapeDtypeStruct((r, c), F32),
        compiler_params=_params(("parallel", "parallel")),
    )(buf.reshape(n, r, c))
    return out.reshape(shape)


def add_half(layers, got, core, *, name):
    n_l = len(layers)
    a, _, r, c = layers[0].shape
    tm, tn = _tile2d(r, c)

    def body(core_ref, *refs):
        got_ref, o_ref = refs[n_l:]
        for l in range(n_l):
            @pl.when(pl.program_id(0) == l)
            def _(l=l):
                o_ref[...] = (refs[l][...] + got_ref[...]).astype(o_ref.dtype)

    def layer_spec(l):
        def index(b, q, i, j, core_ref):
            on = b == l
            return (jnp.where(on, q, 0), core_ref[0], jnp.where(on, i, 0), jnp.where(on, j, 0))
        return pl.BlockSpec((None, None, tm, tn), index)

    part = pl.BlockSpec((None, None, tm, tn), lambda b, q, i, j, core_ref: (b, q, i, j))
    grid_spec = pltpu.PrefetchScalarGridSpec(
        num_scalar_prefetch=1, grid=(n_l, a, r // tm, c // tn),
        in_specs=[layer_spec(l) for l in range(n_l)] + [part], out_specs=part)
    return pl.pallas_call(
        body, name=name, grid_spec=grid_spec, out_shape=jax.ShapeDtypeStruct((n_l, a, r, c), BF16),
        compiler_params=_params(("arbitrary", "parallel", "parallel", "parallel")),
    )(core, *layers, got)


def add_own_block(h, got, chip, axis, *, name):
    _, l, r, c = got.shape
    tm, tn = _tile2d(r, c)
    ncb = c // tn

    def body(chip_ref, h_ref, got_ref, o_ref):
        o_ref[...] = ((h_ref[...].astype(F32) + got_ref[0].astype(F32)) + got_ref[1].astype(F32)) + got_ref[2].astype(F32)

    if axis == 0:
        h_spec = pl.BlockSpec((None, None, tm, tn), lambda b, i, j, chip_ref: (b, chip_ref[0], i, j))
    else:
        h_spec = pl.BlockSpec((None, tm, tn), lambda b, i, j, chip_ref: (b, i, chip_ref[0] * ncb + j))
    grid_spec = pltpu.PrefetchScalarGridSpec(
        num_scalar_prefetch=1, grid=(l, r // tm, ncb),
        in_specs=[h_spec, pl.BlockSpec((3, None, tm, tn), lambda b, i, j, chip_ref: (0, b, i, j))],
        out_specs=pl.BlockSpec((None, tm, tn), lambda b, i, j, chip_ref: (b, i, j)))
    return pl.pallas_call(
        body, name=name, grid_spec=grid_spec, out_shape=jax.ShapeDtypeStruct((l, r, c), F32),
        compiler_params=_params(("parallel", "parallel", "parallel")),
    )(chip, h, got)


def _adam_update(w, g, m, v):
    bc1, bc2 = 1.0 - ADAM_B1 ** ADAM_STEP, 1.0 - ADAM_B2 ** ADAM_STEP
    mn = ADAM_B1 * m + (1.0 - ADAM_B1) * g
    vn = ADAM_B2 * v + (1.0 - ADAM_B2) * (g * g)
    return -ADAM_LR * ((mn / bc1) / (jnp.sqrt(vn / bc2) + ADAM_EPS) + ADAM_WD * w), mn, vn


def adamw(w, g, m, v, *, name):
    shape = w.shape
    c = shape[-1]
    r = math.prod(shape[:-1]) if len(shape) > 1 else 1
    tm, tn = _tile2d(r, c)

    def body(w_ref, g_ref, m_ref, v_ref, d_ref, mo_ref, vo_ref):
        d_ref[...], mo_ref[...], vo_ref[...] = _adam_update(w_ref[...], g_ref[...], m_ref[...], v_ref[...])

    spec = pl.BlockSpec((tm, tn), lambda i, j: (i, j))
    outs = pl.pallas_call(
        body, name=name, grid=(r // tm, c // tn), in_specs=[spec] * 4, out_specs=[spec] * 3,
        out_shape=[jax.ShapeDtypeStruct((r, c), F32)] * 3, compiler_params=_params(("parallel", "parallel")),
    )(*(t.reshape(r, c) for t in (w, g, m, v)))
    return tuple(o.reshape(shape) for o in outs)


def adamw_halves(w, mine, got, m, v, core, *, name):
    l, r, c = w.shape
    rh = r // 2
    tm, tn = _tile2d(rh, c)

    def body(core_ref, w_ref, mine_ref, got_ref, m_ref, v_ref, g_ref, d_ref, mo_ref, vo_ref):
        g = jnp.where(pl.program_id(1) == core_ref[0], mine_ref[...], got_ref[...])
        g_ref[...] = g
        d_ref[...], mo_ref[...], vo_ref[...] = _adam_update(w_ref[...], g, m_ref[...], v_ref[...])

    full = pl.BlockSpec((None, None, tm, tn), lambda b, h, i, j, core_ref: (b, h, i, j))
    half = pl.BlockSpec((None, tm, tn), lambda b, h, i, j, core_ref: (b, i, j))
    grid_spec = pltpu.PrefetchScalarGridSpec(
        num_scalar_prefetch=1, grid=(l, 2, rh // tm, c // tn),
        in_specs=[full, half, half, full, full], out_specs=[full] * 4)
    outs = pl.pallas_call(
        body, name=name, grid_spec=grid_spec, out_shape=[jax.ShapeDtypeStruct((l, 2, rh, c), F32)] * 4,
        compiler_params=_params(("parallel", "parallel", "parallel", "parallel")),
    )(core, w.reshape(l, 2, rh, c), mine, got, m.reshape(l, 2, rh, c), v.reshape(l, 2, rh, c))
    return tuple(o.reshape(l, r, c) for o in outs)


def _copies(plan, in_refs, out_refs, send_sems, recv_sems):
    x, y, c = lax.axis_index("x"), lax.axis_index("y"), lax.axis_index("c")
    out = []
    for n, (src, dst, peer, got) in enumerate(plan(in_refs, out_refs, x, y, c, count_only=False)):
        if peer is None:
            out.append((pltpu.make_async_copy(src, dst, send_sems.at[n]), None))
        else:
            mk = lambda s, d: pltpu.make_async_remote_copy(
                src_ref=s, dst_ref=d, send_sem=send_sems.at[n], recv_sem=recv_sems.at[n], device_id=peer,
                device_id_type=MESH)
            out.append((mk(src, dst), mk(got, got)))
    return out


def _start_all(copies):
    for mine, _ in copies:
        mine.start()


def _wait_all(copies):
    for mine, theirs in copies:
        if theirs is None:
            mine.wait()
        else:
            theirs.wait_recv()
            mine.wait_send()


def _ride_specs(ride):
    if ride is None:
        return [], [], [], [], []
    inputs, out_shapes, plan = ride
    count = plan([None] * len(inputs), [None] * len(out_shapes), 0, 0, 0, count_only=True)
    hbm = pl.BlockSpec(memory_space=pl.ANY)
    sems = [pltpu.SemaphoreType.DMA((count,)), pltpu.SemaphoreType.DMA((count,))]
    return list(inputs), [hbm] * len(inputs), list(out_shapes), [hbm] * len(out_shapes), sems


def _ride_along(ride, in_refs, out_refs, sems, first, last):
    if ride is None:
        return

    @pl.when(first)
    def _():
        _start_all(_copies(ride[2], in_refs, out_refs, *sems))

    @pl.when(last)
    def _():
        _wait_all(_copies(ride[2], in_refs, out_refs, *sems))


def _comm(name, inputs, out_shapes, aliases, plan):
    n_in, n_out = len(inputs), len(out_shapes)
    probe = plan([None] * n_in, [None] * n_out, 0, 0, 0, count_only=True)

    def body(*refs):
        copies = _copies(plan, refs[:n_in], refs[n_in:n_in + n_out], *refs[n_in + n_out:])
        _start_all(copies)
        _wait_all(copies)

    hbm = pl.BlockSpec(memory_space=pl.ANY)
    return pl.pallas_call(
        body, name=name, in_specs=[hbm] * n_in, out_specs=[hbm] * n_out, out_shape=out_shapes,
        input_output_aliases=aliases,
        scratch_shapes=[pltpu.SemaphoreType.DMA((probe,)), pltpu.SemaphoreType.DMA((probe,))],
    )(*inputs)


def _other_chips(x, y):
    return [(1 - x, y), (x, 1 - y), (1 - x, 1 - y)]


def gather_weights(shards, axes, carrier=None):
    def out_shape(sh, ax):
        l, r, c = sh.shape
        return jax.ShapeDtypeStruct((l, N_CHIPS, 2, r // 2, c) if ax == 0 else (l, 2, r // 2, N_CHIPS * c), BF16)

    def piece(ref, ax, chip, half, width):
        if ax == 0:
            return ref.at[:, chip, half]
        return ref.at[:, half, :, pl.ds(chip * width, width)]

    def plan1(in_refs, out_refs, x, y, c, count_only):
        if count_only:
            return 4 * len(shards)
        me, xfers = 2 * x + y, []
        for sh, ax, src, out in zip(shards, axes, in_refs, out_refs):
            rows, width = sh.shape[1] // 2, sh.shape[2]
            mine = src.at[:, pl.ds(c * rows, rows), :]
            xfers.append((mine, piece(out, ax, me, c, width), None, None))
            for cx, cy in _other_chips(x, y):
                xfers.append((mine, piece(out, ax, me, c, width), (cx, cy, c), piece(out, ax, 2 * cx + cy, c, width)))
        return xfers

    def half_of(ref, ax, half):
        return ref.at[:, :, half] if ax == 0 else ref.at[:, half]

    def plan2(in_refs, out_refs, x, y, c, count_only):
        if count_only:
            return len(shards)
        return [(half_of(out, ax, c), half_of(out, ax, c), (x, y, 1 - c), half_of(out, ax, 1 - c))
                for ax, out in zip(axes, out_refs)]

    shapes = [out_shape(sh, ax) for sh, ax in zip(shards, axes)]
    if carrier is None:
        carried, part = None, _comm("gather_chips", list(shards), shapes, {}, plan1)
    else:
        carried, part = carrier((list(shards), shapes, plan1))
    full = _comm("gather_cores", list(part), shapes, {n: n for n in range(len(shards))}, plan2)
    return [f.reshape(sh.shape[0], N_CHIPS * sh.shape[1], sh.shape[2]) if ax == 0
            else f.reshape(sh.shape[0], sh.shape[1], N_CHIPS * sh.shape[2])
            for f, sh, ax in zip(full, shards, axes)], carried


def reduce_scatter_grads(grads, axes, core, chip, carrier=None):
    dims, views = [], []
    for layers, ax in zip(grads, axes):
        r, c = layers[0].shape
        if ax == 0:
            dims.append((len(layers), r // N_CHIPS // 2, c))
            views.append([g.reshape(N_CHIPS, 2, r // N_CHIPS // 2, c) for g in layers])
        else:
            dims.append((len(layers), r // 2, c // N_CHIPS))
            views.append([g.reshape(1, 2, r // 2, c) for g in layers])

    def half_shape(ax, d):
        return (d[0], N_CHIPS, d[1], d[2]) if ax == 0 else (d[0], d[1], N_CHIPS * d[2])

    def plan_a(in_refs, out_refs, x, y, c, count_only):
        if count_only:
            return sum(d[0] for d in dims)
        xfers, n = [], 0
        for d, out in zip(dims, out_refs):
            for l in range(d[0]):
                xfers.append((in_refs[n].at[:, 1 - c], out.at[l], (x, y, 1 - c), out.at[l]))
                n += 1
        return xfers

    got = _comm("reduce_cores", [v for layers in views for v in layers],
                [jax.ShapeDtypeStruct((d[0], N_CHIPS if ax == 0 else 1, d[1], d[2] if ax == 0 else N_CHIPS * d[2]), F32)
                 for ax, d in zip(axes, dims)], {}, plan_a)
    chip_sum = [add_half(layers, g, core, name="sum_cores").reshape(half_shape(ax, d))
                for ax, d, layers, g in zip(axes, dims, views, got)]

    def block(ref, ax, which, width):
        return ref.at[:, which] if ax == 0 else ref.at[:, :, pl.ds(which * width, width)]

    def plan_b(in_refs, out_refs, x, y, c, count_only):
        if count_only:
            return 3 * len(grads)
        xfers = []
        for ax, d, src, out in zip(axes, dims, in_refs, out_refs):
            for k, (cx, cy) in enumerate(_other_chips(x, y)):
                xfers.append((block(src, ax, 2 * cx + cy, d[2]), out.at[k], (cx, cy, c), out.at[k]))
        return xfers

    part_shapes = [jax.ShapeDtypeStruct((3,) + d, BF16) for d in dims]
    if carrier is None:
        carried, parts = None, _comm("reduce_chips", chip_sum, part_shapes, {}, plan_b)
    else:
        carried, parts = carrier((chip_sum, part_shapes, plan_b))
    mine = [add_own_block(h, p, chip, ax, name="sum_chips") for h, p, ax in zip(chip_sum, parts, axes)]

    def plan_c(in_refs, out_refs, x, y, c, count_only):
        if count_only:
            return len(grads)
        return [(src, out, (x, y, 1 - c), out) for src, out in zip(in_refs, out_refs)]

    theirs = _comm("share_cores", mine, [jax.ShapeDtypeStruct(d, F32) for d in dims], {}, plan_c)
    return list(zip(mine, theirs)), carried


def gather_slabs(vec, *, name):
    def plan(in_refs, out_refs, x, y, c, count_only):
        if count_only:
            return 8
        me = 4 * x + 2 * y + c
        xfers = [(in_refs[0], out_refs[0].at[me], None, None)]
        for k in range(1, 8):
            px, py, pc = x ^ (k >> 2), y ^ ((k >> 1) & 1), c ^ (k & 1)
            xfers.append((in_refs[0], out_refs[0].at[me], (px, py, pc), out_refs[0].at[4 * px + 2 * py + pc]))
        return xfers

    return _comm(name, [vec], [jax.ShapeDtypeStruct((8,) + vec.shape, F32)], {}, plan)[0]


def allreduce_small(vec):
    return sum_slots(gather_slabs(vec, name="gather_small"), name="sum_small")


def gather_conv_w(block):
    l, taps, c = block.shape
    flat = block.reshape(-1)
    slabs = gather_slabs(jnp.pad(flat, (0, -flat.size % 1024)).reshape(-1, 128), name="gather_conv_w")
    per_chip = slabs[0::2].reshape(N_CHIPS, -1)[:, :flat.size].reshape(N_CHIPS, l, taps, c)
    return per_chip.transpose(1, 2, 0, 3).reshape(l, taps, N_CHIPS * c)


def _rope_tables(positions):
    inv_freq = ROPE_THETA ** (-jnp.arange(0, ROPE_DIM, 2, dtype=F32) / ROPE_DIM)
    ang = positions.astype(F32)[:, None] * inv_freq
    ang = jnp.concatenate([ang, ang], axis=-1)
    pad = ((0, 0), (0, HEAD - ROPE_DIM))
    return jnp.pad(jnp.cos(ang), pad), jnp.pad(jnp.sin(ang), pad)


def _uq_layout(w, heads):
    ql = w.shape[0]
    w = w.reshape(ql, heads, HEAD + ROPE_DIM)
    rot = jnp.pad(w[:, :, HEAD:], ((0, 0), (0, 0), (0, HEAD - ROPE_DIM)))
    return jnp.concatenate([w[:, :, :HEAD].reshape(ql, heads * HEAD), rot.reshape(ql, heads * HEAD)], axis=1)


def _uq_layout_inv(dw, heads):
    ql = dw.shape[0]
    nope = dw[:, :heads * HEAD].reshape(ql, heads, HEAD)
    rot = dw[:, heads * HEAD:].reshape(ql, heads, HEAD)[:, :, :ROPE_DIM]
    return jnp.concatenate([nope, rot], axis=-1).reshape(ql, heads * (HEAD + ROPE_DIM))


def kernel(x, mem, positions, norm_mix, norm_mem_q, norm_mem_kv, norm_ffn, norm_final, mla_w_down, mla_q_norm, mla_w_uq, mla_kv_norm, mla_w_ukv, mla_w_o, sb_w_qkv, sb_w_o, mem_w_q, mem_w_kv, mem_w_o, ffn_w_in, ffn_conv_w, ffn_conv_b, ffn_w_out, loss_target, m_norm_mix, m_norm_mem_q, m_norm_mem_kv, m_norm_ffn, m_norm_final, m_mla_w_down, m_mla_q_norm, m_mla_w_uq, m_mla_kv_norm, m_mla_w_ukv, m_mla_w_o, m_sb_w_qkv, m_sb_w_o, m_mem_w_q, m_mem_w_kv, m_mem_w_o, m_ffn_w_in, m_ffn_conv_w, m_ffn_conv_b, m_ffn_w_out, v_norm_mix, v_norm_mem_q, v_norm_mem_kv, v_norm_ffn, v_norm_final, v_mla_w_down, v_mla_q_norm, v_mla_w_uq, v_mla_kv_norm, v_mla_w_ukv, v_mla_w_o, v_sb_w_qkv, v_sb_w_o, v_mem_w_q, v_mem_w_kv, v_mem_w_o, v_ffn_w_in, v_ffn_conv_w, v_ffn_conv_b, v_ffn_w_out):
    args = dict(locals())
    big = ["mla_w_down", "mla_w_uq", "mla_w_ukv", "mla_w_o", "sb_w_qkv", "sb_w_o", "mem_w_q", "mem_w_kv",
           "mem_w_o", "ffn_w_in", "ffn_w_out"]
    col_cut = {"mla_w_uq", "mla_w_ukv", "sb_w_qkv", "mem_w_o", "ffn_w_in"}
    axes = [1 if n in col_cut else 0 for n in big]
    small = ["norm_mix", "norm_mem_q", "norm_mem_kv", "norm_ffn", "norm_final", "mla_q_norm", "mla_kv_norm",
             "ffn_conv_b", "ffn_conv_w"]
    order = ["norm_mix", "norm_mem_q", "norm_mem_kv", "norm_ffn", "norm_final", "mla_w_down", "mla_q_norm",
             "mla_w_uq", "mla_kv_norm", "mla_w_ukv", "mla_w_o", "sb_w_qkv", "sb_w_o", "mem_w_q", "mem_w_kv",
             "mem_w_o", "ffn_w_in", "ffn_conv_w", "ffn_conv_b", "ffn_w_out"]

    xs, mems, target = x[0], mem[0], loss_target[0]
    s, d = xs.shape
    depth = norm_mix.shape[0]
    ql, kvl = mla_q_norm.shape[1], mla_kv_norm.shape[1]
    mla_heads = N_CHIPS * mla_w_uq.shape[2] // (HEAD + ROPE_DIM)
    sb_heads = N_CHIPS * sb_w_qkv.shape[2] // (3 * HEAD)
    mem_heads = mem_w_q.shape[2] // HEAD
    ff = N_CHIPS * ffn_w_out.shape[1]
    chip = 2 * lax.axis_index("x") + lax.axis_index("y")
    chip_op = jnp.reshape(chip, (1,)).astype(jnp.int32)
    core_op = jnp.reshape(lax.axis_index("c"), (1,)).astype(jnp.int32)

    axis_of = dict(zip(big, axes))
    early = [n for n in big if not n.startswith("sb_")]
    late_layers = {n: list(range(0 if n.startswith("sb_") else 1, args[n].shape[0])) for n in big}
    whole = {n: {} for n in big}
    early_full, _ = gather_weights([args[n][0:1].astype(BF16) for n in early], [axis_of[n] for n in early])
    for n, f in zip(early, early_full):
        whole[n][0] = f[0]
    w_uq = {0: _uq_layout(whole["mla_w_uq"][0], mla_heads)}
    cos, sin = _rope_tables(positions[0])
    _, sb_tk = _sb_tiles(s)
    tri = (jnp.arange(sb_tk)[:, None] > jnp.arange(sb_tk)[None, :]).astype(BF16)
    conv_w = gather_conv_w(ffn_conv_w)

    saved = []
    xa = xs
    for i in range(depth):
        j = i // 2
        lay = {}
        lay["xa"] = xa
        h1 = rmsnorm_fwd(xa, norm_mix[i], name="norm_fwd")
        lay["h1"] = h1
        if i % 2 == 0:
            down = mm(h1, whole["mla_w_down"][j], out_dtype=F32, name="mm_down")
            cq = rmsnorm_fwd(down[:, :ql], mla_q_norm[j], name="norm_lora_fwd")
            ckv = rmsnorm_fwd(down[:, ql:ql + kvl], mla_kv_norm[j], name="norm_lora_fwd")
            kr_raw = jnp.pad(down[:, ql + kvl:], ((0, 0), (0, HEAD - ROPE_DIM)))
            kr = rope(kr_raw, cos, sin, col0=0, n=1, inverse=False, name="rope_k")
            qall = mm(cq, w_uq[j], name="mm_uq")
            qr = rope(qall, cos, sin, col0=mla_heads, n=mla_heads, inverse=False, name="rope_q")
            kv = mm(ckv, whole["mla_w_ukv"][j], name="mm_ukv")
            if i == 0:
                def attend_and_gather(ride, att=(qall, qr, kv, kr)):
                    o_, lse_, part = mla_fwd(*att, mla_heads, name="mla_fwd_gather", ride=ride)
                    return (o_, lse_), part

                late_full, (o, lse) = gather_weights(
                    [args[n][late_layers[n][0]:].astype(BF16) for n in big], axes, attend_and_gather)
                for n, f in zip(big, late_full):
                    for at, l in enumerate(late_layers[n]):
                        whole[n][l] = f[at]
                for l in late_layers["mla_w_uq"]:
                    w_uq[l] = _uq_layout(whole["mla_w_uq"][l], mla_heads)
            else:
                o, lse = mla_fwd(qall, qr, kv, kr, mla_heads, name="mla_fwd")
            xb = mm(o, whole["mla_w_o"][j], add=xa, out_dtype=F32, name="mm_out_res")
            lay.update(down=down, cq=cq, ckv=ckv, qall=qall, qr=qr, kv=kv, kr=kr, o=o, lse=lse)
        else:
            qkv = mm(h1, whole["sb_w_qkv"][j], name="mm_qkv")
            o, cmat = sb_fwd(qkv, tri, sb_heads, name="sb_fwd")
            xb = mm(o, whole["sb_w_o"][j], add=xa, out_dtype=F32, name="mm_out_res")
            lay.update(qkv=qkv, o=o, cmat=cmat)
        h2 = rmsnorm_fwd(xb, norm_mem_q[i], name="norm_fwd")
        hm = rmsnorm_fwd(mems, norm_mem_kv[i], name="norm_mem_fwd")
        qm = mm(h2, whole["mem_w_q"][i], name="mm_mem_q")
        kvm = mm(hm, whole["mem_w_kv"][i], name="mm_mem_kv")
        om = mem_fwd(qm, kvm, mem_heads, name="mem_fwd")
        xc = mm(om, whole["mem_w_o"][i], add=xb, out_dtype=F32, name="mm_mem_out_res")
        h3 = rmsnorm_fwd(xc, norm_ffn[i], name="norm_fwd")
        u = mm(h3, whole["ffn_w_in"][i], name="mm_ffn_in")
        act = conv_gate_fwd(u, conv_w[i], ffn_conv_b[i][None, :], name="conv_gate_fwd")
        xd = mm(act, whole["ffn_w_out"][i], add=xc, out_dtype=F32, name="mm_ffn_out_res")
        lay.update(xb=xb, h2=h2, hm=hm, qm=qm, kvm=kvm, om=om, xc=xc, h3=h3, u=u, act=act)
        saved.append(lay)
        xa = xd

    dx, dxb, g_final, loss_part = final_loss(xa, norm_final, target, name="final_loss")

    gw = {n: [None] * args[n].shape[0] for n in big}
    g_small = {"norm_mix": [None] * depth, "norm_mem_q": [None] * depth, "norm_mem_kv": [None] * depth,
               "norm_ffn": [None] * depth, "mla_q_norm": [None] * (depth - depth // 2),
               "mla_kv_norm": [None] * (depth - depth // 2), "conv": [None] * depth}
    for i in reversed(range(depth)):
        j = i // 2
        lay = saved[i]
        gw["ffn_w_out"][i] = mm(lay["act"], dxb, mode="tn", out_dtype=F32, name="mm_ffn_out_wgrad")
        da = mm(dxb, whole["ffn_w_out"][i], mode="nt", name="mm_ffn_out_dgrad")
        dcg, dcu, g_small["conv"][i] = conv_gate_bwd(lay["u"], da, conv_w[i], ffn_conv_b[i][None, :], name="conv_gate_bwd")
        du = conv_transpose(dcg, dcu, conv_w[i], name="conv_transpose")
        gw["ffn_w_in"][i] = mm(lay["h3"], du, mode="tn", out_dtype=F32, name="mm_ffn_in_wgrad")
        dh3 = mm(du, whole["ffn_w_in"][i], mode="nt", name="mm_ffn_in_dgrad")
        dx, dxb, g_small["norm_ffn"][i] = rmsnorm_bwd(dh3, lay["xc"], norm_ffn[i], dx, name="norm_bwd")
        gw["mem_w_o"][i] = mm(lay["om"], dxb, mode="tn", out_dtype=F32, name="mm_mem_out_wgrad")
        dom = mm(dxb, whole["mem_w_o"][i], mode="nt", name="mm_mem_out_dgrad")
        dqm, dkm, dvm = mem_bwd(lay["qm"], lay["kvm"], dom, mem_heads, name="mem_bwd")
        dkvm = jnp.concatenate([dkm, dvm], axis=1)
        gw["mem_w_q"][i] = mm(lay["h2"], dqm, mode="tn", out_dtype=F32, name="mm_mem_q_wgrad")
        gw["mem_w_kv"][i] = mm(lay["hm"], dkvm, mode="tn", out_dtype=F32, name="mm_mem_kv_wgrad")
        dh2 = mm(dqm, whole["mem_w_q"][i], mode="nt", name="mm_mem_q_dgrad")
        dhm = mm(dkvm, whole["mem_w_kv"][i], mode="nt", name="mm_mem_kv_dgrad")
        _, _, g_small["norm_mem_kv"][i] = rmsnorm_bwd(dhm, mems, norm_mem_kv[i], name="norm_mem_bwd")
        dx, dxb, g_small["norm_mem_q"][i] = rmsnorm_bwd(dh2, lay["xb"], norm_mem_q[i], dx, name="norm_bwd")
        if i % 2 == 0:
            gw["mla_w_o"][j] = mm(lay["o"], dxb, mode="tn", out_dtype=F32, name="mm_out_wgrad")
            do = mm(dxb, whole["mla_w_o"][j], mode="nt", name="mm_out_dgrad")
            att = (lay["qall"], lay["qr"], lay["kv"], lay["kr"], lay["o"], do, lay["lse"], mla_heads)
            if i == 0:
                def attend_and_reduce(ride, att=att):
                    *dqkv_, parts_ = mla_bwd(*att, name="mla_bwd_reduce", ride=ride)
                    return dqkv_, parts_

                halves_late, (dqn, dqr, dkv, dkr_heads) = reduce_scatter_grads(
                    [gw[n][1:] if n.startswith("mla_") else gw[n] for n in big], axes, core_op, chip_op,
                    attend_and_reduce)
            else:
                dqn, dqr, dkv, dkr_heads = mla_bwd(*att, name="mla_bwd")
            dkr = sum_lane_tiles(dkr_heads, mla_heads, name="sum_heads")
            dqall = jnp.concatenate([dqn, rope(dqr, cos, sin, col0=0, n=mla_heads, inverse=True, name="rope_q_bwd")], axis=1)
            gw["mla_w_ukv"][j] = mm(lay["ckv"], dkv, mode="tn", out_dtype=F32, name="mm_ukv_wgrad")
            dckv = mm(dkv, whole["mla_w_ukv"][j], mode="nt", name="mm_ukv_dgrad")
            gw["mla_w_uq"][j] = _uq_layout_inv(mm(lay["cq"], dqall, mode="tn", out_dtype=F32, name="mm_uq_wgrad"), mla_heads)
            dcq = mm(dqall, w_uq[j], mode="nt", name="mm_uq_dgrad")
            down = lay["down"]
            _, d_q, g_small["mla_q_norm"][j] = rmsnorm_bwd(dcq, down[:, :ql], mla_q_norm[j], name="norm_lora_bwd")
            _, d_kv, g_small["mla_kv_norm"][j] = rmsnorm_bwd(dckv, down[:, ql:ql + kvl], mla_kv_norm[j], name="norm_lora_bwd")
            d_kr = rope(dkr, cos, sin, col0=0, n=1, inverse=True, name="rope_k_bwd")[:, :ROPE_DIM]
            ddown = jnp.concatenate([d_q, d_kv, d_kr], axis=1)
            gw["mla_w_down"][j] = mm(lay["h1"], ddown, mode="tn", out_dtype=F32, name="mm_down_wgrad")
            dh1 = mm(ddown, whole["mla_w_down"][j], mode="nt", name="mm_down_dgrad")
        else:
            gw["sb_w_o"][j] = mm(lay["o"], dxb, mode="tn", out_dtype=F32, name="mm_out_wgrad")
            do = mm(dxb, whole["sb_w_o"][j], mode="nt", name="mm_out_dgrad")
            dq, dk, dv = sb_bwd(lay["qkv"], do, lay["cmat"], tri, sb_heads, name="sb_bwd")
            dqkv = jnp.concatenate([dq, dk, dv], axis=1)
            gw["sb_w_qkv"][j] = mm(lay["h1"], dqkv, mode="tn", out_dtype=F32, name="mm_qkv_wgrad")
            dh1 = mm(dqkv, whole["sb_w_qkv"][j], mode="nt", name="mm_qkv_dgrad")
        dx, dxb, g_small["norm_mix"][i] = rmsnorm_bwd(dh1, lay["xa"], norm_mix[i], dx, name="norm_bwd")

    halves = dict(zip(big, halves_late))
    mla_names = [n for n in big if n.startswith("mla_")]
    halves_first, _ = reduce_scatter_grads([gw[n][:1] for n in mla_names], [axis_of[n] for n in mla_names],
                                           core_op, chip_op)
    for n, first in zip(mla_names, halves_first):
        halves[n] = tuple(jnp.concatenate([a, b], axis=0) for a, b in zip(first, halves[n]))
    conv = jnp.stack(g_small["conv"])
    parts = [jnp.concatenate(g_small[n], axis=0) for n in ("norm_mix", "norm_mem_q", "norm_mem_kv", "norm_ffn")]
    parts += [g_final, jnp.concatenate(g_small["mla_q_norm"], axis=0), jnp.concatenate(g_small["mla_kv_norm"], axis=0),
              conv[:, 3, :], conv[:, :3, :], loss_part[:1, :1]]
    sizes = [p.size for p in parts]
    packed = jnp.concatenate([p.reshape(-1) for p in parts])
    packed = jnp.pad(packed, (0, -packed.size % 1024)).reshape(-1, 128)
    total = allreduce_small(packed).reshape(-1)
    g_rep, at = {}, 0
    for n, p, size in zip(small + ["loss"], parts, sizes):
        g_rep[n] = total[at:at + size].reshape(p.shape)
        at += size
    loss = g_rep.pop("loss").reshape(())
    g_rep["norm_final"] = g_rep["norm_final"].reshape(norm_final.shape)
    width = ffn_conv_w.shape[2]
    g_rep["ffn_conv_w"] = lax.dynamic_slice_in_dim(g_rep["ffn_conv_w"], chip * width, width, axis=2)

    grads, delta, new_m, new_v = {}, {}, {}, {}
    for n in order:
        if n in halves:
            mine, theirs = halves[n]
            grads[n], delta[n], new_m[n], new_v[n] = adamw_halves(
                args[n], mine, theirs, args["m_" + n], args["v_" + n], core_op, name="adamw_big")
        else:
            grads[n] = g_rep[n]
            delta[n], new_m[n], new_v[n] = adamw(args[n], g_rep[n], args["m_" + n], args["v_" + n], name="adamw")
    return (loss, dx[None], *[grads[n] for n in order], *[delta[n] for n in order],
            *[new_m[n] for n in order], *[new_v[n] for n in order])
```

```python
import math

import jax
import jax.numpy as jnp
import numpy as np
from jax import lax
from jax.experimental import pallas as pl
from jax.experimental.pallas import tpu as pltpu

F32 = jnp.float32
BF16 = jnp.bfloat16
MESH = pl.DeviceIdType.MESH

EPS = 1e-6
LOG2E = 1.4426950408889634
CHUNK_SHIFT = 6
HEAD = 128
ROPE_DIM = 64
ROPE_THETA = 10000.0
N_CHIPS = 4
ADAM_LR, ADAM_B1, ADAM_B2, ADAM_EPS, ADAM_WD, ADAM_STEP = 0.001, 0.9, 0.999, 1e-08, 0.01, 10

VMEM_LIMIT_BYTES = 48 * 1024 * 1024
MM_TM, MM_TN, MM_TK = (1024, 1408, 512, 256, 128), (1024, 1408, 512, 256, 128), (1024, 512, 256, 128)
MM_WHOLE_K, MM_WHOLE_TM = 2048, (512, 256, 128)
ROW_TILE = (256, 128, 64, 32, 16, 8)
ATT_T = 512
MLA_HEADS_PER_STEP, MLA_BWD_HEADS_PER_STEP = 4, 2
SB_TQ, SB_TK = 512, 256
SB_HEADS_PER_STEP = 2
SB_SUBS_PER_STEP = 8
SB_DEAD_LOG2 = -200.0
SB_UNVISITED = -1e30
MEM_TQ = 512
CONV_TM, CONV_TN = 512, (512, 256, 128)
HALO = 16


def _pick(dim, prefs):
    for p in prefs:
        if dim % p == 0:
            return p
    return dim


def _dot(a, b, ca, cb):
    return lax.dot_general(a, b, (((ca,), (cb,)), ((), ())), preferred_element_type=F32)


def _params(sem):
    return pltpu.CompilerParams(dimension_semantics=sem, vmem_limit_bytes=VMEM_LIMIT_BYTES)


def _head_groups(heads, per_step):
    hp = per_step if heads % per_step == 0 else 1
    return hp, heads // hp, hp * HEAD


def _tables(pairs):
    arr = np.asarray(pairs, dtype=np.int32)
    return jnp.asarray(arr[:, 0]), jnp.asarray(arr[:, 1])


def mm(a, b, *, mode="nn", add=None, out_dtype=BF16, name):
    if mode == "nn":
        (m, k), (k2, n) = a.shape, b.shape
    elif mode == "nt":
        (m, k), (n, k2) = a.shape, b.shape
    else:
        (k, m), (k2, n) = a.shape, b.shape
    assert k == k2, (a.shape, b.shape, mode)
    ca, cb = {"nn": (1, 0), "nt": (1, 1), "tn": (0, 0)}[mode]
    if k <= MM_WHOLE_K:
        return _mm_whole_k(a, b, add, (m, n, k), (ca, cb), mode, out_dtype, name)
    tm, tn, tk = _pick(m, MM_TM), _pick(n, MM_TN), _pick(k, MM_TK)
    nk = k // tk

    def body(a_ref, b_ref, *rest):
        if add is None:
            o_ref, acc_ref = rest
        else:
            add_ref, o_ref, acc_ref = rest
        kk = pl.program_id(2)

        @pl.when(kk == 0)
        def _():
            acc_ref[...] = jnp.zeros_like(acc_ref)

        acc_ref[...] += _dot(a_ref[...].astype(BF16), b_ref[...].astype(BF16), ca, cb)

        @pl.when(kk == nk - 1)
        def _():
            r = acc_ref[...]
            if add is not None:
                r = r + add_ref[...]
            o_ref[...] = r.astype(o_ref.dtype)

    if mode == "tn":
        a_spec = pl.BlockSpec((tk, tm), lambda i, j, kk: (kk, i))
    else:
        a_spec = pl.BlockSpec((tm, tk), lambda i, j, kk: (i, kk))
    if mode == "nt":
        b_spec = pl.BlockSpec((tn, tk), lambda i, j, kk: (j, kk))
    else:
        b_spec = pl.BlockSpec((tk, tn), lambda i, j, kk: (kk, j))
    o_spec = pl.BlockSpec((tm, tn), lambda i, j, kk: (i, j))
    in_specs, args = [a_spec, b_spec], [a, b]
    if add is not None:
        in_specs.append(o_spec)
        args.append(add)
    return pl.pallas_call(
        body, name=name, grid=(m // tm, n // tn, nk), in_specs=in_specs, out_specs=o_spec,
        out_shape=jax.ShapeDtypeStruct((m, n), out_dtype), scratch_shapes=[pltpu.VMEM((tm, tn), F32)],
        compiler_params=_params(("parallel", "parallel", "arbitrary")),
    )(*args)


def _mm_whole_k(a, b, add, mnk, contract, mode, out_dtype, name):
    m, n, k = mnk
    tm, tn = _pick(m, MM_WHOLE_TM), _pick(n, MM_TN)

    def body(a_ref, b_ref, *rest):
        r = _dot(a_ref[...].astype(BF16), b_ref[...].astype(BF16), *contract)
        if add is not None:
            r = r + rest[0][...]
        rest[-1][...] = r.astype(rest[-1].dtype)

    a_spec = pl.BlockSpec((k, tm), lambda j, i: (0, i)) if mode == "tn" else pl.BlockSpec((tm, k), lambda j, i: (i, 0))
    b_spec = pl.BlockSpec((tn, k), lambda j, i: (j, 0)) if mode == "nt" else pl.BlockSpec((k, tn), lambda j, i: (0, j))
    o_spec = pl.BlockSpec((tm, tn), lambda j, i: (i, j))
    in_specs, args = [a_spec, b_spec], [a, b]
    if add is not None:
        in_specs.append(o_spec)
        args.append(add)
    return pl.pallas_call(
        body, name=name, grid=(n // tn, m // tm), in_specs=in_specs, out_specs=o_spec,
        out_shape=jax.ShapeDtypeStruct((m, n), out_dtype), compiler_params=_params(("parallel", "parallel")),
    )(*args)


def rmsnorm_fwd(x, g, *, name):
    m, d = x.shape
    tm = _pick(m, ROW_TILE)

    def body(x_ref, g_ref, o_ref):
        xv = x_ref[...]
        r = lax.rsqrt(jnp.mean(xv * xv, axis=-1, keepdims=True) + EPS)
        o_ref[...] = (xv * r * g_ref[...]).astype(o_ref.dtype)

    return pl.pallas_call(
        body, name=name, grid=(m // tm,),
        in_specs=[pl.BlockSpec((tm, d), lambda i: (i, 0)), pl.BlockSpec((1, d), lambda i: (0, 0))],
        out_specs=pl.BlockSpec((tm, d), lambda i: (i, 0)), out_shape=jax.ShapeDtypeStruct((m, d), BF16),
        compiler_params=_params(("parallel",)),
    )(x, g.reshape(1, d))


def rmsnorm_bwd(dh, x, g, res=None, *, name):
    m, d = x.shape
    tm = _pick(m, ROW_TILE)

    def body(dh_ref, x_ref, g_ref, *rest):
        if res is None:
            dx_ref, dxb_ref, dg_ref = rest
        else:
            res_ref, dx_ref, dxb_ref, dg_ref = rest

        @pl.when(pl.program_id(0) == 0)
        def _():
            dg_ref[...] = jnp.zeros_like(dg_ref)

        xv = x_ref[...]
        dhv = dh_ref[...].astype(F32)
        r = lax.rsqrt(jnp.mean(xv * xv, axis=-1, keepdims=True) + EPS)
        y = xv * r
        dhg = dhv * g_ref[...]
        dx = r * (dhg - y * jnp.mean(dhg * y, axis=-1, keepdims=True))
        if res is not None:
            dx = dx + res_ref[...]
        dx_ref[...] = dx
        dxb_ref[...] = dx.astype(BF16)
        dg_ref[...] += jnp.sum(dhv * y, axis=0, keepdims=True)

    row = pl.BlockSpec((tm, d), lambda i: (i, 0))
    vec = pl.BlockSpec((1, d), lambda i: (0, 0))
    in_specs, args = [row, row, vec], [dh, x, g.reshape(1, d)]
    if res is not None:
        in_specs.append(row)
        args.append(res)
    return pl.pallas_call(
        body, name=name, grid=(m // tm,), in_specs=in_specs, out_specs=[row, row, vec],
        out_shape=[jax.ShapeDtypeStruct((m, d), F32), jax.ShapeDtypeStruct((m, d), BF16),
                   jax.ShapeDtypeStruct((1, d), F32)],
        compiler_params=_params(("arbitrary",)),
    )(*args)


def final_loss(x, g, target, *, name):
    m, d = x.shape
    tm = _pick(m, ROW_TILE)

    def body(x_ref, g_ref, t_ref, dx_ref, dxb_ref, dg_ref, loss_ref):
        @pl.when(pl.program_id(0) == 0)
        def _():
            dg_ref[...] = jnp.zeros_like(dg_ref)
            loss_ref[...] = jnp.zeros_like(loss_ref)

        xv = x_ref[...]
        gv = g_ref[...]
        r = lax.rsqrt(jnp.mean(xv * xv, axis=-1, keepdims=True) + EPS)
        y = xv * r
        err = y * gv - t_ref[...]
        loss_ref[...] += 0.5 * jnp.sum(jnp.mean(err * err, axis=-1, keepdims=True))
        dy = err * (1.0 / d)
        dyg = dy * gv
        dx = r * (dyg - y * jnp.mean(dyg * y, axis=-1, keepdims=True))
        dx_ref[...] = dx
        dxb_ref[...] = dx.astype(BF16)
        dg_ref[...] += jnp.sum(dy * y, axis=0, keepdims=True)

    row = pl.BlockSpec((tm, d), lambda i: (i, 0))
    vec = pl.BlockSpec((1, d), lambda i: (0, 0))
    return pl.pallas_call(
        body, name=name, grid=(m // tm,), in_specs=[row, vec, row],
        out_specs=[row, row, vec, pl.BlockSpec((8, 128), lambda i: (0, 0))],
        out_shape=[jax.ShapeDtypeStruct((m, d), F32), jax.ShapeDtypeStruct((m, d), BF16),
                   jax.ShapeDtypeStruct((1, d), F32), jax.ShapeDtypeStruct((8, 128), F32)],
        compiler_params=_params(("arbitrary",)),
    )(x, g.reshape(1, d), target)


def rope(xin, cos, sin, *, col0, n, inverse, name):
    s = xin.shape[0]
    tm = _pick(s, ROW_TILE)
    half = ROPE_DIM // 2

    wide = n * HEAD
    assert col0 % n == 0 and wide & (wide - 1) == 0

    def body(x_ref, c_ref, s_ref, o_ref):
        xv = x_ref[...].astype(F32)
        lane = lax.broadcasted_iota(jnp.int32, xv.shape, 1) & (HEAD - 1)
        rot = jnp.where(lane < half, -pltpu.roll(xv, wide - half, 1), pltpu.roll(xv, half, 1))
        sv = jnp.tile(s_ref[...], (1, n))
        if inverse:
            sv = -sv
        o_ref[...] = (xv * jnp.tile(c_ref[...], (1, n)) + rot * sv).astype(o_ref.dtype)

    tab = pl.BlockSpec((tm, HEAD), lambda i: (i, 0))
    return pl.pallas_call(
        body, name=name, grid=(s // tm,),
        in_specs=[pl.BlockSpec((tm, wide), lambda i: (i, col0 // n)), tab, tab],
        out_specs=pl.BlockSpec((tm, wide), lambda i: (i, 0)),
        out_shape=jax.ShapeDtypeStruct((s, wide), BF16),
        compiler_params=_params(("parallel",)),
    )(xin, cos, sin)


def sum_lane_tiles(xin, n, *, name):
    s = xin.shape[0]
    tm = _pick(s, ROW_TILE)

    def body(x_ref, o_ref):
        acc = x_ref[:, :HEAD]
        for k in range(1, n):
            acc = acc + x_ref[:, k * HEAD:(k + 1) * HEAD]
        o_ref[...] = acc

    return pl.pallas_call(
        body, name=name, grid=(s // tm,), in_specs=[pl.BlockSpec((tm, n * HEAD), lambda i: (i, 0))],
        out_specs=pl.BlockSpec((tm, HEAD), lambda i: (i, 0)), out_shape=jax.ShapeDtypeStruct((s, HEAD), F32),
        compiler_params=_params(("parallel",)),
    )(xin)


def _mla_scores(q, k, scale, diagonal):
    sc = _dot(q, k, 1, 1) * scale
    if not diagonal:
        return sc
    qchunk = jnp.right_shift(lax.broadcasted_iota(jnp.int32, sc.shape, 0), CHUNK_SHIFT)
    kchunk = jnp.right_shift(lax.broadcasted_iota(jnp.int32, sc.shape, 1), CHUNK_SHIFT)
    return jnp.where(kchunk <= qchunk, sc, -jnp.inf)


def mla_fwd(qall, qr, kv, kr, heads, *, name, ride=None):
    s = qr.shape[0]
    t = min(ATT_T, s)
    nq = s // t
    scale = 1.0 / math.sqrt(HEAD + ROPE_DIM)
    pairs = [(qi, kj) for qi in range(nq) for kj in range(qi + 1)]
    qtab, ktab = _tables(pairs)
    hp, groups, wide = _head_groups(heads, MLA_HEADS_PER_STEP)
    r_in, r_in_specs, r_out, r_out_specs, r_sems = _ride_specs(ride)

    def body(qt_ref, kt_ref, qn_ref, qr_ref, kv_ref, kr_ref, *refs):
        o_ref, lse_ref = refs[len(r_in):len(r_in) + 2]
        m_ref, l_ref, acc_ref = refs[len(r_in) + 2 + len(r_out):len(r_in) + 5 + len(r_out)]
        st = pl.program_id(1)
        qi, kj = qt_ref[st], kt_ref[st]
        _ride_along(ride, refs[:len(r_in)], refs[len(r_in) + 2:len(r_in) + 2 + len(r_out)], refs[len(refs) - 2:],
                    (pl.program_id(0) == 0) & (st == 0), (pl.program_id(0) == groups - 1) & (st == len(pairs) - 1))

        @pl.when(kj == 0)
        def _():
            m_ref[...] = jnp.full_like(m_ref, -jnp.inf)
            l_ref[...] = jnp.zeros_like(l_ref)
            acc_ref[...] = jnp.zeros_like(acc_ref)

        def step(diagonal):
            for hh in range(hp):
                cols = slice(hh * HEAD, (hh + 1) * HEAD)
                kn = kv_ref[:, 2 * hh * HEAD:(2 * hh + 1) * HEAD]
                v = kv_ref[:, (2 * hh + 1) * HEAD:(2 * hh + 2) * HEAD]
                sc = _mla_scores(jnp.concatenate([qn_ref[:, cols], qr_ref[:, cols]], axis=1),
                                 jnp.concatenate([kn, kr_ref[...]], axis=1), scale, diagonal)
                m_prev = m_ref[hh]
                m_new = jnp.maximum(m_prev, jnp.max(sc, axis=-1, keepdims=True))
                p = jnp.exp(sc - m_new)
                alpha = jnp.exp(m_prev - m_new)
                pv = _dot(p.astype(BF16), jnp.concatenate([v, jnp.ones_like(v)], axis=1), 1, 0)
                l_ref[hh] = alpha * l_ref[hh] + pv[:, HEAD:HEAD + 1]
                acc_ref[:, cols] = alpha * acc_ref[:, cols] + pv[:, :HEAD]
                m_ref[hh] = m_new

        @pl.when(kj < qi)
        def _():
            step(False)

        @pl.when(kj == qi)
        def _():
            step(True)
            for hh in range(hp):
                cols = slice(hh * HEAD, (hh + 1) * HEAD)
                o_ref[:, cols] = (acc_ref[:, cols] / l_ref[hh]).astype(o_ref.dtype)
                lse_ref[:, cols] = jnp.broadcast_to(m_ref[hh] + jnp.log(l_ref[hh]), (t, HEAD))

    qspec = pl.BlockSpec((t, wide), lambda h, st, qt, kt: (qt[st], h))
    grid_spec = pltpu.PrefetchScalarGridSpec(
        num_scalar_prefetch=2, grid=(groups, len(pairs)),
        in_specs=[qspec, qspec,
                  pl.BlockSpec((t, 2 * wide), lambda h, st, qt, kt: (kt[st], h)),
                  pl.BlockSpec((t, HEAD), lambda h, st, qt, kt: (kt[st], 0))] + r_in_specs,
        out_specs=[qspec, qspec] + r_out_specs,
        scratch_shapes=[pltpu.VMEM((hp, t, 1), F32), pltpu.VMEM((hp, t, 1), F32), pltpu.VMEM((t, wide), F32)] + r_sems)
    outs = pl.pallas_call(
        body, name=name, grid_spec=grid_spec,
        out_shape=[jax.ShapeDtypeStruct((s, heads * HEAD), BF16), jax.ShapeDtypeStruct((s, heads * HEAD), F32)] + r_out,
        compiler_params=_params(("arbitrary", "arbitrary") if ride else ("parallel", "arbitrary")),
    )(qtab, ktab, qall, qr, kv, kr, *r_in)
    return outs if ride is None else (outs[0], outs[1], list(outs[2:]))


def mla_bwd(qall, qr, kv, kr, o, do, lse, heads, *, name, ride=None):
    s = qr.shape[0]
    t = min(ATT_T, s)
    nq = s // t
    scale = 1.0 / math.sqrt(HEAD + ROPE_DIM)
    pairs = [(kj, qi) for kj in range(nq) for qi in range(kj, nq)]
    ktab, qtab = _tables(pairs)
    last = len(pairs) - 1
    hp, groups, wide = _head_groups(heads, MLA_BWD_HEADS_PER_STEP)
    r_in, r_in_specs, r_out, r_out_specs, r_sems = _ride_specs(ride)

    def body(kt_ref, qt_ref, qn_ref, qr_ref, kv_ref, kr_ref, o_ref, do_ref, lse_ref, *refs):
        n_i, n_o = len(r_in), len(r_out)
        dqn_ref, dqr_ref, dkv_ref, dkr_ref = refs[n_i:n_i + 4]
        fn_ref, fr_ref, akn_ref, av_ref, akr_ref = refs[n_i + 4 + n_o:n_i + 9 + n_o]
        st = pl.program_id(1)
        kj, qi = kt_ref[st], qt_ref[st]
        _ride_along(ride, refs[:n_i], refs[n_i + 4:n_i + 4 + n_o], refs[len(refs) - 2:],
                    (pl.program_id(0) == 0) & (st == 0), (pl.program_id(0) == groups - 1) & (st == last))

        @pl.when(st == 0)
        def _():
            fn_ref[...] = jnp.zeros_like(fn_ref)
            fr_ref[...] = jnp.zeros_like(fr_ref)

        @pl.when(qi == kj)
        def _():
            akn_ref[...] = jnp.zeros_like(akn_ref)
            av_ref[...] = jnp.zeros_like(av_ref)
            akr_ref[...] = jnp.zeros_like(akr_ref)

        def step(diagonal):
            rows = pl.ds(pl.multiple_of(qi * t, t), t)
            krv = kr_ref[...]
            for hh in range(hp):
                cols = slice(hh * HEAD, (hh + 1) * HEAD)
                dov = do_ref[:, cols]
                q2 = jnp.concatenate([qn_ref[:, cols], qr_ref[:, cols]], axis=1)
                k2 = jnp.concatenate([kv_ref[:, 2 * hh * HEAD:(2 * hh + 1) * HEAD], krv], axis=1)
                v = kv_ref[:, (2 * hh + 1) * HEAD:(2 * hh + 2) * HEAD]
                sc = _mla_scores(q2, k2, scale, diagonal)
                p = jnp.exp(sc - lse_ref[:, hh * HEAD:hh * HEAD + 1])
                delta = jnp.sum(dov.astype(F32) * o_ref[:, cols].astype(F32), axis=-1, keepdims=True)
                dp = _dot(dov, v, 1, 1)
                ds = (p * (dp - delta) * scale).astype(BF16)
                av_ref[:, cols] += _dot(p.astype(BF16), dov, 0, 0)
                dk2 = _dot(ds, q2, 0, 0)
                akn_ref[:, cols] += dk2[:, :HEAD]
                akr_ref[:, cols] += dk2[:, HEAD:]
                dq2 = _dot(ds, k2, 1, 0)
                fn_ref[rows, cols] += dq2[:, :HEAD]
                fr_ref[rows, cols] += dq2[:, HEAD:]

        @pl.when(qi == kj)
        def _():
            step(True)

        @pl.when(qi > kj)
        def _():
            step(False)

        @pl.when(qi == nq - 1)
        def _():
            for hh in range(hp):
                cols = slice(hh * HEAD, (hh + 1) * HEAD)
                dkv_ref[:, 2 * hh * HEAD:(2 * hh + 1) * HEAD] = akn_ref[:, cols].astype(dkv_ref.dtype)
                dkv_ref[:, (2 * hh + 1) * HEAD:(2 * hh + 2) * HEAD] = av_ref[:, cols].astype(dkv_ref.dtype)
            dkr_ref[...] = akr_ref[...]

        @pl.when(st == last)
        def _():
            dqn_ref[...] = fn_ref[...].astype(dqn_ref.dtype)
            dqr_ref[...] = fr_ref[...].astype(dqr_ref.dtype)

    qspec = pl.BlockSpec((t, wide), lambda h, st, kt, qt: (qt[st], h))
    kvspec = pl.BlockSpec((t, 2 * wide), lambda h, st, kt, qt: (kt[st], h))
    krspec = pl.BlockSpec((t, HEAD), lambda h, st, kt, qt: (kt[st], 0))
    headspec = pl.BlockSpec((s, wide), lambda h, st, kt, qt: (0, h))
    grid_spec = pltpu.PrefetchScalarGridSpec(
        num_scalar_prefetch=2, grid=(groups, len(pairs)),
        in_specs=[qspec, qspec, kvspec, krspec, qspec, qspec, qspec] + r_in_specs,
        out_specs=[headspec, headspec, kvspec, pl.BlockSpec((t, wide), lambda h, st, kt, qt: (kt[st], h))] + r_out_specs,
        scratch_shapes=[pltpu.VMEM((s, wide), F32), pltpu.VMEM((s, wide), F32), pltpu.VMEM((t, wide), F32),
                        pltpu.VMEM((t, wide), F32), pltpu.VMEM((t, wide), F32)] + r_sems)
    outs = pl.pallas_call(
        body, name=name, grid_spec=grid_spec,
        out_shape=[jax.ShapeDtypeStruct((s, heads * HEAD), BF16), jax.ShapeDtypeStruct((s, heads * HEAD), BF16),
                   jax.ShapeDtypeStruct((s, heads * 2 * HEAD), BF16), jax.ShapeDtypeStruct((s, heads * HEAD), F32)] + r_out,
        compiler_params=_params(("arbitrary", "arbitrary") if ride else ("parallel", "arbitrary")),
    )(ktab, qtab, qall, qr, kv, kr, o, do, lse, *r_in)
    return outs if ride is None else (*outs[:4], list(outs[4:]))


def _split_dot(val, tri, cb):
    hi = val.astype(BF16)
    lo = (val - hi.astype(F32)).astype(BF16)
    return _dot(hi, tri, 1, cb) + _dot(lo, tri, 1, cb)


def _sb_logs(q, k, offset, scale, masked):
    z = _dot(q, k, 1, 1) * (scale * LOG2E)
    sp = jnp.log2(1.0 + jnp.exp2(-jnp.abs(z)))
    ls = jnp.minimum(z, 0.0) - sp
    lk = ls - z
    if not masked:
        return None, ls, lk
    strict = (lax.broadcasted_iota(jnp.int32, z.shape, 1) + offset) < lax.broadcasted_iota(jnp.int32, z.shape, 0)
    return strict, ls, jnp.where(strict, lk, 0.0)


def _lane_pick(blk, idx):
    lane = lax.broadcasted_iota(jnp.int32, blk.shape, 1)
    return jnp.sum(jnp.where(lane == idx, blk, 0.0), axis=-1, keepdims=True)


def _lane_put(blk, idx, col):
    lane = lax.broadcasted_iota(jnp.int32, blk.shape, 1)
    return jnp.where(lane == idx, col, blk)


def _sb_tiles(s):
    tq = min(SB_TQ, s)
    tk = min(SB_TK, tq)
    assert s // tk <= HEAD
    return tq, tk


def sb_fwd(qkv, tri, heads, *, name):
    s = qkv.shape[0]
    tq, tk = _sb_tiles(s)
    nq = s // tq
    nsub = min(SB_SUBS_PER_STEP, s // tk)
    span = nsub * tk
    scale = 1.0 / math.sqrt(HEAD)
    pairs = [(qi, sb) for qi in range(nq) for sb in range(((qi + 1) * tq - 1) // span, -1, -1)]
    qtab, ktab = _tables(pairs)
    hp, groups, wide = _head_groups(heads, SB_HEADS_PER_STEP)

    def body(qt_ref, kt_ref, q_ref, k_ref, v_ref, tri_ref, o_ref, c_ref, carry_ref, acc_ref):
        st = pl.program_id(1)
        qi, sb = qt_ref[st], kt_ref[st]

        @pl.when(sb == ((qi + 1) * tq - 1) // span)
        def _():
            carry_ref[...] = jnp.zeros_like(carry_ref)
            acc_ref[...] = jnp.zeros_like(acc_ref)
            c_ref[...] = jnp.full_like(c_ref, SB_UNVISITED)

        def sub_block(i, _):
            sub = nsub - 1 - i
            kj = sb * nsub + sub
            keys = pl.ds(pl.multiple_of(sub * tk, tk), tk)
            alive = (kj * tk < (qi + 1) * tq) & (jnp.max(carry_ref[...]) > SB_DEAD_LOG2)

            def step(masked):
                for hh in range(hp):
                    cols = slice(hh * HEAD, (hh + 1) * HEAD)
                    strict, ls, lk = _sb_logs(q_ref[:, cols], k_ref[keys, cols], kj * tk - qi * tq, scale, masked)
                    carry = carry_ref[hh]
                    a = jnp.exp2(ls + _split_dot(lk, tri_ref[...], 0) + carry)
                    if masked:
                        a = jnp.where(strict, a, 0.0)
                    acc_ref[:, cols] += _dot(a.astype(BF16), v_ref[keys, cols], 1, 0)
                    c_ref[:, cols] = _lane_put(c_ref[:, cols], kj, carry)
                    carry_ref[hh] = carry + jnp.sum(lk, axis=-1, keepdims=True)

            @pl.when(alive & ((kj + 1) * tk > qi * tq))
            def _():
                step(True)

            @pl.when(alive & ((kj + 1) * tk <= qi * tq))
            def _():
                step(False)

            return 0

        lax.fori_loop(0, nsub, sub_block, 0)

        @pl.when(sb == 0)
        def _():
            o_ref[...] = acc_ref[...].astype(o_ref.dtype)

    qspec = pl.BlockSpec((tq, wide), lambda h, st, qt, kt: (qt[st], h))
    grid_spec = pltpu.PrefetchScalarGridSpec(
        num_scalar_prefetch=2, grid=(groups, len(pairs)),
        in_specs=[qspec,
                  pl.BlockSpec((span, wide), lambda h, st, qt, kt: (kt[st], groups + h)),
                  pl.BlockSpec((span, wide), lambda h, st, qt, kt: (kt[st], 2 * groups + h)),
                  pl.BlockSpec((tk, tk), lambda h, st, qt, kt: (0, 0))],
        out_specs=[qspec, qspec],
        scratch_shapes=[pltpu.VMEM((hp, tq, 1), F32), pltpu.VMEM((tq, wide), F32)])
    return pl.pallas_call(
        body, name=name, grid_spec=grid_spec,
        out_shape=[jax.ShapeDtypeStruct((s, heads * HEAD), BF16), jax.ShapeDtypeStruct((s, heads * HEAD), F32)],
        compiler_params=_params(("parallel", "arbitrary")),
    )(qtab, ktab, qkv, qkv, qkv, tri)


def sb_bwd(qkv, do, cmat, tri, heads, *, name):
    s = qkv.shape[0]
    tq, tk = _sb_tiles(s)
    nq = s // tq
    nsub = min(SB_SUBS_PER_STEP, s // tk)
    span = nsub * tk
    scale = 1.0 / math.sqrt(HEAD)
    pairs = [(sb, qi) for sb in range(s // span) for qi in range(sb * span // tq, nq)]
    ktab, qtab = _tables(pairs)
    last = len(pairs) - 1
    hp, groups, wide = _head_groups(heads, SB_HEADS_PER_STEP)

    def body(kt_ref, qt_ref, q_ref, k_ref, v_ref, do_ref, c_ref, tri_ref, dq_ref, dk_ref, dv_ref,
             dqf_ref, gsum_ref, ak_ref, av_ref):
        st = pl.program_id(1)
        sb, qi = kt_ref[st], qt_ref[st]

        @pl.when(st == 0)
        def _():
            dqf_ref[...] = jnp.zeros_like(dqf_ref)
            gsum_ref[...] = jnp.zeros_like(gsum_ref)

        @pl.when(qi == sb * span // tq)
        def _():
            ak_ref[...] = jnp.zeros_like(ak_ref)
            av_ref[...] = jnp.zeros_like(av_ref)

        def sub_block(sub, _):
            kj = sb * nsub + sub
            keys = pl.ds(pl.multiple_of(sub * tk, tk), tk)

            def step(masked):
                rows = pl.ds(pl.multiple_of(qi * tq, tq), tq)
                tri_v = tri_ref[...]
                for hh in range(hp):
                    cols = slice(hh * HEAD, (hh + 1) * HEAD)
                    qv, kblk, dov = q_ref[:, cols], k_ref[keys, cols], do_ref[:, cols]
                    strict, ls, lk = _sb_logs(qv, kblk, kj * tk - qi * tq, scale, masked)
                    a = jnp.exp2(ls + _split_dot(lk, tri_v, 0) + _lane_pick(c_ref[:, cols], kj))
                    if masked:
                        a = jnp.where(strict, a, 0.0)
                    g = _dot(dov, v_ref[keys, cols], 1, 1) * a
                    before_all = gsum_ref[hh, rows, :]
                    before = _split_dot(g, tri_v, 1) + before_all[:, :1]
                    beta = jnp.exp2(ls)
                    dz = g * (1.0 - beta) - before * beta
                    if masked:
                        dz = jnp.where(strict, dz, 0.0)
                    dzb = dz.astype(BF16)
                    av_ref[keys, cols] += _dot(a.astype(BF16), dov, 0, 0)
                    ak_ref[keys, cols] += _dot(dzb, qv, 0, 0)
                    dqf_ref[rows, cols] += _dot(dzb, kblk, 1, 0)
                    gsum_ref[hh, rows, :] = before_all + jnp.sum(g, axis=-1, keepdims=True)

            lane = lax.broadcasted_iota(jnp.int32, (tq, wide), 1)
            seen = jnp.max(jnp.where((lane & (HEAD - 1)) == kj, c_ref[...], SB_UNVISITED)) > SB_DEAD_LOG2
            alive = (kj * tk < (qi + 1) * tq) & seen

            @pl.when(alive & ((kj + 1) * tk > qi * tq))
            def _():
                step(True)

            @pl.when(alive & ((kj + 1) * tk <= qi * tq))
            def _():
                step(False)

            return 0

        lax.fori_loop(0, nsub, sub_block, 0)

        @pl.when(qi == nq - 1)
        def _():
            dk_ref[...] = (ak_ref[...] * scale).astype(dk_ref.dtype)
            dv_ref[...] = av_ref[...].astype(dv_ref.dtype)

        @pl.when(st == last)
        def _():
            dq_ref[...] = (dqf_ref[...] * scale).astype(dq_ref.dtype)

    qspec = pl.BlockSpec((tq, wide), lambda h, st, kt, qt: (qt[st], h))
    ospec = pl.BlockSpec((span, wide), lambda h, st, kt, qt: (kt[st], h))
    grid_spec = pltpu.PrefetchScalarGridSpec(
        num_scalar_prefetch=2, grid=(groups, len(pairs)),
        in_specs=[qspec,
                  pl.BlockSpec((span, wide), lambda h, st, kt, qt: (kt[st], groups + h)),
                  pl.BlockSpec((span, wide), lambda h, st, kt, qt: (kt[st], 2 * groups + h)),
                  qspec, qspec, pl.BlockSpec((tk, tk), lambda h, st, kt, qt: (0, 0))],
        out_specs=[pl.BlockSpec((s, wide), lambda h, st, kt, qt: (0, h)), ospec, ospec],
        scratch_shapes=[pltpu.VMEM((s, wide), F32), pltpu.VMEM((hp, s, HEAD), F32), pltpu.VMEM((span, wide), F32),
                        pltpu.VMEM((span, wide), F32)])
    return pl.pallas_call(
        body, name=name, grid_spec=grid_spec,
        out_shape=[jax.ShapeDtypeStruct((s, heads * HEAD), BF16)] * 3,
        compiler_params=_params(("parallel", "arbitrary")),
    )(ktab, qtab, qkv, qkv, qkv, do, cmat, tri)


def _mem_probs(q, k, scale):
    sc = _dot(q, k, 1, 1) * scale
    e = jnp.exp(sc - jnp.max(sc, axis=-1, keepdims=True))
    return e / jnp.sum(e, axis=-1, keepdims=True)


def mem_fwd(q, kvm, heads, *, name):
    s, nm = q.shape[0], kvm.shape[0]
    tq = min(MEM_TQ, s)
    scale = 1.0 / math.sqrt(HEAD)

    def body(q_ref, k_ref, v_ref, o_ref):
        p = _mem_probs(q_ref[...], k_ref[...], scale)
        o_ref[...] = _dot(p.astype(BF16), v_ref[...], 1, 0).astype(o_ref.dtype)

    qspec = pl.BlockSpec((tq, HEAD), lambda h, qi: (qi, h))
    return pl.pallas_call(
        body, name=name, grid=(heads, s // tq),
        in_specs=[qspec, pl.BlockSpec((nm, HEAD), lambda h, qi: (0, h)),
                  pl.BlockSpec((nm, HEAD), lambda h, qi: (0, heads + h))],
        out_specs=qspec, out_shape=jax.ShapeDtypeStruct((s, heads * HEAD), BF16),
        compiler_params=_params(("parallel", "parallel")),
    )(q, kvm, kvm)


def mem_bwd(q, kvm, do, heads, *, name):
    s, nm = q.shape[0], kvm.shape[0]
    tq = min(MEM_TQ, s)
    scale = 1.0 / math.sqrt(HEAD)

    def body(q_ref, k_ref, v_ref, do_ref, dq_ref, dk_ref, dv_ref):
        @pl.when(pl.program_id(1) == 0)
        def _():
            dk_ref[...] = jnp.zeros_like(dk_ref)
            dv_ref[...] = jnp.zeros_like(dv_ref)

        qv, kvv, dov = q_ref[...], k_ref[...], do_ref[...]
        p = _mem_probs(qv, kvv, scale)
        dp = _dot(dov, v_ref[...], 1, 1)
        ds = (p * (dp - jnp.sum(dp * p, axis=-1, keepdims=True)) * scale).astype(BF16)
        dq_ref[...] = _dot(ds, kvv, 1, 0).astype(dq_ref.dtype)
        dk_ref[...] += _dot(ds, qv, 0, 0)
        dv_ref[...] += _dot(p.astype(BF16), dov, 0, 0)

    qspec = pl.BlockSpec((tq, HEAD), lambda h, qi: (qi, h))
    kspec = pl.BlockSpec((nm, HEAD), lambda h, qi: (0, h))
    return pl.pallas_call(
        body, name=name, grid=(heads, s // tq),
        in_specs=[qspec, kspec, pl.BlockSpec((nm, HEAD), lambda h, qi: (0, heads + h)), qspec],
        out_specs=[qspec, kspec, kspec],
        out_shape=[jax.ShapeDtypeStruct((s, heads * HEAD), BF16), jax.ShapeDtypeStruct((nm, heads * HEAD), F32),
                   jax.ShapeDtypeStruct((nm, heads * HEAD), F32)],
        compiler_params=_params(("parallel", "arbitrary")),
    )(q, kvm, kvm, do)


def _conv3(u, halo, w, b):
    tm = u.shape[0]
    row = lax.broadcasted_iota(jnp.int32, u.shape, 0)
    h1, h2 = halo[HALO - 1:HALO, :], halo[HALO - 2:HALO - 1, :]
    u1 = jnp.where(row == 0, h1, pltpu.roll(u, 1, 0))
    u2 = jnp.where(row == 0, h2, jnp.where(row == 1, h1, pltpu.roll(u, 2 % tm, 0)))
    return b + w[0:1, :] * u2 + w[1:2, :] * u1 + w[2:3, :] * u, u1, u2


def _conv_specs(s, f):
    tm, tn = min(CONV_TM, s), _pick(f, CONV_TN)
    return tm, tn, s // tm, f // tn


def _silu_parts(g):
    sg = 1.0 / (1.0 + jnp.exp(-g))
    return g * sg, sg


def conv_gate_fwd(u, cw, cb, *, name):
    s, f = u.shape[0], u.shape[1] // 2
    tm, tn, ni, nj = _conv_specs(s, f)
    hb = tm // HALO

    def body(ug_ref, uu_ref, hg_ref, hu_ref, wg_ref, wu_ref, bg_ref, bu_ref, a_ref):
        keep = (pl.program_id(1) > 0).astype(F32)
        gate, _, _ = _conv3(ug_ref[...].astype(F32), hg_ref[...].astype(F32) * keep, wg_ref[...], bg_ref[...])
        up, _, _ = _conv3(uu_ref[...].astype(F32), hu_ref[...].astype(F32) * keep, wu_ref[...], bu_ref[...])
        a_ref[...] = (_silu_parts(gate)[0] * up).astype(a_ref.dtype)

    def main(off):
        return pl.BlockSpec((tm, tn), lambda j, i: (i, j + off))

    def halo(off):
        return pl.BlockSpec((HALO, tn), lambda j, i: (jnp.maximum(i * hb - 1, 0), j + off))

    def par(rows, off):
        return pl.BlockSpec((rows, tn), lambda j, i: (0, j + off))

    return pl.pallas_call(
        body, name=name, grid=(nj, ni),
        in_specs=[main(0), main(nj), halo(0), halo(nj), par(3, 0), par(3, nj), par(1, 0), par(1, nj)],
        out_specs=main(0), out_shape=jax.ShapeDtypeStruct((s, f), BF16),
        compiler_params=_params(("parallel", "parallel")),
    )(u, u, u, u, cw, cw, cb, cb)


def conv_gate_bwd(u, da, cw, cb, *, name):
    s, f = u.shape[0], u.shape[1] // 2
    tm, tn, ni, nj = _conv_specs(s, f)
    hb = tm // HALO

    def body(ug_ref, uu_ref, hg_ref, hu_ref, da_ref, wg_ref, wu_ref, bg_ref, bu_ref, dg_ref, du_ref, pg_ref, pu_ref):
        @pl.when(pl.program_id(1) == 0)
        def _():
            pg_ref[...] = jnp.zeros_like(pg_ref)
            pu_ref[...] = jnp.zeros_like(pu_ref)

        keep = (pl.program_id(1) > 0).astype(F32)
        ug, uu = ug_ref[...].astype(F32), uu_ref[...].astype(F32)
        gate, ug1, ug2 = _conv3(ug, hg_ref[...].astype(F32) * keep, wg_ref[...], bg_ref[...])
        up, uu1, uu2 = _conv3(uu, hu_ref[...].astype(F32) * keep, wu_ref[...], bu_ref[...])
        act, sg = _silu_parts(gate)
        dav = da_ref[...].astype(F32)
        d_gate = dav * up * (sg * (1.0 + gate * (1.0 - sg)))
        d_up = dav * act
        dg_ref[...] = d_gate.astype(dg_ref.dtype)
        du_ref[...] = d_up.astype(du_ref.dtype)
        for p_ref, dc, taps in ((pg_ref, d_gate, (ug2, ug1, ug)), (pu_ref, d_up, (uu2, uu1, uu))):
            for r, tap in enumerate(taps):
                p_ref[r:r + 1, :] += jnp.sum(dc * tap, axis=0, keepdims=True)
            p_ref[3:4, :] += jnp.sum(dc, axis=0, keepdims=True)

    def main(off):
        return pl.BlockSpec((tm, tn), lambda j, i: (i, j + off))

    def halo(off):
        return pl.BlockSpec((HALO, tn), lambda j, i: (jnp.maximum(i * hb - 1, 0), j + off))

    def par(rows, off):
        return pl.BlockSpec((rows, tn), lambda j, i: (0, j + off))

    dg, du, pg, pu = pl.pallas_call(
        body, name=name, grid=(nj, ni),
        in_specs=[main(0), main(nj), halo(0), halo(nj), main(0), par(3, 0), par(3, nj), par(1, 0), par(1, nj)],
        out_specs=[main(0), main(0), par(8, 0), par(8, 0)],
        out_shape=[jax.ShapeDtypeStruct((s, f), BF16), jax.ShapeDtypeStruct((s, f), BF16),
                   jax.ShapeDtypeStruct((8, f), F32), jax.ShapeDtypeStruct((8, f), F32)],
        compiler_params=_params(("parallel", "arbitrary")),
    )(u, u, u, u, da, cw, cw, cb, cb)
    return dg, du, jnp.concatenate([pg, pu], axis=1)


def conv_transpose(dcg, dcu, w, *, name):
    s, f = dcg.shape
    tm, tn, ni, nj = _conv_specs(s, f)
    hb = tm // HALO

    def body(dg_ref, hg_ref, du_ref, hu_ref, w_ref, o_ref):
        inner = (pl.program_id(1) < ni - 1).astype(F32)

        @pl.when(pl.program_id(0) < nj)
        def _():
            one_half(dg_ref, hg_ref, w_ref, o_ref, inner)

        @pl.when(pl.program_id(0) >= nj)
        def _():
            one_half(du_ref, hu_ref, w_ref, o_ref, inner)

    def one_half(d_ref, h_ref, w_ref, o_ref, inner):
        d = d_ref[...].astype(F32)
        halo = h_ref[...].astype(F32) * inner
        row = lax.broadcasted_iota(jnp.int32, d.shape, 0)
        n0, n1 = halo[0:1, :], halo[1:2, :]
        d1 = jnp.where(row == tm - 1, n0, pltpu.roll(d, tm - 1, 0))
        d2 = jnp.where(row == tm - 2, n0, jnp.where(row == tm - 1, n1, pltpu.roll(d, tm - 2, 0)))
        wv = w_ref[...]
        o_ref[...] = (wv[2:3, :] * d + wv[1:2, :] * d1 + wv[0:1, :] * d2).astype(o_ref.dtype)

    def half_specs(lo):
        def col(j):
            return jnp.clip(j - lo, 0, nj - 1)

        def row(j, i):
            return jnp.where((j >= lo) & (j < lo + nj), i, 0)

        return [pl.BlockSpec((tm, tn), lambda j, i: (row(j, i), col(j))),
                pl.BlockSpec((HALO, tn), lambda j, i: (jnp.minimum((row(j, i) + 1) * hb, s // HALO - 1), col(j)))]

    return pl.pallas_call(
        body, name=name, grid=(2 * nj, ni),
        in_specs=half_specs(0) + half_specs(nj) + [pl.BlockSpec((3, tn), lambda j, i: (0, j))],
        out_specs=pl.BlockSpec((tm, tn), lambda j, i: (i, j)), out_shape=jax.ShapeDtypeStruct((s, 2 * f), BF16),
        compiler_params=_params(("parallel", "parallel")),
    )(dcg, dcg, dcu, dcu, w)


def _tile2d(r, c):
    return _pick(r, ROW_TILE), _pick(c, (2048, 1024, 512, 256, 128))


def sum_slots(buf, *, name):
    n, shape = buf.shape[0], buf.shape[1:]
    r, c = math.prod(shape[:-1]), shape[-1]
    tm, tn = _tile2d(r, c)

    def body(b_ref, o_ref):
        acc = b_ref[0]
        for k in range(1, n):
            acc = acc + b_ref[k]
        o_ref[...] = acc

    out = pl.pallas_call(
        body, name=name, grid=(r // tm, c // tn),
        in_specs=[pl.BlockSpec((n, tm, tn), lambda i, j: (0, i, j))],
        out_specs=pl.BlockSpec((tm, tn), lambda i, j: (i, j)), out_shape=jax.ShapeDtypeStruct((r, c), F32),
        compiler_params=_params(("parallel", "parallel")),
    )(buf.reshape(n, r, c))
    return out.reshape(shape)


def add_half(layers, got, core, *, name):
    n_l = len(layers)
    a, _, r, c = layers[0].shape
    tm, tn = _tile2d(r, c)

    def body(core_ref, *refs):
        got_ref, o_ref = refs[n_l:]
        for l in range(n_l):
            @pl.when(pl.program_id(0) == l)
            def _(l=l):
                o_ref[...] = (refs[l][...] + got_ref[...]).astype(o_ref.dtype)

    def layer_spec(l):
        def index(b, q, i, j, core_ref):
            on = b == l
            return (jnp.where(on, q, 0), core_ref[0], jnp.where(on, i, 0), jnp.where(on, j, 0))
        return pl.BlockSpec((None, None, tm, tn), index)

    part = pl.BlockSpec((None, None, tm, tn), lambda b, q, i, j, core_ref: (b, q, i, j))
    grid_spec = pltpu.PrefetchScalarGridSpec(
        num_scalar_prefetch=1, grid=(n_l, a, r // tm, c // tn),
        in_specs=[layer_spec(l) for l in range(n_l)] + [part], out_specs=part)
    return pl.pallas_call(
        body, name=name, grid_spec=grid_spec, out_shape=jax.ShapeDtypeStruct((n_l, a, r, c), BF16),
        compiler_params=_params(("arbitrary", "parallel", "parallel", "parallel")),
    )(core, *layers, got)


def add_own_block(h, got, chip, axis, *, name):
    _, l, r, c = got.shape
    tm, tn = _tile2d(r, c)
    ncb = c // tn

    def body(chip_ref, h_ref, got_ref, o_ref):
        o_ref[...] = ((h_ref[...].astype(F32) + got_ref[0].astype(F32)) + got_ref[1].astype(F32)) + got_ref[2].astype(F32)

    if axis == 0:
        h_spec = pl.BlockSpec((None, None, tm, tn), lambda b, i, j, chip_ref: (b, chip_ref[0], i, j))
    else:
        h_spec = pl.BlockSpec((None, tm, tn), lambda b, i, j, chip_ref: (b, i, chip_ref[0] * ncb + j))
    grid_spec = pltpu.PrefetchScalarGridSpec(
        num_scalar_prefetch=1, grid=(l, r // tm, ncb),
        in_specs=[h_spec, pl.BlockSpec((3, None, tm, tn), lambda b, i, j, chip_ref: (0, b, i, j))],
        out_specs=pl.BlockSpec((None, tm, tn), lambda b, i, j, chip_ref: (b, i, j)))
    return pl.pallas_call(
        body, name=name, grid_spec=grid_spec, out_shape=jax.ShapeDtypeStruct((l, r, c), F32),
        compiler_params=_params(("parallel", "parallel", "parallel")),
    )(chip, h, got)


def _adam_update(w, g, m, v):
    bc1, bc2 = 1.0 - ADAM_B1 ** ADAM_STEP, 1.0 - ADAM_B2 ** ADAM_STEP
    mn = ADAM_B1 * m + (1.0 - ADAM_B1) * g
    vn = ADAM_B2 * v + (1.0 - ADAM_B2) * (g * g)
    return -ADAM_LR * ((mn / bc1) / (jnp.sqrt(vn / bc2) + ADAM_EPS) + ADAM_WD * w), mn, vn


def adamw(w, g, m, v, *, name):
    shape = w.shape
    c = shape[-1]
    r = math.prod(shape[:-1]) if len(shape) > 1 else 1
    tm, tn = _tile2d(r, c)

    def body(w_ref, g_ref, m_ref, v_ref, d_ref, mo_ref, vo_ref):
        d_ref[...], mo_ref[...], vo_ref[...] = _adam_update(w_ref[...], g_ref[...], m_ref[...], v_ref[...])

    spec = pl.BlockSpec((tm, tn), lambda i, j: (i, j))
    outs = pl.pallas_call(
        body, name=name, grid=(r // tm, c // tn), in_specs=[spec] * 4, out_specs=[spec] * 3,
        out_shape=[jax.ShapeDtypeStruct((r, c), F32)] * 3, compiler_params=_params(("parallel", "parallel")),
    )(*(t.reshape(r, c) for t in (w, g, m, v)))
    return tuple(o.reshape(shape) for o in outs)


def adamw_halves(w, mine, got, m, v, core, *, name):
    l, r, c = w.shape
    rh = r // 2
    tm, tn = _tile2d(rh, c)

    def body(core_ref, w_ref, mine_ref, got_ref, m_ref, v_ref, g_ref, d_ref, mo_ref, vo_ref):
        g = jnp.where(pl.program_id(1) == core_ref[0], mine_ref[...], got_ref[...])
        g_ref[...] = g
        d_ref[...], mo_ref[...], vo_ref[...] = _adam_update(w_ref[...], g, m_ref[...], v_ref[...])

    full = pl.BlockSpec((None, None, tm, tn), lambda b, h, i, j, core_ref: (b, h, i, j))
    half = pl.BlockSpec((None, tm, tn), lambda b, h, i, j, core_ref: (b, i, j))
    grid_spec = pltpu.PrefetchScalarGridSpec(
        num_scalar_prefetch=1, grid=(l, 2, rh // tm, c // tn),
        in_specs=[full, half, half, full, full], out_specs=[full] * 4)
    outs = pl.pallas_call(
        body, name=name, grid_spec=grid_spec, out_shape=[jax.ShapeDtypeStruct((l, 2, rh, c), F32)] * 4,
        compiler_params=_params(("parallel", "parallel", "parallel", "parallel")),
    )(core, w.reshape(l, 2, rh, c), mine, got, m.reshape(l, 2, rh, c), v.reshape(l, 2, rh, c))
    return tuple(o.reshape(l, r, c) for o in outs)


def _copies(plan, in_refs, out_refs, send_sems, recv_sems):
    x, y, c = lax.axis_index("x"), lax.axis_index("y"), lax.axis_index("c")
    out = []
    for n, (src, dst, peer, got) in enumerate(plan(in_refs, out_refs, x, y, c, count_only=False)):
        if peer is None:
            out.append((pltpu.make_async_copy(src, dst, send_sems.at[n]), None))
        else:
            mk = lambda s, d: pltpu.make_async_remote_copy(
                src_ref=s, dst_ref=d, send_sem=send_sems.at[n], recv_sem=recv_sems.at[n], device_id=peer,
                device_id_type=MESH)
            out.append((mk(src, dst), mk(got, got)))
    return out


def _start_all(copies):
    for mine, _ in copies:
        mine.start()


def _wait_all(copies):
    for mine, theirs in copies:
        if theirs is None:
            mine.wait()
        else:
            theirs.wait_recv()
            mine.wait_send()


def _ride_specs(ride):
    if ride is None:
        return [], [], [], [], []
    inputs, out_shapes, plan = ride
    count = plan([None] * len(inputs), [None] * len(out_shapes), 0, 0, 0, count_only=True)
    hbm = pl.BlockSpec(memory_space=pl.ANY)
    sems = [pltpu.SemaphoreType.DMA((count,)), pltpu.SemaphoreType.DMA((count,))]
    return list(inputs), [hbm] * len(inputs), list(out_shapes), [hbm] * len(out_shapes), sems


def _ride_along(ride, in_refs, out_refs, sems, first, last):
    if ride is None:
        return

    @pl.when(first)
    def _():
        _start_all(_copies(ride[2], in_refs, out_refs, *sems))

    @pl.when(last)
    def _():
        _wait_all(_copies(ride[2], in_refs, out_refs, *sems))


def _comm(name, inputs, out_shapes, aliases, plan):
    n_in, n_out = len(inputs), len(out_shapes)
    probe = plan([None] * n_in, [None] * n_out, 0, 0, 0, count_only=True)

    def body(*refs):
        copies = _copies(plan, refs[:n_in], refs[n_in:n_in + n_out], *refs[n_in + n_out:])
        _start_all(copies)
        _wait_all(copies)

    hbm = pl.BlockSpec(memory_space=pl.ANY)
    return pl.pallas_call(
        body, name=name, in_specs=[hbm] * n_in, out_specs=[hbm] * n_out, out_shape=out_shapes,
        input_output_aliases=aliases,
        scratch_shapes=[pltpu.SemaphoreType.DMA((probe,)), pltpu.SemaphoreType.DMA((probe,))],
    )(*inputs)


def _other_chips(x, y):
    return [(1 - x, y), (x, 1 - y), (1 - x, 1 - y)]


def gather_weights(shards, axes, carrier=None):
    def out_shape(sh, ax):
        l, r, c = sh.shape
        return jax.ShapeDtypeStruct((l, N_CHIPS, 2, r // 2, c) if ax == 0 else (l, 2, r // 2, N_CHIPS * c), BF16)

    def piece(ref, ax, chip, half, width):
        if ax == 0:
            return ref.at[:, chip, half]
        return ref.at[:, half, :, pl.ds(chip * width, width)]

    def plan1(in_refs, out_refs, x, y, c, count_only):
        if count_only:
            return 4 * len(shards)
        me, xfers = 2 * x + y, []
        for sh, ax, src, out in zip(shards, axes, in_refs, out_refs):
            rows, width = sh.shape[1] // 2, sh.shape[2]
            mine = src.at[:, pl.ds(c * rows, rows), :]
            xfers.append((mine, piece(out, ax, me, c, width), None, None))
            for cx, cy in _other_chips(x, y):
                xfers.append((mine, piece(out, ax, me, c, width), (cx, cy, c), piece(out, ax, 2 * cx + cy, c, width)))
        return xfers

    def half_of(ref, ax, half):
        return ref.at[:, :, half] if ax == 0 else ref.at[:, half]

    def plan2(in_refs, out_refs, x, y, c, count_only):
        if count_only:
            return len(shards)
        return [(half_of(out, ax, c), half_of(out, ax, c), (x, y, 1 - c), half_of(out, ax, 1 - c))
                for ax, out in zip(axes, out_refs)]

    shapes = [out_shape(sh, ax) for sh, ax in zip(shards, axes)]
    if carrier is None:
        carried, part = None, _comm("gather_chips", list(shards), shapes, {}, plan1)
    else:
        carried, part = carrier((list(shards), shapes, plan1))
    full = _comm("gather_cores", list(part), shapes, {n: n for n in range(len(shards))}, plan2)
    return [f.reshape(sh.shape[0], N_CHIPS * sh.shape[1], sh.shape[2]) if ax == 0
            else f.reshape(sh.shape[0], sh.shape[1], N_CHIPS * sh.shape[2])
            for f, sh, ax in zip(full, shards, axes)], carried


def reduce_scatter_grads(grads, axes, core, chip, carrier=None):
    dims, views = [], []
    for layers, ax in zip(grads, axes):
        r, c = layers[0].shape
        if ax == 0:
            dims.append((len(layers), r // N_CHIPS // 2, c))
            views.append([g.reshape(N_CHIPS, 2, r // N_CHIPS // 2, c) for g in layers])
        else:
            dims.append((len(layers), r // 2, c // N_CHIPS))
            views.append([g.reshape(1, 2, r // 2, c) for g in layers])

    def half_shape(ax, d):
        return (d[0], N_CHIPS, d[1], d[2]) if ax == 0 else (d[0], d[1], N_CHIPS * d[2])

    def plan_a(in_refs, out_refs, x, y, c, count_only):
        if count_only:
            return sum(d[0] for d in dims)
        xfers, n = [], 0
        for d, out in zip(dims, out_refs):
            for l in range(d[0]):
                xfers.append((in_refs[n].at[:, 1 - c], out.at[l], (x, y, 1 - c), out.at[l]))
                n += 1
        return xfers

    got = _comm("reduce_cores", [v for layers in views for v in layers],
                [jax.ShapeDtypeStruct((d[0], N_CHIPS if ax == 0 else 1, d[1], d[2] if ax == 0 else N_CHIPS * d[2]), F32)
                 for ax, d in zip(axes, dims)], {}, plan_a)
    chip_sum = [add_half(layers, g, core, name="sum_cores").reshape(half_shape(ax, d))
                for ax, d, layers, g in zip(axes, dims, views, got)]

    def block(ref, ax, which, width):
        return ref.at[:, which] if ax == 0 else ref.at[:, :, pl.ds(which * width, width)]

    def plan_b(in_refs, out_refs, x, y, c, count_only):
        if count_only:
            return 3 * len(grads)
        xfers = []
        for ax, d, src, out in zip(axes, dims, in_refs, out_refs):
            for k, (cx, cy) in enumerate(_other_chips(x, y)):
                xfers.append((block(src, ax, 2 * cx + cy, d[2]), out.at[k], (cx, cy, c), out.at[k]))
        return xfers

    part_shapes = [jax.ShapeDtypeStruct((3,) + d, BF16) for d in dims]
    if carrier is None:
        carried, parts = None, _comm("reduce_chips", chip_sum, part_shapes, {}, plan_b)
    else:
        carried, parts = carrier((chip_sum, part_shapes, plan_b))
    mine = [add_own_block(h, p, chip, ax, name="sum_chips") for h, p, ax in zip(chip_sum, parts, axes)]

    def plan_c(in_refs, out_refs, x, y, c, count_only):
        if count_only:
            return len(grads)
        return [(src, out, (x, y, 1 - c), out) for src, out in zip(in_refs, out_refs)]

    theirs = _comm("share_cores", mine, [jax.ShapeDtypeStruct(d, F32) for d in dims], {}, plan_c)
    return list(zip(mine, theirs)), carried


def gather_slabs(vec, *, name):
    def plan(in_refs, out_refs, x, y, c, count_only):
        if count_only:
            return 8
        me = 4 * x + 2 * y + c
        xfers = [(in_refs[0], out_refs[0].at[me], None, None)]
        for k in range(1, 8):
            px, py, pc = x ^ (k >> 2), y ^ ((k >> 1) & 1), c ^ (k & 1)
            xfers.append((in_refs[0], out_refs[0].at[me], (px, py, pc), out_refs[0].at[4 * px + 2 * py + pc]))
        return xfers

    return _comm(name, [vec], [jax.ShapeDtypeStruct((8,) + vec.shape, F32)], {}, plan)[0]


def allreduce_small(vec):
    return sum_slots(gather_slabs(vec, name="gather_small"), name="sum_small")


def gather_conv_w(block):
    l, taps, c = block.shape
    flat = block.reshape(-1)
    slabs = gather_slabs(jnp.pad(flat, (0, -flat.size % 1024)).reshape(-1, 128), name="gather_conv_w")
    per_chip = slabs[0::2].reshape(N_CHIPS, -1)[:, :flat.size].reshape(N_CHIPS, l, taps, c)
    return per_chip.transpose(1, 2, 0, 3).reshape(l, taps, N_CHIPS * c)


def _rope_tables(positions):
    inv_freq = ROPE_THETA ** (-jnp.arange(0, ROPE_DIM, 2, dtype=F32) / ROPE_DIM)
    ang = positions.astype(F32)[:, None] * inv_freq
    ang = jnp.concatenate([ang, ang], axis=-1)
    pad = ((0, 0), (0, HEAD - ROPE_DIM))
    return jnp.pad(jnp.cos(ang), pad), jnp.pad(jnp.sin(ang), pad)


def _uq_layout(w, heads):
    ql = w.shape[0]
    w = w.reshape(ql, heads, HEAD + ROPE_DIM)
    rot = jnp.pad(w[:, :, HEAD:], ((0, 0), (0, 0), (0, HEAD - ROPE_DIM)))
    return jnp.concatenate([w[:, :, :HEAD].reshape(ql, heads * HEAD), rot.reshape(ql, heads * HEAD)], axis=1)


def _uq_layout_inv(dw, heads):
    ql = dw.shape[0]
    nope = dw[:, :heads * HEAD].reshape(ql, heads, HEAD)
    rot = dw[:, heads * HEAD:].reshape(ql, heads, HEAD)[:, :, :ROPE_DIM]
    return jnp.concatenate([nope, rot], axis=-1).reshape(ql, heads * (HEAD + ROPE_DIM))


def kernel(x, mem, positions, norm_mix, norm_mem_q, norm_mem_kv, norm_ffn, norm_final, mla_w_down, mla_q_norm, mla_w_uq, mla_kv_norm, mla_w_ukv, mla_w_o, sb_w_qkv, sb_w_o, mem_w_q, mem_w_kv, mem_w_o, ffn_w_in, ffn_conv_w, ffn_conv_b, ffn_w_out, loss_target, m_norm_mix, m_norm_mem_q, m_norm_mem_kv, m_norm_ffn, m_norm_final, m_mla_w_down, m_mla_q_norm, m_mla_w_uq, m_mla_kv_norm, m_mla_w_ukv, m_mla_w_o, m_sb_w_qkv, m_sb_w_o, m_mem_w_q, m_mem_w_kv, m_mem_w_o, m_ffn_w_in, m_ffn_conv_w, m_ffn_conv_b, m_ffn_w_out, v_norm_mix, v_norm_mem_q, v_norm_mem_kv, v_norm_ffn, v_norm_final, v_mla_w_down, v_mla_q_norm, v_mla_w_uq, v_mla_kv_norm, v_mla_w_ukv, v_mla_w_o, v_sb_w_qkv, v_sb_w_o, v_mem_w_q, v_mem_w_kv, v_mem_w_o, v_ffn_w_in, v_ffn_conv_w, v_ffn_conv_b, v_ffn_w_out):
    args = dict(locals())
    big = ["mla_w_down", "mla_w_uq", "mla_w_ukv", "mla_w_o", "sb_w_qkv", "sb_w_o", "mem_w_q", "mem_w_kv",
           "mem_w_o", "ffn_w_in", "ffn_w_out"]
    col_cut = {"mla_w_uq", "mla_w_ukv", "sb_w_qkv", "mem_w_o", "ffn_w_in"}
    axes = [1 if n in col_cut else 0 for n in big]
    small = ["norm_mix", "norm_mem_q", "norm_mem_kv", "norm_ffn", "norm_final", "mla_q_norm", "mla_kv_norm",
             "ffn_conv_b", "ffn_conv_w"]
    order = ["norm_mix", "norm_mem_q", "norm_mem_kv", "norm_ffn", "norm_final", "mla_w_down", "mla_q_norm",
             "mla_w_uq", "mla_kv_norm", "mla_w_ukv", "mla_w_o", "sb_w_qkv", "sb_w_o", "mem_w_q", "mem_w_kv",
             "mem_w_o", "ffn_w_in", "ffn_conv_w", "ffn_conv_b", "ffn_w_out"]

    xs, mems, target = x[0], mem[0], loss_target[0]
    s, d = xs.shape
    depth = norm_mix.shape[0]
    ql, kvl = mla_q_norm.shape[1], mla_kv_norm.shape[1]
    mla_heads = N_CHIPS * mla_w_uq.shape[2] // (HEAD + ROPE_DIM)
    sb_heads = N_CHIPS * sb_w_qkv.shape[2] // (3 * HEAD)
    mem_heads = mem_w_q.shape[2] // HEAD
    ff = N_CHIPS * ffn_w_out.shape[1]
    chip = 2 * lax.axis_index("x") + lax.axis_index("y")
    chip_op = jnp.reshape(chip, (1,)).astype(jnp.int32)
    core_op = jnp.reshape(lax.axis_index("c"), (1,)).astype(jnp.int32)

    axis_of = dict(zip(big, axes))
    early = [n for n in big if not n.startswith("sb_")]
    late_layers = {n: list(range(0 if n.startswith("sb_") else 1, args[n].shape[0])) for n in big}
    whole = {n: {} for n in big}
    early_full, _ = gather_weights([args[n][0:1].astype(BF16) for n in early], [axis_of[n] for n in early])
    for n, f in zip(early, early_full):
        whole[n][0] = f[0]
    w_uq = {0: _uq_layout(whole["mla_w_uq"][0], mla_heads)}
    cos, sin = _rope_tables(positions[0])
    _, sb_tk = _sb_tiles(s)
    tri = (jnp.arange(sb_tk)[:, None] > jnp.arange(sb_tk)[None, :]).astype(BF16)
    conv_w = gather_conv_w(ffn_conv_w)

    saved = []
    xa = xs
    for i in range(depth):
        j = i // 2
        lay = {}
        lay["xa"] = xa
        h1 = rmsnorm_fwd(xa, norm_mix[i], name="norm_fwd")
        lay["h1"] = h1
        if i % 2 == 0:
            down = mm(h1, whole["mla_w_down"][j], out_dtype=F32, name="mm_down")
            cq = rmsnorm_fwd(down[:, :ql], mla_q_norm[j], name="norm_lora_fwd")
            ckv = rmsnorm_fwd(down[:, ql:ql + kvl], mla_kv_norm[j], name="norm_lora_fwd")
            kr_raw = jnp.pad(down[:, ql + kvl:], ((0, 0), (0, HEAD - ROPE_DIM)))
            kr = rope(kr_raw, cos, sin, col0=0, n=1, inverse=False, name="rope_k")
            qall = mm(cq, w_uq[j], name="mm_uq")
            qr = rope(qall, cos, sin, col0=mla_heads, n=mla_heads, inverse=False, name="rope_q")
            kv = mm(ckv, whole["mla_w_ukv"][j], name="mm_ukv")
            if i == 0:
                def attend_and_gather(ride, att=(qall, qr, kv, kr)):
                    o_, lse_, part = mla_fwd(*att, mla_heads, name="mla_fwd_gather", ride=ride)
                    return (o_, lse_), part

                late_full, (o, lse) = gather_weights(
                    [args[n][late_layers[n][0]:].astype(BF16) for n in big], axes, attend_and_gather)
                for n, f in zip(big, late_full):
                    for at, l in enumerate(late_layers[n]):
                        whole[n][l] = f[at]
                for l in late_layers["mla_w_uq"]:
                    w_uq[l] = _uq_layout(whole["mla_w_uq"][l], mla_heads)
            else:
                o, lse = mla_fwd(qall, qr, kv, kr, mla_heads, name="mla_fwd")
            xb = mm(o, whole["mla_w_o"][j], add=xa, out_dtype=F32, name="mm_out_res")
            lay.update(down=down, cq=cq, ckv=ckv, qall=qall, qr=qr, kv=kv, kr=kr, o=o, lse=lse)
        else:
            qkv = mm(h1, whole["sb_w_qkv"][j], name="mm_qkv")
            o, cmat = sb_fwd(qkv, tri, sb_heads, name="sb_fwd")
            xb = mm(o, whole["sb_w_o"][j], add=xa, out_dtype=F32, name="mm_out_res")
            lay.update(qkv=qkv, o=o, cmat=cmat)
        h2 = rmsnorm_fwd(xb, norm_mem_q[i], name="norm_fwd")
        hm = rmsnorm_fwd(mems, norm_mem_kv[i], name="norm_mem_fwd")
        qm = mm(h2, whole["mem_w_q"][i], name="mm_mem_q")
        kvm = mm(hm, whole["mem_w_kv"][i], name="mm_mem_kv")
        om = mem_fwd(qm, kvm, mem_heads, name="mem_fwd")
        xc = mm(om, whole["mem_w_o"][i], add=xb, out_dtype=F32, name="mm_mem_out_res")
        h3 = rmsnorm_fwd(xc, norm_ffn[i], name="norm_fwd")
        u = mm(h3, whole["ffn_w_in"][i], name="mm_ffn_in")
        act = conv_gate_fwd(u, conv_w[i], ffn_conv_b[i][None, :], name="conv_gate_fwd")
        xd = mm(act, whole["ffn_w_out"][i], add=xc, out_dtype=F32, name="mm_ffn_out_res")
        lay.update(xb=xb, h2=h2, hm=hm, qm=qm, kvm=kvm, om=om, xc=xc, h3=h3, u=u, act=act)
        saved.append(lay)
        xa = xd

    dx, dxb, g_final, loss_part = final_loss(xa, norm_final, target, name="final_loss")

    gw = {n: [None] * args[n].shape[0] for n in big}
    g_small = {"norm_mix": [None] * depth, "norm_mem_q": [None] * depth, "norm_mem_kv": [None] * depth,
               "norm_ffn": [None] * depth, "mla_q_norm": [None] * (depth - depth // 2),
               "mla_kv_norm": [None] * (depth - depth // 2), "conv": [None] * depth}
    for i in reversed(range(depth)):
        j = i // 2
        lay = saved[i]
        gw["ffn_w_out"][i] = mm(lay["act"], dxb, mode="tn", out_dtype=F32, name="mm_ffn_out_wgrad")
        da = mm(dxb, whole["ffn_w_out"][i], mode="nt", name="mm_ffn_out_dgrad")
        dcg, dcu, g_small["conv"][i] = conv_gate_bwd(lay["u"], da, conv_w[i], ffn_conv_b[i][None, :], name="conv_gate_bwd")
        du = conv_transpose(dcg, dcu, conv_w[i], name="conv_transpose")
        gw["ffn_w_in"][i] = mm(lay["h3"], du, mode="tn", out_dtype=F32, name="mm_ffn_in_wgrad")
        dh3 = mm(du, whole["ffn_w_in"][i], mode="nt", name="mm_ffn_in_dgrad")
        dx, dxb, g_small["norm_ffn"][i] = rmsnorm_bwd(dh3, lay["xc"], norm_ffn[i], dx, name="norm_bwd")
        gw["mem_w_o"][i] = mm(lay["om"], dxb, mode="tn", out_dtype=F32, name="mm_mem_out_wgrad")
        dom = mm(dxb, whole["mem_w_o"][i], mode="nt", name="mm_mem_out_dgrad")
        dqm, dkm, dvm = mem_bwd(lay["qm"], lay["kvm"], dom, mem_heads, name="mem_bwd")
        dkvm = jnp.concatenate([dkm, dvm], axis=1)
        gw["mem_w_q"][i] = mm(lay["h2"], dqm, mode="tn", out_dtype=F32, name="mm_mem_q_wgrad")
        gw["mem_w_kv"][i] = mm(lay["hm"], dkvm, mode="tn", out_dtype=F32, name="mm_mem_kv_wgrad")
        dh2 = mm(dqm, whole["mem_w_q"][i], mode="nt", name="mm_mem_q_dgrad")
        dhm = mm(dkvm, whole["mem_w_kv"][i], mode="nt", name="mm_mem_kv_dgrad")
        _, _, g_small["norm_mem_kv"][i] = rmsnorm_bwd(dhm, mems, norm_mem_kv[i], name="norm_mem_bwd")
        dx, dxb, g_small["norm_mem_q"][i] = rmsnorm_bwd(dh2, lay["xb"], norm_mem_q[i], dx, name="norm_bwd")
        if i % 2 == 0:
            gw["mla_w_o"][j] = mm(lay["o"], dxb, mode="tn", out_dtype=F32, name="mm_out_wgrad")
            do = mm(dxb, whole["mla_w_o"][j], mode="nt", name="mm_out_dgrad")
            att = (lay["qall"], lay["qr"], lay["kv"], lay["kr"], lay["o"], do, lay["lse"], mla_heads)
            if i == 0:
                def attend_and_reduce(ride, att=att):
                    *dqkv_, parts_ = mla_bwd(*att, name="mla_bwd_reduce", ride=ride)
                    return dqkv_, parts_

                halves_late, (dqn, dqr, dkv, dkr_heads) = reduce_scatter_grads(
                    [gw[n][1:] if n.startswith("mla_") else gw[n] for n in big], axes, core_op, chip_op,
                    attend_and_reduce)
            else:
                dqn, dqr, dkv, dkr_heads = mla_bwd(*att, name="mla_bwd")
            dkr = sum_lane_tiles(dkr_heads, mla_heads, name="sum_heads")
            dqall = jnp.concatenate([dqn, rope(dqr, cos, sin, col0=0, n=mla_heads, inverse=True, name="rope_q_bwd")], axis=1)
            gw["mla_w_ukv"][j] = mm(lay["ckv"], dkv, mode="tn", out_dtype=F32, name="mm_ukv_wgrad")
            dckv = mm(dkv, whole["mla_w_ukv"][j], mode="nt", name="mm_ukv_dgrad")
            gw["mla_w_uq"][j] = _uq_layout_inv(mm(lay["cq"], dqall, mode="tn", out_dtype=F32, name="mm_uq_wgrad"), mla_heads)
            dcq = mm(dqall, w_uq[j], mode="nt", name="mm_uq_dgrad")
            down = lay["down"]
            _, d_q, g_small["mla_q_norm"][j] = rmsnorm_bwd(dcq, down[:, :ql], mla_q_norm[j], name="norm_lora_bwd")
            _, d_kv, g_small["mla_kv_norm"][j] = rmsnorm_bwd(dckv, down[:, ql:ql + kvl], mla_kv_norm[j], name="norm_lora_bwd")
            d_kr = rope(dkr, cos, sin, col0=0, n=1, inverse=True, name="rope_k_bwd")[:, :ROPE_DIM]
            ddown = jnp.concatenate([d_q, d_kv, d_kr], axis=1)
            gw["mla_w_down"][j] = mm(lay["h1"], ddown, mode="tn", out_dtype=F32, name="mm_down_wgrad")
            dh1 = mm(ddown, whole["mla_w_down"][j], mode="nt", name="mm_down_dgrad")
        else:
            gw["sb_w_o"][j] = mm(lay["o"], dxb, mode="tn", out_dtype=F32, name="mm_out_wgrad")
            do = mm(dxb, whole["sb_w_o"][j], mode="nt", name="mm_out_dgrad")
            dq, dk, dv = sb_bwd(lay["qkv"], do, lay["cmat"], tri, sb_heads, name="sb_bwd")
            dqkv = jnp.concatenate([dq, dk, dv], axis=1)
            gw["sb_w_qkv"][j] = mm(lay["h1"], dqkv, mode="tn", out_dtype=F32, name="mm_qkv_wgrad")
            dh1 = mm(dqkv, whole["sb_w_qkv"][j], mode="nt", name="mm_qkv_dgrad")
        dx, dxb, g_small["norm_mix"][i] = rmsnorm_bwd(dh1, lay["xa"], norm_mix[i], dx, name="norm_bwd")

    halves = dict(zip(big, halves_late))
    mla_names = [n for n in big if n.startswith("mla_")]
    halves_first, _ = reduce_scatter_grads([gw[n][:1] for n in mla_names], [axis_of[n] for n in mla_names],
                                           core_op, chip_op)
    for n, first in zip(mla_names, halves_first):
        halves[n] = tuple(jnp.concatenate([a, b], axis=0) for a, b in zip(first, halves[n]))
    conv = jnp.stack(g_small["conv"])
    parts = [jnp.concatenate(g_small[n], axis=0) for n in ("norm_mix", "norm_mem_q", "norm_mem_kv", "norm_ffn")]
    parts += [g_final, jnp.concatenate(g_small["mla_q_norm"], axis=0), jnp.concatenate(g_small["mla_kv_norm"], axis=0),
              conv[:, 3, :], conv[:, :3, :], loss_part[:1, :1]]
    sizes = [p.size for p in parts]
    packed = jnp.concatenate([p.reshape(-1) for p in parts])
    packed = jnp.pad(packed, (0, -packed.size % 1024)).reshape(-1, 128)
    total = allreduce_small(packed).reshape(-1)
    g_rep, at = {}, 0
    for n, p, size in zip(small + ["loss"], parts, sizes):
        g_rep[n] = total[at:at + size].reshape(p.shape)
        at += size
    loss = g_rep.pop("loss").reshape(())
    g_rep["norm_final"] = g_rep["norm_final"].reshape(norm_final.shape)
    width = ffn_conv_w.shape[2]
    g_rep["ffn_conv_w"] = lax.dynamic_slice_in_dim(g_rep["ffn_conv_w"], chip * width, width, axis=2)

    grads, delta, new_m, new_v = {}, {}, {}, {}
    for n in order:
        if n in halves:
            mine, theirs = halves[n]
            grads[n], delta[n], new_m[n], new_v[n] = adamw_halves(
                args[n], mine, theirs, args["m_" + n], args["v_" + n], core_op, name="adamw_big")
        else:
            grads[n] = g_rep[n]
            delta[n], new_m[n], new_v[n] = adamw(args[n], g_rep[n], args["m_" + n], args["v_" + n], name="adamw")
    return (loss, dx[None], *[grads[n] for n in order], *[delta[n] for n in order],
            *[new_m[n] for n in order], *[new_v[n] for n in order])
```
